```python
import jax
import jax.numpy as jnp
from jax import lax
import numpy as np

D_MODEL = 1024
BATCH = 4
SEQ = 8192
DEPTH = 2
DEC_BATCH = 128
DEC_SEQ = 1
PAST_LEN = 16384
PAGE_SIZE = 128

N_EVEN = (DEPTH + 1) // 2
N_ODD = DEPTH // 2
Q_BLOCK = 128
EPS = 1e-6
NEG_INF = -1e30
ROPE_BASE = 10000.0

NSA_HEADS = 8
NSA_DK = 64
NSA_BLK = 64
NSA_TOPN = 16
NSA_WIN = 512
NSA_CMP_HID = 256
MLA_HEADS = 8
MLA_NOPE = 64
MLA_ROPE = 32
MLA_V = 64
MLA_LORA = 128
MLA_ROW = MLA_LORA + MLA_ROPE
SWA_HEADS = 16
SWA_KV_HEADS = 4
SWA_HD = 64
SWA_WIN = 128
N_MEM = 256
MEM_HEADS = 4
MEM_HD = 128
MEM_W = MEM_HEADS * MEM_HD
PEER_HEADS = 8
PEER_NKEYS = 128
PEER_N = PEER_NKEYS * PEER_NKEYS
PEER_DKEY = 128
PEER_TOPK = 16
PEER_CHUNK = 256

NSA_Q_COLS = NSA_HEADS * NSA_DK
NSA_KV_COLS = 3 * 2 * NSA_DK
NSA_GATE_COLS = NSA_HEADS * 3
MLA_Q_COLS = MLA_HEADS * (MLA_NOPE + MLA_ROPE)
EVEN_IN = NSA_Q_COLS + NSA_KV_COLS + NSA_GATE_COLS + MLA_Q_COLS + MLA_ROW
EVEN_OUT = NSA_HEADS * NSA_DK + MLA_HEADS * MLA_V
ODD_IN = SWA_HEADS * SWA_HD + 2 * SWA_KV_HEADS * SWA_HD
ODD_OUT = SWA_HEADS * SWA_HD

kernel_name = "hybrid_nsa_mla_swa_peer_step"


def rmsnorm(x, g):
    xf = x.astype(jnp.float32)
    y = xf * lax.rsqrt(jnp.mean(xf * xf, axis=-1, keepdims=True) + EPS)
    return (y * g.astype(jnp.float32)).astype(x.dtype)


def alibi_slopes(n_heads):
    return jnp.asarray((2.0 ** (-8.0 * np.arange(1, n_heads + 1) / n_heads)).astype(np.float32))


def rope(x, pos):
    d = x.shape[-1]
    inv = jnp.asarray(np.power(ROPE_BASE, -np.arange(0, d, 2, dtype=np.float32) / d).astype(np.float32))
    ang = pos.astype(jnp.float32)[:, None] * inv[None, :]
    cos = jnp.cos(ang)[None, :, None, :]
    sin = jnp.sin(ang)[None, :, None, :]
    xf = x.astype(jnp.float32)
    x1, x2 = xf[..., : d // 2], xf[..., d // 2:]
    return jnp.concatenate([x1 * cos - x2 * sin, x1 * sin + x2 * cos], axis=-1).astype(x.dtype)


def masked_softmax(s, valid, sink=None):
    s = jnp.where(valid, s, NEG_INF)
    m = jnp.max(s, axis=-1, keepdims=True)
    if sink is not None:
        sk = sink.astype(jnp.float32)[:, None, None]
        m = jnp.maximum(m, sk)
    e = jnp.where(valid, jnp.exp(s - m), 0.0)
    den = jnp.sum(e, axis=-1, keepdims=True)
    if sink is not None:
        den = den + jnp.exp(sk - m)
    return e / jnp.maximum(den, 1e-30)


def attend(q, k, v, dist, valid, slopes, sink=None):
    B, Q, H, d = q.shape
    G = k.shape[-2]
    qg = q.reshape(B, Q, G, H // G, d)
    if k.ndim == 5:
        s = jnp.einsum("bqgrd,bqkgd->bgrqk", qg, k)
    else:
        s = jnp.einsum("bqgrd,bkgd->bgrqk", qg, k)
    K = s.shape[-1]
    s = s.reshape(B, H, Q, K).astype(jnp.float32) * (d ** -0.5)
    if slopes is not None:
        s = s - slopes[:, None, None] * dist[..., None, :, :].astype(jnp.float32)
    p = masked_softmax(s, valid[..., None, :, :], sink)
    pg = p.reshape(B, G, H // G, Q, K).astype(v.dtype)
    if v.ndim == 5:
        o = jnp.einsum("bgrqk,bqkgd->bqgrd", pg, v)
    else:
        o = jnp.einsum("bgrqk,bkgd->bqgrd", pg, v)
    return o.reshape(B, Q, H, v.shape[-1]), p


def banded_prompt(q, k, v, window, slopes, sink=None):
    B, T, H, d = q.shape
    pad = ((0, 0), (window, 0), (0, 0), (0, 0))
    kp = jnp.pad(k, pad)
    vp = jnp.pad(v, pad)
    span = Q_BLOCK + window

    def block(i):
        s0 = i * Q_BLOCK
        qb = lax.dynamic_slice_in_dim(q, s0, Q_BLOCK, axis=1)
        kb = lax.dynamic_slice_in_dim(kp, s0, span, axis=1)
        vb = lax.dynamic_slice_in_dim(vp, s0, span, axis=1)
        qpos = s0 + jnp.arange(Q_BLOCK)
        kpos = s0 - window + jnp.arange(span)
        dist = qpos[:, None] - kpos[None, :]
        valid = (dist >= 0) & (dist <= window) & (kpos[None, :] >= 0)
        return attend(qb, kb, vb, dist, valid, slopes, sink)[0]

    o = lax.map(block, jnp.arange(T // Q_BLOCK))
    return o.transpose(1, 0, 2, 3, 4).reshape(B, T, H, -1)


def window_sample(q, k_buf, v_buf, k_new, v_new, past_len, window, slopes, sink=None):
    Wb, Tn = k_buf.shape[1], q.shape[1]
    k = jnp.concatenate([k_buf, k_new], axis=1)
    v = jnp.concatenate([v_buf, v_new], axis=1)
    qpos = past_len + jnp.arange(Tn)
    kpos = past_len - Wb + jnp.arange(Wb + Tn)
    dist = qpos[:, None] - kpos[None, :]
    valid = (dist >= 0) & (dist <= window)
    return attend(q, k, v, dist, valid, slopes, sink)[0]


def nsa_compress(blocks, pos_emb, w1, w2):
    x = blocks + pos_emb
    h = jax.nn.gelu(jnp.einsum("...lcd,lcdh->...ch", x, w1))
    return jnp.einsum("...ch,chd->...cd", h, w2)


def nsa_select(p_cmp, qpos, n_blocks):
    imp = jnp.sum(p_cmp, axis=1)
    blk = jnp.arange(n_blocks)[None, :]
    cur = (qpos // NSA_BLK)[:, None]
    imp = jnp.where(blk == cur, float(NSA_HEADS + 1), jnp.where(blk < cur, imp, -1.0))
    _, idx = lax.top_k(imp, min(NSA_TOPN, n_blocks))
    return idx


def nsa_compressed_branch(q, qpos, kc, vc, c_end, slopes):
    dist = qpos[:, None] - c_end[None, :]
    o, p = attend(q, kc[:, :, None, :], vc[:, :, None, :], dist, dist >= 0, slopes)
    return o, nsa_select(p, qpos, kc.shape[1])


def nsa_selected_branch(q, qpos, sel, idx, slopes):
    B, Q, n = idx.shape
    kpos = (idx[..., None] * NSA_BLK + jnp.arange(NSA_BLK)).reshape(B, Q, n * NSA_BLK)
    dist = qpos[None, :, None] - kpos
    ks = sel[..., 0, :].reshape(B, Q, n * NSA_BLK, 1, NSA_DK)
    vs = sel[..., 1, :].reshape(B, Q, n * NSA_BLK, 1, NSA_DK)
    return attend(q, ks, vs, dist, dist >= 0, slopes)[0]


def nsa_gate(gates, o_cmp, o_slc, o_win):
    g = gates.astype(o_cmp.dtype)
    return g[..., 0:1] * o_cmp + g[..., 1:2] * o_slc + g[..., 2:3] * o_win


def nsa_prompt(q, cmp_kv, slc_kv, win_kv, gates, ep, slopes):
    B, T = q.shape[:2]
    nb = T // NSA_BLK
    comp = nsa_compress(cmp_kv.reshape(B, nb, NSA_BLK, 2, NSA_DK), ep["cmp_pos"], ep["cmp_w1"], ep["cmp_w2"])
    kc = rmsnorm(comp[:, :, 0], ep["g_k"][0])
    vc = comp[:, :, 1]
    c_end = (jnp.arange(nb) + 1) * NSA_BLK - 1
    slc_blocks = slc_kv.reshape(B, nb, NSA_BLK, 2, NSA_DK)

    def block(i):
        s0 = i * Q_BLOCK
        qb = lax.dynamic_slice_in_dim(q, s0, Q_BLOCK, axis=1)
        qpos = s0 + jnp.arange(Q_BLOCK)
        o_cmp, idx = nsa_compressed_branch(qb, qpos, kc, vc, c_end, slopes)
        sel = jax.vmap(lambda blocks, ix: blocks[ix])(slc_blocks, idx)
        return o_cmp, nsa_selected_branch(qb, qpos, sel, idx, slopes)

    o_cmp, o_slc = lax.map(block, jnp.arange(T // Q_BLOCK))
    o_cmp = o_cmp.transpose(1, 0, 2, 3, 4).reshape(B, T, NSA_HEADS, NSA_DK)
    o_slc = o_slc.transpose(1, 0, 2, 3, 4).reshape(B, T, NSA_HEADS, NSA_DK)
    o_win = banded_prompt(q, win_kv[:, :, 0:1], win_kv[:, :, 1:2], NSA_WIN, slopes)
    return nsa_gate(gates, o_cmp, o_slc, o_win)


def nsa_sample(q, qpos, cmp_new, slc_new, win_new, gates, cache_cmp, cache_slc, win_buf, page_table, e, ep, slopes):
    DB, Tn = q.shape[:2]
    n_pages = page_table.shape[1]
    bpp = PAGE_SIZE // NSA_BLK
    nb_past = n_pages * bpp
    past_len = n_pages * PAGE_SIZE
    nb_new = -(-Tn // NSA_BLK)
    pad = ((0, 0), (0, nb_new * NSA_BLK - Tn), (0, 0), (0, 0))

    def compress(rows):
        return nsa_compress(rows, ep["cmp_pos"], ep["cmp_w1"], ep["cmp_w2"])

    comp_past = lax.map(lambda pages: compress(cache_cmp[e, pages].reshape(DB, bpp, NSA_BLK, 2, NSA_DK)), page_table.T)
    comp_past = comp_past.transpose(1, 0, 2, 3, 4).reshape(DB, nb_past, 2, NSA_DK)
    comp_new = compress(jnp.pad(cmp_new, pad).reshape(DB, nb_new, NSA_BLK, 2, NSA_DK))
    comp = jnp.concatenate([comp_past, comp_new], axis=1)
    kc = rmsnorm(comp[:, :, 0], ep["g_k"][0])
    vc = comp[:, :, 1]
    c_end = (jnp.arange(nb_past + nb_new) + 1) * NSA_BLK - 1
    o_cmp, idx = nsa_compressed_branch(q, qpos, kc, vc, c_end, slopes)

    jp = jnp.minimum(idx, nb_past - 1)
    phys_page = jax.vmap(lambda pt, j: pt[j])(page_table, jp // bpp)
    rows = (jp % bpp)[..., None] * NSA_BLK + jnp.arange(NSA_BLK)
    past_sel = cache_slc[e, phys_page[..., None], rows]
    slc_new_b = jnp.pad(slc_new, pad).reshape(DB, nb_new, NSA_BLK, 2, NSA_DK)
    jn = jnp.clip(idx - nb_past, 0, nb_new - 1)
    new_sel = jax.vmap(lambda blocks, ix: blocks[ix])(slc_new_b, jn)
    sel = jnp.where((idx < nb_past)[..., None, None, None], past_sel, new_sel)
    o_slc = nsa_selected_branch(q, qpos, sel, idx, slopes)

    o_win = window_sample(q, win_buf[:, :, 0:1], win_buf[:, :, 1:2], win_new[:, :, 0:1], win_new[:, :, 1:2],
                          past_len, NSA_WIN, slopes)
    return nsa_gate(gates, o_cmp, o_slc, o_win)


def mla_keys(lat, ep):
    c = lat[..., :MLA_LORA]
    kp = lat[..., MLA_LORA:]
    kn = rmsnorm(jnp.einsum("bkc,chd->bkhd", c, ep["w_uk"]), ep["g_kn"])
    v = jnp.einsum("bkc,chd->bkhd", c, ep["w_uv"])
    return kn, kp, v


def mla_scores(qn, qp, kn, kp):
    s = jnp.einsum("bqhd,bkhd->bhqk", qn, kn) + jnp.einsum("bqhd,bkd->bhqk", qp, kp)
    return s.astype(jnp.float32) * ((MLA_NOPE + MLA_ROPE) ** -0.5)


def mla_prompt(qn, qp, lat, ep):
    B, T = qn.shape[:2]
    kn, kp, v = mla_keys(lat, ep)
    kpos = jnp.arange(T)

    def block(i):
        s0 = i * Q_BLOCK
        qnb = lax.dynamic_slice_in_dim(qn, s0, Q_BLOCK, axis=1)
        qpb = lax.dynamic_slice_in_dim(qp, s0, Q_BLOCK, axis=1)
        qpos = s0 + jnp.arange(Q_BLOCK)
        p = masked_softmax(mla_scores(qnb, qpb, kn, kp), qpos[:, None] >= kpos[None, :])
        return jnp.einsum("bhqk,bkhd->bqhd", p.astype(v.dtype), v)

    o = lax.map(block, jnp.arange(T // Q_BLOCK))
    return o.transpose(1, 0, 2, 3, 4).reshape(B, T, MLA_HEADS, MLA_V)


def mla_sample(qn, qp, lat_new, cache_mla, page_table, e, ep):
    DB, Tn = qn.shape[:2]

    def update(carry, kn, kp, v, valid):
        m, l, acc = carry
        s = jnp.where(valid, mla_scores(qn, qp, kn, kp), NEG_INF)
        m_new = jnp.maximum(m, jnp.max(s, axis=-1))
        corr = jnp.exp(m - m_new)
        p = jnp.where(valid, jnp.exp(s - m_new[..., None]), 0.0)
        l = l * corr + jnp.sum(p, axis=-1)
        acc = acc * corr[..., None] + jnp.einsum("bhqk,bkhd->bhqd", p.astype(v.dtype), v).astype(jnp.float32)
        return (m_new, l, acc)

    def step(carry, pages):
        kn, kp, v = mla_keys(cache_mla[e, pages], ep)
        return update(carry, kn, kp, v, True), None

    init = (jnp.full((DB, MLA_HEADS, Tn), NEG_INF, jnp.float32),
            jnp.zeros((DB, MLA_HEADS, Tn), jnp.float32),
            jnp.zeros((DB, MLA_HEADS, Tn, MLA_V), jnp.float32))
    carry, _ = lax.scan(step, init, page_table.T)
    kn, kp, v = mla_keys(lat_new, ep)
    ar = jnp.arange(Tn)
    m, l, acc = update(carry, kn, kp, v, ar[:, None] >= ar[None, :])
    return (acc / l[..., None]).astype(qn.dtype).transpose(0, 2, 1, 3)


def even_project(xn, pos, ep):
    B, T = xn.shape[:2]
    h = xn @ ep["w_in"]
    cuts = np.cumsum([NSA_Q_COLS, NSA_KV_COLS, NSA_GATE_COLS, MLA_Q_COLS]).tolist()
    q, kv, g, mq, lat = jnp.split(h, cuts, axis=-1)
    q = rmsnorm(q.reshape(B, T, NSA_HEADS, NSA_DK), ep["g_q"])
    kv = kv.reshape(B, T, 3, 2, NSA_DK)
    cmp_kv = kv[:, :, 0]
    slc_kv = jnp.stack([rmsnorm(kv[:, :, 1, 0], ep["g_k"][1]), kv[:, :, 1, 1]], axis=2)
    win_kv = jnp.stack([rmsnorm(kv[:, :, 2, 0], ep["g_k"][2]), kv[:, :, 2, 1]], axis=2)
    gates = jax.nn.sigmoid(g.astype(jnp.float32)).reshape(B, T, NSA_HEADS, 3)
    mq = mq.reshape(B, T, MLA_HEADS, MLA_NOPE + MLA_ROPE)
    qn = rmsnorm(mq[..., :MLA_NOPE], ep["g_qn"])
    qp = rope(rmsnorm(mq[..., MLA_NOPE:], ep["g_qpe"]), pos)
    c = rmsnorm(lat[..., :MLA_LORA], ep["g_lat"])
    kp = rope(rmsnorm(lat[..., MLA_LORA:], ep["g_kpe"])[:, :, None, :], pos)[:, :, 0, :]
    return q, cmp_kv, slc_kv, win_kv, gates, qn, qp, jnp.concatenate([c, kp], axis=-1)


def even_output(o_nsa, o_mla, w_out):
    B, T = o_nsa.shape[:2]
    return jnp.concatenate([o_nsa.reshape(B, T, -1), o_mla.reshape(B, T, -1)], axis=-1) @ w_out


def odd_project(xn, op):
    B, T = xn.shape[:2]
    h = xn @ op["w_in"]
    q, k, v = jnp.split(h, [SWA_HEADS * SWA_HD, SWA_HEADS * SWA_HD + SWA_KV_HEADS * SWA_HD], axis=-1)
    q = rmsnorm(q.reshape(B, T, SWA_HEADS, SWA_HD), op["g_q"])
    k = rmsnorm(k.reshape(B, T, SWA_KV_HEADS, SWA_HD), op["g_k"])
    v = v.reshape(B, T, SWA_KV_HEADS, SWA_HD)
    return q, jnp.stack([k, v], axis=2)


def memory_kv(mem, cp):
    B, M = mem.shape[:2]
    mn = rmsnorm(mem, cp["g_mem"])
    k = rmsnorm((mn @ cp["w_k"]).reshape(B, M, MEM_HEADS, MEM_HD), cp["g_k"])
    v = (mn @ cp["w_v"]).reshape(B, M, MEM_HEADS, MEM_HD)
    return jnp.stack([k, v], axis=2)


def memory_cross(xn, mem_kv, cp):
    B, T = xn.shape[:2]
    q = rmsnorm((xn @ cp["w_q"]).reshape(B, T, MEM_HEADS, MEM_HD), cp["g_q"])
    s = jnp.einsum("bthd,bmhd->bhtm", q, mem_kv[:, :, 0]).astype(jnp.float32) * (MEM_HD ** -0.5)
    p = jax.nn.softmax(s, axis=-1).astype(xn.dtype)
    o = jnp.einsum("bhtm,bmhd->bthd", p, mem_kv[:, :, 1]).reshape(B, T, MEM_W)
    return o @ cp["w_o"]


def peer_ffn(xn, pp):
    B, T, D = xn.shape
    n = B * T
    nc = -(-n // PEER_CHUNK)
    xc = jnp.pad(xn.reshape(n, D), ((0, nc * PEER_CHUNK - n), (0, 0))).reshape(nc, PEER_CHUNK, D)
    half = PEER_DKEY // 2

    def chunk(xb):
        q = (xb @ pp["w_q"]).reshape(PEER_CHUNK, PEER_HEADS, PEER_DKEY)
        s1 = jnp.einsum("thd,nd->thn", q[..., :half], pp["subkeys"][0]).astype(jnp.float32)
        s2 = jnp.einsum("thd,nd->thn", q[..., half:], pp["subkeys"][1]).astype(jnp.float32)
        v1, i1 = lax.top_k(s1, PEER_TOPK)
        v2, i2 = lax.top_k(s2, PEER_TOPK)
        cand = (v1[..., :, None] + v2[..., None, :]).reshape(PEER_CHUNK, PEER_HEADS, PEER_TOPK * PEER_TOPK)
        cand_id = (i1[..., :, None] * PEER_NKEYS + i2[..., None, :]).reshape(PEER_CHUNK, PEER_HEADS, PEER_TOPK * PEER_TOPK)
        top_s, pos = lax.top_k(cand, PEER_TOPK)
        experts = jnp.take_along_axis(cand_id, pos, axis=-1)
        gate = jax.nn.softmax(top_s, axis=-1)
        act = jax.nn.gelu(jnp.einsum("thkd,td->thk", pp["u"][experts], xb).astype(jnp.float32))
        w = (gate * act).astype(xb.dtype)
        return jnp.einsum("thk,thkd->td", w, pp["v"][experts])

    y = lax.map(chunk, xc).reshape(nc * PEER_CHUNK, D)[:n]
    return y.reshape(B, T, D)


def setup_inputs(seed: int = 0) -> dict:
    key = jax.random.key(seed)
    ks = iter(jax.random.split(key, 64))
    f32 = jnp.float32

    def normal(shape, scale=1.0):
        return jax.random.normal(next(ks), shape, f32) * scale

    def gain(shape):
        return 1.0 + 0.1 * normal(shape)

    n_pages = PAST_LEN // PAGE_SIZE
    n_used = DEC_BATCH * n_pages
    n_pool = (5 * n_used + 3) // 4
    return {
        "x_prompt": normal((BATCH, SEQ, D_MODEL)),
        "x_sample": normal((DEC_BATCH, DEC_SEQ, D_MODEL)),
        "mem_prompt": normal((BATCH, N_MEM, D_MODEL)),
        "cache_nsa_cmp": normal((N_EVEN, n_pool, PAGE_SIZE, 2, NSA_DK)),
        "cache_nsa_slc": normal((N_EVEN, n_pool, PAGE_SIZE, 2, NSA_DK)),
        "cache_nsa_win": normal((N_EVEN, DEC_BATCH, min(NSA_WIN, PAST_LEN), 2, NSA_DK)),
        "cache_mla": normal((N_EVEN, n_pool, PAGE_SIZE, MLA_ROW)),
        "cache_swa": normal((N_ODD, DEC_BATCH, min(SWA_WIN, PAST_LEN), 2, SWA_KV_HEADS, SWA_HD)),
        "cache_mem": normal((DEPTH, DEC_BATCH, N_MEM, 2, MEM_HEADS, MEM_HD)),
        "page_table": jax.random.permutation(next(ks), n_pool)[:n_used].reshape(DEC_BATCH, n_pages).astype(jnp.int32),
        "norm_mix": gain((DEPTH, D_MODEL)),
        "norm_cross": gain((DEPTH, D_MODEL)),
        "norm_ffn": gain((DEPTH, D_MODEL)),
        "even_w_in": normal((N_EVEN, D_MODEL, EVEN_IN), D_MODEL ** -0.5),
        "even_w_out": normal((N_EVEN, EVEN_OUT, D_MODEL), EVEN_OUT ** -0.5),
        "nsa_g_q": gain((N_EVEN, NSA_DK)),
        "nsa_g_k": gain((N_EVEN, 3, NSA_DK)),
        "nsa_cmp_pos": normal((N_EVEN, NSA_BLK, 2, NSA_DK), 0.1),
        "nsa_cmp_w1": normal((N_EVEN, NSA_BLK, 2, NSA_DK, NSA_CMP_HID), (NSA_BLK * NSA_DK) ** -0.5),
        "nsa_cmp_w2": normal((N_EVEN, 2, NSA_CMP_HID, NSA_DK), NSA_CMP_HID ** -0.5),
        "mla_g_qn": gain((N_EVEN, MLA_NOPE)),
        "mla_g_qpe": gain((N_EVEN, MLA_ROPE)),
        "mla_g_lat": gain((N_EVEN, MLA_LORA)),
        "mla_g_kpe": gain((N_EVEN, MLA_ROPE)),
        "mla_g_kn": gain((N_EVEN, MLA_NOPE)),
        "mla_w_uk": normal((N_EVEN, MLA_LORA, MLA_HEADS, MLA_NOPE), MLA_LORA ** -0.5),
        "mla_w_uv": normal((N_EVEN, MLA_LORA, MLA_HEADS, MLA_V), MLA_LORA ** -0.5),
        "odd_w_in": normal((N_ODD, D_MODEL, ODD_IN), D_MODEL ** -0.5),
        "odd_w_out": normal((N_ODD, ODD_OUT, D_MODEL), ODD_OUT ** -0.5),
        "swa_g_q": gain((N_ODD, SWA_HD)),
        "swa_g_k": gain((N_ODD, SWA_HD)),
        "swa_sinks": normal((N_ODD, SWA_HEADS), 0.5),
        "mem_g": gain((DEPTH, D_MODEL)),
        "mem_w_q": normal((DEPTH, D_MODEL, MEM_W), D_MODEL ** -0.5),
        "mem_w_k": normal((DEPTH, D_MODEL, MEM_W), D_MODEL ** -0.5),
        "mem_w_v": normal((DEPTH, D_MODEL, MEM_W), D_MODEL ** -0.5),
        "mem_g_q": gain((DEPTH, MEM_HD)),
        "mem_g_k": gain((DEPTH, MEM_HD)),
        "mem_w_o": normal((DEPTH, MEM_W, D_MODEL), MEM_W ** -0.5),
        "peer_w_q": normal((DEPTH, D_MODEL, PEER_HEADS * PEER_DKEY), D_MODEL ** -0.5),
        "peer_subkeys": normal((DEPTH, 2, PEER_NKEYS, PEER_DKEY // 2), (PEER_DKEY // 2) ** -0.5),
        "peer_u": normal((DEPTH, PEER_N, D_MODEL), D_MODEL ** -0.5),
        "peer_v": normal((DEPTH, PEER_N, D_MODEL), PEER_HEADS ** -0.5),
    }


def reference(x_prompt, x_sample, mem_prompt, cache_nsa_cmp, cache_nsa_slc, cache_nsa_win, cache_mla, cache_swa,
              cache_mem, page_table, norm_mix, norm_cross, norm_ffn, even_w_in, even_w_out, nsa_g_q, nsa_g_k,
              nsa_cmp_pos, nsa_cmp_w1, nsa_cmp_w2, mla_g_qn, mla_g_qpe, mla_g_lat, mla_g_kpe, mla_g_kn, mla_w_uk,
              mla_w_uv, odd_w_in, odd_w_out, swa_g_q, swa_g_k, swa_sinks, mem_g, mem_w_q, mem_w_k, mem_w_v,
              mem_g_q, mem_g_k, mem_w_o, peer_w_q, peer_subkeys, peer_u, peer_v):
    slopes_nsa = alibi_slopes(NSA_HEADS)
    slopes_swa = alibi_slopes(SWA_HEADS)
    past_len = page_table.shape[1] * PAGE_SIZE
    T, Tn = x_prompt.shape[1], x_sample.shape[1]
    pos_p = jnp.arange(T)
    pos_s = past_len + jnp.arange(Tn)
    xp, xs = x_prompt, x_sample
    st_cmp_p, st_cmp_s, st_slc_p, st_slc_s, st_win_p, st_win_s = [], [], [], [], [], []
    st_mla_p, st_mla_s, st_swa_p, st_swa_s, st_mem_p = [], [], [], [], []

    for li in range(DEPTH):
        if li % 2 == 0:
            e = li // 2
            ep = dict(w_in=even_w_in[e], g_q=nsa_g_q[e], g_k=nsa_g_k[e], cmp_pos=nsa_cmp_pos[e],
                      cmp_w1=nsa_cmp_w1[e], cmp_w2=nsa_cmp_w2[e], g_qn=mla_g_qn[e], g_qpe=mla_g_qpe[e],
                      g_lat=mla_g_lat[e], g_kpe=mla_g_kpe[e], g_kn=mla_g_kn[e], w_uk=mla_w_uk[e], w_uv=mla_w_uv[e])
            q, cmp_kv, slc_kv, win_kv, gates, qn, qp, mla_row = even_project(rmsnorm(xp, norm_mix[li]), pos_p, ep)
            o_nsa = nsa_prompt(q, cmp_kv, slc_kv, win_kv, gates, ep, slopes_nsa)
            o_mla = mla_prompt(qn, qp, mla_row, ep)
            xp = xp + even_output(o_nsa, o_mla, even_w_out[e])
            st_cmp_p.append(cmp_kv)
            st_slc_p.append(slc_kv)
            st_win_p.append(win_kv[:, -min(NSA_WIN, T):])
            st_mla_p.append(mla_row)
            win_buf = cache_nsa_win[e]
            q, cmp_kv, slc_kv, win_kv, gates, qn, qp, mla_row = even_project(rmsnorm(xs, norm_mix[li]), pos_s, ep)
            o_nsa = nsa_sample(q, pos_s, cmp_kv, slc_kv, win_kv, gates, cache_nsa_cmp, cache_nsa_slc, win_buf,
                               page_table, e, ep, slopes_nsa)
            o_mla = mla_sample(qn, qp, mla_row, cache_mla, page_table, e, ep)
            xs = xs + even_output(o_nsa, o_mla, even_w_out[e])
            st_cmp_s.append(cmp_kv)
            st_slc_s.append(slc_kv)
            st_win_s.append(jnp.concatenate([win_buf, win_kv], axis=1)[:, -win_buf.shape[1]:])
            st_mla_s.append(mla_row)
        else:
            o = li // 2
            op = dict(w_in=odd_w_in[o], g_q=swa_g_q[o], g_k=swa_g_k[o])
            B = xp.shape[0]
            q, kv = odd_project(rmsnorm(xp, norm_mix[li]), op)
            att = banded_prompt(q, kv[:, :, 0], kv[:, :, 1], SWA_WIN, slopes_swa, swa_sinks[o])
            xp = xp + att.reshape(B, T, ODD_OUT) @ odd_w_out[o]
            st_swa_p.append(kv[:, -min(SWA_WIN, T):])
            buf = cache_swa[o]
            q, kv = odd_project(rmsnorm(xs, norm_mix[li]), op)
            att = window_sample(q, buf[:, :, 0], buf[:, :, 1], kv[:, :, 0], kv[:, :, 1], past_len, SWA_WIN,
                                slopes_swa, swa_sinks[o])
            xs = xs + att.reshape(xs.shape[0], Tn, ODD_OUT) @ odd_w_out[o]
            st_swa_s.append(jnp.concatenate([buf, kv], axis=1)[:, -buf.shape[1]:])

        cp = dict(g_mem=mem_g[li], w_q=mem_w_q[li], w_k=mem_w_k[li], w_v=mem_w_v[li], g_q=mem_g_q[li],
                  g_k=mem_g_k[li], w_o=mem_w_o[li])
        mem_kv_p = memory_kv(mem_prompt, cp)
        xp = xp + memory_cross(rmsnorm(xp, norm_cross[li]), mem_kv_p, cp)
        xs = xs + memory_cross(rmsnorm(xs, norm_cross[li]), cache_mem[li], cp)
        st_mem_p.append(mem_kv_p)

        pp = dict(w_q=peer_w_q[li], subkeys=peer_subkeys[li], u=peer_u[li], v=peer_v[li])
        xp = xp + peer_ffn(rmsnorm(xp, norm_ffn[li]), pp)
        xs = xs + peer_ffn(rmsnorm(xs, norm_ffn[li]), pp)

    return (xp, xs, jnp.stack(st_cmp_p), jnp.stack(st_cmp_s), jnp.stack(st_slc_p), jnp.stack(st_slc_s),
            jnp.stack(st_win_p), jnp.stack(st_win_s), jnp.stack(st_mla_p), jnp.stack(st_mla_s),
            jnp.stack(st_swa_p), jnp.stack(st_swa_s), jnp.stack(st_mem_p))
```

```python
import functools

import jax
import jax.numpy as jnp
import numpy as np
from jax import lax
from jax.experimental import pallas as pl
from jax.experimental.pallas import tpu as pltpu

PAGE_SIZE = 128
Q_BLOCK = 128
EPS = 1e-6
NEG_INF = -1e30
ROPE_BASE = 10000.0

NSA_HEADS = 8
NSA_DK = 64
NSA_BLK = 64
NSA_TOPN = 16
NSA_WIN = 512
NSA_CMP_HID = 256
MLA_HEADS = 8
MLA_NOPE = 64
MLA_ROPE = 32
MLA_V = 64
MLA_LORA = 128
MLA_ROW = MLA_LORA + MLA_ROPE
SWA_HEADS = 16
SWA_KV_HEADS = 4
SWA_HD = 64
SWA_WIN = 128
N_MEM = 256
MEM_HEADS = 4
MEM_HD = 128
MEM_W = MEM_HEADS * MEM_HD
PEER_HEADS = 8
PEER_NKEYS = 128
PEER_N = PEER_NKEYS * PEER_NKEYS
PEER_DKEY = 128
PEER_TOPK = 16
PEER_CHUNK = 256

NSA_Q_COLS = NSA_HEADS * NSA_DK
NSA_KV_COLS = 3 * 2 * NSA_DK
NSA_GATE_COLS = NSA_HEADS * 3
MLA_Q_COLS = MLA_HEADS * (MLA_NOPE + MLA_ROPE)
EVEN_IN = NSA_Q_COLS + NSA_KV_COLS + NSA_GATE_COLS + MLA_Q_COLS + MLA_ROW
EVEN_OUT = NSA_HEADS * NSA_DK + MLA_HEADS * MLA_V
ODD_IN = SWA_HEADS * SWA_HD + 2 * SWA_KV_HEADS * SWA_HD
ODD_OUT = SWA_HEADS * SWA_HD

VMEM_LIMIT_BYTES = 48 * 1024 * 1024


def _row_tile(n, target=512):
    t = min(n, target)
    while n % t:
        t //= 2
    return t


def _norm_matmul_kernel(x_ref, g_ref, w_ref, o_ref):
    x = x_ref[...]
    y = x * lax.rsqrt(jnp.mean(x * x, axis=-1, keepdims=True) + EPS) * g_ref[...]
    o_ref[...] = jnp.dot(y.astype(jnp.bfloat16), w_ref[...], preferred_element_type=jnp.float32)


def norm_matmul(x, g, w):
    n, d = x.shape
    c = w.shape[1]
    tm = _row_tile(n)
    return pl.pallas_call(
        _norm_matmul_kernel,
        grid=(n // tm,),
        in_specs=[pl.BlockSpec((tm, d), lambda i: (i, 0)),
                  pl.BlockSpec((1, d), lambda i: (0, 0)),
                  pl.BlockSpec((d, c), lambda i: (0, 0))],
        out_specs=pl.BlockSpec((tm, c), lambda i: (i, 0)),
        out_shape=jax.ShapeDtypeStruct((n, c), jnp.float32),
        compiler_params=pltpu.CompilerParams(dimension_semantics=("arbitrary",),
                                             vmem_limit_bytes=VMEM_LIMIT_BYTES),
        name="norm_matmul",
    )(x, g.reshape(1, d), w.astype(jnp.bfloat16))


def _matmul_res_kernel(a_ref, w_ref, r_ref, o_ref):
    o_ref[...] = r_ref[...] + jnp.dot(a_ref[...].astype(jnp.bfloat16), w_ref[...],
                                      preferred_element_type=jnp.float32)


def matmul_res(a, w, res):
    n, k = a.shape
    c = w.shape[1]
    tm = _row_tile(n)
    return pl.pallas_call(
        _matmul_res_kernel,
        grid=(n // tm,),
        in_specs=[pl.BlockSpec((tm, k), lambda i: (i, 0)),
                  pl.BlockSpec((k, c), lambda i: (0, 0)),
                  pl.BlockSpec((tm, c), lambda i: (i, 0))],
        out_specs=pl.BlockSpec((tm, c), lambda i: (i, 0)),
        out_shape=jax.ShapeDtypeStruct((n, c), jnp.float32),
        compiler_params=pltpu.CompilerParams(dimension_semantics=("arbitrary",),
                                             vmem_limit_bytes=VMEM_LIMIT_BYTES),
        name="matmul_res",
    )(a, w.astype(jnp.bfloat16), res)


def rmsnorm(x, g):
    xf = x.astype(jnp.float32)
    y = xf * lax.rsqrt(jnp.mean(xf * xf, axis=-1, keepdims=True) + EPS)
    return (y * g.astype(jnp.float32)).astype(x.dtype)


def alibi_slopes(n_heads):
    return jnp.asarray((2.0 ** (-8.0 * np.arange(1, n_heads + 1) / n_heads)).astype(np.float32))


def rope(x, pos):
    d = x.shape[-1]
    inv = jnp.asarray(np.power(ROPE_BASE, -np.arange(0, d, 2, dtype=np.float32) / d).astype(np.float32))
    ang = pos.astype(jnp.float32)[:, None] * inv[None, :]
    cos = jnp.cos(ang)[None, :, None, :]
    sin = jnp.sin(ang)[None, :, None, :]
    xf = x.astype(jnp.float32)
    x1, x2 = xf[..., : d // 2], xf[..., d // 2:]
    return jnp.concatenate([x1 * cos - x2 * sin, x1 * sin + x2 * cos], axis=-1).astype(x.dtype)


def masked_softmax(s, valid, sink=None):
    s = jnp.where(valid, s, NEG_INF)
    m = jnp.max(s, axis=-1, keepdims=True)
    if sink is not None:
        sk = sink.astype(jnp.float32)[:, None, None]
        m = jnp.maximum(m, sk)
    e = jnp.where(valid, jnp.exp(s - m), 0.0)
    den = jnp.sum(e, axis=-1, keepdims=True)
    if sink is not None:
        den = den + jnp.exp(sk - m)
    return e / jnp.maximum(den, 1e-30)


def attend(q, k, v, dist, valid, slopes, sink=None):
    B, Q, H, d = q.shape
    G = k.shape[-2]
    qg = q.reshape(B, Q, G, H // G, d)
    if k.ndim == 5:
        s = jnp.einsum("bqgrd,bqkgd->bgrqk", qg, k)
    else:
        s = jnp.einsum("bqgrd,bkgd->bgrqk", qg, k)
    K = s.shape[-1]
    s = s.reshape(B, H, Q, K).astype(jnp.float32) * (d ** -0.5)
    if slopes is not None:
        s = s - slopes[:, None, None] * dist[..., None, :, :].astype(jnp.float32)
    p = masked_softmax(s, valid[..., None, :, :], sink)
    pg = p.reshape(B, G, H // G, Q, K).astype(v.dtype)
    if v.ndim == 5:
        o = jnp.einsum("bgrqk,bqkgd->bqgrd", pg, v)
    else:
        o = jnp.einsum("bgrqk,bkgd->bqgrd", pg, v)
    return o.reshape(B, Q, H, v.shape[-1]), p


def banded_prompt(q, k, v, window, slopes, sink=None):
    B, T, H, d = q.shape
    pad = ((0, 0), (window, 0), (0, 0), (0, 0))
    kp = jnp.pad(k, pad)
    vp = jnp.pad(v, pad)
    span = Q_BLOCK + window

    def block(i):
        s0 = i * Q_BLOCK
        qb = lax.dynamic_slice_in_dim(q, s0, Q_BLOCK, axis=1)
        kb = lax.dynamic_slice_in_dim(kp, s0, span, axis=1)
        vb = lax.dynamic_slice_in_dim(vp, s0, span, axis=1)
        qpos = s0 + jnp.arange(Q_BLOCK)
        kpos = s0 - window + jnp.arange(span)
        dist = qpos[:, None] - kpos[None, :]
        valid = (dist >= 0) & (dist <= window) & (kpos[None, :] >= 0)
        return attend(qb, kb, vb, dist, valid, slopes, sink)[0]

    o = lax.map(block, jnp.arange(T // Q_BLOCK))
    return o.transpose(1, 0, 2, 3, 4).reshape(B, T, H, -1)


def window_sample(q, k_buf, v_buf, k_new, v_new, past_len, window, slopes, sink=None):
    Wb, Tn = k_buf.shape[1], q.shape[1]
    k = jnp.concatenate([k_buf, k_new], axis=1)
    v = jnp.concatenate([v_buf, v_new], axis=1)
    qpos = past_len + jnp.arange(Tn)
    kpos = past_len - Wb + jnp.arange(Wb + Tn)
    dist = qpos[:, None] - kpos[None, :]
    valid = (dist >= 0) & (dist <= window)
    return attend(q, k, v, dist, valid, slopes, sink)[0]


def nsa_compress(blocks, pos_emb, w1, w2):
    x = blocks + pos_emb
    h = jax.nn.gelu(jnp.einsum("...lcd,lcdh->...ch", x, w1))
    return jnp.einsum("...ch,chd->...cd", h, w2)


def nsa_select(p_cmp, qpos, n_blocks):
    imp = jnp.sum(p_cmp, axis=1)
    blk = jnp.arange(n_blocks)[None, :]
    cur = (qpos // NSA_BLK)[:, None]
    imp = jnp.where(blk == cur, float(NSA_HEADS + 1), jnp.where(blk < cur, imp, -1.0))
    _, idx = lax.top_k(imp, min(NSA_TOPN, n_blocks))
    return idx


def nsa_compressed_branch(q, qpos, kc, vc, c_end, slopes):
    dist = qpos[:, None] - c_end[None, :]
    o, p = attend(q, kc[:, :, None, :], vc[:, :, None, :], dist, dist >= 0, slopes)
    return o, nsa_select(p, qpos, kc.shape[1])


def nsa_selected_branch(q, qpos, sel, idx, slopes):
    B, Q, n = idx.shape
    kpos = (idx[..., None] * NSA_BLK + jnp.arange(NSA_BLK)).reshape(B, Q, n * NSA_BLK)
    dist = qpos[None, :, None] - kpos
    ks = sel[..., 0, :].reshape(B, Q, n * NSA_BLK, 1, NSA_DK)
    vs = sel[..., 1, :].reshape(B, Q, n * NSA_BLK, 1, NSA_DK)
    return attend(q, ks, vs, dist, dist >= 0, slopes)[0]


def nsa_gate(gates, o_cmp, o_slc, o_win):
    g = gates.astype(o_cmp.dtype)
    return g[..., 0:1] * o_cmp + g[..., 1:2] * o_slc + g[..., 2:3] * o_win


def nsa_prompt(q, cmp_kv, slc_kv, win_kv, gates, ep, slopes):
    B, T = q.shape[:2]
    nb = T // NSA_BLK
    comp = nsa_compress(cmp_kv.reshape(B, nb, NSA_BLK, 2, NSA_DK), ep["cmp_pos"], ep["cmp_w1"], ep["cmp_w2"])
    kc = rmsnorm(comp[:, :, 0], ep["g_k"][0])
    vc = comp[:, :, 1]
    c_end = (jnp.arange(nb) + 1) * NSA_BLK - 1
    slc_blocks = slc_kv.reshape(B, nb, NSA_BLK, 2, NSA_DK)

    def block(i):
        s0 = i * Q_BLOCK
        qb = lax.dynamic_slice_in_dim(q, s0, Q_BLOCK, axis=1)
        qpos = s0 + jnp.arange(Q_BLOCK)
        o_cmp, idx = nsa_compressed_branch(qb, qpos, kc, vc, c_end, slopes)
        sel = jax.vmap(lambda blocks, ix: blocks[ix])(slc_blocks, idx)
        return o_cmp, nsa_selected_branch(qb, qpos, sel, idx, slopes)

    o_cmp, o_slc = lax.map(block, jnp.arange(T // Q_BLOCK))
    o_cmp = o_cmp.transpose(1, 0, 2, 3, 4).reshape(B, T, NSA_HEADS, NSA_DK)
    o_slc = o_slc.transpose(1, 0, 2, 3, 4).reshape(B, T, NSA_HEADS, NSA_DK)
    o_win = banded_prompt(q, win_kv[:, :, 0:1], win_kv[:, :, 1:2], NSA_WIN, slopes)
    return nsa_gate(gates, o_cmp, o_slc, o_win)


def nsa_sample(q, qpos, cmp_new, slc_new, win_new, gates, cache_cmp, cache_slc, win_buf, page_table, e, ep, slopes):
    DB, Tn = q.shape[:2]
    n_pages = page_table.shape[1]
    bpp = PAGE_SIZE // NSA_BLK
    nb_past = n_pages * bpp
    past_len = n_pages * PAGE_SIZE
    nb_new = -(-Tn // NSA_BLK)
    pad = ((0, 0), (0, nb_new * NSA_BLK - Tn), (0, 0), (0, 0))

    def compress(rows):
        return nsa_compress(rows, ep["cmp_pos"], ep["cmp_w1"], ep["cmp_w2"])

    comp_past = lax.map(lambda pages: compress(cache_cmp[e, pages].reshape(DB, bpp, NSA_BLK, 2, NSA_DK)), page_table.T)
    comp_past = comp_past.transpose(1, 0, 2, 3, 4).reshape(DB, nb_past, 2, NSA_DK)
    comp_new = compress(jnp.pad(cmp_new, pad).reshape(DB, nb_new, NSA_BLK, 2, NSA_DK))
    comp = jnp.concatenate([comp_past, comp_new], axis=1)
    kc = rmsnorm(comp[:, :, 0], ep["g_k"][0])
    vc = comp[:, :, 1]
    c_end = (jnp.arange(nb_past + nb_new) + 1) * NSA_BLK - 1
    o_cmp, idx = nsa_compressed_branch(q, qpos, kc, vc, c_end, slopes)

    jp = jnp.minimum(idx, nb_past - 1)
    phys_page = jax.vmap(lambda pt, j: pt[j])(page_table, jp // bpp)
    rows = (jp % bpp)[..., None] * NSA_BLK + jnp.arange(NSA_BLK)
    past_sel = cache_slc[e, phys_page[..., None], rows]
    slc_new_b = jnp.pad(slc_new, pad).reshape(DB, nb_new, NSA_BLK, 2, NSA_DK)
    jn = jnp.clip(idx - nb_past, 0, nb_new - 1)
    new_sel = jax.vmap(lambda blocks, ix: blocks[ix])(slc_new_b, jn)
    sel = jnp.where((idx < nb_past)[..., None, None, None], past_sel, new_sel)
    o_slc = nsa_selected_branch(q, qpos, sel, idx, slopes)

    o_win = window_sample(q, win_buf[:, :, 0:1], win_buf[:, :, 1:2], win_new[:, :, 0:1], win_new[:, :, 1:2],
                          past_len, NSA_WIN, slopes)
    return nsa_gate(gates, o_cmp, o_slc, o_win)


def mla_keys(lat, ep):
    c = lat[..., :MLA_LORA]
    kp = lat[..., MLA_LORA:]
    kn = rmsnorm(jnp.einsum("bkc,chd->bkhd", c, ep["w_uk"]), ep["g_kn"])
    v = jnp.einsum("bkc,chd->bkhd", c, ep["w_uv"])
    return kn, kp, v


def mla_scores(qn, qp, kn, kp):
    s = jnp.einsum("bqhd,bkhd->bhqk", qn, kn) + jnp.einsum("bqhd,bkd->bhqk", qp, kp)
    return s.astype(jnp.float32) * ((MLA_NOPE + MLA_ROPE) ** -0.5)


def mla_prompt(qn, qp, lat, ep):
    B, T = qn.shape[:2]
    kn, kp, v = mla_keys(lat, ep)
    kpos = jnp.arange(T)

    def block(i):
        s0 = i * Q_BLOCK
        qnb = lax.dynamic_slice_in_dim(qn, s0, Q_BLOCK, axis=1)
        qpb = lax.dynamic_slice_in_dim(qp, s0, Q_BLOCK, axis=1)
        qpos = s0 + jnp.arange(Q_BLOCK)
        p = masked_softmax(mla_scores(qnb, qpb, kn, kp), qpos[:, None] >= kpos[None, :])
        return jnp.einsum("bhqk,bkhd->bqhd", p.astype(v.dtype), v)

    o = lax.map(block, jnp.arange(T // Q_BLOCK))
    return o.transpose(1, 0, 2, 3, 4).reshape(B, T, MLA_HEADS, MLA_V)


def mla_sample(qn, qp, lat_new, cache_mla, page_table, e, ep):
    DB, Tn = qn.shape[:2]

    def update(carry, kn, kp, v, valid):
        m, l, acc = carry
        s = jnp.where(valid, mla_scores(qn, qp, kn, kp), NEG_INF)
        m_new = jnp.maximum(m, jnp.max(s, axis=-1))
        corr = jnp.exp(m - m_new)
        p = jnp.where(valid, jnp.exp(s - m_new[..., None]), 0.0)
        l = l * corr + jnp.sum(p, axis=-1)
        acc = acc * corr[..., None] + jnp.einsum("bhqk,bkhd->bhqd", p.astype(v.dtype), v).astype(jnp.float32)
        return (m_new, l, acc)

    def step(carry, pages):
        kn, kp, v = mla_keys(cache_mla[e, pages], ep)
        return update(carry, kn, kp, v, True), None

    init = (jnp.full((DB, MLA_HEADS, Tn), NEG_INF, jnp.float32),
            jnp.zeros((DB, MLA_HEADS, Tn), jnp.float32),
            jnp.zeros((DB, MLA_HEADS, Tn, MLA_V), jnp.float32))
    carry, _ = lax.scan(step, init, page_table.T)
    kn, kp, v = mla_keys(lat_new, ep)
    ar = jnp.arange(Tn)
    m, l, acc = update(carry, kn, kp, v, ar[:, None] >= ar[None, :])
    return (acc / l[..., None]).astype(qn.dtype).transpose(0, 2, 1, 3)


def even_project(x, g_norm, pos, ep):
    B, T, D = x.shape
    h = norm_matmul(x.reshape(B * T, D), g_norm, ep["w_in"]).reshape(B, T, EVEN_IN)
    cuts = np.cumsum([NSA_Q_COLS, NSA_KV_COLS, NSA_GATE_COLS, MLA_Q_COLS]).tolist()
    q, kv, g, mq, lat = jnp.split(h, cuts, axis=-1)
    q = rmsnorm(q.reshape(B, T, NSA_HEADS, NSA_DK), ep["g_q"])
    kv = kv.reshape(B, T, 3, 2, NSA_DK)
    cmp_kv = kv[:, :, 0]
    slc_kv = jnp.stack([rmsnorm(kv[:, :, 1, 0], ep["g_k"][1]), kv[:, :, 1, 1]], axis=2)
    win_kv = jnp.stack([rmsnorm(kv[:, :, 2, 0], ep["g_k"][2]), kv[:, :, 2, 1]], axis=2)
    gates = jax.nn.sigmoid(g.astype(jnp.float32)).reshape(B, T, NSA_HEADS, 3)
    mq = mq.reshape(B, T, MLA_HEADS, MLA_NOPE + MLA_ROPE)
    qn = rmsnorm(mq[..., :MLA_NOPE], ep["g_qn"])
    qp = rope(rmsnorm(mq[..., MLA_NOPE:], ep["g_qpe"]), pos)
    c = rmsnorm(lat[..., :MLA_LORA], ep["g_lat"])
    kp = rope(rmsnorm(lat[..., MLA_LORA:], ep["g_kpe"])[:, :, None, :], pos)[:, :, 0, :]
    return q, cmp_kv, slc_kv, win_kv, gates, qn, qp, jnp.concatenate([c, kp], axis=-1)


def even_output(x, o_nsa, o_mla, w_out):
    B, T, D = x.shape
    a = jnp.concatenate([o_nsa.reshape(B, T, -1), o_mla.reshape(B, T, -1)], axis=-1)
    return matmul_res(a.reshape(B * T, EVEN_OUT), w_out, x.reshape(B * T, D)).reshape(B, T, D)


def odd_project(x, g_norm, op):
    B, T, D = x.shape
    h = norm_matmul(x.reshape(B * T, D), g_norm, op["w_in"]).reshape(B, T, ODD_IN)
    q, k, v = jnp.split(h, [SWA_HEADS * SWA_HD, SWA_HEADS * SWA_HD + SWA_KV_HEADS * SWA_HD], axis=-1)
    q = rmsnorm(q.reshape(B, T, SWA_HEADS, SWA_HD), op["g_q"])
    k = rmsnorm(k.reshape(B, T, SWA_KV_HEADS, SWA_HD), op["g_k"])
    v = v.reshape(B, T, SWA_KV_HEADS, SWA_HD)
    return q, jnp.stack([k, v], axis=2)


def memory_kv(mem, cp):
    B, M, D = mem.shape
    w_kv = jnp.concatenate([cp["w_k"], cp["w_v"]], axis=1)
    kv = norm_matmul(mem.reshape(B * M, D), cp["g_mem"], w_kv).reshape(B, M, 2, MEM_HEADS, MEM_HD)
    k = rmsnorm(kv[:, :, 0], cp["g_k"])
    return jnp.stack([k, kv[:, :, 1]], axis=2)


def memory_cross(x, g_norm, mem_kv, cp):
    B, T, D = x.shape
    q = norm_matmul(x.reshape(B * T, D), g_norm, cp["w_q"]).reshape(B, T, MEM_HEADS, MEM_HD)
    q = rmsnorm(q, cp["g_q"])
    s = jnp.einsum("bthd,bmhd->bhtm", q, mem_kv[:, :, 0]).astype(jnp.float32) * (MEM_HD ** -0.5)
    p = jax.nn.softmax(s, axis=-1).astype(x.dtype)
    o = jnp.einsum("bhtm,bmhd->bthd", p, mem_kv[:, :, 1]).reshape(B * T, MEM_W)
    return matmul_res(o, cp["w_o"], x.reshape(B * T, D)).reshape(B, T, D)


def peer_ffn(x, g_norm, pp):
    B, T, D = x.shape
    n = B * T
    nc = -(-n // PEER_CHUNK)
    xn = rmsnorm(x, g_norm)
    xc = jnp.pad(xn.reshape(n, D), ((0, nc * PEER_CHUNK - n), (0, 0))).reshape(nc, PEER_CHUNK, D)
    qall = norm_matmul(x.reshape(n, D), g_norm, pp["w_q"])
    qc = jnp.pad(qall, ((0, nc * PEER_CHUNK - n), (0, 0))).reshape(nc, PEER_CHUNK, PEER_HEADS, PEER_DKEY)
    half = PEER_DKEY // 2

    def chunk(args):
        xb, q = args
        s1 = jnp.einsum("thd,nd->thn", q[..., :half], pp["subkeys"][0]).astype(jnp.float32)
        s2 = jnp.einsum("thd,nd->thn", q[..., half:], pp["subkeys"][1]).astype(jnp.float32)
        v1, i1 = lax.top_k(s1, PEER_TOPK)
        v2, i2 = lax.top_k(s2, PEER_TOPK)
        cand = (v1[..., :, None] + v2[..., None, :]).reshape(PEER_CHUNK, PEER_HEADS, PEER_TOPK * PEER_TOPK)
        cand_id = (i1[..., :, None] * PEER_NKEYS + i2[..., None, :]).reshape(PEER_CHUNK, PEER_HEADS, PEER_TOPK * PEER_TOPK)
        top_s, pos = lax.top_k(cand, PEER_TOPK)
        experts = jnp.take_along_axis(cand_id, pos, axis=-1)
        gate = jax.nn.softmax(top_s, axis=-1)
        act = jax.nn.gelu(jnp.einsum("thkd,td->thk", pp["u"][experts], xb).astype(jnp.float32))
        w = (gate * act).astype(xb.dtype)
        return jnp.einsum("thk,thkd->td", w, pp["v"][experts])

    y = lax.map(chunk, (xc, qc)).reshape(nc * PEER_CHUNK, D)[:n]
    return x + y.reshape(B, T, D)


def kernel(x_prompt, x_sample, mem_prompt, cache_nsa_cmp, cache_nsa_slc, cache_nsa_win, cache_mla, cache_swa,
           cache_mem, page_table, norm_mix, norm_cross, norm_ffn, even_w_in, even_w_out, nsa_g_q, nsa_g_k,
           nsa_cmp_pos, nsa_cmp_w1, nsa_cmp_w2, mla_g_qn, mla_g_qpe, mla_g_lat, mla_g_kpe, mla_g_kn, mla_w_uk,
           mla_w_uv, odd_w_in, odd_w_out, swa_g_q, swa_g_k, swa_sinks, mem_g, mem_w_q, mem_w_k, mem_w_v,
           mem_g_q, mem_g_k, mem_w_o, peer_w_q, peer_subkeys, peer_u, peer_v):
    depth = norm_mix.shape[0]
    slopes_nsa = alibi_slopes(NSA_HEADS)
    slopes_swa = alibi_slopes(SWA_HEADS)
    past_len = page_table.shape[1] * PAGE_SIZE
    T, Tn = x_prompt.shape[1], x_sample.shape[1]
    pos_p = jnp.arange(T)
    pos_s = past_len + jnp.arange(Tn)
    xp, xs = x_prompt, x_sample
    st_cmp_p, st_cmp_s, st_slc_p, st_slc_s, st_win_p, st_win_s = [], [], [], [], [], []
    st_mla_p, st_mla_s, st_swa_p, st_swa_s, st_mem_p = [], [], [], [], []

    for li in range(depth):
        if li % 2 == 0:
            e = li // 2
            ep = dict(w_in=even_w_in[e], g_q=nsa_g_q[e], g_k=nsa_g_k[e], cmp_pos=nsa_cmp_pos[e],
                      cmp_w1=nsa_cmp_w1[e], cmp_w2=nsa_cmp_w2[e], g_qn=mla_g_qn[e], g_qpe=mla_g_qpe[e],
                      g_lat=mla_g_lat[e], g_kpe=mla_g_kpe[e], g_kn=mla_g_kn[e], w_uk=mla_w_uk[e], w_uv=mla_w_uv[e])
            q, cmp_kv, slc_kv, win_kv, gates, qn, qp, mla_row = even_project(xp, norm_mix[li], pos_p, ep)
            o_nsa = nsa_prompt(q, cmp_kv, slc_kv, win_kv, gates, ep, slopes_nsa)
            o_mla = mla_prompt(qn, qp, mla_row, ep)
            xp = even_output(xp, o_nsa, o_mla, even_w_out[e])
            st_cmp_p.append(cmp_kv)
            st_slc_p.append(slc_kv)
            st_win_p.append(win_kv[:, -min(NSA_WIN, T):])
            st_mla_p.append(mla_row)
            win_buf = cache_nsa_win[e]
            q, cmp_kv, slc_kv, win_kv, gates, qn, qp, mla_row = even_project(xs, norm_mix[li], pos_s, ep)
            o_nsa = nsa_sample(q, pos_s, cmp_kv, slc_kv, win_kv, gates, cache_nsa_cmp, cache_nsa_slc, win_buf,
                               page_table, e, ep, slopes_nsa)
            o_mla = mla_sample(qn, qp, mla_row, cache_mla, page_table, e, ep)
            xs = even_output(xs, o_nsa, o_mla, even_w_out[e])
            st_cmp_s.append(cmp_kv)
            st_slc_s.append(slc_kv)
            st_win_s.append(jnp.concatenate([win_buf, win_kv], axis=1)[:, -win_buf.shape[1]:])
            st_mla_s.append(mla_row)
        else:
            o = li // 2
            op = dict(w_in=odd_w_in[o], g_q=swa_g_q[o], g_k=swa_g_k[o])
            B = xp.shape[0]
            q, kv = odd_project(xp, norm_mix[li], op)
            att = banded_prompt(q, kv[:, :, 0], kv[:, :, 1], SWA_WIN, slopes_swa, swa_sinks[o])
            xp = matmul_res(att.reshape(B * T, ODD_OUT), odd_w_out[o], xp.reshape(B * T, -1)).reshape(xp.shape)
            st_swa_p.append(kv[:, -min(SWA_WIN, T):])
            buf = cache_swa[o]
            q, kv = odd_project(xs, norm_mix[li], op)
            att = window_sample(q, buf[:, :, 0], buf[:, :, 1], kv[:, :, 0], kv[:, :, 1], past_len, SWA_WIN,
                                slopes_swa, swa_sinks[o])
            xs = matmul_res(att.reshape(xs.shape[0] * Tn, ODD_OUT), odd_w_out[o],
                            xs.reshape(xs.shape[0] * Tn, -1)).reshape(xs.shape)
            st_swa_s.append(jnp.concatenate([buf, kv], axis=1)[:, -buf.shape[1]:])

        cp = dict(g_mem=mem_g[li], w_q=mem_w_q[li], w_k=mem_w_k[li], w_v=mem_w_v[li], g_q=mem_g_q[li],
                  g_k=mem_g_k[li], w_o=mem_w_o[li])
        mem_kv_p = memory_kv(mem_prompt, cp)
        xp = memory_cross(xp, norm_cross[li], mem_kv_p, cp)
        xs = memory_cross(xs, norm_cross[li], cache_mem[li], cp)
        st_mem_p.append(mem_kv_p)

        pp = dict(w_q=peer_w_q[li], subkeys=peer_subkeys[li], u=peer_u[li], v=peer_v[li])
        xp = peer_ffn(xp, norm_ffn[li], pp)
        xs = peer_ffn(xs, norm_ffn[li], pp)

    return (xp, xs, jnp.stack(st_cmp_p), jnp.stack(st_cmp_s), jnp.stack(st_slc_p), jnp.stack(st_slc_s),
            jnp.stack(st_win_p), jnp.stack(st_win_s), jnp.stack(st_mla_p), jnp.stack(st_mla_s),
            jnp.stack(st_swa_p), jnp.stack(st_swa_s), jnp.stack(st_mem_p))
```

```python
import functools

import jax
import jax.numpy as jnp
import numpy as np
from jax import lax
from jax.experimental import pallas as pl
from jax.experimental.pallas import tpu as pltpu

PAGE_SIZE = 128
Q_BLOCK = 128
EPS = 1e-6
NEG_INF = -1e30
ROPE_BASE = 10000.0

NSA_HEADS = 8
NSA_DK = 64
NSA_BLK = 64
NSA_TOPN = 16
NSA_WIN = 512
NSA_CMP_HID = 256
MLA_HEADS = 8
MLA_NOPE = 64
MLA_ROPE = 32
MLA_V = 64
MLA_LORA = 128
MLA_ROW = MLA_LORA + MLA_ROPE
SWA_HEADS = 16
SWA_KV_HEADS = 4
SWA_HD = 64
SWA_WIN = 128
N_MEM = 256
MEM_HEADS = 4
MEM_HD = 128
MEM_W = MEM_HEADS * MEM_HD
PEER_HEADS = 8
PEER_NKEYS = 128
PEER_N = PEER_NKEYS * PEER_NKEYS
PEER_DKEY = 128
PEER_TOPK = 16
PEER_CHUNK = 256

NSA_Q_COLS = NSA_HEADS * NSA_DK
NSA_KV_COLS = 3 * 2 * NSA_DK
NSA_GATE_COLS = NSA_HEADS * 3
MLA_Q_COLS = MLA_HEADS * (MLA_NOPE + MLA_ROPE)
EVEN_IN = NSA_Q_COLS + NSA_KV_COLS + NSA_GATE_COLS + MLA_Q_COLS + MLA_ROW
EVEN_OUT = NSA_HEADS * NSA_DK + MLA_HEADS * MLA_V
ODD_IN = SWA_HEADS * SWA_HD + 2 * SWA_KV_HEADS * SWA_HD
ODD_OUT = SWA_HEADS * SWA_HD

VMEM_LIMIT_BYTES = 48 * 1024 * 1024


def _row_tile(n, target=512):
    t = min(n, target)
    while n % t:
        t //= 2
    return t


def _norm_matmul_kernel(x_ref, g_ref, w_ref, o_ref):
    x = x_ref[...]
    y = x * lax.rsqrt(jnp.mean(x * x, axis=-1, keepdims=True) + EPS) * g_ref[...]
    o_ref[...] = jnp.dot(y.astype(jnp.bfloat16), w_ref[...], preferred_element_type=jnp.float32)


def norm_matmul(x, g, w):
    n, d = x.shape
    c = w.shape[1]
    tm = _row_tile(n)
    return pl.pallas_call(
        _norm_matmul_kernel,
        grid=(n // tm,),
        in_specs=[pl.BlockSpec((tm, d), lambda i: (i, 0)),
                  pl.BlockSpec((1, d), lambda i: (0, 0)),
                  pl.BlockSpec((d, c), lambda i: (0, 0))],
        out_specs=pl.BlockSpec((tm, c), lambda i: (i, 0)),
        out_shape=jax.ShapeDtypeStruct((n, c), jnp.float32),
        compiler_params=pltpu.CompilerParams(dimension_semantics=("arbitrary",),
                                             vmem_limit_bytes=VMEM_LIMIT_BYTES),
        name="norm_matmul",
    )(x, g.reshape(1, d), w.astype(jnp.bfloat16))


def _matmul_res_kernel(a_ref, w_ref, r_ref, o_ref):
    o_ref[...] = r_ref[...] + jnp.dot(a_ref[...].astype(jnp.bfloat16), w_ref[...],
                                      preferred_element_type=jnp.float32)


def matmul_res(a, w, res):
    n, k = a.shape
    c = w.shape[1]
    tm = _row_tile(n)
    return pl.pallas_call(
        _matmul_res_kernel,
        grid=(n // tm,),
        in_specs=[pl.BlockSpec((tm, k), lambda i: (i, 0)),
                  pl.BlockSpec((k, c), lambda i: (0, 0)),
                  pl.BlockSpec((tm, c), lambda i: (i, 0))],
        out_specs=pl.BlockSpec((tm, c), lambda i: (i, 0)),
        out_shape=jax.ShapeDtypeStruct((n, c), jnp.float32),
        compiler_params=pltpu.CompilerParams(dimension_semantics=("arbitrary",),
                                             vmem_limit_bytes=VMEM_LIMIT_BYTES),
        name="matmul_res",
    )(a, w.astype(jnp.bfloat16), res)


def rmsnorm(x, g):
    xf = x.astype(jnp.float32)
    y = xf * lax.rsqrt(jnp.mean(xf * xf, axis=-1, keepdims=True) + EPS)
    return (y * g.astype(jnp.float32)).astype(x.dtype)


def alibi_slopes(n_heads):
    return jnp.asarray((2.0 ** (-8.0 * np.arange(1, n_heads + 1) / n_heads)).astype(np.float32))


def rope(x, pos):
    d = x.shape[-1]
    inv = jnp.asarray(np.power(ROPE_BASE, -np.arange(0, d, 2, dtype=np.float32) / d).astype(np.float32))
    ang = pos.astype(jnp.float32)[:, None] * inv[None, :]
    cos = jnp.cos(ang)[None, :, None, :]
    sin = jnp.sin(ang)[None, :, None, :]
    xf = x.astype(jnp.float32)
    x1, x2 = xf[..., : d // 2], xf[..., d // 2:]
    return jnp.concatenate([x1 * cos - x2 * sin, x1 * sin + x2 * cos], axis=-1).astype(x.dtype)


def masked_softmax(s, valid, sink=None):
    s = jnp.where(valid, s, NEG_INF)
    m = jnp.max(s, axis=-1, keepdims=True)
    if sink is not None:
        sk = sink.astype(jnp.float32)[:, None, None]
        m = jnp.maximum(m, sk)
    e = jnp.where(valid, jnp.exp(s - m), 0.0)
    den = jnp.sum(e, axis=-1, keepdims=True)
    if sink is not None:
        den = den + jnp.exp(sk - m)
    return e / jnp.maximum(den, 1e-30)


def attend(q, k, v, dist, valid, slopes, sink=None):
    B, Q, H, d = q.shape
    G = k.shape[-2]
    qg = q.reshape(B, Q, G, H // G, d)
    if k.ndim == 5:
        s = jnp.einsum("bqgrd,bqkgd->bgrqk", qg, k)
    else:
        s = jnp.einsum("bqgrd,bkgd->bgrqk", qg, k)
    K = s.shape[-1]
    s = s.reshape(B, H, Q, K).astype(jnp.float32) * (d ** -0.5)
    if slopes is not None:
        s = s - slopes[:, None, None] * dist[..., None, :, :].astype(jnp.float32)
    p = masked_softmax(s, valid[..., None, :, :], sink)
    pg = p.reshape(B, G, H // G, Q, K).astype(v.dtype)
    if v.ndim == 5:
        o = jnp.einsum("bgrqk,bqkgd->bqgrd", pg, v)
    else:
        o = jnp.einsum("bgrqk,bkgd->bqgrd", pg, v)
    return o.reshape(B, Q, H, v.shape[-1]), p


def banded_prompt(q, k, v, window, slopes, sink=None):
    B, T, H, d = q.shape
    pad = ((0, 0), (window, 0), (0, 0), (0, 0))
    kp = jnp.pad(k, pad)
    vp = jnp.pad(v, pad)
    span = Q_BLOCK + window

    def block(i):
        s0 = i * Q_BLOCK
        qb = lax.dynamic_slice_in_dim(q, s0, Q_BLOCK, axis=1)
        kb = lax.dynamic_slice_in_dim(kp, s0, span, axis=1)
        vb = lax.dynamic_slice_in_dim(vp, s0, span, axis=1)
        qpos = s0 + jnp.arange(Q_BLOCK)
        kpos = s0 - window + jnp.arange(span)
        dist = qpos[:, None] - kpos[None, :]
        valid = (dist >= 0) & (dist <= window) & (kpos[None, :] >= 0)
        return attend(qb, kb, vb, dist, valid, slopes, sink)[0]

    o = lax.map(block, jnp.arange(T // Q_BLOCK))
    return o.transpose(1, 0, 2, 3, 4).reshape(B, T, H, -1)


def window_sample(q, k_buf, v_buf, k_new, v_new, past_len, window, slopes, sink=None):
    Wb, Tn = k_buf.shape[1], q.shape[1]
    k = jnp.concatenate([k_buf, k_new], axis=1)
    v = jnp.concatenate([v_buf, v_new], axis=1)
    qpos = past_len + jnp.arange(Tn)
    kpos = past_len - Wb + jnp.arange(Wb + Tn)
    dist = qpos[:, None] - kpos[None, :]
    valid = (dist >= 0) & (dist <= window)
    return attend(q, k, v, dist, valid, slopes, sink)[0]


def nsa_compress(blocks, pos_emb, w1, w2):
    x = blocks + pos_emb
    h = jax.nn.gelu(jnp.einsum("...lcd,lcdh->...ch", x, w1))
    return jnp.einsum("...ch,chd->...cd", h, w2)


def nsa_select(p_cmp, qpos, n_blocks):
    imp = jnp.sum(p_cmp, axis=1)
    blk = jnp.arange(n_blocks)[None, :]
    cur = (qpos // NSA_BLK)[:, None]
    imp = jnp.where(blk == cur, float(NSA_HEADS + 1), jnp.where(blk < cur, imp, -1.0))
    _, idx = lax.top_k(imp, min(NSA_TOPN, n_blocks))
    return idx


def nsa_compressed_branch(q, qpos, kc, vc, c_end, slopes):
    dist = qpos[:, None] - c_end[None, :]
    o, p = attend(q, kc[:, :, None, :], vc[:, :, None, :], dist, dist >= 0, slopes)
    return o, nsa_select(p, qpos, kc.shape[1])


def nsa_selected_branch(q, qpos, sel, idx, slopes):
    B, Q, n = idx.shape
    kpos = (idx[..., None] * NSA_BLK + jnp.arange(NSA_BLK)).reshape(B, Q, n * NSA_BLK)
    dist = qpos[None, :, None] - kpos
    ks = sel[..., 0, :].reshape(B, Q, n * NSA_BLK, 1, NSA_DK)
    vs = sel[..., 1, :].reshape(B, Q, n * NSA_BLK, 1, NSA_DK)
    return attend(q, ks, vs, dist, dist >= 0, slopes)[0]


def nsa_gate(gates, o_cmp, o_slc, o_win):
    g = gates.astype(o_cmp.dtype)
    return g[..., 0:1] * o_cmp + g[..., 1:2] * o_slc + g[..., 2:3] * o_win


def nsa_prompt(q, cmp_kv, slc_kv, win_kv, gates, ep, slopes):
    B, T = q.shape[:2]
    nb = T // NSA_BLK
    comp = nsa_compress(cmp_kv.reshape(B, nb, NSA_BLK, 2, NSA_DK), ep["cmp_pos"], ep["cmp_w1"], ep["cmp_w2"])
    kc = rmsnorm(comp[:, :, 0], ep["g_k"][0])
    vc = comp[:, :, 1]
    c_end = (jnp.arange(nb) + 1) * NSA_BLK - 1
    slc_blocks = slc_kv.reshape(B, nb, NSA_BLK, 2, NSA_DK)

    def block(i):
        s0 = i * Q_BLOCK
        qb = lax.dynamic_slice_in_dim(q, s0, Q_BLOCK, axis=1)
        qpos = s0 + jnp.arange(Q_BLOCK)
        o_cmp, idx = nsa_compressed_branch(qb, qpos, kc, vc, c_end, slopes)
        sel = jax.vmap(lambda blocks, ix: blocks[ix])(slc_blocks, idx)
        return o_cmp, nsa_selected_branch(qb, qpos, sel, idx, slopes)

    o_cmp, o_slc = lax.map(block, jnp.arange(T // Q_BLOCK))
    o_cmp = o_cmp.transpose(1, 0, 2, 3, 4).reshape(B, T, NSA_HEADS, NSA_DK)
    o_slc = o_slc.transpose(1, 0, 2, 3, 4).reshape(B, T, NSA_HEADS, NSA_DK)
    o_win = banded_prompt(q, win_kv[:, :, 0:1], win_kv[:, :, 1:2], NSA_WIN, slopes)
    return nsa_gate(gates, o_cmp, o_slc, o_win)


def nsa_sample(q, qpos, cmp_new, slc_new, win_new, gates, cache_cmp, cache_slc, win_buf, page_table, e, ep, slopes):
    DB, Tn = q.shape[:2]
    n_pages = page_table.shape[1]
    bpp = PAGE_SIZE // NSA_BLK
    nb_past = n_pages * bpp
    past_len = n_pages * PAGE_SIZE
    nb_new = -(-Tn // NSA_BLK)
    pad = ((0, 0), (0, nb_new * NSA_BLK - Tn), (0, 0), (0, 0))

    def compress(rows):
        return nsa_compress(rows, ep["cmp_pos"], ep["cmp_w1"], ep["cmp_w2"])

    comp_past = lax.map(lambda pages: compress(cache_cmp[e, pages].reshape(DB, bpp, NSA_BLK, 2, NSA_DK)), page_table.T)
    comp_past = comp_past.transpose(1, 0, 2, 3, 4).reshape(DB, nb_past, 2, NSA_DK)
    comp_new = compress(jnp.pad(cmp_new, pad).reshape(DB, nb_new, NSA_BLK, 2, NSA_DK))
    comp = jnp.concatenate([comp_past, comp_new], axis=1)
    kc = rmsnorm(comp[:, :, 0], ep["g_k"][0])
    vc = comp[:, :, 1]
    c_end = (jnp.arange(nb_past + nb_new) + 1) * NSA_BLK - 1
    o_cmp, idx = nsa_compressed_branch(q, qpos, kc, vc, c_end, slopes)

    jp = jnp.minimum(idx, nb_past - 1)
    phys_page = jax.vmap(lambda pt, j: pt[j])(page_table, jp // bpp)
    rows = (jp % bpp)[..., None] * NSA_BLK + jnp.arange(NSA_BLK)
    past_sel = cache_slc[e, phys_page[..., None], rows]
    slc_new_b = jnp.pad(slc_new, pad).reshape(DB, nb_new, NSA_BLK, 2, NSA_DK)
    jn = jnp.clip(idx - nb_past, 0, nb_new - 1)
    new_sel = jax.vmap(lambda blocks, ix: blocks[ix])(slc_new_b, jn)
    sel = jnp.where((idx < nb_past)[..., None, None, None], past_sel, new_sel)
    o_slc = nsa_selected_branch(q, qpos, sel, idx, slopes)

    o_win = window_sample(q, win_buf[:, :, 0:1], win_buf[:, :, 1:2], win_new[:, :, 0:1], win_new[:, :, 1:2],
                          past_len, NSA_WIN, slopes)
    return nsa_gate(gates, o_cmp, o_slc, o_win)


def mla_keys(lat, ep):
    c = lat[..., :MLA_LORA]
    kp = lat[..., MLA_LORA:]
    kn = rmsnorm(jnp.einsum("bkc,chd->bkhd", c, ep["w_uk"]), ep["g_kn"])
    v = jnp.einsum("bkc,chd->bkhd", c, ep["w_uv"])
    return kn, kp, v


def mla_scores(qn, qp, kn, kp):
    s = jnp.einsum("bqhd,bkhd->bhqk", qn, kn) + jnp.einsum("bqhd,bkd->bhqk", qp, kp)
    return s.astype(jnp.float32) * ((MLA_NOPE + MLA_ROPE) ** -0.5)


def mla_prompt(qn, qp, lat, ep):
    B, T = qn.shape[:2]
    kn, kp, v = mla_keys(lat, ep)
    kpos = jnp.arange(T)

    def block(i):
        s0 = i * Q_BLOCK
        qnb = lax.dynamic_slice_in_dim(qn, s0, Q_BLOCK, axis=1)
        qpb = lax.dynamic_slice_in_dim(qp, s0, Q_BLOCK, axis=1)
        qpos = s0 + jnp.arange(Q_BLOCK)
        p = masked_softmax(mla_scores(qnb, qpb, kn, kp), qpos[:, None] >= kpos[None, :])
        return jnp.einsum("bhqk,bkhd->bqhd", p.astype(v.dtype), v)

    o = lax.map(block, jnp.arange(T // Q_BLOCK))
    return o.transpose(1, 0, 2, 3, 4).reshape(B, T, MLA_HEADS, MLA_V)


def mla_sample(qn, qp, lat_new, cache_mla, page_table, e, ep):
    DB, Tn = qn.shape[:2]

    def update(carry, kn, kp, v, valid):
        m, l, acc = carry
        s = jnp.where(valid, mla_scores(qn, qp, kn, kp), NEG_INF)
        m_new = jnp.maximum(m, jnp.max(s, axis=-1))
        corr = jnp.exp(m - m_new)
        p = jnp.where(valid, jnp.exp(s - m_new[..., None]), 0.0)
        l = l * corr + jnp.sum(p, axis=-1)
        acc = acc * corr[..., None] + jnp.einsum("bhqk,bkhd->bhqd", p.astype(v.dtype), v).astype(jnp.float32)
        return (m_new, l, acc)

    def step(carry, pages):
        kn, kp, v = mla_keys(cache_mla[e, pages], ep)
        return update(carry, kn, kp, v, True), None

    init = (jnp.full((DB, MLA_HEADS, Tn), NEG_INF, jnp.float32),
            jnp.zeros((DB, MLA_HEADS, Tn), jnp.float32),
            jnp.zeros((DB, MLA_HEADS, Tn, MLA_V), jnp.float32))
    carry, _ = lax.scan(step, init, page_table.T)
    kn, kp, v = mla_keys(lat_new, ep)
    ar = jnp.arange(Tn)
    m, l, acc = update(carry, kn, kp, v, ar[:, None] >= ar[None, :])
    return (acc / l[..., None]).astype(qn.dtype).transpose(0, 2, 1, 3)


def even_project(x, g_norm, pos, ep):
    B, T, D = x.shape
    h = norm_matmul(x.reshape(B * T, D), g_norm, ep["w_in"]).reshape(B, T, EVEN_IN)
    cuts = np.cumsum([NSA_Q_COLS, NSA_KV_COLS, NSA_GATE_COLS, MLA_Q_COLS]).tolist()
    q, kv, g, mq, lat = jnp.split(h, cuts, axis=-1)
    q = rmsnorm(q.reshape(B, T, NSA_HEADS, NSA_DK), ep["g_q"])
    kv = kv.reshape(B, T, 3, 2, NSA_DK)
    cmp_kv = kv[:, :, 0]
    slc_kv = jnp.stack([rmsnorm(kv[:, :, 1, 0], ep["g_k"][1]), kv[:, :, 1, 1]], axis=2)
    win_kv = jnp.stack([rmsnorm(kv[:, :, 2, 0], ep["g_k"][2]), kv[:, :, 2, 1]], axis=2)
    gates = jax.nn.sigmoid(g.astype(jnp.float32)).reshape(B, T, NSA_HEADS, 3)
    mq = mq.reshape(B, T, MLA_HEADS, MLA_NOPE + MLA_ROPE)
    qn = rmsnorm(mq[..., :MLA_NOPE], ep["g_qn"])
    qp = rope(rmsnorm(mq[..., MLA_NOPE:], ep["g_qpe"]), pos)
    c = rmsnorm(lat[..., :MLA_LORA], ep["g_lat"])
    kp = rope(rmsnorm(lat[..., MLA_LORA:], ep["g_kpe"])[:, :, None, :], pos)[:, :, 0, :]
    return q, cmp_kv, slc_kv, win_kv, gates, qn, qp, jnp.concatenate([c, kp], axis=-1)


def even_output(x, o_nsa, o_mla, w_out):
    B, T, D = x.shape
    a = jnp.concatenate([o_nsa.reshape(B, T, -1), o_mla.reshape(B, T, -1)], axis=-1)
    return matmul_res(a.reshape(B * T, EVEN_OUT), w_out, x.reshape(B * T, D)).reshape(B, T, D)


def odd_project(x, g_norm, op):
    B, T, D = x.shape
    h = norm_matmul(x.reshape(B * T, D), g_norm, op["w_in"]).reshape(B, T, ODD_IN)
    q, k, v = jnp.split(h, [SWA_HEADS * SWA_HD, SWA_HEADS * SWA_HD + SWA_KV_HEADS * SWA_HD], axis=-1)
    q = rmsnorm(q.reshape(B, T, SWA_HEADS, SWA_HD), op["g_q"])
    k = rmsnorm(k.reshape(B, T, SWA_KV_HEADS, SWA_HD), op["g_k"])
    v = v.reshape(B, T, SWA_KV_HEADS, SWA_HD)
    return q, jnp.stack([k, v], axis=2)


def memory_kv(mem, cp):
    B, M, D = mem.shape
    w_kv = jnp.concatenate([cp["w_k"], cp["w_v"]], axis=1)
    kv = norm_matmul(mem.reshape(B * M, D), cp["g_mem"], w_kv).reshape(B, M, 2, MEM_HEADS, MEM_HD)
    k = rmsnorm(kv[:, :, 0], cp["g_k"])
    return jnp.stack([k, kv[:, :, 1]], axis=2)


def memory_cross(x, g_norm, mem_kv, cp):
    B, T, D = x.shape
    q = norm_matmul(x.reshape(B * T, D), g_norm, cp["w_q"]).reshape(B, T, MEM_HEADS, MEM_HD)
    q = rmsnorm(q, cp["g_q"])
    s = jnp.einsum("bthd,bmhd->bhtm", q, mem_kv[:, :, 0]).astype(jnp.float32) * (MEM_HD ** -0.5)
    p = jax.nn.softmax(s, axis=-1).astype(x.dtype)
    o = jnp.einsum("bhtm,bmhd->bthd", p, mem_kv[:, :, 1]).reshape(B * T, MEM_W)
    return matmul_res(o, cp["w_o"], x.reshape(B * T, D)).reshape(B, T, D)


PEER_TB = 128
PEER_VMEM_LIMIT_BYTES = 56 * 1024 * 1024
F32_NEG_INF = float("-inf")


def _topk_rows(s, row, k):
    nrow = float(s.shape[0])
    vals, ids = [], []
    for _ in range(k):
        m = jnp.max(s, axis=0, keepdims=True)
        i = jnp.min(jnp.where(s == m, row, nrow), axis=0, keepdims=True)
        vals.append(m)
        ids.append(i)
        s = jnp.where(row == i, F32_NEG_INF, s)
    return vals, ids


def _peer_route_kernel(x_ref, g_ref, wq_ref, sk0_ref, sk1_ref, xn_ref, idx_ref, gate_ref):
    x = x_ref[...]
    xn = x * lax.rsqrt(jnp.mean(x * x, axis=-1, keepdims=True) + EPS) * g_ref[...]
    xn_ref[...] = xn
    q = jnp.dot(xn.astype(jnp.bfloat16), wq_ref[...], preferred_element_type=jnp.float32)
    tb = x.shape[0]
    row_k = lax.broadcasted_iota(jnp.int32, (PEER_NKEYS, tb), 0).astype(jnp.float32)
    row_c = lax.broadcasted_iota(jnp.int32, (PEER_TOPK * PEER_TOPK, tb), 0).astype(jnp.float32)
    nt = (((1,), (1,)), ((), ()))
    ids, gates = [], []
    for h in range(PEER_HEADS):
        qh = q[:, h * PEER_DKEY:(h + 1) * PEER_DKEY].astype(jnp.bfloat16)
        s1 = lax.dot_general(sk0_ref[...], qh, nt, preferred_element_type=jnp.float32)
        s2 = lax.dot_general(sk1_ref[...], qh, nt, preferred_element_type=jnp.float32)
        v1, i1 = _topk_rows(s1, row_k, PEER_TOPK)
        v2, i2 = _topk_rows(s2, row_k, PEER_TOPK)
        v2c = jnp.concatenate(v2, axis=0)
        i2c = jnp.concatenate(i2, axis=0)
        cand = jnp.concatenate([v1[a] + v2c for a in range(PEER_TOPK)], axis=0)
        cid = jnp.concatenate([i1[a] * float(PEER_NKEYS) + i2c for a in range(PEER_TOPK)], axis=0)
        tops, tids = [], []
        for _ in range(PEER_TOPK):
            m = jnp.max(cand, axis=0, keepdims=True)
            pos = jnp.min(jnp.where(cand == m, row_c, float(PEER_TOPK * PEER_TOPK)), axis=0, keepdims=True)
            hit = row_c == pos
            tids.append(jnp.sum(jnp.where(hit, cid, 0.0), axis=0, keepdims=True))
            tops.append(m)
            cand = jnp.where(hit, F32_NEG_INF, cand)
        ts = jnp.concatenate(tops, axis=0)
        e = jnp.exp(ts - tops[0])
        gates.append(e / jnp.sum(e, axis=0, keepdims=True))
        ids.append(jnp.concatenate(tids, axis=0))
    idx_ref[...] = jnp.concatenate(ids, axis=0).T.astype(jnp.int32)
    gate_ref[...] = jnp.concatenate(gates, axis=0).T


def peer_route(x, g, w_q, subkeys):
    n, d = x.shape
    tb = min(PEER_TB, n)
    hk = PEER_HEADS * PEER_TOPK
    half = PEER_DKEY // 2
    z = jnp.zeros((PEER_NKEYS, half), jnp.float32)
    sk0 = jnp.concatenate([subkeys[0], z], axis=1).astype(jnp.bfloat16)
    sk1 = jnp.concatenate([z, subkeys[1]], axis=1).astype(jnp.bfloat16)
    return pl.pallas_call(
        _peer_route_kernel,
        grid=(n // tb,),
        in_specs=[pl.BlockSpec((tb, d), lambda i: (i, 0)),
                  pl.BlockSpec((1, d), lambda i: (0, 0)),
                  pl.BlockSpec((d, PEER_HEADS * PEER_DKEY), lambda i: (0, 0)),
                  pl.BlockSpec((PEER_NKEYS, PEER_DKEY), lambda i: (0, 0)),
                  pl.BlockSpec((PEER_NKEYS, PEER_DKEY), lambda i: (0, 0))],
        out_specs=[pl.BlockSpec((tb, d), lambda i: (i, 0)),
                   pl.BlockSpec((tb, hk), lambda i: (i, 0)),
                   pl.BlockSpec((tb, hk), lambda i: (i, 0))],
        out_shape=[jax.ShapeDtypeStruct((n, d), jnp.float32),
                   jax.ShapeDtypeStruct((n, hk), jnp.int32),
                   jax.ShapeDtypeStruct((n, hk), jnp.float32)],
        compiler_params=pltpu.CompilerParams(dimension_semantics=("arbitrary",),
                                             vmem_limit_bytes=PEER_VMEM_LIMIT_BYTES),
        name="peer_route",
    )(x, g.reshape(1, d), w_q.astype(jnp.bfloat16), sk0, sk1)


def pack_table(t):
    e, d = t.shape
    b = lax.bitcast_convert_type(t.astype(jnp.bfloat16), jnp.uint16).astype(jnp.uint32)
    w = b[:, : d // 2] | (b[:, d // 2:] << 16)
    return w.reshape(e * d // 256, 128)


def _table_spec(tab):
    return pl.BlockSpec(tab.shape, lambda i: (0, 0), pipeline_mode=pl.Buffered(1))


def _gather_row(tab, e, rows):
    wds = tab[pl.ds(pl.multiple_of(e * rows, rows), rows), :]
    lo = pltpu.bitcast(wds << 16, jnp.float32)
    hi = pltpu.bitcast(wds & jnp.uint32(0xFFFF0000), jnp.float32)
    return lo, hi


def _peer_u_kernel(idx_ref, xn_ref, gate_ref, tab, w_ref, slots, rsum, act):
    tb, hk = gate_ref.shape
    rows = xn_ref.shape[1] // 2

    def token(t, c):
        xt = xn_ref[t]
        xlo, xhi = xt[:rows], xt[rows:]
        for k in range(hk):
            lo, hi = _gather_row(tab, idx_ref[t, k], rows)
            slots[k * rows:(k + 1) * rows, :] = lo * xlo + hi * xhi
        r = slots[pl.ds(0, hk, stride=rows), :]
        for s in range(1, rows):
            r = r + slots[pl.ds(s, hk, stride=rows), :]
        rsum[pl.ds(pl.multiple_of(t * hk, hk), hk), :] = r
        return c

    lax.fori_loop(0, tb, token, 0)

    ones = jnp.ones((128, 128), jnp.bfloat16)
    grp = 8
    ri = lax.broadcasted_iota(jnp.int32, (grp * hk, 128), 0)
    ci = lax.broadcasted_iota(jnp.int32, (grp * hk, 128), 1)
    eye = (ri % hk) == ci

    def lane_sum(c, carry):
        rr = rsum[pl.ds(pl.multiple_of(c * grp * hk, grp * hk), grp * hk), :]
        hi = rr.astype(jnp.bfloat16)
        lo = (rr - hi.astype(jnp.float32)).astype(jnp.bfloat16)
        m = (jnp.dot(hi, ones, preferred_element_type=jnp.float32)
             + jnp.dot(lo, ones, preferred_element_type=jnp.float32))
        d = jnp.sum(jnp.where(eye, m, 0.0).reshape(grp, hk, 128), axis=1)
        act[pl.ds(pl.multiple_of(c * grp, grp), grp), :] = d
        return carry

    lax.fori_loop(0, tb // grp, lane_sum, 0)
    w_ref[...] = gate_ref[...] * jax.nn.gelu(act[...])


def peer_activate(idx, xn, gate, tab):
    n, d = xn.shape
    hk = idx.shape[1]
    tb = min(PEER_TB, n)
    sub = d // 128
    return pl.pallas_call(
        _peer_u_kernel,
        grid=(n // tb,),
        in_specs=[pl.BlockSpec((tb, hk), lambda i: (i, 0), memory_space=pltpu.SMEM),
                  pl.BlockSpec((tb, sub, 128), lambda i: (i, 0, 0)),
                  pl.BlockSpec((tb, hk), lambda i: (i, 0)),
                  _table_spec(tab)],
        out_specs=pl.BlockSpec((tb, hk), lambda i: (i, 0)),
        out_shape=jax.ShapeDtypeStruct((n, hk), jnp.float32),
        scratch_shapes=[pltpu.VMEM((hk * sub // 2, 128), jnp.float32),
                        pltpu.VMEM((tb * hk, 128), jnp.float32),
                        pltpu.VMEM((tb, hk), jnp.float32)],
        compiler_params=pltpu.CompilerParams(dimension_semantics=("arbitrary",),
                                             vmem_limit_bytes=PEER_VMEM_LIMIT_BYTES),
        name="peer_u",
    )(idx, xn.reshape(n, sub, 128), gate, tab)


def _peer_v_kernel(idx_ref, w_ref, x_ref, tab, o_ref):
    tb, hk = idx_ref.shape
    rows = x_ref.shape[1] // 2

    def token(t, c):
        nacc = 2
        lo_acc = [jnp.zeros((rows, 128), jnp.float32) for _ in range(nacc)]
        hi_acc = [jnp.zeros((rows, 128), jnp.float32) for _ in range(nacc)]
        for k in range(hk):
            lo, hi = _gather_row(tab, idx_ref[t, k], rows)
            wk = w_ref[t, k]
            lo_acc[k % nacc] = lo_acc[k % nacc] + wk * lo
            hi_acc[k % nacc] = hi_acc[k % nacc] + wk * hi
        xt = x_ref[t]
        o_ref[t, :rows, :] = xt[:rows] + (lo_acc[0] + lo_acc[1])
        o_ref[t, rows:, :] = xt[rows:] + (hi_acc[0] + hi_acc[1])
        return c

    lax.fori_loop(0, tb, token, 0)


def peer_combine(idx, w, x, tab):
    n, d = x.shape
    hk = idx.shape[1]
    tb = min(PEER_TB, n)
    sub = d // 128
    out = pl.pallas_call(
        _peer_v_kernel,
        grid=(n // tb,),
        in_specs=[pl.BlockSpec((tb, hk), lambda i: (i, 0), memory_space=pltpu.SMEM),
                  pl.BlockSpec((tb, hk), lambda i: (i, 0), memory_space=pltpu.SMEM),
                  pl.BlockSpec((tb, sub, 128), lambda i: (i, 0, 0)),
                  _table_spec(tab)],
        out_specs=pl.BlockSpec((tb, sub, 128), lambda i: (i, 0, 0)),
        out_shape=jax.ShapeDtypeStruct((n, sub, 128), jnp.float32),
        compiler_params=pltpu.CompilerParams(dimension_semantics=("arbitrary",),
                                             vmem_limit_bytes=PEER_VMEM_LIMIT_BYTES),
        name="peer_v",
    )(idx, w, x.reshape(n, sub, 128), tab)
    return out.reshape(n, d)


def peer_ffn(x, g_norm, pp):
    B, T, D = x.shape
    x2 = x.reshape(B * T, D)
    xn, idx, gate = peer_route(x2, g_norm, pp["w_q"], pp["subkeys"])
    w = peer_activate(idx, xn, gate, pp["u_packed"])
    return peer_combine(idx, w, x2, pp["v_packed"]).reshape(B, T, D)


def kernel(x_prompt, x_sample, mem_prompt, cache_nsa_cmp, cache_nsa_slc, cache_nsa_win, cache_mla, cache_swa,
           cache_mem, page_table, norm_mix, norm_cross, norm_ffn, even_w_in, even_w_out, nsa_g_q, nsa_g_k,
           nsa_cmp_pos, nsa_cmp_w1, nsa_cmp_w2, mla_g_qn, mla_g_qpe, mla_g_lat, mla_g_kpe, mla_g_kn, mla_w_uk,
           mla_w_uv, odd_w_in, odd_w_out, swa_g_q, swa_g_k, swa_sinks, mem_g, mem_w_q, mem_w_k, mem_w_v,
           mem_g_q, mem_g_k, mem_w_o, peer_w_q, peer_subkeys, peer_u, peer_v):
    depth = norm_mix.shape[0]
    slopes_nsa = alibi_slopes(NSA_HEADS)
    slopes_swa = alibi_slopes(SWA_HEADS)
    past_len = page_table.shape[1] * PAGE_SIZE
    T, Tn = x_prompt.shape[1], x_sample.shape[1]
    pos_p = jnp.arange(T)
    pos_s = past_len + jnp.arange(Tn)
    xp, xs = x_prompt, x_sample
    st_cmp_p, st_cmp_s, st_slc_p, st_slc_s, st_win_p, st_win_s = [], [], [], [], [], []
    st_mla_p, st_mla_s, st_swa_p, st_swa_s, st_mem_p = [], [], [], [], []

    for li in range(depth):
        if li % 2 == 0:
            e = li // 2
            ep = dict(w_in=even_w_in[e], g_q=nsa_g_q[e], g_k=nsa_g_k[e], cmp_pos=nsa_cmp_pos[e],
                      cmp_w1=nsa_cmp_w1[e], cmp_w2=nsa_cmp_w2[e], g_qn=mla_g_qn[e], g_qpe=mla_g_qpe[e],
                      g_lat=mla_g_lat[e], g_kpe=mla_g_kpe[e], g_kn=mla_g_kn[e], w_uk=mla_w_uk[e], w_uv=mla_w_uv[e])
            q, cmp_kv, slc_kv, win_kv, gates, qn, qp, mla_row = even_project(xp, norm_mix[li], pos_p, ep)
            o_nsa = nsa_prompt(q, cmp_kv, slc_kv, win_kv, gates, ep, slopes_nsa)
            o_mla = mla_prompt(qn, qp, mla_row, ep)
            xp = even_output(xp, o_nsa, o_mla, even_w_out[e])
            st_cmp_p.append(cmp_kv)
            st_slc_p.append(slc_kv)
            st_win_p.append(win_kv[:, -min(NSA_WIN, T):])
            st_mla_p.append(mla_row)
            win_buf = cache_nsa_win[e]
            q, cmp_kv, slc_kv, win_kv, gates, qn, qp, mla_row = even_project(xs, norm_mix[li], pos_s, ep)
            o_nsa = nsa_sample(q, pos_s, cmp_kv, slc_kv, win_kv, gates, cache_nsa_cmp, cache_nsa_slc, win_buf,
                               page_table, e, ep, slopes_nsa)
            o_mla = mla_sample(qn, qp, mla_row, cache_mla, page_table, e, ep)
            xs = even_output(xs, o_nsa, o_mla, even_w_out[e])
            st_cmp_s.append(cmp_kv)
            st_slc_s.append(slc_kv)
            st_win_s.append(jnp.concatenate([win_buf, win_kv], axis=1)[:, -win_buf.shape[1]:])
            st_mla_s.append(mla_row)
        else:
            o = li // 2
            op = dict(w_in=odd_w_in[o], g_q=swa_g_q[o], g_k=swa_g_k[o])
            B = xp.shape[0]
            q, kv = odd_project(xp, norm_mix[li], op)
            att = banded_prompt(q, kv[:, :, 0], kv[:, :, 1], SWA_WIN, slopes_swa, swa_sinks[o])
            xp = matmul_res(att.reshape(B * T, ODD_OUT), odd_w_out[o], xp.reshape(B * T, -1)).reshape(xp.shape)
            st_swa_p.append(kv[:, -min(SWA_WIN, T):])
            buf = cache_swa[o]
            q, kv = odd_project(xs, norm_mix[li], op)
            att = window_sample(q, buf[:, :, 0], buf[:, :, 1], kv[:, :, 0], kv[:, :, 1], past_len, SWA_WIN,
                                slopes_swa, swa_sinks[o])
            xs = matmul_res(att.reshape(xs.shape[0] * Tn, ODD_OUT), odd_w_out[o],
                            xs.reshape(xs.shape[0] * Tn, -1)).reshape(xs.shape)
            st_swa_s.append(jnp.concatenate([buf, kv], axis=1)[:, -buf.shape[1]:])

        cp = dict(g_mem=mem_g[li], w_q=mem_w_q[li], w_k=mem_w_k[li], w_v=mem_w_v[li], g_q=mem_g_q[li],
                  g_k=mem_g_k[li], w_o=mem_w_o[li])
        mem_kv_p = memory_kv(mem_prompt, cp)
        xp = memory_cross(xp, norm_cross[li], mem_kv_p, cp)
        xs = memory_cross(xs, norm_cross[li], cache_mem[li], cp)
        st_mem_p.append(mem_kv_p)

        pp = dict(w_q=peer_w_q[li], subkeys=peer_subkeys[li],
                  u_packed=pack_table(peer_u[li]), v_packed=pack_table(peer_v[li]))
        xp = peer_ffn(xp, norm_ffn[li], pp)
        xs = peer_ffn(xs, norm_ffn[li], pp)

    return (xp, xs, jnp.stack(st_cmp_p), jnp.stack(st_cmp_s), jnp.stack(st_slc_p), jnp.stack(st_slc_s),
            jnp.stack(st_win_p), jnp.stack(st_win_s), jnp.stack(st_mla_p), jnp.stack(st_mla_s),
            jnp.stack(st_swa_p), jnp.stack(st_swa_s), jnp.stack(st_mem_p))
```

```python
import functools
import math

import jax
import jax.numpy as jnp
import numpy as np
from jax import lax
from jax.experimental import pallas as pl
from jax.experimental.pallas import tpu as pltpu

PAGE_SIZE = 128
Q_BLOCK = 128
EPS = 1e-6
NEG_INF = -1e30
ROPE_BASE = 10000.0

NSA_HEADS = 8
NSA_DK = 64
NSA_BLK = 64
NSA_TOPN = 16
NSA_WIN = 512
NSA_CMP_HID = 256
MLA_HEADS = 8
MLA_NOPE = 64
MLA_ROPE = 32
MLA_V = 64
MLA_LORA = 128
MLA_ROW = MLA_LORA + MLA_ROPE
SWA_HEADS = 16
SWA_KV_HEADS = 4
SWA_HD = 64
SWA_WIN = 128
N_MEM = 256
MEM_HEADS = 4
MEM_HD = 128
MEM_W = MEM_HEADS * MEM_HD
PEER_HEADS = 8
PEER_NKEYS = 128
PEER_N = PEER_NKEYS * PEER_NKEYS
PEER_DKEY = 128
PEER_TOPK = 16
PEER_CHUNK = 256

NSA_Q_COLS = NSA_HEADS * NSA_DK
NSA_KV_COLS = 3 * 2 * NSA_DK
NSA_GATE_COLS = NSA_HEADS * 3
MLA_Q_COLS = MLA_HEADS * (MLA_NOPE + MLA_ROPE)
EVEN_IN = NSA_Q_COLS + NSA_KV_COLS + NSA_GATE_COLS + MLA_Q_COLS + MLA_ROW
EVEN_OUT = NSA_HEADS * NSA_DK + MLA_HEADS * MLA_V
ODD_IN = SWA_HEADS * SWA_HD + 2 * SWA_KV_HEADS * SWA_HD
ODD_OUT = SWA_HEADS * SWA_HD

VMEM_LIMIT_BYTES = 48 * 1024 * 1024


def _row_tile(n, target=512):
    t = min(n, target)
    while n % t:
        t //= 2
    return t


def _norm_matmul_kernel(x_ref, g_ref, w_ref, o_ref):
    x = x_ref[...]
    y = x * lax.rsqrt(jnp.mean(x * x, axis=-1, keepdims=True) + EPS) * g_ref[...]
    o_ref[...] = jnp.dot(y.astype(jnp.bfloat16), w_ref[...], preferred_element_type=jnp.float32)


def norm_matmul(x, g, w):
    n, d = x.shape
    c = w.shape[1]
    tm = _row_tile(n)
    return pl.pallas_call(
        _norm_matmul_kernel,
        grid=(n // tm,),
        in_specs=[pl.BlockSpec((tm, d), lambda i: (i, 0)),
                  pl.BlockSpec((1, d), lambda i: (0, 0)),
                  pl.BlockSpec((d, c), lambda i: (0, 0))],
        out_specs=pl.BlockSpec((tm, c), lambda i: (i, 0)),
        out_shape=jax.ShapeDtypeStruct((n, c), jnp.float32),
        compiler_params=pltpu.CompilerParams(dimension_semantics=("arbitrary",),
                                             vmem_limit_bytes=VMEM_LIMIT_BYTES),
        name="norm_matmul",
    )(x, g.reshape(1, d), w.astype(jnp.bfloat16))


def _matmul_res_kernel(a_ref, w_ref, r_ref, o_ref):
    o_ref[...] = r_ref[...] + jnp.dot(a_ref[...].astype(jnp.bfloat16), w_ref[...],
                                      preferred_element_type=jnp.float32)


def matmul_res(a, w, res):
    n, k = a.shape
    c = w.shape[1]
    tm = _row_tile(n)
    return pl.pallas_call(
        _matmul_res_kernel,
        grid=(n // tm,),
        in_specs=[pl.BlockSpec((tm, k), lambda i: (i, 0)),
                  pl.BlockSpec((k, c), lambda i: (0, 0)),
                  pl.BlockSpec((tm, c), lambda i: (i, 0))],
        out_specs=pl.BlockSpec((tm, c), lambda i: (i, 0)),
        out_shape=jax.ShapeDtypeStruct((n, c), jnp.float32),
        compiler_params=pltpu.CompilerParams(dimension_semantics=("arbitrary",),
                                             vmem_limit_bytes=VMEM_LIMIT_BYTES),
        name="matmul_res",
    )(a, w.astype(jnp.bfloat16), res)


def rmsnorm(x, g):
    xf = x.astype(jnp.float32)
    y = xf * lax.rsqrt(jnp.mean(xf * xf, axis=-1, keepdims=True) + EPS)
    return (y * g.astype(jnp.float32)).astype(x.dtype)


def alibi_slopes(n_heads):
    return jnp.asarray((2.0 ** (-8.0 * np.arange(1, n_heads + 1) / n_heads)).astype(np.float32))


def rope(x, pos):
    d = x.shape[-1]
    inv = jnp.asarray(np.power(ROPE_BASE, -np.arange(0, d, 2, dtype=np.float32) / d).astype(np.float32))
    ang = pos.astype(jnp.float32)[:, None] * inv[None, :]
    cos = jnp.cos(ang)[None, :, None, :]
    sin = jnp.sin(ang)[None, :, None, :]
    xf = x.astype(jnp.float32)
    x1, x2 = xf[..., : d // 2], xf[..., d // 2:]
    return jnp.concatenate([x1 * cos - x2 * sin, x1 * sin + x2 * cos], axis=-1).astype(x.dtype)


def masked_softmax(s, valid, sink=None):
    s = jnp.where(valid, s, NEG_INF)
    m = jnp.max(s, axis=-1, keepdims=True)
    if sink is not None:
        sk = sink.astype(jnp.float32)[:, None, None]
        m = jnp.maximum(m, sk)
    e = jnp.where(valid, jnp.exp(s - m), 0.0)
    den = jnp.sum(e, axis=-1, keepdims=True)
    if sink is not None:
        den = den + jnp.exp(sk - m)
    return e / jnp.maximum(den, 1e-30)


def attend(q, k, v, dist, valid, slopes, sink=None):
    B, Q, H, d = q.shape
    G = k.shape[-2]
    qg = q.reshape(B, Q, G, H // G, d)
    if k.ndim == 5:
        s = jnp.einsum("bqgrd,bqkgd->bgrqk", qg, k)
    else:
        s = jnp.einsum("bqgrd,bkgd->bgrqk", qg, k)
    K = s.shape[-1]
    s = s.reshape(B, H, Q, K).astype(jnp.float32) * (d ** -0.5)
    if slopes is not None:
        s = s - slopes[:, None, None] * dist[..., None, :, :].astype(jnp.float32)
    p = masked_softmax(s, valid[..., None, :, :], sink)
    pg = p.reshape(B, G, H // G, Q, K).astype(v.dtype)
    if v.ndim == 5:
        o = jnp.einsum("bgrqk,bqkgd->bqgrd", pg, v)
    else:
        o = jnp.einsum("bgrqk,bkgd->bqgrd", pg, v)
    return o.reshape(B, Q, H, v.shape[-1]), p


def alibi_slopes_np(n_heads):
    return (2.0 ** (-8.0 * np.arange(1, n_heads + 1) / n_heads)).astype(np.float32)


Q_TILE = 128
SLC_CHUNK = 512
MASK_BIG = -(2.0 ** 100)


def _nsa_prompt_kernel(q_ref, kt_ref, vs_ref, kwt_ref, vw_ref, kct_ref, vc_ref, g_ref, slope_ref, wbias_ref, cbias_ref,
                       o_ref, *, n_blocks):
    H, TQ = NSA_HEADS, Q_TILE
    i = pl.program_id(1)
    s0 = i * TQ
    q_lo = q_ref[0].reshape(H * TQ, 128)
    slope = slope_ref[...]

    s = jnp.dot(q_lo, kct_ref[0], preferred_element_type=jnp.float32).reshape(H, TQ, n_blocks)
    tpos = s0 + lax.broadcasted_iota(jnp.int32, (TQ, n_blocks), 0)
    blk = lax.broadcasted_iota(jnp.int32, (TQ, n_blocks), 1)
    dist = tpos - ((blk + 1) * NSA_BLK - 1)
    valid = (dist >= 0)[None]
    s = jnp.where(valid, s - slope * dist.astype(jnp.float32)[None], NEG_INF)
    m = jnp.max(s, axis=-1, keepdims=True)
    e = jnp.where(valid, jnp.exp(s - m), 0.0)
    p = e / jnp.maximum(jnp.sum(e, axis=-1, keepdims=True), 1e-30)
    o_cmp = jnp.dot(p.reshape(H * TQ, n_blocks).astype(jnp.bfloat16), vc_ref[0],
                    preferred_element_type=jnp.float32)
    imp = jnp.sum(p, axis=0)

    cur = tpos // NSA_BLK
    imp = jnp.where(blk == cur, float(H + 1), jnp.where(blk < cur, imp, -1.0))
    blk_f = blk.astype(jnp.float32)
    sel = jnp.zeros((TQ, n_blocks), jnp.bool_)
    for _ in range(min(NSA_TOPN, n_blocks)):
        mx = jnp.max(imp, axis=-1, keepdims=True)
        pick = jnp.min(jnp.where(imp == mx, blk_f, float(n_blocks)), axis=-1, keepdims=True)
        hit = blk_f == pick
        sel = sel | hit
        imp = jnp.where(hit, -2.0, imp)
    nsb = jnp.where(sel, 0.0, MASK_BIG).astype(jnp.bfloat16)
    q_aug = jnp.concatenate([q_lo, jnp.broadcast_to(nsb[None], (H, TQ, n_blocks)).reshape(H * TQ, n_blocks)], axis=1)

    KC = SLC_CHUNK

    def slc_step(c, carry, bias):
        m_i, l_i, acc = carry
        k0 = pl.multiple_of(c * KC, KC)
        sc = jnp.dot(q_aug, kt_ref[0, :, pl.ds(k0, KC)], preferred_element_type=jnp.float32)
        if bias is not None:
            sc = (sc.reshape(H, TQ, KC) + bias[None]).reshape(H * TQ, KC)
        m_new = jnp.maximum(m_i, jnp.max(sc, axis=-1, keepdims=True))
        corr = jnp.exp(m_i - m_new)
        pe = jnp.exp(sc - m_new)
        l_new = l_i * corr + jnp.sum(pe, axis=-1, keepdims=True)
        acc_new = acc * corr + jnp.dot(pe.astype(jnp.bfloat16), vs_ref[0, pl.ds(k0, KC), :],
                                       preferred_element_type=jnp.float32)
        return m_new, l_new, acc_new

    init = (jnp.full((H * TQ, 1), NEG_INF, jnp.float32), jnp.zeros((H * TQ, 1), jnp.float32),
            jnp.zeros((H * TQ, NSA_DK), jnp.float32))
    c_diag = s0 // KC
    carry = lax.fori_loop(0, c_diag, lambda c, cr: slc_step(c, cr, None), init)
    m_i, l_i, acc = slc_step(c_diag, carry, cbias_ref[(s0 % KC) // TQ])
    o_slc = acc / l_i

    nwc = NSA_WIN // TQ + 1
    parts = []
    for r in range(nwc):
        cidx = i - (nwc - 1) + r
        k0 = pl.multiple_of(jnp.maximum(cidx, 0) * TQ, TQ)
        sw = jnp.dot(q_lo[:, :NSA_DK], kwt_ref[0, :, pl.ds(k0, TQ)], preferred_element_type=jnp.float32)
        b = jnp.where(cidx >= 0, wbias_ref[r], NEG_INF)
        parts.append((sw.reshape(H, TQ, TQ) + b).reshape(H * TQ, TQ))
    mw = parts[0].max(axis=-1, keepdims=True)
    for r in range(1, nwc):
        mw = jnp.maximum(mw, parts[r].max(axis=-1, keepdims=True))
    lw = jnp.zeros((H * TQ, 1), jnp.float32)
    accw = jnp.zeros((H * TQ, NSA_DK), jnp.float32)
    for r in range(nwc):
        cidx = i - (nwc - 1) + r
        k0 = pl.multiple_of(jnp.maximum(cidx, 0) * TQ, TQ)
        pe = jnp.exp(parts[r] - mw)
        lw = lw + jnp.sum(pe, axis=-1, keepdims=True)
        accw = accw + jnp.dot(pe.astype(jnp.bfloat16), vw_ref[0, pl.ds(k0, TQ), :], preferred_element_type=jnp.float32)
    o_win = accw / lw

    g = g_ref[0].reshape(H * TQ, 3)
    o = g[:, 0:1] * o_cmp + g[:, 1:2] * o_slc + g[:, 2:3] * o_win
    o_ref[0] = o.reshape(H, TQ, NSA_DK)


def nsa_prompt_attention(q, slc_k, slc_v, win_k, win_v, kc, vc, gates):
    B, T, H, dk = q.shape
    nb = T // NSA_BLK
    nbp = 128
    TQ = Q_TILE
    assert nb <= nbp and T % SLC_CHUNK == 0 and NSA_WIN % TQ == 0
    slopes = alibi_slopes_np(H)
    bf = jnp.bfloat16
    qh = jnp.transpose(q, (0, 2, 1, 3)) * (dk ** -0.5)
    al = np.zeros((H, 128 - dk), np.float32)
    al[:, 0] = slopes * 128.0
    al[:, 1] = slopes
    q_lo = jnp.concatenate([qh, jnp.broadcast_to(jnp.asarray(al)[None, :, None, :], (B, H, T, 128 - dk))],
                           axis=-1).astype(bf)
    pos = np.arange(T)
    crow = np.zeros((128 - dk + nbp, T), np.float32)
    crow[0] = pos // 128
    crow[1] = pos % 128
    crow[128 - dk + pos // NSA_BLK, pos] = 1.0
    kt = jnp.concatenate([jnp.transpose(slc_k, (0, 2, 1)),
                          jnp.broadcast_to(jnp.asarray(crow)[None], (B,) + crow.shape)], axis=1).astype(bf)
    kwt = jnp.transpose(win_k, (0, 2, 1)).astype(bf)
    kct = jnp.pad(jnp.transpose(kc, (0, 2, 1)), ((0, 0), (0, 128 - dk), (0, nbp - nb))).astype(bf)
    vcp = jnp.pad(vc, ((0, 0), (0, nbp - nb), (0, 0))).astype(bf)
    gh = jnp.transpose(gates, (0, 2, 1, 3))
    nwc = NSA_WIN // TQ + 1
    ii = np.arange(TQ)[:, None]
    jj = np.arange(TQ)[None, :]
    wb = np.zeros((nwc, H, TQ, TQ), np.float32)
    for r in range(nwc):
        d = ii - jj + TQ * (nwc - 1 - r)
        ok = (d >= 0) & (d <= NSA_WIN)
        wb[r] = np.where(ok[None], -slopes[:, None, None] * d[None].astype(np.float32), NEG_INF)
    nv = SLC_CHUNK // TQ
    cb = np.zeros((nv, TQ, SLC_CHUNK), np.float32)
    for v in range(nv):
        cb[v] = np.where(np.arange(SLC_CHUNK)[None, :] <= v * TQ + ii, 0.0, NEG_INF)
    kern = functools.partial(_nsa_prompt_kernel, n_blocks=nbp)
    return pl.pallas_call(
        kern,
        grid=(B, T // TQ),
        in_specs=[pl.BlockSpec((1, H, TQ, 128), lambda b, i: (b, 0, i, 0)),
                  pl.BlockSpec((1, 256, T), lambda b, i: (b, 0, 0)),
                  pl.BlockSpec((1, T, dk), lambda b, i: (b, 0, 0)),
                  pl.BlockSpec((1, dk, T), lambda b, i: (b, 0, 0)),
                  pl.BlockSpec((1, T, dk), lambda b, i: (b, 0, 0)),
                  pl.BlockSpec((1, 128, nbp), lambda b, i: (b, 0, 0)),
                  pl.BlockSpec((1, nbp, dk), lambda b, i: (b, 0, 0)),
                  pl.BlockSpec((1, H, TQ, 3), lambda b, i: (b, 0, i, 0)),
                  pl.BlockSpec((H, 1, 1), lambda b, i: (0, 0, 0)),
                  pl.BlockSpec((nwc, H, TQ, TQ), lambda b, i: (0, 0, 0, 0)),
                  pl.BlockSpec((nv, TQ, SLC_CHUNK), lambda b, i: (0, 0, 0))],
        out_specs=pl.BlockSpec((1, H, TQ, dk), lambda b, i: (b, 0, i, 0)),
        out_shape=jax.ShapeDtypeStruct((B, H, T, dk), jnp.float32),
        compiler_params=pltpu.CompilerParams(dimension_semantics=("arbitrary", "arbitrary"),
                                             vmem_limit_bytes=VMEM_LIMIT_BYTES),
        name="nsa_prompt",
    )(q_lo, kt, slc_v.astype(bf), kwt, win_v.astype(bf), kct, vcp, gh,
      jnp.asarray(slopes).reshape(H, 1, 1), jnp.asarray(wb), jnp.asarray(cb))


MLA_TILE = 512


def _mla_prompt_kernel(q_ref, kt_ref, v_ref, tril_ref, o_ref, *, scale_log2e):
    TQ = MLA_TILE
    i = pl.program_id(2)
    q = q_ref[0, 0]

    def step(c, carry, bias):
        m_i, l_i, acc = carry
        k0 = pl.multiple_of(c * TQ, TQ)
        s = jnp.dot(q, kt_ref[0, 0, :, pl.ds(k0, TQ)], preferred_element_type=jnp.float32)
        if bias is not None:
            s = s + bias
        m_new = jnp.maximum(m_i, jnp.max(s, axis=-1, keepdims=True))
        corr = jnp.exp2((m_i - m_new) * scale_log2e)
        p = jnp.exp2((s - m_new) * scale_log2e)
        l_new = l_i * corr + jnp.sum(p, axis=-1, keepdims=True)
        acc_new = acc * corr + jnp.dot(p.astype(jnp.bfloat16), v_ref[0, 0, pl.ds(k0, TQ), :],
                                       preferred_element_type=jnp.float32)
        return m_new, l_new, acc_new

    init = (jnp.full((TQ, 1), NEG_INF, jnp.float32), jnp.zeros((TQ, 1), jnp.float32),
            jnp.zeros((TQ, MLA_V), jnp.float32))
    carry = lax.fori_loop(0, i, lambda c, cr: step(c, cr, None), init)
    m_i, l_i, acc = step(i, carry, tril_ref[...])
    o_ref[0, 0] = acc / l_i


def mla_prompt_attention(qn, qp, kn, kp, v):
    B, T, H, _ = qn.shape
    TQ = MLA_TILE
    assert T % TQ == 0
    bf = jnp.bfloat16
    dq = MLA_NOPE + MLA_ROPE
    qcat = jnp.concatenate([qn, qp, jnp.zeros((B, T, H, 128 - dq), qn.dtype)], axis=-1)
    qcat = jnp.transpose(qcat, (0, 2, 1, 3)).astype(bf)
    kcat = jnp.concatenate([kn, jnp.broadcast_to(kp[:, :, None, :], (B, T, H, MLA_ROPE)),
                            jnp.zeros((B, T, H, 128 - dq), kn.dtype)], axis=-1)
    kt = jnp.transpose(kcat, (0, 2, 3, 1)).astype(bf)
    vh = jnp.transpose(v, (0, 2, 1, 3)).astype(bf)
    tril = np.where(np.arange(TQ)[None, :] <= np.arange(TQ)[:, None], 0.0, NEG_INF).astype(np.float32)
    kern = functools.partial(_mla_prompt_kernel, scale_log2e=float(dq ** -0.5 * math.log2(math.e)))
    return pl.pallas_call(
        kern,
        grid=(B, H, T // TQ),
        in_specs=[pl.BlockSpec((1, 1, TQ, 128), lambda b, h, i: (b, h, i, 0)),
                  pl.BlockSpec((1, 1, 128, T), lambda b, h, i: (b, h, 0, 0)),
                  pl.BlockSpec((1, 1, T, MLA_V), lambda b, h, i: (b, h, 0, 0)),
                  pl.BlockSpec((TQ, TQ), lambda b, h, i: (0, 0))],
        out_specs=pl.BlockSpec((1, 1, TQ, MLA_V), lambda b, h, i: (b, h, i, 0)),
        out_shape=jax.ShapeDtypeStruct((B, H, T, MLA_V), jnp.float32),
        compiler_params=pltpu.CompilerParams(dimension_semantics=("arbitrary", "arbitrary", "arbitrary"),
                                             vmem_limit_bytes=VMEM_LIMIT_BYTES),
        name="mla_prompt",
    )(qcat, kt, vh, jnp.asarray(tril))


def _swa_prompt_kernel(q_ref, kt_ref, v_ref, bias_ref, sink_ref, o_ref):
    TQ = Q_TILE
    R = SWA_HEADS // SWA_KV_HEADS
    i = pl.program_id(2)
    q = q_ref[0, 0].reshape(R * TQ, SWA_HD)
    kprev = pl.multiple_of(jnp.maximum(i - 1, 0) * TQ, TQ)
    kcur = pl.multiple_of(i * TQ, TQ)
    s0 = jnp.dot(q, kt_ref[0, 0, :, pl.ds(kprev, TQ)], preferred_element_type=jnp.float32).reshape(R, TQ, TQ)
    s1 = jnp.dot(q, kt_ref[0, 0, :, pl.ds(kcur, TQ)], preferred_element_type=jnp.float32).reshape(R, TQ, TQ)
    s0 = s0 + jnp.where(i > 0, bias_ref[0, 0], NEG_INF)
    s1 = s1 + bias_ref[0, 1]
    sink = sink_ref[0]
    m = jnp.maximum(jnp.maximum(s0.max(axis=-1, keepdims=True), s1.max(axis=-1, keepdims=True)), sink)
    p0 = jnp.exp(s0 - m)
    p1 = jnp.exp(s1 - m)
    den = p0.sum(axis=-1, keepdims=True) + p1.sum(axis=-1, keepdims=True) + jnp.exp(sink - m)
    o = (jnp.dot(p0.reshape(R * TQ, TQ).astype(jnp.bfloat16), v_ref[0, 0, pl.ds(kprev, TQ), :],
                 preferred_element_type=jnp.float32)
         + jnp.dot(p1.reshape(R * TQ, TQ).astype(jnp.bfloat16), v_ref[0, 0, pl.ds(kcur, TQ), :],
                   preferred_element_type=jnp.float32))
    o_ref[0, 0] = o.reshape(R, TQ, SWA_HD) / jnp.maximum(den, 1e-30)


def swa_prompt_attention(q, k, v, sinks):
    B, T, H, hd = q.shape
    G = k.shape[2]
    R = H // G
    TQ = Q_TILE
    assert SWA_WIN == TQ and T % TQ == 0
    bf = jnp.bfloat16
    slopes = alibi_slopes_np(H)
    qh = (jnp.transpose(q, (0, 2, 1, 3)) * (hd ** -0.5)).astype(bf).reshape(B, G, R, T, hd)
    kt = jnp.transpose(k, (0, 2, 3, 1)).astype(bf)
    vh = jnp.transpose(v, (0, 2, 1, 3)).astype(bf)
    ii = np.arange(TQ)[:, None]
    jj = np.arange(TQ)[None, :]
    bias = np.zeros((G, 2, R, TQ, TQ), np.float32)
    for r in range(2):
        d = ii - jj + TQ * (1 - r)
        ok = (d >= 0) & (d <= SWA_WIN)
        bias[:, r] = np.where(ok[None, None],
                              -slopes.reshape(G, R)[:, :, None, None] * d[None, None].astype(np.float32), NEG_INF)
    return pl.pallas_call(
        _swa_prompt_kernel,
        grid=(B, G, T // TQ),
        in_specs=[pl.BlockSpec((1, 1, R, TQ, hd), lambda b, g, i: (b, g, 0, i, 0)),
                  pl.BlockSpec((1, 1, hd, T), lambda b, g, i: (b, g, 0, 0)),
                  pl.BlockSpec((1, 1, T, hd), lambda b, g, i: (b, g, 0, 0)),
                  pl.BlockSpec((1, 2, R, TQ, TQ), lambda b, g, i: (g, 0, 0, 0, 0)),
                  pl.BlockSpec((1, R, 1, 1), lambda b, g, i: (g, 0, 0, 0))],
        out_specs=pl.BlockSpec((1, 1, R, TQ, hd), lambda b, g, i: (b, g, 0, i, 0)),
        out_shape=jax.ShapeDtypeStruct((B, G, R, T, hd), jnp.float32),
        compiler_params=pltpu.CompilerParams(dimension_semantics=("arbitrary", "arbitrary", "arbitrary"),
                                             vmem_limit_bytes=VMEM_LIMIT_BYTES),
        name="swa_prompt",
    )(qh, kt, vh, jnp.asarray(bias), sinks.reshape(G, R, 1, 1)).reshape(B, H, T, hd)


def window_sample(q, k_buf, v_buf, k_new, v_new, past_len, window, slopes, sink=None):
    Wb, Tn = k_buf.shape[1], q.shape[1]
    k = jnp.concatenate([k_buf, k_new], axis=1)
    v = jnp.concatenate([v_buf, v_new], axis=1)
    qpos = past_len + jnp.arange(Tn)
    kpos = past_len - Wb + jnp.arange(Wb + Tn)
    dist = qpos[:, None] - kpos[None, :]
    valid = (dist >= 0) & (dist <= window)
    return attend(q, k, v, dist, valid, slopes, sink)[0]


def nsa_compress(blocks, pos_emb, w1, w2):
    x = blocks + pos_emb
    h = jax.nn.gelu(jnp.einsum("...lcd,lcdh->...ch", x, w1))
    return jnp.einsum("...ch,chd->...cd", h, w2)


def nsa_select(p_cmp, qpos, n_blocks):
    imp = jnp.sum(p_cmp, axis=1)
    blk = jnp.arange(n_blocks)[None, :]
    cur = (qpos // NSA_BLK)[:, None]
    imp = jnp.where(blk == cur, float(NSA_HEADS + 1), jnp.where(blk < cur, imp, -1.0))
    _, idx = lax.top_k(imp, min(NSA_TOPN, n_blocks))
    return idx


def nsa_compressed_branch(q, qpos, kc, vc, c_end, slopes):
    dist = qpos[:, None] - c_end[None, :]
    o, p = attend(q, kc[:, :, None, :], vc[:, :, None, :], dist, dist >= 0, slopes)
    return o, nsa_select(p, qpos, kc.shape[1])


def nsa_selected_branch(q, qpos, sel, idx, slopes):
    B, Q, n = idx.shape
    kpos = (idx[..., None] * NSA_BLK + jnp.arange(NSA_BLK)).reshape(B, Q, n * NSA_BLK)
    dist = qpos[None, :, None] - kpos
    ks = sel[..., 0, :].reshape(B, Q, n * NSA_BLK, 1, NSA_DK)
    vs = sel[..., 1, :].reshape(B, Q, n * NSA_BLK, 1, NSA_DK)
    return attend(q, ks, vs, dist, dist >= 0, slopes)[0]


def nsa_gate(gates, o_cmp, o_slc, o_win):
    g = gates.astype(o_cmp.dtype)
    return g[..., 0:1] * o_cmp + g[..., 1:2] * o_slc + g[..., 2:3] * o_win


def nsa_prompt(q, cmp_kv, slc_kv, win_kv, gates, ep, slopes):
    B, T = q.shape[:2]
    nb = T // NSA_BLK
    comp = nsa_compress(cmp_kv.reshape(B, nb, NSA_BLK, 2, NSA_DK), ep["cmp_pos"], ep["cmp_w1"], ep["cmp_w2"])
    kc = rmsnorm(comp[:, :, 0], ep["g_k"][0])
    vc = comp[:, :, 1]
    o = nsa_prompt_attention(q, slc_kv[:, :, 0], slc_kv[:, :, 1], win_kv[:, :, 0], win_kv[:, :, 1], kc, vc, gates)
    return jnp.transpose(o, (0, 2, 1, 3))


def nsa_sample(q, qpos, cmp_new, slc_new, win_new, gates, cache_cmp, cache_slc, win_buf, page_table, e, ep, slopes):
    DB, Tn = q.shape[:2]
    n_pages = page_table.shape[1]
    bpp = PAGE_SIZE // NSA_BLK
    nb_past = n_pages * bpp
    past_len = n_pages * PAGE_SIZE
    nb_new = -(-Tn // NSA_BLK)
    pad = ((0, 0), (0, nb_new * NSA_BLK - Tn), (0, 0), (0, 0))

    def compress(rows):
        return nsa_compress(rows, ep["cmp_pos"], ep["cmp_w1"], ep["cmp_w2"])

    comp_past = lax.map(lambda pages: compress(cache_cmp[e, pages].reshape(DB, bpp, NSA_BLK, 2, NSA_DK)), page_table.T)
    comp_past = comp_past.transpose(1, 0, 2, 3, 4).reshape(DB, nb_past, 2, NSA_DK)
    comp_new = compress(jnp.pad(cmp_new, pad).reshape(DB, nb_new, NSA_BLK, 2, NSA_DK))
    comp = jnp.concatenate([comp_past, comp_new], axis=1)
    kc = rmsnorm(comp[:, :, 0], ep["g_k"][0])
    vc = comp[:, :, 1]
    c_end = (jnp.arange(nb_past + nb_new) + 1) * NSA_BLK - 1
    o_cmp, idx = nsa_compressed_branch(q, qpos, kc, vc, c_end, slopes)

    jp = jnp.minimum(idx, nb_past - 1)
    phys_page = jax.vmap(lambda pt, j: pt[j])(page_table, jp // bpp)
    rows = (jp % bpp)[..., None] * NSA_BLK + jnp.arange(NSA_BLK)
    past_sel = cache_slc[e, phys_page[..., None], rows]
    slc_new_b = jnp.pad(slc_new, pad).reshape(DB, nb_new, NSA_BLK, 2, NSA_DK)
    jn = jnp.clip(idx - nb_past, 0, nb_new - 1)
    new_sel = jax.vmap(lambda blocks, ix: blocks[ix])(slc_new_b, jn)
    sel = jnp.where((idx < nb_past)[..., None, None, None], past_sel, new_sel)
    o_slc = nsa_selected_branch(q, qpos, sel, idx, slopes)

    o_win = window_sample(q, win_buf[:, :, 0:1], win_buf[:, :, 1:2], win_new[:, :, 0:1], win_new[:, :, 1:2],
                          past_len, NSA_WIN, slopes)
    return nsa_gate(gates, o_cmp, o_slc, o_win)


def mla_keys(lat, ep):
    c = lat[..., :MLA_LORA]
    kp = lat[..., MLA_LORA:]
    kn = rmsnorm(jnp.einsum("bkc,chd->bkhd", c, ep["w_uk"]), ep["g_kn"])
    v = jnp.einsum("bkc,chd->bkhd", c, ep["w_uv"])
    return kn, kp, v


def mla_scores(qn, qp, kn, kp):
    s = jnp.einsum("bqhd,bkhd->bhqk", qn, kn) + jnp.einsum("bqhd,bkd->bhqk", qp, kp)
    return s.astype(jnp.float32) * ((MLA_NOPE + MLA_ROPE) ** -0.5)


def mla_prompt(qn, qp, lat, ep):
    B, T = qn.shape[:2]
    kn, kp, v = mla_keys(lat, ep)
    return jnp.transpose(mla_prompt_attention(qn, qp, kn, kp, v), (0, 2, 1, 3))


def mla_sample(qn, qp, lat_new, cache_mla, page_table, e, ep):
    DB, Tn = qn.shape[:2]

    def update(carry, kn, kp, v, valid):
        m, l, acc = carry
        s = jnp.where(valid, mla_scores(qn, qp, kn, kp), NEG_INF)
        m_new = jnp.maximum(m, jnp.max(s, axis=-1))
        corr = jnp.exp(m - m_new)
        p = jnp.where(valid, jnp.exp(s - m_new[..., None]), 0.0)
        l = l * corr + jnp.sum(p, axis=-1)
        acc = acc * corr[..., None] + jnp.einsum("bhqk,bkhd->bhqd", p.astype(v.dtype), v).astype(jnp.float32)
        return (m_new, l, acc)

    def step(carry, pages):
        kn, kp, v = mla_keys(cache_mla[e, pages], ep)
        return update(carry, kn, kp, v, True), None

    init = (jnp.full((DB, MLA_HEADS, Tn), NEG_INF, jnp.float32),
            jnp.zeros((DB, MLA_HEADS, Tn), jnp.float32),
            jnp.zeros((DB, MLA_HEADS, Tn, MLA_V), jnp.float32))
    carry, _ = lax.scan(step, init, page_table.T)
    kn, kp, v = mla_keys(lat_new, ep)
    ar = jnp.arange(Tn)
    m, l, acc = update(carry, kn, kp, v, ar[:, None] >= ar[None, :])
    return (acc / l[..., None]).astype(qn.dtype).transpose(0, 2, 1, 3)


def even_project(x, g_norm, pos, ep):
    B, T, D = x.shape
    h = norm_matmul(x.reshape(B * T, D), g_norm, ep["w_in"]).reshape(B, T, EVEN_IN)
    cuts = np.cumsum([NSA_Q_COLS, NSA_KV_COLS, NSA_GATE_COLS, MLA_Q_COLS]).tolist()
    q, kv, g, mq, lat = jnp.split(h, cuts, axis=-1)
    q = rmsnorm(q.reshape(B, T, NSA_HEADS, NSA_DK), ep["g_q"])
    kv = kv.reshape(B, T, 3, 2, NSA_DK)
    cmp_kv = kv[:, :, 0]
    slc_kv = jnp.stack([rmsnorm(kv[:, :, 1, 0], ep["g_k"][1]), kv[:, :, 1, 1]], axis=2)
    win_kv = jnp.stack([rmsnorm(kv[:, :, 2, 0], ep["g_k"][2]), kv[:, :, 2, 1]], axis=2)
    gates = jax.nn.sigmoid(g.astype(jnp.float32)).reshape(B, T, NSA_HEADS, 3)
    mq = mq.reshape(B, T, MLA_HEADS, MLA_NOPE + MLA_ROPE)
    qn = rmsnorm(mq[..., :MLA_NOPE], ep["g_qn"])
    qp = rope(rmsnorm(mq[..., MLA_NOPE:], ep["g_qpe"]), pos)
    c = rmsnorm(lat[..., :MLA_LORA], ep["g_lat"])
    kp = rope(rmsnorm(lat[..., MLA_LORA:], ep["g_kpe"])[:, :, None, :], pos)[:, :, 0, :]
    return q, cmp_kv, slc_kv, win_kv, gates, qn, qp, jnp.concatenate([c, kp], axis=-1)


def even_output(x, o_nsa, o_mla, w_out):
    B, T, D = x.shape
    a = jnp.concatenate([o_nsa.reshape(B, T, -1), o_mla.reshape(B, T, -1)], axis=-1)
    return matmul_res(a.reshape(B * T, EVEN_OUT), w_out, x.reshape(B * T, D)).reshape(B, T, D)


def odd_project(x, g_norm, op):
    B, T, D = x.shape
    h = norm_matmul(x.reshape(B * T, D), g_norm, op["w_in"]).reshape(B, T, ODD_IN)
    q, k, v = jnp.split(h, [SWA_HEADS * SWA_HD, SWA_HEADS * SWA_HD + SWA_KV_HEADS * SWA_HD], axis=-1)
    q = rmsnorm(q.reshape(B, T, SWA_HEADS, SWA_HD), op["g_q"])
    k = rmsnorm(k.reshape(B, T, SWA_KV_HEADS, SWA_HD), op["g_k"])
    v = v.reshape(B, T, SWA_KV_HEADS, SWA_HD)
    return q, jnp.stack([k, v], axis=2)


def memory_kv(mem, cp):
    B, M, D = mem.shape
    w_kv = jnp.concatenate([cp["w_k"], cp["w_v"]], axis=1)
    kv = norm_matmul(mem.reshape(B * M, D), cp["g_mem"], w_kv).reshape(B, M, 2, MEM_HEADS, MEM_HD)
    k = rmsnorm(kv[:, :, 0], cp["g_k"])
    return jnp.stack([k, kv[:, :, 1]], axis=2)


def memory_cross(x, g_norm, mem_kv, cp):
    B, T, D = x.shape
    q = norm_matmul(x.reshape(B * T, D), g_norm, cp["w_q"]).reshape(B, T, MEM_HEADS, MEM_HD)
    q = rmsnorm(q, cp["g_q"])
    s = jnp.einsum("bthd,bmhd->bhtm", q, mem_kv[:, :, 0]).astype(jnp.float32) * (MEM_HD ** -0.5)
    p = jax.nn.softmax(s, axis=-1).astype(x.dtype)
    o = jnp.einsum("bhtm,bmhd->bthd", p, mem_kv[:, :, 1]).reshape(B * T, MEM_W)
    return matmul_res(o, cp["w_o"], x.reshape(B * T, D)).reshape(B, T, D)


PEER_TB = 128
PEER_VMEM_LIMIT_BYTES = 56 * 1024 * 1024
F32_NEG_INF = float("-inf")


def _topk_rows(s, row, k):
    nrow = float(s.shape[0])
    vals, ids = [], []
    for _ in range(k):
        m = jnp.max(s, axis=0, keepdims=True)
        i = jnp.min(jnp.where(s == m, row, nrow), axis=0, keepdims=True)
        vals.append(m)
        ids.append(i)
        s = jnp.where(row == i, F32_NEG_INF, s)
    return vals, ids


def _peer_route_kernel(x_ref, g_ref, wq_ref, sk0_ref, sk1_ref, xn_ref, idx_ref, gate_ref):
    x = x_ref[...]
    xn = x * lax.rsqrt(jnp.mean(x * x, axis=-1, keepdims=True) + EPS) * g_ref[...]
    xn_ref[...] = xn
    q = jnp.dot(xn.astype(jnp.bfloat16), wq_ref[...], preferred_element_type=jnp.float32)
    tb = x.shape[0]
    row_k = lax.broadcasted_iota(jnp.int32, (PEER_NKEYS, tb), 0).astype(jnp.float32)
    row_c = lax.broadcasted_iota(jnp.int32, (PEER_TOPK * PEER_TOPK, tb), 0).astype(jnp.float32)
    nt = (((1,), (1,)), ((), ()))
    ids, gates = [], []
    for h in range(PEER_HEADS):
        qh = q[:, h * PEER_DKEY:(h + 1) * PEER_DKEY].astype(jnp.bfloat16)
        s1 = lax.dot_general(sk0_ref[...], qh, nt, preferred_element_type=jnp.float32)
        s2 = lax.dot_general(sk1_ref[...], qh, nt, preferred_element_type=jnp.float32)
        v1, i1 = _topk_rows(s1, row_k, PEER_TOPK)
        v2, i2 = _topk_rows(s2, row_k, PEER_TOPK)
        v2c = jnp.concatenate(v2, axis=0)
        i2c = jnp.concatenate(i2, axis=0)
        cand = jnp.concatenate([v1[a] + v2c for a in range(PEER_TOPK)], axis=0)
        cid = jnp.concatenate([i1[a] * float(PEER_NKEYS) + i2c for a in range(PEER_TOPK)], axis=0)
        tops, tids = [], []
        for _ in range(PEER_TOPK):
            m = jnp.max(cand, axis=0, keepdims=True)
            pos = jnp.min(jnp.where(cand == m, row_c, float(PEER_TOPK * PEER_TOPK)), axis=0, keepdims=True)
            hit = row_c == pos
            tids.append(jnp.sum(jnp.where(hit, cid, 0.0), axis=0, keepdims=True))
            tops.append(m)
            cand = jnp.where(hit, F32_NEG_INF, cand)
        ts = jnp.concatenate(tops, axis=0)
        e = jnp.exp(ts - tops[0])
        gates.append(e / jnp.sum(e, axis=0, keepdims=True))
        ids.append(jnp.concatenate(tids, axis=0))
    idx_ref[...] = jnp.concatenate(ids, axis=0).T.astype(jnp.int32)
    gate_ref[...] = jnp.concatenate(gates, axis=0).T


def peer_route(x, g, w_q, subkeys):
    n, d = x.shape
    tb = min(PEER_TB, n)
    hk = PEER_HEADS * PEER_TOPK
    half = PEER_DKEY // 2
    z = jnp.zeros((PEER_NKEYS, half), jnp.float32)
    sk0 = jnp.concatenate([subkeys[0], z], axis=1).astype(jnp.bfloat16)
    sk1 = jnp.concatenate([z, subkeys[1]], axis=1).astype(jnp.bfloat16)
    return pl.pallas_call(
        _peer_route_kernel,
        grid=(n // tb,),
        in_specs=[pl.BlockSpec((tb, d), lambda i: (i, 0)),
                  pl.BlockSpec((1, d), lambda i: (0, 0)),
                  pl.BlockSpec((d, PEER_HEADS * PEER_DKEY), lambda i: (0, 0)),
                  pl.BlockSpec((PEER_NKEYS, PEER_DKEY), lambda i: (0, 0)),
                  pl.BlockSpec((PEER_NKEYS, PEER_DKEY), lambda i: (0, 0))],
        out_specs=[pl.BlockSpec((tb, d), lambda i: (i, 0)),
                   pl.BlockSpec((tb, hk), lambda i: (i, 0)),
                   pl.BlockSpec((tb, hk), lambda i: (i, 0))],
        out_shape=[jax.ShapeDtypeStruct((n, d), jnp.float32),
                   jax.ShapeDtypeStruct((n, hk), jnp.int32),
                   jax.ShapeDtypeStruct((n, hk), jnp.float32)],
        compiler_params=pltpu.CompilerParams(dimension_semantics=("arbitrary",),
                                             vmem_limit_bytes=PEER_VMEM_LIMIT_BYTES),
        name="peer_route",
    )(x, g.reshape(1, d), w_q.astype(jnp.bfloat16), sk0, sk1)


def pack_table(t):
    e, d = t.shape
    b = lax.bitcast_convert_type(t.astype(jnp.bfloat16), jnp.uint16).astype(jnp.uint32)
    w = b[:, : d // 2] | (b[:, d // 2:] << 16)
    return w.reshape(e * d // 256, 128)


def _table_spec(tab):
    return pl.BlockSpec(tab.shape, lambda i: (0, 0), pipeline_mode=pl.Buffered(1))


def _gather_row(tab, e, rows):
    wds = tab[pl.ds(pl.multiple_of(e * rows, rows), rows), :]
    lo = pltpu.bitcast(wds << 16, jnp.float32)
    hi = pltpu.bitcast(wds & jnp.uint32(0xFFFF0000), jnp.float32)
    return lo, hi


def _peer_u_kernel(idx_ref, xn_ref, gate_ref, tab, w_ref, slots, rsum, act):
    tb, hk = gate_ref.shape
    rows = xn_ref.shape[1] // 2

    def token(t, c):
        xt = xn_ref[t]
        xlo, xhi = xt[:rows], xt[rows:]
        for k in range(hk):
            lo, hi = _gather_row(tab, idx_ref[t, k], rows)
            slots[k * rows:(k + 1) * rows, :] = lo * xlo + hi * xhi
        r = slots[pl.ds(0, hk, stride=rows), :]
        for s in range(1, rows):
            r = r + slots[pl.ds(s, hk, stride=rows), :]
        rsum[pl.ds(pl.multiple_of(t * hk, hk), hk), :] = r
        return c

    lax.fori_loop(0, tb, token, 0)

    ones = jnp.ones((128, 128), jnp.bfloat16)
    grp = 8
    ri = lax.broadcasted_iota(jnp.int32, (grp * hk, 128), 0)
    ci = lax.broadcasted_iota(jnp.int32, (grp * hk, 128), 1)
    eye = (ri % hk) == ci

    def lane_sum(c, carry):
        rr = rsum[pl.ds(pl.multiple_of(c * grp * hk, grp * hk), grp * hk), :]
        hi = rr.astype(jnp.bfloat16)
        lo = (rr - hi.astype(jnp.float32)).astype(jnp.bfloat16)
        m = (jnp.dot(hi, ones, preferred_element_type=jnp.float32)
             + jnp.dot(lo, ones, preferred_element_type=jnp.float32))
        d = jnp.sum(jnp.where(eye, m, 0.0).reshape(grp, hk, 128), axis=1)
        act[pl.ds(pl.multiple_of(c * grp, grp), grp), :] = d
        return carry

    lax.fori_loop(0, tb // grp, lane_sum, 0)
    w_ref[...] = gate_ref[...] * jax.nn.gelu(act[...])


def peer_activate(idx, xn, gate, tab):
    n, d = xn.shape
    hk = idx.shape[1]
    tb = min(PEER_TB, n)
    sub = d // 128
    return pl.pallas_call(
        _peer_u_kernel,
        grid=(n // tb,),
        in_specs=[pl.BlockSpec((tb, hk), lambda i: (i, 0), memory_space=pltpu.SMEM),
                  pl.BlockSpec((tb, sub, 128), lambda i: (i, 0, 0)),
                  pl.BlockSpec((tb, hk), lambda i: (i, 0)),
                  _table_spec(tab)],
        out_specs=pl.BlockSpec((tb, hk), lambda i: (i, 0)),
        out_shape=jax.ShapeDtypeStruct((n, hk), jnp.float32),
        scratch_shapes=[pltpu.VMEM((hk * sub // 2, 128), jnp.float32),
                        pltpu.VMEM((tb * hk, 128), jnp.float32),
                        pltpu.VMEM((tb, hk), jnp.float32)],
        compiler_params=pltpu.CompilerParams(dimension_semantics=("arbitrary",),
                                             vmem_limit_bytes=PEER_VMEM_LIMIT_BYTES),
        name="peer_u",
    )(idx, xn.reshape(n, sub, 128), gate, tab)


def _peer_v_kernel(idx_ref, w_ref, x_ref, tab, o_ref):
    tb, hk = idx_ref.shape
    rows = x_ref.shape[1] // 2

    def token(t, c):
        nacc = 2
        lo_acc = [jnp.zeros((rows, 128), jnp.float32) for _ in range(nacc)]
        hi_acc = [jnp.zeros((rows, 128), jnp.float32) for _ in range(nacc)]
        for k in range(hk):
            lo, hi = _gather_row(tab, idx_ref[t, k], rows)
            wk = w_ref[t, k]
            lo_acc[k % nacc] = lo_acc[k % nacc] + wk * lo
            hi_acc[k % nacc] = hi_acc[k % nacc] + wk * hi
        xt = x_ref[t]
        o_ref[t, :rows, :] = xt[:rows] + (lo_acc[0] + lo_acc[1])
        o_ref[t, rows:, :] = xt[rows:] + (hi_acc[0] + hi_acc[1])
        return c

    lax.fori_loop(0, tb, token, 0)


def peer_combine(idx, w, x, tab):
    n, d = x.shape
    hk = idx.shape[1]
    tb = min(PEER_TB, n)
    sub = d // 128
    out = pl.pallas_call(
        _peer_v_kernel,
        grid=(n // tb,),
        in_specs=[pl.BlockSpec((tb, hk), lambda i: (i, 0), memory_space=pltpu.SMEM),
                  pl.BlockSpec((tb, hk), lambda i: (i, 0), memory_space=pltpu.SMEM),
                  pl.BlockSpec((tb, sub, 128), lambda i: (i, 0, 0)),
                  _table_spec(tab)],
        out_specs=pl.BlockSpec((tb, sub, 128), lambda i: (i, 0, 0)),
        out_shape=jax.ShapeDtypeStruct((n, sub, 128), jnp.float32),
        compiler_params=pltpu.CompilerParams(dimension_semantics=("arbitrary",),
                                             vmem_limit_bytes=PEER_VMEM_LIMIT_BYTES),
        name="peer_v",
    )(idx, w, x.reshape(n, sub, 128), tab)
    return out.reshape(n, d)


def peer_ffn(x, g_norm, pp):
    B, T, D = x.shape
    x2 = x.reshape(B * T, D)
    xn, idx, gate = peer_route(x2, g_norm, pp["w_q"], pp["subkeys"])
    w = peer_activate(idx, xn, gate, pp["u_packed"])
    return peer_combine(idx, w, x2, pp["v_packed"]).reshape(B, T, D)


def kernel(x_prompt, x_sample, mem_prompt, cache_nsa_cmp, cache_nsa_slc, cache_nsa_win, cache_mla, cache_swa,
           cache_mem, page_table, norm_mix, norm_cross, norm_ffn, even_w_in, even_w_out, nsa_g_q, nsa_g_k,
           nsa_cmp_pos, nsa_cmp_w1, nsa_cmp_w2, mla_g_qn, mla_g_qpe, mla_g_lat, mla_g_kpe, mla_g_kn, mla_w_uk,
           mla_w_uv, odd_w_in, odd_w_out, swa_g_q, swa_g_k, swa_sinks, mem_g, mem_w_q, mem_w_k, mem_w_v,
           mem_g_q, mem_g_k, mem_w_o, peer_w_q, peer_subkeys, peer_u, peer_v):
    depth = norm_mix.shape[0]
    slopes_nsa = alibi_slopes(NSA_HEADS)
    slopes_swa = alibi_slopes(SWA_HEADS)
    past_len = page_table.shape[1] * PAGE_SIZE
    T, Tn = x_prompt.shape[1], x_sample.shape[1]
    pos_p = jnp.arange(T)
    pos_s = past_len + jnp.arange(Tn)
    xp, xs = x_prompt, x_sample
    st_cmp_p, st_cmp_s, st_slc_p, st_slc_s, st_win_p, st_win_s = [], [], [], [], [], []
    st_mla_p, st_mla_s, st_swa_p, st_swa_s, st_mem_p = [], [], [], [], []

    for li in range(depth):
        if li % 2 == 0:
            e = li // 2
            ep = dict(w_in=even_w_in[e], g_q=nsa_g_q[e], g_k=nsa_g_k[e], cmp_pos=nsa_cmp_pos[e],
                      cmp_w1=nsa_cmp_w1[e], cmp_w2=nsa_cmp_w2[e], g_qn=mla_g_qn[e], g_qpe=mla_g_qpe[e],
                      g_lat=mla_g_lat[e], g_kpe=mla_g_kpe[e], g_kn=mla_g_kn[e], w_uk=mla_w_uk[e], w_uv=mla_w_uv[e])
            q, cmp_kv, slc_kv, win_kv, gates, qn, qp, mla_row = even_project(xp, norm_mix[li], pos_p, ep)
            o_nsa = nsa_prompt(q, cmp_kv, slc_kv, win_kv, gates, ep, slopes_nsa)
            o_mla = mla_prompt(qn, qp, mla_row, ep)
            xp = even_output(xp, o_nsa, o_mla, even_w_out[e])
            st_cmp_p.append(cmp_kv)
            st_slc_p.append(slc_kv)
            st_win_p.append(win_kv[:, -min(NSA_WIN, T):])
            st_mla_p.append(mla_row)
            win_buf = cache_nsa_win[e]
            q, cmp_kv, slc_kv, win_kv, gates, qn, qp, mla_row = even_project(xs, norm_mix[li], pos_s, ep)
            o_nsa = nsa_sample(q, pos_s, cmp_kv, slc_kv, win_kv, gates, cache_nsa_cmp, cache_nsa_slc, win_buf,
                               page_table, e, ep, slopes_nsa)
            o_mla = mla_sample(qn, qp, mla_row, cache_mla, page_table, e, ep)
            xs = even_output(xs, o_nsa, o_mla, even_w_out[e])
            st_cmp_s.append(cmp_kv)
            st_slc_s.append(slc_kv)
            st_win_s.append(jnp.concatenate([win_buf, win_kv], axis=1)[:, -win_buf.shape[1]:])
            st_mla_s.append(mla_row)
        else:
            o = li // 2
            op = dict(w_in=odd_w_in[o], g_q=swa_g_q[o], g_k=swa_g_k[o])
            B = xp.shape[0]
            q, kv = odd_project(xp, norm_mix[li], op)
            att = jnp.transpose(swa_prompt_attention(q, kv[:, :, 0], kv[:, :, 1], swa_sinks[o]), (0, 2, 1, 3))
            xp = matmul_res(att.reshape(B * T, ODD_OUT), odd_w_out[o], xp.reshape(B * T, -1)).reshape(xp.shape)
            st_swa_p.append(kv[:, -min(SWA_WIN, T):])
            buf = cache_swa[o]
            q, kv = odd_project(xs, norm_mix[li], op)
            att = window_sample(q, buf[:, :, 0], buf[:, :, 1], kv[:, :, 0], kv[:, :, 1], past_len, SWA_WIN,
                                slopes_swa, swa_sinks[o])
            xs = matmul_res(att.reshape(xs.shape[0] * Tn, ODD_OUT), odd_w_out[o],
                            xs.reshape(xs.shape[0] * Tn, -1)).reshape(xs.shape)
            st_swa_s.append(jnp.concatenate([buf, kv], axis=1)[:, -buf.shape[1]:])

        cp = dict(g_mem=mem_g[li], w_q=mem_w_q[li], w_k=mem_w_k[li], w_v=mem_w_v[li], g_q=mem_g_q[li],
                  g_k=mem_g_k[li], w_o=mem_w_o[li])
        mem_kv_p = memory_kv(mem_prompt, cp)
        xp = memory_cross(xp, norm_cross[li], mem_kv_p, cp)
        xs = memory_cross(xs, norm_cross[li], cache_mem[li], cp)
        st_mem_p.append(mem_kv_p)

        pp = dict(w_q=peer_w_q[li], subkeys=peer_subkeys[li],
                  u_packed=pack_table(peer_u[li]), v_packed=pack_table(peer_v[li]))
        xp = peer_ffn(xp, norm_ffn[li], pp)
        xs = peer_ffn(xs, norm_ffn[li], pp)

    return (xp, xs, jnp.stack(st_cmp_p), jnp.stack(st_cmp_s), jnp.stack(st_slc_p), jnp.stack(st_slc_s),
            jnp.stack(st_win_p), jnp.stack(st_win_s), jnp.stack(st_mla_p), jnp.stack(st_mla_s),
            jnp.stack(st_swa_p), jnp.stack(st_swa_s), jnp.stack(st_mem_p))
```

```python
import functools
import math

import jax
import jax.numpy as jnp
import numpy as np
from jax import lax
from jax.experimental import pallas as pl
from jax.experimental.pallas import tpu as pltpu

PAGE_SIZE = 128
Q_BLOCK = 128
EPS = 1e-6
NEG_INF = -1e30
ROPE_BASE = 10000.0

NSA_HEADS = 8
NSA_DK = 64
NSA_BLK = 64
NSA_TOPN = 16
NSA_WIN = 512
NSA_CMP_HID = 256
MLA_HEADS = 8
MLA_NOPE = 64
MLA_ROPE = 32
MLA_V = 64
MLA_LORA = 128
MLA_ROW = MLA_LORA + MLA_ROPE
SWA_HEADS = 16
SWA_KV_HEADS = 4
SWA_HD = 64
SWA_WIN = 128
N_MEM = 256
MEM_HEADS = 4
MEM_HD = 128
MEM_W = MEM_HEADS * MEM_HD
PEER_HEADS = 8
PEER_NKEYS = 128
PEER_N = PEER_NKEYS * PEER_NKEYS
PEER_DKEY = 128
PEER_TOPK = 16
PEER_CHUNK = 256

NSA_Q_COLS = NSA_HEADS * NSA_DK
NSA_KV_COLS = 3 * 2 * NSA_DK
NSA_GATE_COLS = NSA_HEADS * 3
MLA_Q_COLS = MLA_HEADS * (MLA_NOPE + MLA_ROPE)
EVEN_IN = NSA_Q_COLS + NSA_KV_COLS + NSA_GATE_COLS + MLA_Q_COLS + MLA_ROW
EVEN_OUT = NSA_HEADS * NSA_DK + MLA_HEADS * MLA_V
ODD_IN = SWA_HEADS * SWA_HD + 2 * SWA_KV_HEADS * SWA_HD
ODD_OUT = SWA_HEADS * SWA_HD

VMEM_LIMIT_BYTES = 48 * 1024 * 1024


def _row_tile(n, target=512):
    t = min(n, target)
    while n % t:
        t //= 2
    return t


def _norm_matmul_kernel(x_ref, g_ref, w_ref, o_ref):
    x = x_ref[...]
    y = x * lax.rsqrt(jnp.mean(x * x, axis=-1, keepdims=True) + EPS) * g_ref[...]
    o_ref[...] = jnp.dot(y.astype(jnp.bfloat16), w_ref[...], preferred_element_type=jnp.float32)


def norm_matmul(x, g, w):
    n, d = x.shape
    c = w.shape[1]
    tm = _row_tile(n)
    return pl.pallas_call(
        _norm_matmul_kernel,
        grid=(n // tm,),
        in_specs=[pl.BlockSpec((tm, d), lambda i: (i, 0)),
                  pl.BlockSpec((1, d), lambda i: (0, 0)),
                  pl.BlockSpec((d, c), lambda i: (0, 0))],
        out_specs=pl.BlockSpec((tm, c), lambda i: (i, 0)),
        out_shape=jax.ShapeDtypeStruct((n, c), jnp.float32),
        compiler_params=pltpu.CompilerParams(dimension_semantics=("arbitrary",),
                                             vmem_limit_bytes=VMEM_LIMIT_BYTES),
        name="norm_matmul",
    )(x, g.reshape(1, d), w.astype(jnp.bfloat16))


def _matmul_res_kernel(a_ref, w_ref, r_ref, o_ref):
    o_ref[...] = r_ref[...] + jnp.dot(a_ref[...].astype(jnp.bfloat16), w_ref[...],
                                      preferred_element_type=jnp.float32)


def matmul_res(a, w, res):
    n, k = a.shape
    c = w.shape[1]
    tm = _row_tile(n)
    return pl.pallas_call(
        _matmul_res_kernel,
        grid=(n // tm,),
        in_specs=[pl.BlockSpec((tm, k), lambda i: (i, 0)),
                  pl.BlockSpec((k, c), lambda i: (0, 0)),
                  pl.BlockSpec((tm, c), lambda i: (i, 0))],
        out_specs=pl.BlockSpec((tm, c), lambda i: (i, 0)),
        out_shape=jax.ShapeDtypeStruct((n, c), jnp.float32),
        compiler_params=pltpu.CompilerParams(dimension_semantics=("arbitrary",),
                                             vmem_limit_bytes=VMEM_LIMIT_BYTES),
        name="matmul_res",
    )(a, w.astype(jnp.bfloat16), res)


def rmsnorm(x, g):
    xf = x.astype(jnp.float32)
    y = xf * lax.rsqrt(jnp.mean(xf * xf, axis=-1, keepdims=True) + EPS)
    return (y * g.astype(jnp.float32)).astype(x.dtype)


def alibi_slopes(n_heads):
    return jnp.asarray((2.0 ** (-8.0 * np.arange(1, n_heads + 1) / n_heads)).astype(np.float32))


def rope(x, pos):
    d = x.shape[-1]
    inv = jnp.asarray(np.power(ROPE_BASE, -np.arange(0, d, 2, dtype=np.float32) / d).astype(np.float32))
    ang = pos.astype(jnp.float32)[:, None] * inv[None, :]
    cos = jnp.cos(ang)[None, :, None, :]
    sin = jnp.sin(ang)[None, :, None, :]
    xf = x.astype(jnp.float32)
    x1, x2 = xf[..., : d // 2], xf[..., d // 2:]
    return jnp.concatenate([x1 * cos - x2 * sin, x1 * sin + x2 * cos], axis=-1).astype(x.dtype)


def masked_softmax(s, valid, sink=None):
    s = jnp.where(valid, s, NEG_INF)
    m = jnp.max(s, axis=-1, keepdims=True)
    if sink is not None:
        sk = sink.astype(jnp.float32)[:, None, None]
        m = jnp.maximum(m, sk)
    e = jnp.where(valid, jnp.exp(s - m), 0.0)
    den = jnp.sum(e, axis=-1, keepdims=True)
    if sink is not None:
        den = den + jnp.exp(sk - m)
    return e / jnp.maximum(den, 1e-30)


def attend(q, k, v, dist, valid, slopes, sink=None):
    B, Q, H, d = q.shape
    G = k.shape[-2]
    qg = q.reshape(B, Q, G, H // G, d)
    if k.ndim == 5:
        s = jnp.einsum("bqgrd,bqkgd->bgrqk", qg, k)
    else:
        s = jnp.einsum("bqgrd,bkgd->bgrqk", qg, k)
    K = s.shape[-1]
    s = s.reshape(B, H, Q, K).astype(jnp.float32) * (d ** -0.5)
    if slopes is not None:
        s = s - slopes[:, None, None] * dist[..., None, :, :].astype(jnp.float32)
    p = masked_softmax(s, valid[..., None, :, :], sink)
    pg = p.reshape(B, G, H // G, Q, K).astype(v.dtype)
    if v.ndim == 5:
        o = jnp.einsum("bgrqk,bqkgd->bqgrd", pg, v)
    else:
        o = jnp.einsum("bgrqk,bkgd->bqgrd", pg, v)
    return o.reshape(B, Q, H, v.shape[-1]), p


def alibi_slopes_np(n_heads):
    return (2.0 ** (-8.0 * np.arange(1, n_heads + 1) / n_heads)).astype(np.float32)


Q_TILE = 128
SLC_CHUNK = 512
MASK_BIG = -(2.0 ** 100)


def _nsa_prompt_kernel(q_ref, kt_ref, vs_ref, kwt_ref, vw_ref, kct_ref, vc_ref, g_ref, slope_ref, wbias_ref, cbias_ref,
                       o_ref, *, n_blocks):
    H, TQ = NSA_HEADS, Q_TILE
    i = pl.program_id(1)
    s0 = i * TQ
    q_lo = q_ref[0].reshape(H * TQ, 128)
    slope = slope_ref[...]

    s = jnp.dot(q_lo, kct_ref[0], preferred_element_type=jnp.float32).reshape(H, TQ, n_blocks)
    tpos = s0 + lax.broadcasted_iota(jnp.int32, (TQ, n_blocks), 0)
    blk = lax.broadcasted_iota(jnp.int32, (TQ, n_blocks), 1)
    dist = tpos - ((blk + 1) * NSA_BLK - 1)
    valid = (dist >= 0)[None]
    s = jnp.where(valid, s - slope * dist.astype(jnp.float32)[None], NEG_INF)
    m = jnp.max(s, axis=-1, keepdims=True)
    e = jnp.where(valid, jnp.exp(s - m), 0.0)
    p = e / jnp.maximum(jnp.sum(e, axis=-1, keepdims=True), 1e-30)
    o_cmp = jnp.dot(p.reshape(H * TQ, n_blocks).astype(jnp.bfloat16), vc_ref[0],
                    preferred_element_type=jnp.float32)
    imp = jnp.sum(p, axis=0)

    cur = tpos // NSA_BLK
    imp = jnp.where(blk == cur, float(H + 1), jnp.where(blk < cur, imp, -1.0))
    blk_f = blk.astype(jnp.float32)
    sel = jnp.zeros((TQ, n_blocks), jnp.bool_)
    for _ in range(min(NSA_TOPN, n_blocks)):
        mx = jnp.max(imp, axis=-1, keepdims=True)
        pick = jnp.min(jnp.where(imp == mx, blk_f, float(n_blocks)), axis=-1, keepdims=True)
        hit = blk_f == pick
        sel = sel | hit
        imp = jnp.where(hit, -2.0, imp)
    nsb = jnp.where(sel, 0.0, MASK_BIG).astype(jnp.bfloat16)
    q_aug = jnp.concatenate([q_lo, jnp.broadcast_to(nsb[None], (H, TQ, n_blocks)).reshape(H * TQ, n_blocks)], axis=1)

    KC = SLC_CHUNK

    def slc_step(c, carry, bias):
        m_i, l_i, acc = carry
        k0 = pl.multiple_of(c * KC, KC)
        sc = jnp.dot(q_aug, kt_ref[0, :, pl.ds(k0, KC)], preferred_element_type=jnp.float32)
        if bias is not None:
            sc = (sc.reshape(H, TQ, KC) + bias[None]).reshape(H * TQ, KC)
        m_new = jnp.maximum(m_i, jnp.max(sc, axis=-1, keepdims=True))
        corr = jnp.exp(m_i - m_new)
        pe = jnp.exp(sc - m_new)
        l_new = l_i * corr + jnp.sum(pe, axis=-1, keepdims=True)
        acc_new = acc * corr + jnp.dot(pe.astype(jnp.bfloat16), vs_ref[0, pl.ds(k0, KC), :],
                                       preferred_element_type=jnp.float32)
        return m_new, l_new, acc_new

    init = (jnp.full((H * TQ, 1), NEG_INF, jnp.float32), jnp.zeros((H * TQ, 1), jnp.float32),
            jnp.zeros((H * TQ, NSA_DK), jnp.float32))
    c_diag = s0 // KC
    carry = lax.fori_loop(0, c_diag, lambda c, cr: slc_step(c, cr, None), init)
    m_i, l_i, acc = slc_step(c_diag, carry, cbias_ref[(s0 % KC) // TQ])
    o_slc = acc / l_i

    nwc = NSA_WIN // TQ + 1
    parts = []
    for r in range(nwc):
        cidx = i - (nwc - 1) + r
        k0 = pl.multiple_of(jnp.maximum(cidx, 0) * TQ, TQ)
        sw = jnp.dot(q_lo[:, :NSA_DK], kwt_ref[0, :, pl.ds(k0, TQ)], preferred_element_type=jnp.float32)
        b = jnp.where(cidx >= 0, wbias_ref[r], NEG_INF)
        parts.append((sw.reshape(H, TQ, TQ) + b).reshape(H * TQ, TQ))
    mw = parts[0].max(axis=-1, keepdims=True)
    for r in range(1, nwc):
        mw = jnp.maximum(mw, parts[r].max(axis=-1, keepdims=True))
    lw = jnp.zeros((H * TQ, 1), jnp.float32)
    accw = jnp.zeros((H * TQ, NSA_DK), jnp.float32)
    for r in range(nwc):
        cidx = i - (nwc - 1) + r
        k0 = pl.multiple_of(jnp.maximum(cidx, 0) * TQ, TQ)
        pe = jnp.exp(parts[r] - mw)
        lw = lw + jnp.sum(pe, axis=-1, keepdims=True)
        accw = accw + jnp.dot(pe.astype(jnp.bfloat16), vw_ref[0, pl.ds(k0, TQ), :], preferred_element_type=jnp.float32)
    o_win = accw / lw

    g = g_ref[0].reshape(H * TQ, 3)
    o = g[:, 0:1] * o_cmp + g[:, 1:2] * o_slc + g[:, 2:3] * o_win
    o_ref[0] = o.reshape(H, TQ, NSA_DK)


def nsa_prompt_attention(q, slc_k, slc_v, win_k, win_v, kc, vc, gates):
    B, T, H, dk = q.shape
    nb = T // NSA_BLK
    nbp = 128
    TQ = Q_TILE
    assert nb <= nbp and T % SLC_CHUNK == 0 and NSA_WIN % TQ == 0
    slopes = alibi_slopes_np(H)
    bf = jnp.bfloat16
    qh = jnp.transpose(q, (0, 2, 1, 3)) * (dk ** -0.5)
    al = np.zeros((H, 128 - dk), np.float32)
    al[:, 0] = slopes * 128.0
    al[:, 1] = slopes
    q_lo = jnp.concatenate([qh, jnp.broadcast_to(jnp.asarray(al)[None, :, None, :], (B, H, T, 128 - dk))],
                           axis=-1).astype(bf)
    pos = np.arange(T)
    crow = np.zeros((128 - dk + nbp, T), np.float32)
    crow[0] = pos // 128
    crow[1] = pos % 128
    crow[128 - dk + pos // NSA_BLK, pos] = 1.0
    kt = jnp.concatenate([jnp.transpose(slc_k, (0, 2, 1)),
                          jnp.broadcast_to(jnp.asarray(crow)[None], (B,) + crow.shape)], axis=1).astype(bf)
    kwt = jnp.transpose(win_k, (0, 2, 1)).astype(bf)
    kct = jnp.pad(jnp.transpose(kc, (0, 2, 1)), ((0, 0), (0, 128 - dk), (0, nbp - nb))).astype(bf)
    vcp = jnp.pad(vc, ((0, 0), (0, nbp - nb), (0, 0))).astype(bf)
    gh = jnp.transpose(gates, (0, 2, 1, 3))
    nwc = NSA_WIN // TQ + 1
    ii = np.arange(TQ)[:, None]
    jj = np.arange(TQ)[None, :]
    wb = np.zeros((nwc, H, TQ, TQ), np.float32)
    for r in range(nwc):
        d = ii - jj + TQ * (nwc - 1 - r)
        ok = (d >= 0) & (d <= NSA_WIN)
        wb[r] = np.where(ok[None], -slopes[:, None, None] * d[None].astype(np.float32), NEG_INF)
    nv = SLC_CHUNK // TQ
    cb = np.zeros((nv, TQ, SLC_CHUNK), np.float32)
    for v in range(nv):
        cb[v] = np.where(np.arange(SLC_CHUNK)[None, :] <= v * TQ + ii, 0.0, NEG_INF)
    kern = functools.partial(_nsa_prompt_kernel, n_blocks=nbp)
    return pl.pallas_call(
        kern,
        grid=(B, T // TQ),
        in_specs=[pl.BlockSpec((1, H, TQ, 128), lambda b, i: (b, 0, i, 0)),
                  pl.BlockSpec((1, 256, T), lambda b, i: (b, 0, 0)),
                  pl.BlockSpec((1, T, dk), lambda b, i: (b, 0, 0)),
                  pl.BlockSpec((1, dk, T), lambda b, i: (b, 0, 0)),
                  pl.BlockSpec((1, T, dk), lambda b, i: (b, 0, 0)),
                  pl.BlockSpec((1, 128, nbp), lambda b, i: (b, 0, 0)),
                  pl.BlockSpec((1, nbp, dk), lambda b, i: (b, 0, 0)),
                  pl.BlockSpec((1, H, TQ, 3), lambda b, i: (b, 0, i, 0)),
                  pl.BlockSpec((H, 1, 1), lambda b, i: (0, 0, 0)),
                  pl.BlockSpec((nwc, H, TQ, TQ), lambda b, i: (0, 0, 0, 0)),
                  pl.BlockSpec((nv, TQ, SLC_CHUNK), lambda b, i: (0, 0, 0))],
        out_specs=pl.BlockSpec((1, H, TQ, dk), lambda b, i: (b, 0, i, 0)),
        out_shape=jax.ShapeDtypeStruct((B, H, T, dk), jnp.float32),
        compiler_params=pltpu.CompilerParams(dimension_semantics=("arbitrary", "arbitrary"),
                                             vmem_limit_bytes=VMEM_LIMIT_BYTES),
        name="nsa_prompt",
    )(q_lo, kt, slc_v.astype(bf), kwt, win_v.astype(bf), kct, vcp, gh,
      jnp.asarray(slopes).reshape(H, 1, 1), jnp.asarray(wb), jnp.asarray(cb))


MLA_TILE = 512


def _mla_prompt_kernel(q_ref, kt_ref, v_ref, tril_ref, o_ref, *, scale_log2e):
    TQ = MLA_TILE
    i = pl.program_id(2)
    q = q_ref[0, 0]

    def step(c, carry, bias):
        m_i, l_i, acc = carry
        k0 = pl.multiple_of(c * TQ, TQ)
        s = jnp.dot(q, kt_ref[0, 0, :, pl.ds(k0, TQ)], preferred_element_type=jnp.float32)
        if bias is not None:
            s = s + bias
        m_new = jnp.maximum(m_i, jnp.max(s, axis=-1, keepdims=True))
        corr = jnp.exp2((m_i - m_new) * scale_log2e)
        p = jnp.exp2((s - m_new) * scale_log2e)
        l_new = l_i * corr + jnp.sum(p, axis=-1, keepdims=True)
        acc_new = acc * corr + jnp.dot(p.astype(jnp.bfloat16), v_ref[0, 0, pl.ds(k0, TQ), :],
                                       preferred_element_type=jnp.float32)
        return m_new, l_new, acc_new

    init = (jnp.full((TQ, 1), NEG_INF, jnp.float32), jnp.zeros((TQ, 1), jnp.float32),
            jnp.zeros((TQ, MLA_V), jnp.float32))
    carry = lax.fori_loop(0, i, lambda c, cr: step(c, cr, None), init)
    m_i, l_i, acc = step(i, carry, tril_ref[...])
    o_ref[0, 0] = acc / l_i


def mla_prompt_attention(qn, qp, kn, kp, v):
    B, T, H, _ = qn.shape
    TQ = MLA_TILE
    assert T % TQ == 0
    bf = jnp.bfloat16
    dq = MLA_NOPE + MLA_ROPE
    qcat = jnp.concatenate([qn, qp, jnp.zeros((B, T, H, 128 - dq), qn.dtype)], axis=-1)
    qcat = jnp.transpose(qcat, (0, 2, 1, 3)).astype(bf)
    kcat = jnp.concatenate([kn, jnp.broadcast_to(kp[:, :, None, :], (B, T, H, MLA_ROPE)),
                            jnp.zeros((B, T, H, 128 - dq), kn.dtype)], axis=-1)
    kt = jnp.transpose(kcat, (0, 2, 3, 1)).astype(bf)
    vh = jnp.transpose(v, (0, 2, 1, 3)).astype(bf)
    tril = np.where(np.arange(TQ)[None, :] <= np.arange(TQ)[:, None], 0.0, NEG_INF).astype(np.float32)
    kern = functools.partial(_mla_prompt_kernel, scale_log2e=float(dq ** -0.5 * math.log2(math.e)))
    return pl.pallas_call(
        kern,
        grid=(B, H, T // TQ),
        in_specs=[pl.BlockSpec((1, 1, TQ, 128), lambda b, h, i: (b, h, i, 0)),
                  pl.BlockSpec((1, 1, 128, T), lambda b, h, i: (b, h, 0, 0)),
                  pl.BlockSpec((1, 1, T, MLA_V), lambda b, h, i: (b, h, 0, 0)),
                  pl.BlockSpec((TQ, TQ), lambda b, h, i: (0, 0))],
        out_specs=pl.BlockSpec((1, 1, TQ, MLA_V), lambda b, h, i: (b, h, i, 0)),
        out_shape=jax.ShapeDtypeStruct((B, H, T, MLA_V), jnp.float32),
        compiler_params=pltpu.CompilerParams(dimension_semantics=("arbitrary", "arbitrary", "arbitrary"),
                                             vmem_limit_bytes=VMEM_LIMIT_BYTES),
        name="mla_prompt",
    )(qcat, kt, vh, jnp.asarray(tril))


def _swa_prompt_kernel(q_ref, kt_ref, v_ref, bias_ref, sink_ref, o_ref):
    TQ = Q_TILE
    R = SWA_HEADS // SWA_KV_HEADS
    i = pl.program_id(2)
    q = q_ref[0, 0].reshape(R * TQ, SWA_HD)
    kprev = pl.multiple_of(jnp.maximum(i - 1, 0) * TQ, TQ)
    kcur = pl.multiple_of(i * TQ, TQ)
    s0 = jnp.dot(q, kt_ref[0, 0, :, pl.ds(kprev, TQ)], preferred_element_type=jnp.float32).reshape(R, TQ, TQ)
    s1 = jnp.dot(q, kt_ref[0, 0, :, pl.ds(kcur, TQ)], preferred_element_type=jnp.float32).reshape(R, TQ, TQ)
    s0 = s0 + jnp.where(i > 0, bias_ref[0, 0], NEG_INF)
    s1 = s1 + bias_ref[0, 1]
    sink = sink_ref[0]
    m = jnp.maximum(jnp.maximum(s0.max(axis=-1, keepdims=True), s1.max(axis=-1, keepdims=True)), sink)
    p0 = jnp.exp(s0 - m)
    p1 = jnp.exp(s1 - m)
    den = p0.sum(axis=-1, keepdims=True) + p1.sum(axis=-1, keepdims=True) + jnp.exp(sink - m)
    o = (jnp.dot(p0.reshape(R * TQ, TQ).astype(jnp.bfloat16), v_ref[0, 0, pl.ds(kprev, TQ), :],
                 preferred_element_type=jnp.float32)
         + jnp.dot(p1.reshape(R * TQ, TQ).astype(jnp.bfloat16), v_ref[0, 0, pl.ds(kcur, TQ), :],
                   preferred_element_type=jnp.float32))
    o_ref[0, 0] = o.reshape(R, TQ, SWA_HD) / jnp.maximum(den, 1e-30)


def swa_prompt_attention(q, k, v, sinks):
    B, T, H, hd = q.shape
    G = k.shape[2]
    R = H // G
    TQ = Q_TILE
    assert SWA_WIN == TQ and T % TQ == 0
    bf = jnp.bfloat16
    slopes = alibi_slopes_np(H)
    qh = (jnp.transpose(q, (0, 2, 1, 3)) * (hd ** -0.5)).astype(bf).reshape(B, G, R, T, hd)
    kt = jnp.transpose(k, (0, 2, 3, 1)).astype(bf)
    vh = jnp.transpose(v, (0, 2, 1, 3)).astype(bf)
    ii = np.arange(TQ)[:, None]
    jj = np.arange(TQ)[None, :]
    bias = np.zeros((G, 2, R, TQ, TQ), np.float32)
    for r in range(2):
        d = ii - jj + TQ * (1 - r)
        ok = (d >= 0) & (d <= SWA_WIN)
        bias[:, r] = np.where(ok[None, None],
                              -slopes.reshape(G, R)[:, :, None, None] * d[None, None].astype(np.float32), NEG_INF)
    return pl.pallas_call(
        _swa_prompt_kernel,
        grid=(B, G, T // TQ),
        in_specs=[pl.BlockSpec((1, 1, R, TQ, hd), lambda b, g, i: (b, g, 0, i, 0)),
                  pl.BlockSpec((1, 1, hd, T), lambda b, g, i: (b, g, 0, 0)),
                  pl.BlockSpec((1, 1, T, hd), lambda b, g, i: (b, g, 0, 0)),
                  pl.BlockSpec((1, 2, R, TQ, TQ), lambda b, g, i: (g, 0, 0, 0, 0)),
                  pl.BlockSpec((1, R, 1, 1), lambda b, g, i: (g, 0, 0, 0))],
        out_specs=pl.BlockSpec((1, 1, R, TQ, hd), lambda b, g, i: (b, g, 0, i, 0)),
        out_shape=jax.ShapeDtypeStruct((B, G, R, T, hd), jnp.float32),
        compiler_params=pltpu.CompilerParams(dimension_semantics=("arbitrary", "arbitrary", "arbitrary"),
                                             vmem_limit_bytes=VMEM_LIMIT_BYTES),
        name="swa_prompt",
    )(qh, kt, vh, jnp.asarray(bias), sinks.reshape(G, R, 1, 1)).reshape(B, H, T, hd)


def window_sample(q, k_buf, v_buf, k_new, v_new, past_len, window, slopes, sink=None):
    Wb, Tn = k_buf.shape[1], q.shape[1]
    k = jnp.concatenate([k_buf, k_new], axis=1)
    v = jnp.concatenate([v_buf, v_new], axis=1)
    qpos = past_len + jnp.arange(Tn)
    kpos = past_len - Wb + jnp.arange(Wb + Tn)
    dist = qpos[:, None] - kpos[None, :]
    valid = (dist >= 0) & (dist <= window)
    return attend(q, k, v, dist, valid, slopes, sink)[0]


def nsa_compress(blocks, pos_emb, w1, w2):
    x = blocks + pos_emb
    h = jax.nn.gelu(jnp.einsum("...lcd,lcdh->...ch", x, w1))
    return jnp.einsum("...ch,chd->...cd", h, w2)


def nsa_select(p_cmp, qpos, n_blocks):
    imp = jnp.sum(p_cmp, axis=1)
    blk = jnp.arange(n_blocks)[None, :]
    cur = (qpos // NSA_BLK)[:, None]
    imp = jnp.where(blk == cur, float(NSA_HEADS + 1), jnp.where(blk < cur, imp, -1.0))
    _, idx = lax.top_k(imp, min(NSA_TOPN, n_blocks))
    return idx


def nsa_compressed_branch(q, qpos, kc, vc, c_end, slopes):
    dist = qpos[:, None] - c_end[None, :]
    o, p = attend(q, kc[:, :, None, :], vc[:, :, None, :], dist, dist >= 0, slopes)
    return o, nsa_select(p, qpos, kc.shape[1])


def nsa_selected_branch(q, qpos, sel, idx, slopes):
    B, Q, n = idx.shape
    kpos = (idx[..., None] * NSA_BLK + jnp.arange(NSA_BLK)).reshape(B, Q, n * NSA_BLK)
    dist = qpos[None, :, None] - kpos
    ks = sel[..., 0, :].reshape(B, Q, n * NSA_BLK, 1, NSA_DK)
    vs = sel[..., 1, :].reshape(B, Q, n * NSA_BLK, 1, NSA_DK)
    return attend(q, ks, vs, dist, dist >= 0, slopes)[0]


def nsa_gate(gates, o_cmp, o_slc, o_win):
    g = gates.astype(o_cmp.dtype)
    return g[..., 0:1] * o_cmp + g[..., 1:2] * o_slc + g[..., 2:3] * o_win


CMP_PAGES_PER_STEP = 64
CMP_CHUNK = 1024


def _nsa_compress_kernel(pt_ref, pos_ref, w1t_ref, w2_ref, *rest, n_pages):
    page_refs = rest[:n_pages]
    o_ref, slab = rest[n_pages:]
    rpp = page_refs[0].shape[1]
    nchunk = rpp // 2
    nblk = 2 * n_pages
    nlt = CMP_CHUNK // 128
    for j, r in enumerate(page_refs):
        page = r[0]
        for c in range(nlt):
            slab[c, j * rpp:(j + 1) * rpp, :] = page[:, c * 128:(c + 1) * 128]
    nt = (((1,), (1,)), ((), ()))
    ht = jnp.zeros((w1t_ref.shape[0], nblk), jnp.float32)
    for k in range(nchunk):
        xk = jnp.concatenate([slab[c, pl.ds(k, nblk, stride=nchunk), :] for c in range(nlt)], axis=1) + pos_ref[k:k + 1, :]
        ht = ht + lax.dot_general(w1t_ref[:, k * CMP_CHUNK:(k + 1) * CMP_CHUNK], xk.astype(jnp.bfloat16), nt,
                                  preferred_element_type=jnp.float32)
    h = jax.nn.gelu(ht).T.astype(jnp.bfloat16)
    o_ref[0] = jnp.dot(h, w2_ref[...], preferred_element_type=jnp.float32)


def nsa_compress_paged(pages, page_table, pos_emb, w1, w2):
    S, P = page_table.shape
    PG = math.gcd(CMP_PAGES_PER_STEP, P)
    bf = jnp.bfloat16
    blk_elems = NSA_BLK * 2 * NSA_DK
    rpp = 2 * blk_elems // CMP_CHUNK
    assert PAGE_SIZE == 2 * NSA_BLK and (2 * PG) % 8 == 0
    pv = pages.reshape(pages.shape[0], rpp, CMP_CHUNK)
    pos = pos_emb.reshape(blk_elems // CMP_CHUNK, CMP_CHUNK)
    eye = jnp.eye(2, dtype=w1.dtype)
    w1t = jnp.einsum("lcdh,ce->chled", w1, eye).reshape(2 * NSA_CMP_HID, blk_elems).astype(bf)
    w2z = jnp.einsum("chd,ce->ched", w2, eye).reshape(2 * NSA_CMP_HID, 2 * NSA_DK).astype(bf)

    def page_spec(j):
        return pl.BlockSpec((1, rpp, CMP_CHUNK), lambda s, g, pt: (pt[s, g * PG + j], 0, 0))

    grid_spec = pltpu.PrefetchScalarGridSpec(
        num_scalar_prefetch=1,
        grid=(S, P // PG),
        in_specs=[pl.BlockSpec(pos.shape, lambda s, g, pt: (0, 0)),
                  pl.BlockSpec(w1t.shape, lambda s, g, pt: (0, 0)),
                  pl.BlockSpec(w2z.shape, lambda s, g, pt: (0, 0))] + [page_spec(j) for j in range(PG)],
        out_specs=pl.BlockSpec((1, 2 * PG, 2 * NSA_DK), lambda s, g, pt: (s, g, 0)),
        scratch_shapes=[pltpu.VMEM((CMP_CHUNK // 128, PG * rpp, 128), jnp.float32)])
    out = pl.pallas_call(
        functools.partial(_nsa_compress_kernel, n_pages=PG),
        grid_spec=grid_spec,
        out_shape=jax.ShapeDtypeStruct((S, 2 * P, 2 * NSA_DK), jnp.float32),
        compiler_params=pltpu.CompilerParams(dimension_semantics=("arbitrary", "arbitrary"),
                                             vmem_limit_bytes=VMEM_LIMIT_BYTES),
        name="nsa_compress",
    )(page_table, pos, w1t, w2z, *([pv] * PG))
    return out.reshape(S, 2 * P, 2, NSA_DK)


def nsa_prompt(q, cmp_kv, slc_kv, win_kv, gates, ep, slopes):
    B, T = q.shape[:2]
    nb = T // NSA_BLK
    ppb = T // PAGE_SIZE
    prompt_pages = jnp.arange(B * ppb, dtype=jnp.int32).reshape(B, ppb)
    comp = nsa_compress_paged(cmp_kv.reshape(B * ppb, PAGE_SIZE, 2, NSA_DK), prompt_pages,
                              ep["cmp_pos"], ep["cmp_w1"], ep["cmp_w2"])
    kc = rmsnorm(comp[:, :, 0], ep["g_k"][0])
    vc = comp[:, :, 1]
    o = nsa_prompt_attention(q, slc_kv[:, :, 0], slc_kv[:, :, 1], win_kv[:, :, 0], win_kv[:, :, 1], kc, vc, gates)
    return jnp.transpose(o, (0, 2, 1, 3))


def _nsa_sample_selected_kernel(phys_ref, idx_ref, q_ref, new_ref, slope_ref, *rest, n_sel, nb_past, qpos):
    blk_refs = rest[:n_sel]
    o_ref, slab = rest[n_sel:]
    b = pl.program_id(0)
    nt = (((1,), (1,)), ((), ()))
    row = lax.broadcasted_iota(jnp.int32, (NSA_BLK, 2 * NSA_DK), 0)
    new_blk = jnp.where(row == 0, jnp.broadcast_to(new_ref[0, 0:1, :], (NSA_BLK, 2 * NSA_DK)), 0.0)
    lane = lax.broadcasted_iota(jnp.int32, (1, n_sel * NSA_BLK), 1)
    dist = jnp.zeros((1, n_sel * NSA_BLK), jnp.int32)
    for n, r in enumerate(blk_refs):
        blk_id = idx_ref[b, n]
        slab[n * NSA_BLK:(n + 1) * NSA_BLK, :] = jnp.where(blk_id >= nb_past, new_blk, r[0]).astype(jnp.bfloat16)
        dist = jnp.where(lane // NSA_BLK == n, qpos - blk_id * NSA_BLK - (lane - n * NSA_BLK), dist)
    kv = slab[...]
    s = lax.dot_general(q_ref[0], kv, nt, preferred_element_type=jnp.float32)
    s = jnp.where(dist >= 0, s - slope_ref[...] * dist.astype(jnp.float32), NEG_INF)
    m = jnp.max(s, axis=-1, keepdims=True)
    e = jnp.where(dist >= 0, jnp.exp(s - m), 0.0)
    p = e / jnp.maximum(jnp.sum(e, axis=-1, keepdims=True), 1e-30)
    o_ref[0] = jnp.dot(p.astype(jnp.bfloat16), kv, preferred_element_type=jnp.float32)


def nsa_sample_selected(q, idx, slc_new, cache_slc, page_table, e, qpos):
    DB, Tn, H, dk = q.shape
    n_sel = idx.shape[-1]
    assert Tn == 1
    P = page_table.shape[1]
    bpp = PAGE_SIZE // NSA_BLK
    nb_past = P * bpp
    bf = jnp.bfloat16
    idx2 = idx[:, 0].astype(jnp.int32)
    jp = jnp.minimum(idx2, nb_past - 1)
    phys = (jnp.take_along_axis(page_table, jp // bpp, axis=1) + e * cache_slc.shape[1]) * bpp + jp % bpp
    qpad = jnp.concatenate([q[:, 0] * (dk ** -0.5), jnp.zeros((DB, H, dk), q.dtype)], axis=-1).astype(bf)
    newp = jnp.pad(slc_new.reshape(DB, 1, 2 * dk), ((0, 0), (0, 7), (0, 0)))
    blocks = cache_slc.reshape(-1, NSA_BLK, 2 * dk)
    kern = functools.partial(_nsa_sample_selected_kernel, n_sel=n_sel, nb_past=nb_past, qpos=int(qpos))

    def blk_spec(n):
        return pl.BlockSpec((1, NSA_BLK, 2 * dk), lambda b, ph, ix: (ph[b, n], 0, 0))

    grid_spec = pltpu.PrefetchScalarGridSpec(
        num_scalar_prefetch=2,
        grid=(DB,),
        in_specs=[pl.BlockSpec((1, H, 2 * dk), lambda b, ph, ix: (b, 0, 0)),
                  pl.BlockSpec((1, 8, 2 * dk), lambda b, ph, ix: (b, 0, 0)),
                  pl.BlockSpec((H, 1), lambda b, ph, ix: (0, 0))] + [blk_spec(n) for n in range(n_sel)],
        out_specs=pl.BlockSpec((1, H, 2 * dk), lambda b, ph, ix: (b, 0, 0)),
        scratch_shapes=[pltpu.VMEM((n_sel * NSA_BLK, 2 * dk), bf)])
    out = pl.pallas_call(
        kern,
        grid_spec=grid_spec,
        out_shape=jax.ShapeDtypeStruct((DB, H, 2 * dk), jnp.float32),
        compiler_params=pltpu.CompilerParams(dimension_semantics=("arbitrary",), vmem_limit_bytes=VMEM_LIMIT_BYTES),
        name="nsa_sample_selected",
    )(phys, idx2, qpad, newp, jnp.asarray(alibi_slopes_np(H)).reshape(H, 1), *([blocks] * n_sel))
    return out[:, None, :, dk:]


def nsa_sample(q, qpos, cmp_new, slc_new, win_new, gates, cache_cmp, cache_slc, win_buf, page_table, e, ep, slopes):
    DB, Tn = q.shape[:2]
    n_pages = page_table.shape[1]
    bpp = PAGE_SIZE // NSA_BLK
    nb_past = n_pages * bpp
    past_len = n_pages * PAGE_SIZE
    nb_new = -(-Tn // NSA_BLK)
    pad = ((0, 0), (0, nb_new * NSA_BLK - Tn), (0, 0), (0, 0))

    def compress(rows):
        return nsa_compress(rows, ep["cmp_pos"], ep["cmp_w1"], ep["cmp_w2"])

    comp_past = nsa_compress_paged(cache_cmp.reshape(-1, PAGE_SIZE, 2, NSA_DK), page_table + e * cache_cmp.shape[1],
                                   ep["cmp_pos"], ep["cmp_w1"], ep["cmp_w2"])
    comp_new = compress(jnp.pad(cmp_new, pad).reshape(DB, nb_new, NSA_BLK, 2, NSA_DK))
    comp = jnp.concatenate([comp_past, comp_new], axis=1)
    kc = rmsnorm(comp[:, :, 0], ep["g_k"][0])
    vc = comp[:, :, 1]
    c_end = (jnp.arange(nb_past + nb_new) + 1) * NSA_BLK - 1
    o_cmp, idx = nsa_compressed_branch(q, qpos, kc, vc, c_end, slopes)

    o_slc = nsa_sample_selected(q, idx, slc_new, cache_slc, page_table, e, past_len)

    o_win = window_sample(q, win_buf[:, :, 0:1], win_buf[:, :, 1:2], win_new[:, :, 0:1], win_new[:, :, 1:2],
                          past_len, NSA_WIN, slopes)
    return nsa_gate(gates, o_cmp, o_slc, o_win)


def mla_keys(lat, ep):
    c = lat[..., :MLA_LORA]
    kp = lat[..., MLA_LORA:]
    kn = rmsnorm(jnp.einsum("bkc,chd->bkhd", c, ep["w_uk"]), ep["g_kn"])
    v = jnp.einsum("bkc,chd->bkhd", c, ep["w_uv"])
    return kn, kp, v


def mla_scores(qn, qp, kn, kp):
    s = jnp.einsum("bqhd,bkhd->bhqk", qn, kn) + jnp.einsum("bqhd,bkd->bhqk", qp, kp)
    return s.astype(jnp.float32) * ((MLA_NOPE + MLA_ROPE) ** -0.5)


def mla_prompt(qn, qp, lat, ep):
    B, T = qn.shape[:2]
    kn, kp, v = mla_keys(lat, ep)
    return jnp.transpose(mla_prompt_attention(qn, qp, kn, kp, v), (0, 2, 1, 3))


MLA_PAGES_PER_STEP = 16


def _mla_sample_kernel(pt_ref, qg_ref, qp_ref, wukt_ref, wuk_ref, wuv_ref, gt_ref, new_ref, *rest, scale, n_pages):
    page_refs = rest[:n_pages]
    o_ref, c_scr, kp_scr, a_scr, m_scr, l_scr, acc_scr = rest[n_pages:]
    g = pl.program_id(1)
    ng = pl.num_programs(1)
    nt = (((1,), (1,)), ((), ()))

    @pl.when(g == 0)
    def _():
        a_scr[...] = jnp.dot(qg_ref[0], wukt_ref[...], preferred_element_type=jnp.float32).astype(jnp.bfloat16)
        m_scr[...] = jnp.full(m_scr.shape, NEG_INF, jnp.float32)
        l_scr[...] = jnp.zeros(l_scr.shape, jnp.float32)
        acc_scr[...] = jnp.zeros(acc_scr.shape, jnp.float32)

    def scores(c, kp):
        proj = jnp.dot(c, wuk_ref[...], preferred_element_type=jnp.float32)
        ss = lax.dot_general(gt_ref[...], (proj * proj).astype(jnp.bfloat16), nt, preferred_element_type=jnp.float32)
        num = lax.dot_general(a_scr[...], c, nt, preferred_element_type=jnp.float32)
        sp = lax.dot_general(qp_ref[0], kp, nt, preferred_element_type=jnp.float32)
        return (num * lax.rsqrt(ss * (1.0 / MLA_NOPE) + EPS) + sp) * scale

    def softmax_update(c, st):
        m_i = m_scr[...]
        m_new = jnp.maximum(m_i, jnp.max(st, axis=-1, keepdims=True))
        corr = jnp.exp(m_i - m_new)
        p = jnp.exp(st - m_new)
        m_scr[...] = m_new
        l_scr[...] = l_scr[...] * corr + jnp.sum(p, axis=-1, keepdims=True)
        acc_scr[...] = acc_scr[...] * corr + jnp.dot(p.astype(jnp.bfloat16), c, preferred_element_type=jnp.float32)

    for j, r in enumerate(page_refs):
        page = r[0]
        c_scr[j * PAGE_SIZE:(j + 1) * PAGE_SIZE, :] = page[:, :MLA_LORA].astype(jnp.bfloat16)
        kp_scr[j * PAGE_SIZE:(j + 1) * PAGE_SIZE, :] = page[:, MLA_LORA:].astype(jnp.bfloat16)
    c_all = c_scr[...]
    softmax_update(c_all, scores(c_all, kp_scr[...]))

    @pl.when(g == ng - 1)
    def _():
        new = new_ref[0]
        c = new[:, :MLA_LORA].astype(jnp.bfloat16)
        st = scores(c, new[:, MLA_LORA:].astype(jnp.bfloat16))
        key = lax.broadcasted_iota(jnp.int32, st.shape, 1)
        softmax_update(c, jnp.where(key == 0, st, NEG_INF))
        o8 = jnp.dot((acc_scr[...] / l_scr[...]).astype(jnp.bfloat16), wuv_ref[...],
                     preferred_element_type=jnp.float32)
        row = lax.broadcasted_iota(jnp.int32, o8.shape, 0)
        col = lax.broadcasted_iota(jnp.int32, o8.shape, 1)
        o_ref[0] = jnp.sum(jnp.where(col // MLA_V == row, o8, 0.0), axis=0, keepdims=True)


def mla_sample(qn, qp, lat_new, cache_mla, page_table, e, ep):
    DB, Tn = qn.shape[:2]
    H = MLA_HEADS
    P = page_table.shape[1]
    PG = math.gcd(MLA_PAGES_PER_STEP, P)
    assert Tn == 1
    bf = jnp.bfloat16
    dq = MLA_NOPE + MLA_ROPE
    qg = qn[:, 0] * ep["g_kn"]
    eye = jnp.eye(H, dtype=qg.dtype)
    qg_exp = (qg[:, :, None, :] * eye[None, :, :, None]).reshape(DB, H, H * MLA_NOPE).astype(bf)
    qp_h = qp[:, 0].astype(bf)
    wuk = ep["w_uk"].reshape(MLA_LORA, H * MLA_NOPE)
    wuv = ep["w_uv"].reshape(MLA_LORA, H * MLA_V).astype(bf)
    gt = np.zeros((H, H * MLA_NOPE), np.float32)
    gt[np.arange(H * MLA_NOPE) // MLA_NOPE, np.arange(H * MLA_NOPE)] = 1.0
    new_page = jnp.pad(lat_new, ((0, 0), (0, 7), (0, 0)))
    kern = functools.partial(_mla_sample_kernel, scale=float(dq ** -0.5), n_pages=PG)

    def page_spec(j):
        return pl.BlockSpec((1, PAGE_SIZE, MLA_ROW), lambda b, g, pt: (pt[b, g * PG + j], 0, 0))

    const2 = lambda b, g, pt: (0, 0)
    grid_spec = pltpu.PrefetchScalarGridSpec(
        num_scalar_prefetch=1,
        grid=(DB, P // PG),
        in_specs=[pl.BlockSpec((1, H, H * MLA_NOPE), lambda b, g, pt: (b, 0, 0)),
                  pl.BlockSpec((1, H, MLA_ROPE), lambda b, g, pt: (b, 0, 0)),
                  pl.BlockSpec((H * MLA_NOPE, MLA_LORA), const2),
                  pl.BlockSpec((MLA_LORA, H * MLA_NOPE), const2),
                  pl.BlockSpec((MLA_LORA, H * MLA_V), const2),
                  pl.BlockSpec((H, H * MLA_NOPE), const2),
                  pl.BlockSpec((1, 8, MLA_ROW), lambda b, g, pt: (b, 0, 0))]
                 + [page_spec(j) for j in range(PG)],
        out_specs=pl.BlockSpec((1, 1, H * MLA_V), lambda b, g, pt: (b, 0, 0)),
        scratch_shapes=[pltpu.VMEM((PG * PAGE_SIZE, MLA_LORA), bf),
                        pltpu.VMEM((PG * PAGE_SIZE, MLA_ROPE), bf),
                        pltpu.VMEM((H, MLA_LORA), bf),
                        pltpu.VMEM((8, 1), jnp.float32),
                        pltpu.VMEM((8, 1), jnp.float32),
                        pltpu.VMEM((8, MLA_LORA), jnp.float32)])
    out = pl.pallas_call(
        kern,
        grid_spec=grid_spec,
        out_shape=jax.ShapeDtypeStruct((DB, 1, H * MLA_V), jnp.float32),
        compiler_params=pltpu.CompilerParams(dimension_semantics=("arbitrary", "arbitrary"),
                                             vmem_limit_bytes=VMEM_LIMIT_BYTES),
        name="mla_sample",
    )(page_table + e * cache_mla.shape[1], qg_exp, qp_h, wuk.T.astype(bf), wuk.astype(bf), wuv,
      jnp.asarray(gt).astype(bf), new_page, *([cache_mla.reshape(-1, PAGE_SIZE, MLA_ROW)] * PG))
    return out.reshape(DB, 1, H, MLA_V)


def even_project(x, g_norm, pos, ep):
    B, T, D = x.shape
    h = norm_matmul(x.reshape(B * T, D), g_norm, ep["w_in"]).reshape(B, T, EVEN_IN)
    cuts = np.cumsum([NSA_Q_COLS, NSA_KV_COLS, NSA_GATE_COLS, MLA_Q_COLS]).tolist()
    q, kv, g, mq, lat = jnp.split(h, cuts, axis=-1)
    q = rmsnorm(q.reshape(B, T, NSA_HEADS, NSA_DK), ep["g_q"])
    kv = kv.reshape(B, T, 3, 2, NSA_DK)
    cmp_kv = kv[:, :, 0]
    slc_kv = jnp.stack([rmsnorm(kv[:, :, 1, 0], ep["g_k"][1]), kv[:, :, 1, 1]], axis=2)
    win_kv = jnp.stack([rmsnorm(kv[:, :, 2, 0], ep["g_k"][2]), kv[:, :, 2, 1]], axis=2)
    gates = jax.nn.sigmoid(g.astype(jnp.float32)).reshape(B, T, NSA_HEADS, 3)
    mq = mq.reshape(B, T, MLA_HEADS, MLA_NOPE + MLA_ROPE)
    qn = rmsnorm(mq[..., :MLA_NOPE], ep["g_qn"])
    qp = rope(rmsnorm(mq[..., MLA_NOPE:], ep["g_qpe"]), pos)
    c = rmsnorm(lat[..., :MLA_LORA], ep["g_lat"])
    kp = rope(rmsnorm(lat[..., MLA_LORA:], ep["g_kpe"])[:, :, None, :], pos)[:, :, 0, :]
    return q, cmp_kv, slc_kv, win_kv, gates, qn, qp, jnp.concatenate([c, kp], axis=-1)


def even_output(x, o_nsa, o_mla, w_out):
    B, T, D = x.shape
    a = jnp.concatenate([o_nsa.reshape(B, T, -1), o_mla.reshape(B, T, -1)], axis=-1)
    return matmul_res(a.reshape(B * T, EVEN_OUT), w_out, x.reshape(B * T, D)).reshape(B, T, D)


def odd_project(x, g_norm, op):
    B, T, D = x.shape
    h = norm_matmul(x.reshape(B * T, D), g_norm, op["w_in"]).reshape(B, T, ODD_IN)
    q, k, v = jnp.split(h, [SWA_HEADS * SWA_HD, SWA_HEADS * SWA_HD + SWA_KV_HEADS * SWA_HD], axis=-1)
    q = rmsnorm(q.reshape(B, T, SWA_HEADS, SWA_HD), op["g_q"])
    k = rmsnorm(k.reshape(B, T, SWA_KV_HEADS, SWA_HD), op["g_k"])
    v = v.reshape(B, T, SWA_KV_HEADS, SWA_HD)
    return q, jnp.stack([k, v], axis=2)


def memory_kv(mem, cp):
    B, M, D = mem.shape
    w_kv = jnp.concatenate([cp["w_k"], cp["w_v"]], axis=1)
    kv = norm_matmul(mem.reshape(B * M, D), cp["g_mem"], w_kv).reshape(B, M, 2, MEM_HEADS, MEM_HD)
    k = rmsnorm(kv[:, :, 0], cp["g_k"])
    return jnp.stack([k, kv[:, :, 1]], axis=2)


def memory_cross(x, g_norm, mem_kv, cp):
    B, T, D = x.shape
    q = norm_matmul(x.reshape(B * T, D), g_norm, cp["w_q"]).reshape(B, T, MEM_HEADS, MEM_HD)
    q = rmsnorm(q, cp["g_q"])
    s = jnp.einsum("bthd,bmhd->bhtm", q, mem_kv[:, :, 0]).astype(jnp.float32) * (MEM_HD ** -0.5)
    p = jax.nn.softmax(s, axis=-1).astype(x.dtype)
    o = jnp.einsum("bhtm,bmhd->bthd", p, mem_kv[:, :, 1]).reshape(B * T, MEM_W)
    return matmul_res(o, cp["w_o"], x.reshape(B * T, D)).reshape(B, T, D)


PEER_TB = 128
PEER_VMEM_LIMIT_BYTES = 56 * 1024 * 1024
F32_NEG_INF = float("-inf")


def _topk_rows(s, row, k):
    nrow = float(s.shape[0])
    vals, ids = [], []
    for _ in range(k):
        m = jnp.max(s, axis=0, keepdims=True)
        i = jnp.min(jnp.where(s == m, row, nrow), axis=0, keepdims=True)
        vals.append(m)
        ids.append(i)
        s = jnp.where(row == i, F32_NEG_INF, s)
    return vals, ids


def _peer_route_kernel(x_ref, g_ref, wq_ref, sk0_ref, sk1_ref, xn_ref, idx_ref, gate_ref):
    x = x_ref[...]
    xn = x * lax.rsqrt(jnp.mean(x * x, axis=-1, keepdims=True) + EPS) * g_ref[...]
    xn_ref[...] = xn
    q = jnp.dot(xn.astype(jnp.bfloat16), wq_ref[...], preferred_element_type=jnp.float32)
    tb = x.shape[0]
    row_k = lax.broadcasted_iota(jnp.int32, (PEER_NKEYS, tb), 0).astype(jnp.float32)
    row_c = lax.broadcasted_iota(jnp.int32, (PEER_TOPK * PEER_TOPK, tb), 0).astype(jnp.float32)
    nt = (((1,), (1,)), ((), ()))
    ids, gates = [], []
    for h in range(PEER_HEADS):
        qh = q[:, h * PEER_DKEY:(h + 1) * PEER_DKEY].astype(jnp.bfloat16)
        s1 = lax.dot_general(sk0_ref[...], qh, nt, preferred_element_type=jnp.float32)
        s2 = lax.dot_general(sk1_ref[...], qh, nt, preferred_element_type=jnp.float32)
        v1, i1 = _topk_rows(s1, row_k, PEER_TOPK)
        v2, i2 = _topk_rows(s2, row_k, PEER_TOPK)
        v2c = jnp.concatenate(v2, axis=0)
        i2c = jnp.concatenate(i2, axis=0)
        cand = jnp.concatenate([v1[a] + v2c for a in range(PEER_TOPK)], axis=0)
        cid = jnp.concatenate([i1[a] * float(PEER_NKEYS) + i2c for a in range(PEER_TOPK)], axis=0)
        tops, tids = [], []
        for _ in range(PEER_TOPK):
            m = jnp.max(cand, axis=0, keepdims=True)
            pos = jnp.min(jnp.where(cand == m, row_c, float(PEER_TOPK * PEER_TOPK)), axis=0, keepdims=True)
            hit = row_c == pos
            tids.append(jnp.sum(jnp.where(hit, cid, 0.0), axis=0, keepdims=True))
            tops.append(m)
            cand = jnp.where(hit, F32_NEG_INF, cand)
        ts = jnp.concatenate(tops, axis=0)
        e = jnp.exp(ts - tops[0])
        gates.append(e / jnp.sum(e, axis=0, keepdims=True))
        ids.append(jnp.concatenate(tids, axis=0))
    table_rows = x.shape[1] // 256
    idx_ref[...] = (jnp.concatenate(ids, axis=0) * float(table_rows)).T.astype(jnp.int32)
    gate_ref[...] = jnp.concatenate(gates, axis=0).T


def peer_route(x, g, w_q, subkeys):
    n, d = x.shape
    tb = min(PEER_TB, n)
    hk = PEER_HEADS * PEER_TOPK
    half = PEER_DKEY // 2
    z = jnp.zeros((PEER_NKEYS, half), jnp.float32)
    sk0 = jnp.concatenate([subkeys[0], z], axis=1).astype(jnp.bfloat16)
    sk1 = jnp.concatenate([z, subkeys[1]], axis=1).astype(jnp.bfloat16)
    return pl.pallas_call(
        _peer_route_kernel,
        grid=(n // tb,),
        in_specs=[pl.BlockSpec((tb, d), lambda i: (i, 0)),
                  pl.BlockSpec((1, d), lambda i: (0, 0)),
                  pl.BlockSpec((d, PEER_HEADS * PEER_DKEY), lambda i: (0, 0)),
                  pl.BlockSpec((PEER_NKEYS, PEER_DKEY), lambda i: (0, 0)),
                  pl.BlockSpec((PEER_NKEYS, PEER_DKEY), lambda i: (0, 0))],
        out_specs=[pl.BlockSpec((tb, d), lambda i: (i, 0)),
                   pl.BlockSpec((tb, hk), lambda i: (i, 0)),
                   pl.BlockSpec((tb, hk), lambda i: (i, 0))],
        out_shape=[jax.ShapeDtypeStruct((n, d), jnp.float32),
                   jax.ShapeDtypeStruct((n, hk), jnp.int32),
                   jax.ShapeDtypeStruct((n, hk), jnp.float32)],
        compiler_params=pltpu.CompilerParams(dimension_semantics=("arbitrary",),
                                             vmem_limit_bytes=PEER_VMEM_LIMIT_BYTES),
        name="peer_route",
    )(x, g.reshape(1, d), w_q.astype(jnp.bfloat16), sk0, sk1)


def pack_table(t):
    e, d = t.shape
    b = lax.bitcast_convert_type(t.astype(jnp.bfloat16), jnp.uint16).astype(jnp.uint32)
    w = b[:, : d // 2] | (b[:, d // 2:] << 16)
    return w.reshape(e * d // 256, 128)


def _table_spec(tab):
    return pl.BlockSpec(tab.shape, lambda i: (0, 0), pipeline_mode=pl.Buffered(1))


def _gather_row(tab, row0, rows):
    wds = tab[pl.ds(pl.multiple_of(row0, rows), rows), :]
    lo = pltpu.bitcast(wds << 16, jnp.float32)
    hi = pltpu.bitcast(wds & jnp.uint32(0xFFFF0000), jnp.float32)
    return lo, hi


def _peer_u_kernel(idx_ref, xn_ref, gate_ref, tab, w_ref, slots, rsum, act):
    tb, hk = gate_ref.shape
    rows = xn_ref.shape[1] // 2

    def token(t, c):
        xt = xn_ref[t]
        xlo, xhi = xt[:rows], xt[rows:]
        for k in range(hk):
            lo, hi = _gather_row(tab, idx_ref[t, k], rows)
            slots[k * rows:(k + 1) * rows, :] = lo * xlo + hi * xhi
        r = slots[pl.ds(0, hk, stride=rows), :]
        for s in range(1, rows):
            r = r + slots[pl.ds(s, hk, stride=rows), :]
        rsum[pl.ds(pl.multiple_of(t * hk, hk), hk), :] = r
        return c

    lax.fori_loop(0, tb, token, 0)

    ones = jnp.ones((128, 128), jnp.bfloat16)
    grp = 8
    ri = lax.broadcasted_iota(jnp.int32, (grp * hk, 128), 0)
    ci = lax.broadcasted_iota(jnp.int32, (grp * hk, 128), 1)
    eye = (ri % hk) == ci

    def lane_sum(c, carry):
        rr = rsum[pl.ds(pl.multiple_of(c * grp * hk, grp * hk), grp * hk), :]
        hi = rr.astype(jnp.bfloat16)
        lo = (rr - hi.astype(jnp.float32)).astype(jnp.bfloat16)
        m = (jnp.dot(hi, ones, preferred_element_type=jnp.float32)
             + jnp.dot(lo, ones, preferred_element_type=jnp.float32))
        d = jnp.sum(jnp.where(eye, m, 0.0).reshape(grp, hk, 128), axis=1)
        act[pl.ds(pl.multiple_of(c * grp, grp), grp), :] = d
        return carry

    lax.fori_loop(0, tb // grp, lane_sum, 0)
    w_ref[...] = gate_ref[...] * jax.nn.gelu(act[...])


def peer_activate(idx, xn, gate, tab):
    n, d = xn.shape
    hk = idx.shape[1]
    tb = min(PEER_TB, n)
    sub = d // 128
    return pl.pallas_call(
        _peer_u_kernel,
        grid=(n // tb,),
        in_specs=[pl.BlockSpec((tb, hk), lambda i: (i, 0), memory_space=pltpu.SMEM),
                  pl.BlockSpec((tb, sub, 128), lambda i: (i, 0, 0)),
                  pl.BlockSpec((tb, hk), lambda i: (i, 0)),
                  _table_spec(tab)],
        out_specs=pl.BlockSpec((tb, hk), lambda i: (i, 0)),
        out_shape=jax.ShapeDtypeStruct((n, hk), jnp.float32),
        scratch_shapes=[pltpu.VMEM((hk * sub // 2, 128), jnp.float32),
                        pltpu.VMEM((tb * hk, 128), jnp.float32),
                        pltpu.VMEM((tb, hk), jnp.float32)],
        compiler_params=pltpu.CompilerParams(dimension_semantics=("arbitrary",),
                                             vmem_limit_bytes=PEER_VMEM_LIMIT_BYTES),
        name="peer_u",
    )(idx, xn.reshape(n, sub, 128), gate, tab)


def _peer_v_kernel(idx_ref, w_ref, x_ref, tab, o_ref):
    tb, hk = idx_ref.shape
    rows = x_ref.shape[1] // 2

    def token(t, c):
        nacc = 2
        lo_acc = [jnp.zeros((rows, 128), jnp.float32) for _ in range(nacc)]
        hi_acc = [jnp.zeros((rows, 128), jnp.float32) for _ in range(nacc)]
        for k in range(hk):
            lo, hi = _gather_row(tab, idx_ref[t, k], rows)
            wk = w_ref[t, k]
            lo_acc[k % nacc] = lo_acc[k % nacc] + wk * lo
            hi_acc[k % nacc] = hi_acc[k % nacc] + wk * hi
        xt = x_ref[t]
        o_ref[t, :rows, :] = xt[:rows] + (lo_acc[0] + lo_acc[1])
        o_ref[t, rows:, :] = xt[rows:] + (hi_acc[0] + hi_acc[1])
        return c

    lax.fori_loop(0, tb, token, 0)


def peer_combine(idx, w, x, tab):
    n, d = x.shape
    hk = idx.shape[1]
    tb = min(PEER_TB, n)
    sub = d // 128
    out = pl.pallas_call(
        _peer_v_kernel,
        grid=(n // tb,),
        in_specs=[pl.BlockSpec((tb, hk), lambda i: (i, 0), memory_space=pltpu.SMEM),
                  pl.BlockSpec((tb, hk), lambda i: (i, 0), memory_space=pltpu.SMEM),
                  pl.BlockSpec((tb, sub, 128), lambda i: (i, 0, 0)),
                  _table_spec(tab)],
        out_specs=pl.BlockSpec((tb, sub, 128), lambda i: (i, 0, 0)),
        out_shape=jax.ShapeDtypeStruct((n, sub, 128), jnp.float32),
        compiler_params=pltpu.CompilerParams(dimension_semantics=("arbitrary",),
                                             vmem_limit_bytes=PEER_VMEM_LIMIT_BYTES),
        name="peer_v",
    )(idx, w, x.reshape(n, sub, 128), tab)
    return out.reshape(n, d)


def peer_ffn(x, g_norm, pp):
    B, T, D = x.shape
    x2 = x.reshape(B * T, D)
    xn, idx, gate = peer_route(x2, g_norm, pp["w_q"], pp["subkeys"])
    w = peer_activate(idx, xn, gate, pp["u_packed"])
    return peer_combine(idx, w, x2, pp["v_packed"]).reshape(B, T, D)


def kernel(x_prompt, x_sample, mem_prompt, cache_nsa_cmp, cache_nsa_slc, cache_nsa_win, cache_mla, cache_swa,
           cache_mem, page_table, norm_mix, norm_cross, norm_ffn, even_w_in, even_w_out, nsa_g_q, nsa_g_k,
           nsa_cmp_pos, nsa_cmp_w1, nsa_cmp_w2, mla_g_qn, mla_g_qpe, mla_g_lat, mla_g_kpe, mla_g_kn, mla_w_uk,
           mla_w_uv, odd_w_in, odd_w_out, swa_g_q, swa_g_k, swa_sinks, mem_g, mem_w_q, mem_w_k, mem_w_v,
           mem_g_q, mem_g_k, mem_w_o, peer_w_q, peer_subkeys, peer_u, peer_v):
    depth = norm_mix.shape[0]
    slopes_nsa = alibi_slopes(NSA_HEADS)
    slopes_swa = alibi_slopes(SWA_HEADS)
    past_len = page_table.shape[1] * PAGE_SIZE
    T, Tn = x_prompt.shape[1], x_sample.shape[1]
    pos_p = jnp.arange(T)
    pos_s = past_len + jnp.arange(Tn)
    xp, xs = x_prompt, x_sample
    st_cmp_p, st_cmp_s, st_slc_p, st_slc_s, st_win_p, st_win_s = [], [], [], [], [], []
    st_mla_p, st_mla_s, st_swa_p, st_swa_s, st_mem_p = [], [], [], [], []

    for li in range(depth):
        if li % 2 == 0:
            e = li // 2
            ep = dict(w_in=even_w_in[e], g_q=nsa_g_q[e], g_k=nsa_g_k[e], cmp_pos=nsa_cmp_pos[e],
                      cmp_w1=nsa_cmp_w1[e], cmp_w2=nsa_cmp_w2[e], g_qn=mla_g_qn[e], g_qpe=mla_g_qpe[e],
                      g_lat=mla_g_lat[e], g_kpe=mla_g_kpe[e], g_kn=mla_g_kn[e], w_uk=mla_w_uk[e], w_uv=mla_w_uv[e])
            q, cmp_kv, slc_kv, win_kv, gates, qn, qp, mla_row = even_project(xp, norm_mix[li], pos_p, ep)
            o_nsa = nsa_prompt(q, cmp_kv, slc_kv, win_kv, gates, ep, slopes_nsa)
            o_mla = mla_prompt(qn, qp, mla_row, ep)
            xp = even_output(xp, o_nsa, o_mla, even_w_out[e])
            st_cmp_p.append(cmp_kv)
            st_slc_p.append(slc_kv)
            st_win_p.append(win_kv[:, -min(NSA_WIN, T):])
            st_mla_p.append(mla_row)
            win_buf = cache_nsa_win[e]
            q, cmp_kv, slc_kv, win_kv, gates, qn, qp, mla_row = even_project(xs, norm_mix[li], pos_s, ep)
            o_nsa = nsa_sample(q, pos_s, cmp_kv, slc_kv, win_kv, gates, cache_nsa_cmp, cache_nsa_slc, win_buf,
                               page_table, e, ep, slopes_nsa)
            o_mla = mla_sample(qn, qp, mla_row, cache_mla, page_table, e, ep)
            xs = even_output(xs, o_nsa, o_mla, even_w_out[e])
            st_cmp_s.append(cmp_kv)
            st_slc_s.append(slc_kv)
            st_win_s.append(jnp.concatenate([win_buf, win_kv], axis=1)[:, -win_buf.shape[1]:])
            st_mla_s.append(mla_row)
        else:
            o = li // 2
            op = dict(w_in=odd_w_in[o], g_q=swa_g_q[o], g_k=swa_g_k[o])
            B = xp.shape[0]
            q, kv = odd_project(xp, norm_mix[li], op)
            att = jnp.transpose(swa_prompt_attention(q, kv[:, :, 0], kv[:, :, 1], swa_sinks[o]), (0, 2, 1, 3))
            xp = matmul_res(att.reshape(B * T, ODD_OUT), odd_w_out[o], xp.reshape(B * T, -1)).reshape(xp.shape)
            st_swa_p.append(kv[:, -min(SWA_WIN, T):])
            buf = cache_swa[o]
            q, kv = odd_project(xs, norm_mix[li], op)
            att = window_sample(q, buf[:, :, 0], buf[:, :, 1], kv[:, :, 0], kv[:, :, 1], past_len, SWA_WIN,
                                slopes_swa, swa_sinks[o])
            xs = matmul_res(att.reshape(xs.shape[0] * Tn, ODD_OUT), odd_w_out[o],
                            xs.reshape(xs.shape[0] * Tn, -1)).reshape(xs.shape)
            st_swa_s.append(jnp.concatenate([buf, kv], axis=1)[:, -buf.shape[1]:])

        cp = dict(g_mem=mem_g[li], w_q=mem_w_q[li], w_k=mem_w_k[li], w_v=mem_w_v[li], g_q=mem_g_q[li],
                  g_k=mem_g_k[li], w_o=mem_w_o[li])
        mem_kv_p = memory_kv(mem_prompt, cp)
        xp = memory_cross(xp, norm_cross[li], mem_kv_p, cp)
        xs = memory_cross(xs, norm_cross[li], cache_mem[li], cp)
        st_mem_p.append(mem_kv_p)

        pp = dict(w_q=peer_w_q[li], subkeys=peer_subkeys[li],
                  u_packed=pack_table(peer_u[li]), v_packed=pack_table(peer_v[li]))
        xp = peer_ffn(xp, norm_ffn[li], pp)
        xs = peer_ffn(xs, norm_ffn[li], pp)

    return (xp, xs, jnp.stack(st_cmp_p), jnp.stack(st_cmp_s), jnp.stack(st_slc_p), jnp.stack(st_slc_s),
            jnp.stack(st_win_p), jnp.stack(st_win_s), jnp.stack(st_mla_p), jnp.stack(st_mla_s),
            jnp.stack(st_swa_p), jnp.stack(st_swa_s), jnp.stack(st_mem_p))
```

```python
import functools
import math

import jax
import jax.numpy as jnp
import numpy as np
from jax import lax
from jax.experimental import pallas as pl
from jax.experimental.pallas import tpu as pltpu

PAGE_SIZE = 128
Q_BLOCK = 128
EPS = 1e-6
NEG_INF = -1e30
ROPE_BASE = 10000.0

NSA_HEADS = 8
NSA_DK = 64
NSA_BLK = 64
NSA_TOPN = 16
NSA_WIN = 512
NSA_CMP_HID = 256
MLA_HEADS = 8
MLA_NOPE = 64
MLA_ROPE = 32
MLA_V = 64
MLA_LORA = 128
MLA_ROW = MLA_LORA + MLA_ROPE
SWA_HEADS = 16
SWA_KV_HEADS = 4
SWA_HD = 64
SWA_WIN = 128
N_MEM = 256
MEM_HEADS = 4
MEM_HD = 128
MEM_W = MEM_HEADS * MEM_HD
PEER_HEADS = 8
PEER_NKEYS = 128
PEER_N = PEER_NKEYS * PEER_NKEYS
PEER_DKEY = 128
PEER_TOPK = 16
PEER_CHUNK = 256

NSA_Q_COLS = NSA_HEADS * NSA_DK
NSA_KV_COLS = 3 * 2 * NSA_DK
NSA_GATE_COLS = NSA_HEADS * 3
MLA_Q_COLS = MLA_HEADS * (MLA_NOPE + MLA_ROPE)
EVEN_IN = NSA_Q_COLS + NSA_KV_COLS + NSA_GATE_COLS + MLA_Q_COLS + MLA_ROW
EVEN_OUT = NSA_HEADS * NSA_DK + MLA_HEADS * MLA_V
ODD_IN = SWA_HEADS * SWA_HD + 2 * SWA_KV_HEADS * SWA_HD
ODD_OUT = SWA_HEADS * SWA_HD

VMEM_LIMIT_BYTES = 48 * 1024 * 1024


def _row_tile(n, target=512):
    t = min(n, target)
    while n % t:
        t //= 2
    return t


def _norm_matmul_kernel(x_ref, g_ref, w_ref, o_ref):
    x = x_ref[...]
    y = x * lax.rsqrt(jnp.mean(x * x, axis=-1, keepdims=True) + EPS) * g_ref[...]
    o_ref[...] = jnp.dot(y.astype(jnp.bfloat16), w_ref[...], preferred_element_type=jnp.float32)


def norm_matmul(x, g, w):
    n, d = x.shape
    c = w.shape[1]
    tm = _row_tile(n)
    return pl.pallas_call(
        _norm_matmul_kernel,
        grid=(n // tm,),
        in_specs=[pl.BlockSpec((tm, d), lambda i: (i, 0)),
                  pl.BlockSpec((1, d), lambda i: (0, 0)),
                  pl.BlockSpec((d, c), lambda i: (0, 0))],
        out_specs=pl.BlockSpec((tm, c), lambda i: (i, 0)),
        out_shape=jax.ShapeDtypeStruct((n, c), jnp.float32),
        compiler_params=pltpu.CompilerParams(dimension_semantics=("arbitrary",),
                                             vmem_limit_bytes=VMEM_LIMIT_BYTES),
        name="norm_matmul",
    )(x, g.reshape(1, d), w.astype(jnp.bfloat16))


def _matmul_res_kernel(a_ref, w_ref, r_ref, o_ref):
    o_ref[...] = r_ref[...] + jnp.dot(a_ref[...].astype(jnp.bfloat16), w_ref[...],
                                      preferred_element_type=jnp.float32)


def matmul_res(a, w, res):
    n, k = a.shape
    c = w.shape[1]
    tm = _row_tile(n)
    return pl.pallas_call(
        _matmul_res_kernel,
        grid=(n // tm,),
        in_specs=[pl.BlockSpec((tm, k), lambda i: (i, 0)),
                  pl.BlockSpec((k, c), lambda i: (0, 0)),
                  pl.BlockSpec((tm, c), lambda i: (i, 0))],
        out_specs=pl.BlockSpec((tm, c), lambda i: (i, 0)),
        out_shape=jax.ShapeDtypeStruct((n, c), jnp.float32),
        compiler_params=pltpu.CompilerParams(dimension_semantics=("arbitrary",),
                                             vmem_limit_bytes=VMEM_LIMIT_BYTES),
        name="matmul_res",
    )(a, w.astype(jnp.bfloat16), res)


def rmsnorm(x, g):
    xf = x.astype(jnp.float32)
    y = xf * lax.rsqrt(jnp.mean(xf * xf, axis=-1, keepdims=True) + EPS)
    return (y * g.astype(jnp.float32)).astype(x.dtype)


def alibi_slopes(n_heads):
    return jnp.asarray((2.0 ** (-8.0 * np.arange(1, n_heads + 1) / n_heads)).astype(np.float32))


def rope(x, pos):
    d = x.shape[-1]
    inv = jnp.asarray(np.power(ROPE_BASE, -np.arange(0, d, 2, dtype=np.float32) / d).astype(np.float32))
    ang = pos.astype(jnp.float32)[:, None] * inv[None, :]
    cos = jnp.cos(ang)[None, :, None, :]
    sin = jnp.sin(ang)[None, :, None, :]
    xf = x.astype(jnp.float32)
    x1, x2 = xf[..., : d // 2], xf[..., d // 2:]
    return jnp.concatenate([x1 * cos - x2 * sin, x1 * sin + x2 * cos], axis=-1).astype(x.dtype)


def masked_softmax(s, valid, sink=None):
    s = jnp.where(valid, s, NEG_INF)
    m = jnp.max(s, axis=-1, keepdims=True)
    if sink is not None:
        sk = sink.astype(jnp.float32)[:, None, None]
        m = jnp.maximum(m, sk)
    e = jnp.where(valid, jnp.exp(s - m), 0.0)
    den = jnp.sum(e, axis=-1, keepdims=True)
    if sink is not None:
        den = den + jnp.exp(sk - m)
    return e / jnp.maximum(den, 1e-30)


def attend(q, k, v, dist, valid, slopes, sink=None):
    B, Q, H, d = q.shape
    G = k.shape[-2]
    qg = q.reshape(B, Q, G, H // G, d)
    if k.ndim == 5:
        s = jnp.einsum("bqgrd,bqkgd->bgrqk", qg, k)
    else:
        s = jnp.einsum("bqgrd,bkgd->bgrqk", qg, k)
    K = s.shape[-1]
    s = s.reshape(B, H, Q, K).astype(jnp.float32) * (d ** -0.5)
    if slopes is not None:
        s = s - slopes[:, None, None] * dist[..., None, :, :].astype(jnp.float32)
    p = masked_softmax(s, valid[..., None, :, :], sink)
    pg = p.reshape(B, G, H // G, Q, K).astype(v.dtype)
    if v.ndim == 5:
        o = jnp.einsum("bgrqk,bqkgd->bqgrd", pg, v)
    else:
        o = jnp.einsum("bgrqk,bkgd->bqgrd", pg, v)
    return o.reshape(B, Q, H, v.shape[-1]), p


def alibi_slopes_np(n_heads):
    return (2.0 ** (-8.0 * np.arange(1, n_heads + 1) / n_heads)).astype(np.float32)


Q_TILE = 128
SLC_CHUNK = 512
MASK_BIG = -(2.0 ** 100)


def _nsa_prompt_kernel(q_ref, kt_ref, vs_ref, kwt_ref, vw_ref, kct_ref, vc_ref, g_ref, slope_ref, wbias_ref, cbias_ref,
                       o_ref, *, n_blocks):
    H, TQ = NSA_HEADS, Q_TILE
    i = pl.program_id(1)
    s0 = i * TQ
    q_lo = q_ref[0].reshape(H * TQ, 128)
    slope = slope_ref[...]

    s = jnp.dot(q_lo, kct_ref[0], preferred_element_type=jnp.float32).reshape(H, TQ, n_blocks)
    tpos = s0 + lax.broadcasted_iota(jnp.int32, (TQ, n_blocks), 0)
    blk = lax.broadcasted_iota(jnp.int32, (TQ, n_blocks), 1)
    dist = tpos - ((blk + 1) * NSA_BLK - 1)
    valid = (dist >= 0)[None]
    s = jnp.where(valid, s - slope * dist.astype(jnp.float32)[None], NEG_INF)
    m = jnp.max(s, axis=-1, keepdims=True)
    e = jnp.where(valid, jnp.exp(s - m), 0.0)
    p = e / jnp.maximum(jnp.sum(e, axis=-1, keepdims=True), 1e-30)
    o_cmp = jnp.dot(p.reshape(H * TQ, n_blocks).astype(jnp.bfloat16), vc_ref[0],
                    preferred_element_type=jnp.float32)
    imp = jnp.sum(p, axis=0)

    cur = tpos // NSA_BLK
    imp = jnp.where(blk == cur, float(H + 1), jnp.where(blk < cur, imp, -1.0))
    blk_f = blk.astype(jnp.float32)
    sel = jnp.zeros((TQ, n_blocks), jnp.bool_)
    for _ in range(min(NSA_TOPN, n_blocks)):
        mx = jnp.max(imp, axis=-1, keepdims=True)
        pick = jnp.min(jnp.where(imp == mx, blk_f, float(n_blocks)), axis=-1, keepdims=True)
        hit = blk_f == pick
        sel = sel | hit
        imp = jnp.where(hit, -2.0, imp)
    nsb = jnp.where(sel, 0.0, MASK_BIG).astype(jnp.bfloat16)
    q_aug = jnp.concatenate([q_lo, jnp.broadcast_to(nsb[None], (H, TQ, n_blocks)).reshape(H * TQ, n_blocks)], axis=1)

    KC = SLC_CHUNK

    def slc_step(c, carry, bias):
        m_i, l_i, acc = carry
        k0 = pl.multiple_of(c * KC, KC)
        sc = jnp.dot(q_aug, kt_ref[0, :, pl.ds(k0, KC)], preferred_element_type=jnp.float32)
        if bias is not None:
            sc = (sc.reshape(H, TQ, KC) + bias[None]).reshape(H * TQ, KC)
        m_new = jnp.maximum(m_i, jnp.max(sc, axis=-1, keepdims=True))
        corr = jnp.exp(m_i - m_new)
        pe = jnp.exp(sc - m_new)
        l_new = l_i * corr + jnp.sum(pe, axis=-1, keepdims=True)
        acc_new = acc * corr + jnp.dot(pe.astype(jnp.bfloat16), vs_ref[0, pl.ds(k0, KC), :],
                                       preferred_element_type=jnp.float32)
        return m_new, l_new, acc_new

    init = (jnp.full((H * TQ, 1), NEG_INF, jnp.float32), jnp.zeros((H * TQ, 1), jnp.float32),
            jnp.zeros((H * TQ, NSA_DK), jnp.float32))
    c_diag = s0 // KC
    carry = lax.fori_loop(0, c_diag, lambda c, cr: slc_step(c, cr, None), init)
    m_i, l_i, acc = slc_step(c_diag, carry, cbias_ref[(s0 % KC) // TQ])
    o_slc = acc / l_i

    nwc = NSA_WIN // TQ + 1
    parts = []
    for r in range(nwc):
        cidx = i - (nwc - 1) + r
        k0 = pl.multiple_of(jnp.maximum(cidx, 0) * TQ, TQ)
        sw = jnp.dot(q_lo[:, :NSA_DK], kwt_ref[0, :, pl.ds(k0, TQ)], preferred_element_type=jnp.float32)
        b = jnp.where(cidx >= 0, wbias_ref[r], NEG_INF)
        parts.append((sw.reshape(H, TQ, TQ) + b).reshape(H * TQ, TQ))
    mw = parts[0].max(axis=-1, keepdims=True)
    for r in range(1, nwc):
        mw = jnp.maximum(mw, parts[r].max(axis=-1, keepdims=True))
    lw = jnp.zeros((H * TQ, 1), jnp.float32)
    accw = jnp.zeros((H * TQ, NSA_DK), jnp.float32)
    for r in range(nwc):
        cidx = i - (nwc - 1) + r
        k0 = pl.multiple_of(jnp.maximum(cidx, 0) * TQ, TQ)
        pe = jnp.exp(parts[r] - mw)
        lw = lw + jnp.sum(pe, axis=-1, keepdims=True)
        accw = accw + jnp.dot(pe.astype(jnp.bfloat16), vw_ref[0, pl.ds(k0, TQ), :], preferred_element_type=jnp.float32)
    o_win = accw / lw

    g = g_ref[0].reshape(H * TQ, 3)
    o = g[:, 0:1] * o_cmp + g[:, 1:2] * o_slc + g[:, 2:3] * o_win
    o_ref[0] = o.reshape(H, TQ, NSA_DK)


def nsa_prompt_attention(q, slc_k, slc_v, win_k, win_v, kc, vc, gates):
    B, T, H, dk = q.shape
    nb = T // NSA_BLK
    nbp = 128
    TQ = Q_TILE
    assert nb <= nbp and T % SLC_CHUNK == 0 and NSA_WIN % TQ == 0
    slopes = alibi_slopes_np(H)
    bf = jnp.bfloat16
    qh = jnp.transpose(q, (0, 2, 1, 3)) * (dk ** -0.5)
    al = np.zeros((H, 128 - dk), np.float32)
    al[:, 0] = slopes * 128.0
    al[:, 1] = slopes
    q_lo = jnp.concatenate([qh, jnp.broadcast_to(jnp.asarray(al)[None, :, None, :], (B, H, T, 128 - dk))],
                           axis=-1).astype(bf)
    pos = np.arange(T)
    crow = np.zeros((128 - dk + nbp, T), np.float32)
    crow[0] = pos // 128
    crow[1] = pos % 128
    crow[128 - dk + pos // NSA_BLK, pos] = 1.0
    kt = jnp.concatenate([jnp.transpose(slc_k, (0, 2, 1)),
                          jnp.broadcast_to(jnp.asarray(crow)[None], (B,) + crow.shape)], axis=1).astype(bf)
    kwt = jnp.transpose(win_k, (0, 2, 1)).astype(bf)
    kct = jnp.pad(jnp.transpose(kc, (0, 2, 1)), ((0, 0), (0, 128 - dk), (0, nbp - nb))).astype(bf)
    vcp = jnp.pad(vc, ((0, 0), (0, nbp - nb), (0, 0))).astype(bf)
    gh = jnp.transpose(gates, (0, 2, 1, 3))
    nwc = NSA_WIN // TQ + 1
    ii = np.arange(TQ)[:, None]
    jj = np.arange(TQ)[None, :]
    wb = np.zeros((nwc, H, TQ, TQ), np.float32)
    for r in range(nwc):
        d = ii - jj + TQ * (nwc - 1 - r)
        ok = (d >= 0) & (d <= NSA_WIN)
        wb[r] = np.where(ok[None], -slopes[:, None, None] * d[None].astype(np.float32), NEG_INF)
    nv = SLC_CHUNK // TQ
    cb = np.zeros((nv, TQ, SLC_CHUNK), np.float32)
    for v in range(nv):
        cb[v] = np.where(np.arange(SLC_CHUNK)[None, :] <= v * TQ + ii, 0.0, NEG_INF)
    kern = functools.partial(_nsa_prompt_kernel, n_blocks=nbp)
    return pl.pallas_call(
        kern,
        grid=(B, T // TQ),
        in_specs=[pl.BlockSpec((1, H, TQ, 128), lambda b, i: (b, 0, i, 0)),
                  pl.BlockSpec((1, 256, T), lambda b, i: (b, 0, 0)),
                  pl.BlockSpec((1, T, dk), lambda b, i: (b, 0, 0)),
                  pl.BlockSpec((1, dk, T), lambda b, i: (b, 0, 0)),
                  pl.BlockSpec((1, T, dk), lambda b, i: (b, 0, 0)),
                  pl.BlockSpec((1, 128, nbp), lambda b, i: (b, 0, 0)),
                  pl.BlockSpec((1, nbp, dk), lambda b, i: (b, 0, 0)),
                  pl.BlockSpec((1, H, TQ, 3), lambda b, i: (b, 0, i, 0)),
                  pl.BlockSpec((H, 1, 1), lambda b, i: (0, 0, 0)),
                  pl.BlockSpec((nwc, H, TQ, TQ), lambda b, i: (0, 0, 0, 0)),
                  pl.BlockSpec((nv, TQ, SLC_CHUNK), lambda b, i: (0, 0, 0))],
        out_specs=pl.BlockSpec((1, H, TQ, dk), lambda b, i: (b, 0, i, 0)),
        out_shape=jax.ShapeDtypeStruct((B, H, T, dk), jnp.float32),
        compiler_params=pltpu.CompilerParams(dimension_semantics=("arbitrary", "arbitrary"),
                                             vmem_limit_bytes=VMEM_LIMIT_BYTES),
        name="nsa_prompt",
    )(q_lo, kt, slc_v.astype(bf), kwt, win_v.astype(bf), kct, vcp, gh,
      jnp.asarray(slopes).reshape(H, 1, 1), jnp.asarray(wb), jnp.asarray(cb))


MLA_TILE = 512


def _mla_prompt_kernel(q_ref, kt_ref, v_ref, tril_ref, o_ref, *, scale_log2e):
    TQ = MLA_TILE
    i = pl.program_id(2)
    q = q_ref[0, 0]

    def step(c, carry, bias):
        m_i, l_i, acc = carry
        k0 = pl.multiple_of(c * TQ, TQ)
        s = jnp.dot(q, kt_ref[0, 0, :, pl.ds(k0, TQ)], preferred_element_type=jnp.float32)
        if bias is not None:
            s = s + bias
        m_new = jnp.maximum(m_i, jnp.max(s, axis=-1, keepdims=True))
        corr = jnp.exp2((m_i - m_new) * scale_log2e)
        p = jnp.exp2((s - m_new) * scale_log2e)
        l_new = l_i * corr + jnp.sum(p, axis=-1, keepdims=True)
        acc_new = acc * corr + jnp.dot(p.astype(jnp.bfloat16), v_ref[0, 0, pl.ds(k0, TQ), :],
                                       preferred_element_type=jnp.float32)
        return m_new, l_new, acc_new

    init = (jnp.full((TQ, 1), NEG_INF, jnp.float32), jnp.zeros((TQ, 1), jnp.float32),
            jnp.zeros((TQ, MLA_V), jnp.float32))
    carry = lax.fori_loop(0, i, lambda c, cr: step(c, cr, None), init)
    m_i, l_i, acc = step(i, carry, tril_ref[...])
    o_ref[0, 0] = acc / l_i


def mla_prompt_attention(qn, qp, kn, kp, v):
    B, T, H, _ = qn.shape
    TQ = MLA_TILE
    assert T % TQ == 0
    bf = jnp.bfloat16
    dq = MLA_NOPE + MLA_ROPE
    qcat = jnp.concatenate([qn, qp, jnp.zeros((B, T, H, 128 - dq), qn.dtype)], axis=-1)
    qcat = jnp.transpose(qcat, (0, 2, 1, 3)).astype(bf)
    kcat = jnp.concatenate([kn, jnp.broadcast_to(kp[:, :, None, :], (B, T, H, MLA_ROPE)),
                            jnp.zeros((B, T, H, 128 - dq), kn.dtype)], axis=-1)
    kt = jnp.transpose(kcat, (0, 2, 3, 1)).astype(bf)
    vh = jnp.transpose(v, (0, 2, 1, 3)).astype(bf)
    tril = np.where(np.arange(TQ)[None, :] <= np.arange(TQ)[:, None], 0.0, NEG_INF).astype(np.float32)
    kern = functools.partial(_mla_prompt_kernel, scale_log2e=float(dq ** -0.5 * math.log2(math.e)))
    return pl.pallas_call(
        kern,
        grid=(B, H, T // TQ),
        in_specs=[pl.BlockSpec((1, 1, TQ, 128), lambda b, h, i: (b, h, i, 0)),
                  pl.BlockSpec((1, 1, 128, T), lambda b, h, i: (b, h, 0, 0)),
                  pl.BlockSpec((1, 1, T, MLA_V), lambda b, h, i: (b, h, 0, 0)),
                  pl.BlockSpec((TQ, TQ), lambda b, h, i: (0, 0))],
        out_specs=pl.BlockSpec((1, 1, TQ, MLA_V), lambda b, h, i: (b, h, i, 0)),
        out_shape=jax.ShapeDtypeStruct((B, H, T, MLA_V), jnp.float32),
        compiler_params=pltpu.CompilerParams(dimension_semantics=("arbitrary", "arbitrary", "arbitrary"),
                                             vmem_limit_bytes=VMEM_LIMIT_BYTES),
        name="mla_prompt",
    )(qcat, kt, vh, jnp.asarray(tril))


def _swa_prompt_kernel(q_ref, kt_ref, v_ref, bias_ref, sink_ref, o_ref):
    TQ = Q_TILE
    R = SWA_HEADS // SWA_KV_HEADS
    i = pl.program_id(2)
    q = q_ref[0, 0].reshape(R * TQ, SWA_HD)
    kprev = pl.multiple_of(jnp.maximum(i - 1, 0) * TQ, TQ)
    kcur = pl.multiple_of(i * TQ, TQ)
    s0 = jnp.dot(q, kt_ref[0, 0, :, pl.ds(kprev, TQ)], preferred_element_type=jnp.float32).reshape(R, TQ, TQ)
    s1 = jnp.dot(q, kt_ref[0, 0, :, pl.ds(kcur, TQ)], preferred_element_type=jnp.float32).reshape(R, TQ, TQ)
    s0 = s0 + jnp.where(i > 0, bias_ref[0, 0], NEG_INF)
    s1 = s1 + bias_ref[0, 1]
    sink = sink_ref[0]
    m = jnp.maximum(jnp.maximum(s0.max(axis=-1, keepdims=True), s1.max(axis=-1, keepdims=True)), sink)
    p0 = jnp.exp(s0 - m)
    p1 = jnp.exp(s1 - m)
    den = p0.sum(axis=-1, keepdims=True) + p1.sum(axis=-1, keepdims=True) + jnp.exp(sink - m)
    o = (jnp.dot(p0.reshape(R * TQ, TQ).astype(jnp.bfloat16), v_ref[0, 0, pl.ds(kprev, TQ), :],
                 preferred_element_type=jnp.float32)
         + jnp.dot(p1.reshape(R * TQ, TQ).astype(jnp.bfloat16), v_ref[0, 0, pl.ds(kcur, TQ), :],
                   preferred_element_type=jnp.float32))
    o_ref[0, 0] = o.reshape(R, TQ, SWA_HD) / jnp.maximum(den, 1e-30)


def swa_prompt_attention(q, k, v, sinks):
    B, T, H, hd = q.shape
    G = k.shape[2]
    R = H // G
    TQ = Q_TILE
    assert SWA_WIN == TQ and T % TQ == 0
    bf = jnp.bfloat16
    slopes = alibi_slopes_np(H)
    qh = (jnp.transpose(q, (0, 2, 1, 3)) * (hd ** -0.5)).astype(bf).reshape(B, G, R, T, hd)
    kt = jnp.transpose(k, (0, 2, 3, 1)).astype(bf)
    vh = jnp.transpose(v, (0, 2, 1, 3)).astype(bf)
    ii = np.arange(TQ)[:, None]
    jj = np.arange(TQ)[None, :]
    bias = np.zeros((G, 2, R, TQ, TQ), np.float32)
    for r in range(2):
        d = ii - jj + TQ * (1 - r)
        ok = (d >= 0) & (d <= SWA_WIN)
        bias[:, r] = np.where(ok[None, None],
                              -slopes.reshape(G, R)[:, :, None, None] * d[None, None].astype(np.float32), NEG_INF)
    return pl.pallas_call(
        _swa_prompt_kernel,
        grid=(B, G, T // TQ),
        in_specs=[pl.BlockSpec((1, 1, R, TQ, hd), lambda b, g, i: (b, g, 0, i, 0)),
                  pl.BlockSpec((1, 1, hd, T), lambda b, g, i: (b, g, 0, 0)),
                  pl.BlockSpec((1, 1, T, hd), lambda b, g, i: (b, g, 0, 0)),
                  pl.BlockSpec((1, 2, R, TQ, TQ), lambda b, g, i: (g, 0, 0, 0, 0)),
                  pl.BlockSpec((1, R, 1, 1), lambda b, g, i: (g, 0, 0, 0))],
        out_specs=pl.BlockSpec((1, 1, R, TQ, hd), lambda b, g, i: (b, g, 0, i, 0)),
        out_shape=jax.ShapeDtypeStruct((B, G, R, T, hd), jnp.float32),
        compiler_params=pltpu.CompilerParams(dimension_semantics=("arbitrary", "arbitrary", "arbitrary"),
                                             vmem_limit_bytes=VMEM_LIMIT_BYTES),
        name="swa_prompt",
    )(qh, kt, vh, jnp.asarray(bias), sinks.reshape(G, R, 1, 1)).reshape(B, H, T, hd)


def window_sample(q, k_buf, v_buf, k_new, v_new, past_len, window, slopes, sink=None):
    Wb, Tn = k_buf.shape[1], q.shape[1]
    k = jnp.concatenate([k_buf, k_new], axis=1)
    v = jnp.concatenate([v_buf, v_new], axis=1)
    qpos = past_len + jnp.arange(Tn)
    kpos = past_len - Wb + jnp.arange(Wb + Tn)
    dist = qpos[:, None] - kpos[None, :]
    valid = (dist >= 0) & (dist <= window)
    return attend(q, k, v, dist, valid, slopes, sink)[0]


def nsa_compress(blocks, pos_emb, w1, w2):
    x = blocks + pos_emb
    h = jax.nn.gelu(jnp.einsum("...lcd,lcdh->...ch", x, w1))
    return jnp.einsum("...ch,chd->...cd", h, w2)


def nsa_select(p_cmp, qpos, n_blocks):
    imp = jnp.sum(p_cmp, axis=1)
    blk = jnp.arange(n_blocks)[None, :]
    cur = (qpos // NSA_BLK)[:, None]
    imp = jnp.where(blk == cur, float(NSA_HEADS + 1), jnp.where(blk < cur, imp, -1.0))
    _, idx = lax.top_k(imp, min(NSA_TOPN, n_blocks))
    return idx


def nsa_compressed_branch(q, qpos, kc, vc, c_end, slopes):
    dist = qpos[:, None] - c_end[None, :]
    o, p = attend(q, kc[:, :, None, :], vc[:, :, None, :], dist, dist >= 0, slopes)
    return o, nsa_select(p, qpos, kc.shape[1])


def nsa_selected_branch(q, qpos, sel, idx, slopes):
    B, Q, n = idx.shape
    kpos = (idx[..., None] * NSA_BLK + jnp.arange(NSA_BLK)).reshape(B, Q, n * NSA_BLK)
    dist = qpos[None, :, None] - kpos
    ks = sel[..., 0, :].reshape(B, Q, n * NSA_BLK, 1, NSA_DK)
    vs = sel[..., 1, :].reshape(B, Q, n * NSA_BLK, 1, NSA_DK)
    return attend(q, ks, vs, dist, dist >= 0, slopes)[0]


def nsa_gate(gates, o_cmp, o_slc, o_win):
    g = gates.astype(o_cmp.dtype)
    return g[..., 0:1] * o_cmp + g[..., 1:2] * o_slc + g[..., 2:3] * o_win


CMP_PAGES_PER_STEP = 64
PAGED_VMEM_LIMIT_BYTES = 56 * 1024 * 1024


def pages_feature_major(cache):
    return jnp.transpose(cache, (0, 1, 3, 4, 2)).reshape(-1, 2, NSA_DK, PAGE_SIZE)


def _nsa_compress_kernel(pt_ref, pos_ref, w1b_ref, w2_ref, *rest, n_pages):
    page_refs = rest[:n_pages]
    o0_ref, o1_ref, slab = rest[n_pages:]
    feat = 2 * NSA_DK
    for j, r in enumerate(page_refs):
        slab[j * feat:(j + 1) * feat, :] = r[0].reshape(feat, PAGE_SIZE)
    hid2 = 2 * NSA_CMP_HID
    hs = []
    for c in range(2):
        h = jnp.zeros((n_pages, hid2), jnp.float32)
        for d in range(NSA_DK):
            f = c * NSA_DK + d
            xf = slab[pl.ds(f, n_pages, stride=feat), :] + pos_ref[f:f + 1, :]
            h = h + jnp.dot(xf.astype(jnp.bfloat16), w1b_ref[f], preferred_element_type=jnp.float32)
        hs.append(jax.nn.gelu(h).astype(jnp.bfloat16))
    for j, o_ref in enumerate((o0_ref, o1_ref)):
        hj = jnp.concatenate([hs[c][:, j * NSA_CMP_HID:(j + 1) * NSA_CMP_HID] for c in range(2)], axis=1)
        o_ref[0] = jnp.dot(hj, w2_ref[...], preferred_element_type=jnp.float32)


def nsa_compress_paged(pages_t, page_table, pos_emb, w1, w2):
    S, P = page_table.shape
    PG = math.gcd(CMP_PAGES_PER_STEP, P)
    assert PAGE_SIZE == 2 * NSA_BLK and PG % 8 == 0
    bf = jnp.bfloat16
    feat = 2 * NSA_DK
    pos_t = jnp.tile(jnp.transpose(pos_emb, (1, 2, 0)).reshape(feat, NSA_BLK), (1, 2))
    w1f = jnp.transpose(w1, (1, 2, 0, 3)).reshape(feat, NSA_BLK, NSA_CMP_HID)
    z = jnp.zeros_like(w1f)
    w1b = jnp.concatenate([jnp.concatenate([w1f, z], axis=2), jnp.concatenate([z, w1f], axis=2)], axis=1).astype(bf)
    eye = jnp.eye(2, dtype=w2.dtype)
    w2z = jnp.einsum("chd,ce->ched", w2, eye).reshape(2 * NSA_CMP_HID, feat).astype(bf)

    def page_spec(j):
        return pl.BlockSpec((1, 2, NSA_DK, PAGE_SIZE), lambda s, g, pt: (pt[s, g * PG + j], 0, 0, 0))

    grid_spec = pltpu.PrefetchScalarGridSpec(
        num_scalar_prefetch=1,
        grid=(S, P // PG),
        in_specs=[pl.BlockSpec(pos_t.shape, lambda s, g, pt: (0, 0)),
                  pl.BlockSpec(w1b.shape, lambda s, g, pt: (0, 0, 0), pipeline_mode=pl.Buffered(1)),
                  pl.BlockSpec(w2z.shape, lambda s, g, pt: (0, 0))] + [page_spec(j) for j in range(PG)],
        out_specs=[pl.BlockSpec((1, PG, feat), lambda s, g, pt: (s, g, 0)),
                   pl.BlockSpec((1, PG, feat), lambda s, g, pt: (s, g, 0))],
        scratch_shapes=[pltpu.VMEM((PG * feat, PAGE_SIZE), jnp.float32)])
    o0, o1 = pl.pallas_call(
        functools.partial(_nsa_compress_kernel, n_pages=PG),
        grid_spec=grid_spec,
        out_shape=[jax.ShapeDtypeStruct((S, P, feat), jnp.float32)] * 2,
        compiler_params=pltpu.CompilerParams(dimension_semantics=("arbitrary", "arbitrary"),
                                             vmem_limit_bytes=PAGED_VMEM_LIMIT_BYTES),
        name="nsa_compress",
    )(page_table, pos_t, w1b, w2z, *([pages_t] * PG))
    return jnp.stack([o0, o1], axis=2).reshape(S, 2 * P, 2, NSA_DK)


def nsa_prompt(q, cmp_kv, slc_kv, win_kv, gates, ep, slopes):
    B, T = q.shape[:2]
    nb = T // NSA_BLK
    ppb = T // PAGE_SIZE
    prompt_pages = jnp.arange(B * ppb, dtype=jnp.int32).reshape(B, ppb)
    comp = nsa_compress_paged(pages_feature_major(cmp_kv.reshape(1, B * ppb, PAGE_SIZE, 2, NSA_DK)), prompt_pages,
                              ep["cmp_pos"], ep["cmp_w1"], ep["cmp_w2"])
    kc = rmsnorm(comp[:, :, 0], ep["g_k"][0])
    vc = comp[:, :, 1]
    o = nsa_prompt_attention(q, slc_kv[:, :, 0], slc_kv[:, :, 1], win_kv[:, :, 0], win_kv[:, :, 1], kc, vc, gates)
    return jnp.transpose(o, (0, 2, 1, 3))


def _nsa_sample_selected_kernel(phys_ref, idx_ref, q_ref, new_ref, slope_ref, *rest, n_sel, nb_past, qpos):
    page_refs = rest[:n_sel]
    o_ref = rest[n_sel]
    b = pl.program_id(0)
    nt = (((1,), (1,)), ((), ()))
    q = q_ref[0]
    lane = lax.broadcasted_iota(jnp.int32, (1, PAGE_SIZE), 1)
    ss, vts, dists = [], [], []
    for n, r in enumerate(page_refs):
        blk_id = idx_ref[b, n]
        is_new = blk_id >= nb_past
        half = jnp.where(is_new, 0, blk_id % 2)
        kt = jnp.where(is_new, new_ref[0, 0], r[0, 0]).astype(jnp.bfloat16)
        vts.append(jnp.where(is_new, new_ref[0, 1], r[0, 1]).astype(jnp.bfloat16))
        ss.append(jnp.dot(q, kt, preferred_element_type=jnp.float32))
        in_blk = (lane // NSA_BLK) == half
        dists.append(jnp.where(in_blk, qpos - blk_id * NSA_BLK - (lane % NSA_BLK), -1))
    s = jnp.concatenate(ss, axis=1)
    dist = jnp.concatenate(dists, axis=1)
    s = jnp.where(dist >= 0, s - slope_ref[...] * dist.astype(jnp.float32), NEG_INF)
    m = jnp.max(s, axis=-1, keepdims=True)
    e = jnp.where(dist >= 0, jnp.exp(s - m), 0.0)
    p = (e / jnp.maximum(jnp.sum(e, axis=-1, keepdims=True), 1e-30)).astype(jnp.bfloat16)
    o = jnp.zeros((q.shape[0], NSA_DK), jnp.float32)
    for n in range(n_sel):
        o = o + lax.dot_general(p[:, n * PAGE_SIZE:(n + 1) * PAGE_SIZE], vts[n], nt, preferred_element_type=jnp.float32)
    o_ref[0] = o


def nsa_sample_selected(q, idx, slc_new, cache_slc, page_table, e, qpos):
    DB, Tn, H, dk = q.shape
    n_sel = idx.shape[-1]
    assert Tn == 1
    P = page_table.shape[1]
    bpp = PAGE_SIZE // NSA_BLK
    nb_past = P * bpp
    idx2 = idx[:, 0].astype(jnp.int32)
    jp = jnp.minimum(idx2, nb_past - 1)
    phys = jnp.take_along_axis(page_table, jp // bpp, axis=1) + e * cache_slc.shape[1]
    qs = (q[:, 0] * (dk ** -0.5)).astype(jnp.bfloat16)
    new_t = jnp.zeros((DB, 2, dk, PAGE_SIZE), jnp.float32).at[:, :, :, 0].set(slc_new[:, 0])
    kern = functools.partial(_nsa_sample_selected_kernel, n_sel=n_sel, nb_past=nb_past, qpos=int(qpos))

    def page_spec(n):
        return pl.BlockSpec((1, 2, dk, PAGE_SIZE), lambda b, ph, ix: (ph[b, n], 0, 0, 0))

    grid_spec = pltpu.PrefetchScalarGridSpec(
        num_scalar_prefetch=2,
        grid=(DB,),
        in_specs=[pl.BlockSpec((1, H, dk), lambda b, ph, ix: (b, 0, 0)),
                  pl.BlockSpec((1, 2, dk, PAGE_SIZE), lambda b, ph, ix: (b, 0, 0, 0)),
                  pl.BlockSpec((H, 1), lambda b, ph, ix: (0, 0))] + [page_spec(n) for n in range(n_sel)],
        out_specs=pl.BlockSpec((1, H, dk), lambda b, ph, ix: (b, 0, 0)))
    out = pl.pallas_call(
        kern,
        grid_spec=grid_spec,
        out_shape=jax.ShapeDtypeStruct((DB, H, dk), jnp.float32),
        compiler_params=pltpu.CompilerParams(dimension_semantics=("arbitrary",), vmem_limit_bytes=VMEM_LIMIT_BYTES),
        name="nsa_sample_selected",
    )(phys, idx2, qs, new_t, jnp.asarray(alibi_slopes_np(H)).reshape(H, 1), *([pages_feature_major(cache_slc)] * n_sel))
    return out[:, None]


def nsa_sample(q, qpos, cmp_new, slc_new, win_new, gates, cache_cmp, cache_slc, win_buf, page_table, e, ep, slopes):
    DB, Tn = q.shape[:2]
    n_pages = page_table.shape[1]
    bpp = PAGE_SIZE // NSA_BLK
    nb_past = n_pages * bpp
    past_len = n_pages * PAGE_SIZE
    nb_new = -(-Tn // NSA_BLK)
    pad = ((0, 0), (0, nb_new * NSA_BLK - Tn), (0, 0), (0, 0))

    def compress(rows):
        return nsa_compress(rows, ep["cmp_pos"], ep["cmp_w1"], ep["cmp_w2"])

    comp_past = nsa_compress_paged(pages_feature_major(cache_cmp), page_table + e * cache_cmp.shape[1],
                                   ep["cmp_pos"], ep["cmp_w1"], ep["cmp_w2"])
    comp_new = compress(jnp.pad(cmp_new, pad).reshape(DB, nb_new, NSA_BLK, 2, NSA_DK))
    comp = jnp.concatenate([comp_past, comp_new], axis=1)
    kc = rmsnorm(comp[:, :, 0], ep["g_k"][0])
    vc = comp[:, :, 1]
    c_end = (jnp.arange(nb_past + nb_new) + 1) * NSA_BLK - 1
    o_cmp, idx = nsa_compressed_branch(q, qpos, kc, vc, c_end, slopes)

    o_slc = nsa_sample_selected(q, idx, slc_new, cache_slc, page_table, e, past_len)

    o_win = window_sample(q, win_buf[:, :, 0:1], win_buf[:, :, 1:2], win_new[:, :, 0:1], win_new[:, :, 1:2],
                          past_len, NSA_WIN, slopes)
    return nsa_gate(gates, o_cmp, o_slc, o_win)


def mla_keys(lat, ep):
    c = lat[..., :MLA_LORA]
    kp = lat[..., MLA_LORA:]
    kn = rmsnorm(jnp.einsum("bkc,chd->bkhd", c, ep["w_uk"]), ep["g_kn"])
    v = jnp.einsum("bkc,chd->bkhd", c, ep["w_uv"])
    return kn, kp, v


def mla_scores(qn, qp, kn, kp):
    s = jnp.einsum("bqhd,bkhd->bhqk", qn, kn) + jnp.einsum("bqhd,bkd->bhqk", qp, kp)
    return s.astype(jnp.float32) * ((MLA_NOPE + MLA_ROPE) ** -0.5)


def mla_prompt(qn, qp, lat, ep):
    B, T = qn.shape[:2]
    kn, kp, v = mla_keys(lat, ep)
    return jnp.transpose(mla_prompt_attention(qn, qp, kn, kp, v), (0, 2, 1, 3))


MLA_PAGES_PER_STEP = 16


def _mla_sample_kernel(pt_ref, qg_ref, qp_ref, wukt_ref, wuv_ref, gt_ref, new_ref, *rest, scale, n_pages):
    page_refs = rest[:n_pages]
    o_ref, c_scr, kp_scr, a_scr, m_scr, l_scr, acc_scr = rest[n_pages:]
    g = pl.program_id(1)
    ng = pl.num_programs(1)
    nt = (((1,), (1,)), ((), ()))

    @pl.when(g == 0)
    def _():
        a_scr[...] = lax.dot_general(qg_ref[0], wukt_ref[...], (((1,), (0,)), ((), ())),
                                     preferred_element_type=jnp.float32).astype(jnp.bfloat16)
        m_scr[...] = jnp.full(m_scr.shape, NEG_INF, jnp.float32)
        l_scr[...] = jnp.zeros(l_scr.shape, jnp.float32)
        acc_scr[...] = jnp.zeros(acc_scr.shape, jnp.float32)

    def scores(ct, kpt):
        projt = jnp.dot(wukt_ref[...], ct, preferred_element_type=jnp.float32)
        ss = jnp.dot(gt_ref[...], (projt * projt).astype(jnp.bfloat16), preferred_element_type=jnp.float32)
        num = jnp.dot(a_scr[...], ct, preferred_element_type=jnp.float32)
        sp = jnp.dot(qp_ref[0], kpt, preferred_element_type=jnp.float32)
        return (num * lax.rsqrt(ss * (1.0 / MLA_NOPE) + EPS) + sp) * scale

    def softmax_update(ct, st):
        m_i = m_scr[...]
        m_new = jnp.maximum(m_i, jnp.max(st, axis=-1, keepdims=True))
        corr = jnp.exp(m_i - m_new)
        p = jnp.exp(st - m_new)
        m_scr[...] = m_new
        l_scr[...] = l_scr[...] * corr + jnp.sum(p, axis=-1, keepdims=True)
        acc_scr[...] = acc_scr[...] * corr + lax.dot_general(p.astype(jnp.bfloat16), ct, nt,
                                                             preferred_element_type=jnp.float32)

    for j, r in enumerate(page_refs):
        page = r[0]
        c_scr[:, j * PAGE_SIZE:(j + 1) * PAGE_SIZE] = page[:MLA_LORA, :].astype(jnp.bfloat16)
        kp_scr[:, j * PAGE_SIZE:(j + 1) * PAGE_SIZE] = page[MLA_LORA:, :].astype(jnp.bfloat16)
    c_all = c_scr[...]
    softmax_update(c_all, scores(c_all, kp_scr[...]))

    @pl.when(g == ng - 1)
    def _():
        new = new_ref[0]
        ct = new[:MLA_LORA, :].astype(jnp.bfloat16)
        st = scores(ct, new[MLA_LORA:, :].astype(jnp.bfloat16))
        key = lax.broadcasted_iota(jnp.int32, st.shape, 1)
        softmax_update(ct, jnp.where(key == 0, st, NEG_INF))
        o8 = jnp.dot((acc_scr[...] / l_scr[...]).astype(jnp.bfloat16), wuv_ref[...],
                     preferred_element_type=jnp.float32)
        row = lax.broadcasted_iota(jnp.int32, o8.shape, 0)
        col = lax.broadcasted_iota(jnp.int32, o8.shape, 1)
        o_ref[0] = jnp.sum(jnp.where(col // MLA_V == row, o8, 0.0), axis=0, keepdims=True)


def mla_sample(qn, qp, lat_new, cache_mla, page_table, e, ep):
    DB, Tn = qn.shape[:2]
    H = MLA_HEADS
    P = page_table.shape[1]
    PG = math.gcd(MLA_PAGES_PER_STEP, P)
    assert Tn == 1
    bf = jnp.bfloat16
    dq = MLA_NOPE + MLA_ROPE
    qg = qn[:, 0] * ep["g_kn"]
    eye = jnp.eye(H, dtype=qg.dtype)
    qg_exp = (qg[:, :, None, :] * eye[None, :, :, None]).reshape(DB, H, H * MLA_NOPE).astype(bf)
    qp_h = qp[:, 0].astype(bf)
    wukt = ep["w_uk"].reshape(MLA_LORA, H * MLA_NOPE).T.astype(bf)
    wuv = ep["w_uv"].reshape(MLA_LORA, H * MLA_V).astype(bf)
    gt = np.zeros((H, H * MLA_NOPE), np.float32)
    gt[np.arange(H * MLA_NOPE) // MLA_NOPE, np.arange(H * MLA_NOPE)] = 1.0
    pages_t = jnp.swapaxes(cache_mla, -1, -2).reshape(-1, MLA_ROW, PAGE_SIZE)
    new_t = jnp.zeros((DB, MLA_ROW, PAGE_SIZE), jnp.float32).at[:, :, 0].set(lat_new[:, 0])
    kern = functools.partial(_mla_sample_kernel, scale=float(dq ** -0.5), n_pages=PG)

    def page_spec(j):
        return pl.BlockSpec((1, MLA_ROW, PAGE_SIZE), lambda b, g, pt: (pt[b, g * PG + j], 0, 0))

    const2 = lambda b, g, pt: (0, 0)
    grid_spec = pltpu.PrefetchScalarGridSpec(
        num_scalar_prefetch=1,
        grid=(DB, P // PG),
        in_specs=[pl.BlockSpec((1, H, H * MLA_NOPE), lambda b, g, pt: (b, 0, 0)),
                  pl.BlockSpec((1, H, MLA_ROPE), lambda b, g, pt: (b, 0, 0)),
                  pl.BlockSpec((H * MLA_NOPE, MLA_LORA), const2),
                  pl.BlockSpec((MLA_LORA, H * MLA_V), const2),
                  pl.BlockSpec((H, H * MLA_NOPE), const2),
                  pl.BlockSpec((1, MLA_ROW, PAGE_SIZE), lambda b, g, pt: (b, 0, 0))]
                 + [page_spec(j) for j in range(PG)],
        out_specs=pl.BlockSpec((1, 1, H * MLA_V), lambda b, g, pt: (b, 0, 0)),
        scratch_shapes=[pltpu.VMEM((MLA_LORA, PG * PAGE_SIZE), bf),
                        pltpu.VMEM((MLA_ROPE, PG * PAGE_SIZE), bf),
                        pltpu.VMEM((H, MLA_LORA), bf),
                        pltpu.VMEM((8, 1), jnp.float32),
                        pltpu.VMEM((8, 1), jnp.float32),
                        pltpu.VMEM((8, MLA_LORA), jnp.float32)])
    out = pl.pallas_call(
        kern,
        grid_spec=grid_spec,
        out_shape=jax.ShapeDtypeStruct((DB, 1, H * MLA_V), jnp.float32),
        compiler_params=pltpu.CompilerParams(dimension_semantics=("arbitrary", "arbitrary"),
                                             vmem_limit_bytes=VMEM_LIMIT_BYTES),
        name="mla_sample",
    )(page_table + e * cache_mla.shape[1], qg_exp, qp_h, wukt, wuv, jnp.asarray(gt).astype(bf), new_t,
      *([pages_t] * PG))
    return out.reshape(DB, 1, H, MLA_V)


def even_project(x, g_norm, pos, ep):
    B, T, D = x.shape
    h = norm_matmul(x.reshape(B * T, D), g_norm, ep["w_in"]).reshape(B, T, EVEN_IN)
    cuts = np.cumsum([NSA_Q_COLS, NSA_KV_COLS, NSA_GATE_COLS, MLA_Q_COLS]).tolist()
    q, kv, g, mq, lat = jnp.split(h, cuts, axis=-1)
    q = rmsnorm(q.reshape(B, T, NSA_HEADS, NSA_DK), ep["g_q"])
    kv = kv.reshape(B, T, 3, 2, NSA_DK)
    cmp_kv = kv[:, :, 0]
    slc_kv = jnp.stack([rmsnorm(kv[:, :, 1, 0], ep["g_k"][1]), kv[:, :, 1, 1]], axis=2)
    win_kv = jnp.stack([rmsnorm(kv[:, :, 2, 0], ep["g_k"][2]), kv[:, :, 2, 1]], axis=2)
    gates = jax.nn.sigmoid(g.astype(jnp.float32)).reshape(B, T, NSA_HEADS, 3)
    mq = mq.reshape(B, T, MLA_HEADS, MLA_NOPE + MLA_ROPE)
    qn = rmsnorm(mq[..., :MLA_NOPE], ep["g_qn"])
    qp = rope(rmsnorm(mq[..., MLA_NOPE:], ep["g_qpe"]), pos)
    c = rmsnorm(lat[..., :MLA_LORA], ep["g_lat"])
    kp = rope(rmsnorm(lat[..., MLA_LORA:], ep["g_kpe"])[:, :, None, :], pos)[:, :, 0, :]
    return q, cmp_kv, slc_kv, win_kv, gates, qn, qp, jnp.concatenate([c, kp], axis=-1)


def even_output(x, o_nsa, o_mla, w_out):
    B, T, D = x.shape
    a = jnp.concatenate([o_nsa.reshape(B, T, -1), o_mla.reshape(B, T, -1)], axis=-1)
    return matmul_res(a.reshape(B * T, EVEN_OUT), w_out, x.reshape(B * T, D)).reshape(B, T, D)


def odd_project(x, g_norm, op):
    B, T, D = x.shape
    h = norm_matmul(x.reshape(B * T, D), g_norm, op["w_in"]).reshape(B, T, ODD_IN)
    q, k, v = jnp.split(h, [SWA_HEADS * SWA_HD, SWA_HEADS * SWA_HD + SWA_KV_HEADS * SWA_HD], axis=-1)
    q = rmsnorm(q.reshape(B, T, SWA_HEADS, SWA_HD), op["g_q"])
    k = rmsnorm(k.reshape(B, T, SWA_KV_HEADS, SWA_HD), op["g_k"])
    v = v.reshape(B, T, SWA_KV_HEADS, SWA_HD)
    return q, jnp.stack([k, v], axis=2)


def memory_kv(mem, cp):
    B, M, D = mem.shape
    w_kv = jnp.concatenate([cp["w_k"], cp["w_v"]], axis=1)
    kv = norm_matmul(mem.reshape(B * M, D), cp["g_mem"], w_kv).reshape(B, M, 2, MEM_HEADS, MEM_HD)
    k = rmsnorm(kv[:, :, 0], cp["g_k"])
    return jnp.stack([k, kv[:, :, 1]], axis=2)


def memory_cross(x, g_norm, mem_kv, cp):
    B, T, D = x.shape
    q = norm_matmul(x.reshape(B * T, D), g_norm, cp["w_q"]).reshape(B, T, MEM_HEADS, MEM_HD)
    q = rmsnorm(q, cp["g_q"])
    s = jnp.einsum("bthd,bmhd->bhtm", q, mem_kv[:, :, 0]).astype(jnp.float32) * (MEM_HD ** -0.5)
    p = jax.nn.softmax(s, axis=-1).astype(x.dtype)
    o = jnp.einsum("bhtm,bmhd->bthd", p, mem_kv[:, :, 1]).reshape(B * T, MEM_W)
    return matmul_res(o, cp["w_o"], x.reshape(B * T, D)).reshape(B, T, D)


PEER_TB = 128
PEER_VMEM_LIMIT_BYTES = 56 * 1024 * 1024
F32_NEG_INF = float("-inf")


def _topk_rows(s, row, k):
    nrow = float(s.shape[0])
    vals, ids = [], []
    for _ in range(k):
        m = jnp.max(s, axis=0, keepdims=True)
        i = jnp.min(jnp.where(s == m, row, nrow), axis=0, keepdims=True)
        vals.append(m)
        ids.append(i)
        s = jnp.where(row == i, F32_NEG_INF, s)
    return vals, ids


def _peer_route_kernel(x_ref, g_ref, wq_ref, sk0_ref, sk1_ref, xn_ref, idx_ref, gate_ref):
    x = x_ref[...]
    xn = x * lax.rsqrt(jnp.mean(x * x, axis=-1, keepdims=True) + EPS) * g_ref[...]
    xn_ref[...] = xn
    q = jnp.dot(xn.astype(jnp.bfloat16), wq_ref[...], preferred_element_type=jnp.float32)
    tb = x.shape[0]
    row_k = lax.broadcasted_iota(jnp.int32, (PEER_NKEYS, tb), 0).astype(jnp.float32)
    sub8 = lax.broadcasted_iota(jnp.int32, (8, tb), 0).astype(jnp.float32)
    assert PEER_TOPK % 16 == 0
    nt = (((1,), (1,)), ((), ()))
    ids, gates = [], []
    for h in range(PEER_HEADS):
        qh = q[:, h * PEER_DKEY:(h + 1) * PEER_DKEY].astype(jnp.bfloat16)
        s1 = lax.dot_general(sk0_ref[...], qh, nt, preferred_element_type=jnp.float32)
        s2 = lax.dot_general(sk1_ref[...], qh, nt, preferred_element_type=jnp.float32)
        v1, i1 = _topk_rows(s1, row_k, PEER_TOPK)
        v2, i2 = _topk_rows(s2, row_k, PEER_TOPK)
        v1c = jnp.concatenate(v1, axis=0)
        i1c = jnp.concatenate(i1, axis=0)
        v2c = jnp.concatenate(v2, axis=0)
        i2c = jnp.concatenate(i2, axis=0)
        cands, poss, cids = [], [], []
        for a in range(PEER_TOPK // 2):
            bmax = PEER_TOPK // (a + 1) - 1
            for b0 in range(0, bmax + 1, 8):
                c = v1[a] + v2c[b0:b0 + 8]
                if bmax - b0 + 1 < 8:
                    c = jnp.where(sub8 <= float(bmax - b0), c, F32_NEG_INF)
                cands.append(c)
                poss.append(sub8 + float(a * PEER_TOPK + b0))
                cids.append(i1[a] * float(PEER_NKEYS) + i2c[b0:b0 + 8])
        for a0 in range(PEER_TOPK // 2, PEER_TOPK, 8):
            cands.append(v1c[a0:a0 + 8] + v2[0])
            poss.append((sub8 + float(a0)) * float(PEER_TOPK))
            cids.append(i1c[a0:a0 + 8] * float(PEER_NKEYS) + i2[0])
        cand = jnp.concatenate(cands, axis=0)
        row_c = jnp.concatenate(poss, axis=0)
        cid = jnp.concatenate(cids, axis=0)
        tops, tids = [], []
        for _ in range(PEER_TOPK):
            m = jnp.max(cand, axis=0, keepdims=True)
            pos = jnp.min(jnp.where(cand == m, row_c, float(PEER_TOPK * PEER_TOPK)), axis=0, keepdims=True)
            hit = row_c == pos
            tids.append(jnp.sum(jnp.where(hit, cid, 0.0), axis=0, keepdims=True))
            tops.append(m)
            cand = jnp.where(hit, F32_NEG_INF, cand)
        ts = jnp.concatenate(tops, axis=0)
        e = jnp.exp(ts - tops[0])
        gates.append(e / jnp.sum(e, axis=0, keepdims=True))
        ids.append(jnp.concatenate(tids, axis=0))
    table_rows = x.shape[1] // 256
    idx_ref[...] = (jnp.concatenate(ids, axis=0) * float(table_rows)).T.astype(jnp.int32)
    gate_ref[...] = jnp.concatenate(gates, axis=0).T


def peer_route(x, g, w_q, subkeys):
    n, d = x.shape
    tb = min(PEER_TB, n)
    hk = PEER_HEADS * PEER_TOPK
    half = PEER_DKEY // 2
    z = jnp.zeros((PEER_NKEYS, half), jnp.float32)
    sk0 = jnp.concatenate([subkeys[0], z], axis=1).astype(jnp.bfloat16)
    sk1 = jnp.concatenate([z, subkeys[1]], axis=1).astype(jnp.bfloat16)
    return pl.pallas_call(
        _peer_route_kernel,
        grid=(n // tb,),
        in_specs=[pl.BlockSpec((tb, d), lambda i: (i, 0)),
                  pl.BlockSpec((1, d), lambda i: (0, 0)),
                  pl.BlockSpec((d, PEER_HEADS * PEER_DKEY), lambda i: (0, 0)),
                  pl.BlockSpec((PEER_NKEYS, PEER_DKEY), lambda i: (0, 0)),
                  pl.BlockSpec((PEER_NKEYS, PEER_DKEY), lambda i: (0, 0))],
        out_specs=[pl.BlockSpec((tb, d), lambda i: (i, 0)),
                   pl.BlockSpec((tb, hk), lambda i: (i, 0)),
                   pl.BlockSpec((tb, hk), lambda i: (i, 0))],
        out_shape=[jax.ShapeDtypeStruct((n, d), jnp.float32),
                   jax.ShapeDtypeStruct((n, hk), jnp.int32),
                   jax.ShapeDtypeStruct((n, hk), jnp.float32)],
        compiler_params=pltpu.CompilerParams(dimension_semantics=("arbitrary",),
                                             vmem_limit_bytes=PEER_VMEM_LIMIT_BYTES),
        name="peer_route",
    )(x, g.reshape(1, d), w_q.astype(jnp.bfloat16), sk0, sk1)


def pack_table(t):
    e, d = t.shape
    b = lax.bitcast_convert_type(t.astype(jnp.bfloat16), jnp.uint16).astype(jnp.uint32)
    w = b[:, : d // 2] | (b[:, d // 2:] << 16)
    return w.reshape(e * d // 256, 128)


def _table_spec(tab):
    return pl.BlockSpec(tab.shape, lambda i: (0, 0), pipeline_mode=pl.Buffered(1))


def _gather_row(tab, row0, rows):
    wds = tab[pl.ds(pl.multiple_of(row0, rows), rows), :]
    lo = pltpu.bitcast(wds << 16, jnp.float32)
    hi = pltpu.bitcast(wds & jnp.uint32(0xFFFF0000), jnp.float32)
    return lo, hi


def _peer_u_kernel(idx_ref, xn_ref, gate_ref, tab, w_ref, slots, rsum, act):
    tb, hk = gate_ref.shape
    rows = xn_ref.shape[1] // 2

    def token(t, c):
        xt = xn_ref[t]
        xlo, xhi = xt[:rows], xt[rows:]
        for k in range(hk):
            lo, hi = _gather_row(tab, idx_ref[t, k], rows)
            slots[k * rows:(k + 1) * rows, :] = lo * xlo + hi * xhi
        r = slots[pl.ds(0, hk, stride=rows), :]
        for s in range(1, rows):
            r = r + slots[pl.ds(s, hk, stride=rows), :]
        rsum[pl.ds(pl.multiple_of(t * hk, hk), hk), :] = r
        return c

    lax.fori_loop(0, tb, token, 0)

    ones = jnp.ones((128, 128), jnp.bfloat16)
    grp = 8
    ri = lax.broadcasted_iota(jnp.int32, (grp * hk, 128), 0)
    ci = lax.broadcasted_iota(jnp.int32, (grp * hk, 128), 1)
    eye = (ri % hk) == ci

    def lane_sum(c, carry):
        rr = rsum[pl.ds(pl.multiple_of(c * grp * hk, grp * hk), grp * hk), :]
        hi = rr.astype(jnp.bfloat16)
        lo = (rr - hi.astype(jnp.float32)).astype(jnp.bfloat16)
        m = (jnp.dot(hi, ones, preferred_element_type=jnp.float32)
             + jnp.dot(lo, ones, preferred_element_type=jnp.float32))
        d = jnp.sum(jnp.where(eye, m, 0.0).reshape(grp, hk, 128), axis=1)
        act[pl.ds(pl.multiple_of(c * grp, grp), grp), :] = d
        return carry

    lax.fori_loop(0, tb // grp, lane_sum, 0)
    w_ref[...] = gate_ref[...] * jax.nn.gelu(act[...])


def peer_activate(idx, xn, gate, tab):
    n, d = xn.shape
    hk = idx.shape[1]
    tb = min(PEER_TB, n)
    sub = d // 128
    return pl.pallas_call(
        _peer_u_kernel,
        grid=(n // tb,),
        in_specs=[pl.BlockSpec((tb, hk), lambda i: (i, 0), memory_space=pltpu.SMEM),
                  pl.BlockSpec((tb, sub, 128), lambda i: (i, 0, 0)),
                  pl.BlockSpec((tb, hk), lambda i: (i, 0)),
                  _table_spec(tab)],
        out_specs=pl.BlockSpec((tb, hk), lambda i: (i, 0)),
        out_shape=jax.ShapeDtypeStruct((n, hk), jnp.float32),
        scratch_shapes=[pltpu.VMEM((hk * sub // 2, 128), jnp.float32),
                        pltpu.VMEM((tb * hk, 128), jnp.float32),
                        pltpu.VMEM((tb, hk), jnp.float32)],
        compiler_params=pltpu.CompilerParams(dimension_semantics=("arbitrary",),
                                             vmem_limit_bytes=PEER_VMEM_LIMIT_BYTES),
        name="peer_u",
    )(idx, xn.reshape(n, sub, 128), gate, tab)


def _peer_v_kernel(idx_ref, w_ref, x_ref, tab, o_ref):
    tb, hk = idx_ref.shape
    rows = x_ref.shape[1] // 2

    def token(t, c):
        nacc = 2
        lo_acc = [jnp.zeros((rows, 128), jnp.float32) for _ in range(nacc)]
        hi_acc = [jnp.zeros((rows, 128), jnp.float32) for _ in range(nacc)]
        for k in range(hk):
            lo, hi = _gather_row(tab, idx_ref[t, k], rows)
            wk = w_ref[t, k]
            lo_acc[k % nacc] = lo_acc[k % nacc] + wk * lo
            hi_acc[k % nacc] = hi_acc[k % nacc] + wk * hi
        xt = x_ref[t]
        o_ref[t, :rows, :] = xt[:rows] + (lo_acc[0] + lo_acc[1])
        o_ref[t, rows:, :] = xt[rows:] + (hi_acc[0] + hi_acc[1])
        return c

    lax.fori_loop(0, tb, token, 0)


def peer_combine(idx, w, x, tab):
    n, d = x.shape
    hk = idx.shape[1]
    tb = min(PEER_TB, n)
    sub = d // 128
    out = pl.pallas_call(
        _peer_v_kernel,
        grid=(n // tb,),
        in_specs=[pl.BlockSpec((tb, hk), lambda i: (i, 0), memory_space=pltpu.SMEM),
                  pl.BlockSpec((tb, hk), lambda i: (i, 0), memory_space=pltpu.SMEM),
                  pl.BlockSpec((tb, sub, 128), lambda i: (i, 0, 0)),
                  _table_spec(tab)],
        out_specs=pl.BlockSpec((tb, sub, 128), lambda i: (i, 0, 0)),
        out_shape=jax.ShapeDtypeStruct((n, sub, 128), jnp.float32),
        compiler_params=pltpu.CompilerParams(dimension_semantics=("arbitrary",),
                                             vmem_limit_bytes=PEER_VMEM_LIMIT_BYTES),
        name="peer_v",
    )(idx, w, x.reshape(n, sub, 128), tab)
    return out.reshape(n, d)


def peer_ffn(x, g_norm, pp):
    B, T, D = x.shape
    x2 = x.reshape(B * T, D)
    xn, idx, gate = peer_route(x2, g_norm, pp["w_q"], pp["subkeys"])
    w = peer_activate(idx, xn, gate, pp["u_packed"])
    return peer_combine(idx, w, x2, pp["v_packed"]).reshape(B, T, D)


def kernel(x_prompt, x_sample, mem_prompt, cache_nsa_cmp, cache_nsa_slc, cache_nsa_win, cache_mla, cache_swa,
           cache_mem, page_table, norm_mix, norm_cross, norm_ffn, even_w_in, even_w_out, nsa_g_q, nsa_g_k,
           nsa_cmp_pos, nsa_cmp_w1, nsa_cmp_w2, mla_g_qn, mla_g_qpe, mla_g_lat, mla_g_kpe, mla_g_kn, mla_w_uk,
           mla_w_uv, odd_w_in, odd_w_out, swa_g_q, swa_g_k, swa_sinks, mem_g, mem_w_q, mem_w_k, mem_w_v,
           mem_g_q, mem_g_k, mem_w_o, peer_w_q, peer_subkeys, peer_u, peer_v):
    depth = norm_mix.shape[0]
    slopes_nsa = alibi_slopes(NSA_HEADS)
    slopes_swa = alibi_slopes(SWA_HEADS)
    past_len = page_table.shape[1] * PAGE_SIZE
    T, Tn = x_prompt.shape[1], x_sample.shape[1]
    pos_p = jnp.arange(T)
    pos_s = past_len + jnp.arange(Tn)
    xp, xs = x_prompt, x_sample
    st_cmp_p, st_cmp_s, st_slc_p, st_slc_s, st_win_p, st_win_s = [], [], [], [], [], []
    st_mla_p, st_mla_s, st_swa_p, st_swa_s, st_mem_p = [], [], [], [], []

    for li in range(depth):
        if li % 2 == 0:
            e = li // 2
            ep = dict(w_in=even_w_in[e], g_q=nsa_g_q[e], g_k=nsa_g_k[e], cmp_pos=nsa_cmp_pos[e],
                      cmp_w1=nsa_cmp_w1[e], cmp_w2=nsa_cmp_w2[e], g_qn=mla_g_qn[e], g_qpe=mla_g_qpe[e],
                      g_lat=mla_g_lat[e], g_kpe=mla_g_kpe[e], g_kn=mla_g_kn[e], w_uk=mla_w_uk[e], w_uv=mla_w_uv[e])
            q, cmp_kv, slc_kv, win_kv, gates, qn, qp, mla_row = even_project(xp, norm_mix[li], pos_p, ep)
            o_nsa = nsa_prompt(q, cmp_kv, slc_kv, win_kv, gates, ep, slopes_nsa)
            o_mla = mla_prompt(qn, qp, mla_row, ep)
            xp = even_output(xp, o_nsa, o_mla, even_w_out[e])
            st_cmp_p.append(cmp_kv)
            st_slc_p.append(slc_kv)
            st_win_p.append(win_kv[:, -min(NSA_WIN, T):])
            st_mla_p.append(mla_row)
            win_buf = cache_nsa_win[e]
            q, cmp_kv, slc_kv, win_kv, gates, qn, qp, mla_row = even_project(xs, norm_mix[li], pos_s, ep)
            o_nsa = nsa_sample(q, pos_s, cmp_kv, slc_kv, win_kv, gates, cache_nsa_cmp, cache_nsa_slc, win_buf,
                               page_table, e, ep, slopes_nsa)
            o_mla = mla_sample(qn, qp, mla_row, cache_mla, page_table, e, ep)
            xs = even_output(xs, o_nsa, o_mla, even_w_out[e])
            st_cmp_s.append(cmp_kv)
            st_slc_s.append(slc_kv)
            st_win_s.append(jnp.concatenate([win_buf, win_kv], axis=1)[:, -win_buf.shape[1]:])
            st_mla_s.append(mla_row)
        else:
            o = li // 2
            op = dict(w_in=odd_w_in[o], g_q=swa_g_q[o], g_k=swa_g_k[o])
            B = xp.shape[0]
            q, kv = odd_project(xp, norm_mix[li], op)
            att = jnp.transpose(swa_prompt_attention(q, kv[:, :, 0], kv[:, :, 1], swa_sinks[o]), (0, 2, 1, 3))
            xp = matmul_res(att.reshape(B * T, ODD_OUT), odd_w_out[o], xp.reshape(B * T, -1)).reshape(xp.shape)
            st_swa_p.append(kv[:, -min(SWA_WIN, T):])
            buf = cache_swa[o]
            q, kv = odd_project(xs, norm_mix[li], op)
            att = window_sample(q, buf[:, :, 0], buf[:, :, 1], kv[:, :, 0], kv[:, :, 1], past_len, SWA_WIN,
                                slopes_swa, swa_sinks[o])
            xs = matmul_res(att.reshape(xs.shape[0] * Tn, ODD_OUT), odd_w_out[o],
                            xs.reshape(xs.shape[0] * Tn, -1)).reshape(xs.shape)
            st_swa_s.append(jnp.concatenate([buf, kv], axis=1)[:, -buf.shape[1]:])

        cp = dict(g_mem=mem_g[li], w_q=mem_w_q[li], w_k=mem_w_k[li], w_v=mem_w_v[li], g_q=mem_g_q[li],
                  g_k=mem_g_k[li], w_o=mem_w_o[li])
        mem_kv_p = memory_kv(mem_prompt, cp)
        xp = memory_cross(xp, norm_cross[li], mem_kv_p, cp)
        xs = memory_cross(xs, norm_cross[li], cache_mem[li], cp)
        st_mem_p.append(mem_kv_p)

        pp = dict(w_q=peer_w_q[li], subkeys=peer_subkeys[li],
                  u_packed=pack_table(peer_u[li]), v_packed=pack_table(peer_v[li]))
        xp = peer_ffn(xp, norm_ffn[li], pp)
        xs = peer_ffn(xs, norm_ffn[li], pp)

    return (xp, xs, jnp.stack(st_cmp_p), jnp.stack(st_cmp_s), jnp.stack(st_slc_p), jnp.stack(st_slc_s),
            jnp.stack(st_win_p), jnp.stack(st_win_s), jnp.stack(st_mla_p), jnp.stack(st_mla_s),
            jnp.stack(st_swa_p), jnp.stack(st_swa_s), jnp.stack(st_mem_p))
```

```python
import functools
import math

import jax
import jax.numpy as jnp
import numpy as np
from jax import lax
from jax.experimental import pallas as pl
from jax.experimental.pallas import tpu as pltpu

PAGE_SIZE = 128
Q_BLOCK = 128
EPS = 1e-6
NEG_INF = -1e30
ROPE_BASE = 10000.0

NSA_HEADS = 8
NSA_DK = 64
NSA_BLK = 64
NSA_TOPN = 16
NSA_WIN = 512
NSA_CMP_HID = 256
MLA_HEADS = 8
MLA_NOPE = 64
MLA_ROPE = 32
MLA_V = 64
MLA_LORA = 128
MLA_ROW = MLA_LORA + MLA_ROPE
SWA_HEADS = 16
SWA_KV_HEADS = 4
SWA_HD = 64
SWA_WIN = 128
N_MEM = 256
MEM_HEADS = 4
MEM_HD = 128
MEM_W = MEM_HEADS * MEM_HD
PEER_HEADS = 8
PEER_NKEYS = 128
PEER_N = PEER_NKEYS * PEER_NKEYS
PEER_DKEY = 128
PEER_TOPK = 16
PEER_CHUNK = 256

NSA_Q_COLS = NSA_HEADS * NSA_DK
NSA_KV_COLS = 3 * 2 * NSA_DK
NSA_GATE_COLS = NSA_HEADS * 3
MLA_Q_COLS = MLA_HEADS * (MLA_NOPE + MLA_ROPE)
EVEN_IN = NSA_Q_COLS + NSA_KV_COLS + NSA_GATE_COLS + MLA_Q_COLS + MLA_ROW
EVEN_OUT = NSA_HEADS * NSA_DK + MLA_HEADS * MLA_V
ODD_IN = SWA_HEADS * SWA_HD + 2 * SWA_KV_HEADS * SWA_HD
ODD_OUT = SWA_HEADS * SWA_HD

VMEM_LIMIT_BYTES = 48 * 1024 * 1024


def _row_tile(n, target=512):
    t = min(n, target)
    while n % t:
        t //= 2
    return t


def _norm_matmul_kernel(x_ref, g_ref, w_ref, o_ref):
    x = x_ref[...]
    y = x * lax.rsqrt(jnp.mean(x * x, axis=-1, keepdims=True) + EPS) * g_ref[...]
    o_ref[...] = jnp.dot(y.astype(jnp.bfloat16), w_ref[...], preferred_element_type=jnp.float32)


def norm_matmul(x, g, w):
    n, d = x.shape
    c = w.shape[1]
    tm = _row_tile(n)
    return pl.pallas_call(
        _norm_matmul_kernel,
        grid=(n // tm,),
        in_specs=[pl.BlockSpec((tm, d), lambda i: (i, 0)),
                  pl.BlockSpec((1, d), lambda i: (0, 0)),
                  pl.BlockSpec((d, c), lambda i: (0, 0))],
        out_specs=pl.BlockSpec((tm, c), lambda i: (i, 0)),
        out_shape=jax.ShapeDtypeStruct((n, c), jnp.float32),
        compiler_params=pltpu.CompilerParams(dimension_semantics=("arbitrary",),
                                             vmem_limit_bytes=VMEM_LIMIT_BYTES),
        name="norm_matmul",
    )(x, g.reshape(1, d), w.astype(jnp.bfloat16))


def _matmul_res_kernel(a_ref, w_ref, r_ref, o_ref):
    o_ref[...] = r_ref[...] + jnp.dot(a_ref[...].astype(jnp.bfloat16), w_ref[...],
                                      preferred_element_type=jnp.float32)


def matmul_res(a, w, res):
    n, k = a.shape
    c = w.shape[1]
    tm = _row_tile(n)
    return pl.pallas_call(
        _matmul_res_kernel,
        grid=(n // tm,),
        in_specs=[pl.BlockSpec((tm, k), lambda i: (i, 0)),
                  pl.BlockSpec((k, c), lambda i: (0, 0)),
                  pl.BlockSpec((tm, c), lambda i: (i, 0))],
        out_specs=pl.BlockSpec((tm, c), lambda i: (i, 0)),
        out_shape=jax.ShapeDtypeStruct((n, c), jnp.float32),
        compiler_params=pltpu.CompilerParams(dimension_semantics=("arbitrary",),
                                             vmem_limit_bytes=VMEM_LIMIT_BYTES),
        name="matmul_res",
    )(a, w.astype(jnp.bfloat16), res)


def rmsnorm(x, g):
    xf = x.astype(jnp.float32)
    y = xf * lax.rsqrt(jnp.mean(xf * xf, axis=-1, keepdims=True) + EPS)
    return (y * g.astype(jnp.float32)).astype(x.dtype)


def alibi_slopes(n_heads):
    return jnp.asarray((2.0 ** (-8.0 * np.arange(1, n_heads + 1) / n_heads)).astype(np.float32))


def rope(x, pos):
    d = x.shape[-1]
    inv = jnp.asarray(np.power(ROPE_BASE, -np.arange(0, d, 2, dtype=np.float32) / d).astype(np.float32))
    ang = pos.astype(jnp.float32)[:, None] * inv[None, :]
    cos = jnp.cos(ang)[None, :, None, :]
    sin = jnp.sin(ang)[None, :, None, :]
    xf = x.astype(jnp.float32)
    x1, x2 = xf[..., : d // 2], xf[..., d // 2:]
    return jnp.concatenate([x1 * cos - x2 * sin, x1 * sin + x2 * cos], axis=-1).astype(x.dtype)


def masked_softmax(s, valid, sink=None):
    s = jnp.where(valid, s, NEG_INF)
    m = jnp.max(s, axis=-1, keepdims=True)
    if sink is not None:
        sk = sink.astype(jnp.float32)[:, None, None]
        m = jnp.maximum(m, sk)
    e = jnp.where(valid, jnp.exp(s - m), 0.0)
    den = jnp.sum(e, axis=-1, keepdims=True)
    if sink is not None:
        den = den + jnp.exp(sk - m)
    return e / jnp.maximum(den, 1e-30)


def attend(q, k, v, dist, valid, slopes, sink=None):
    B, Q, H, d = q.shape
    G = k.shape[-2]
    qg = q.reshape(B, Q, G, H // G, d)
    if k.ndim == 5:
        s = jnp.einsum("bqgrd,bqkgd->bgrqk", qg, k)
    else:
        s = jnp.einsum("bqgrd,bkgd->bgrqk", qg, k)
    K = s.shape[-1]
    s = s.reshape(B, H, Q, K).astype(jnp.float32) * (d ** -0.5)
    if slopes is not None:
        s = s - slopes[:, None, None] * dist[..., None, :, :].astype(jnp.float32)
    p = masked_softmax(s, valid[..., None, :, :], sink)
    pg = p.reshape(B, G, H // G, Q, K).astype(v.dtype)
    if v.ndim == 5:
        o = jnp.einsum("bgrqk,bqkgd->bqgrd", pg, v)
    else:
        o = jnp.einsum("bgrqk,bkgd->bqgrd", pg, v)
    return o.reshape(B, Q, H, v.shape[-1]), p


def alibi_slopes_np(n_heads):
    return (2.0 ** (-8.0 * np.arange(1, n_heads + 1) / n_heads)).astype(np.float32)


Q_TILE = 128
SLC_CHUNK = 512
MASK_BIG = -(2.0 ** 100)


def _nsa_prompt_kernel(q_ref, kt_ref, vs_ref, kwt_ref, vw_ref, kct_ref, vc_ref, g_ref, slope_ref, wbias_ref, cbias_ref,
                       o_ref, *, n_blocks):
    H, TQ = NSA_HEADS, Q_TILE
    i = pl.program_id(1)
    s0 = i * TQ
    q_lo = q_ref[0].reshape(H * TQ, 128)
    slope = slope_ref[...]

    s = jnp.dot(q_lo, kct_ref[0], preferred_element_type=jnp.float32).reshape(H, TQ, n_blocks)
    tpos = s0 + lax.broadcasted_iota(jnp.int32, (TQ, n_blocks), 0)
    blk = lax.broadcasted_iota(jnp.int32, (TQ, n_blocks), 1)
    dist = tpos - ((blk + 1) * NSA_BLK - 1)
    valid = (dist >= 0)[None]
    s = jnp.where(valid, s - slope * dist.astype(jnp.float32)[None], NEG_INF)
    m = jnp.max(s, axis=-1, keepdims=True)
    e = jnp.where(valid, jnp.exp(s - m), 0.0)
    p = e / jnp.maximum(jnp.sum(e, axis=-1, keepdims=True), 1e-30)
    o_cmp = jnp.dot(p.reshape(H * TQ, n_blocks).astype(jnp.bfloat16), vc_ref[0],
                    preferred_element_type=jnp.float32)
    imp = jnp.sum(p, axis=0)

    cur = tpos // NSA_BLK
    imp = jnp.where(blk == cur, float(H + 1), jnp.where(blk < cur, imp, -1.0))
    blk_f = blk.astype(jnp.float32)
    sel = jnp.zeros((TQ, n_blocks), jnp.bool_)
    for _ in range(min(NSA_TOPN, n_blocks)):
        mx = jnp.max(imp, axis=-1, keepdims=True)
        pick = jnp.min(jnp.where(imp == mx, blk_f, float(n_blocks)), axis=-1, keepdims=True)
        hit = blk_f == pick
        sel = sel | hit
        imp = jnp.where(hit, -2.0, imp)
    nsb = jnp.where(sel, 0.0, MASK_BIG).astype(jnp.bfloat16)
    q_aug = jnp.concatenate([q_lo, jnp.broadcast_to(nsb[None], (H, TQ, n_blocks)).reshape(H * TQ, n_blocks)], axis=1)

    KC = SLC_CHUNK

    def slc_step(c, carry, bias):
        m_i, acc = carry
        k0 = pl.multiple_of(c * KC, KC)
        sc = jnp.dot(q_aug, kt_ref[0, :, pl.ds(k0, KC)], preferred_element_type=jnp.float32)
        if bias is not None:
            sc = (sc.reshape(H, TQ, KC) + bias[None]).reshape(H * TQ, KC)
        m_new = jnp.maximum(m_i, jnp.max(sc, axis=-1, keepdims=True))
        corr = jnp.exp(m_i - m_new)
        pe = jnp.exp(sc - m_new)
        acc_new = acc * corr + jnp.dot(pe.astype(jnp.bfloat16), vs_ref[0, pl.ds(k0, KC), :],
                                       preferred_element_type=jnp.float32)
        return m_new, acc_new

    init = (jnp.full((H * TQ, 1), NEG_INF, jnp.float32), jnp.zeros((H * TQ, 128), jnp.float32))
    c_diag = s0 // KC
    carry = lax.fori_loop(0, c_diag, lambda c, cr: slc_step(c, cr, None), init)
    m_i, acc = slc_step(c_diag, carry, cbias_ref[(s0 % KC) // TQ])
    o_slc = acc[:, :NSA_DK] / acc[:, NSA_DK:NSA_DK + 1]

    nwc = NSA_WIN // TQ + 1
    parts = []
    for r in range(nwc):
        cidx = i - (nwc - 1) + r
        k0 = pl.multiple_of(jnp.maximum(cidx, 0) * TQ, TQ)
        sw = jnp.dot(q_lo[:, :NSA_DK], kwt_ref[0, :, pl.ds(k0, TQ)], preferred_element_type=jnp.float32)
        b = jnp.where(cidx >= 0, wbias_ref[r], NEG_INF)
        parts.append((sw.reshape(H, TQ, TQ) + b).reshape(H * TQ, TQ))
    mw = parts[0].max(axis=-1, keepdims=True)
    for r in range(1, nwc):
        mw = jnp.maximum(mw, parts[r].max(axis=-1, keepdims=True))
    accw = jnp.zeros((H * TQ, 128), jnp.float32)
    for r in range(nwc):
        cidx = i - (nwc - 1) + r
        k0 = pl.multiple_of(jnp.maximum(cidx, 0) * TQ, TQ)
        pe = jnp.exp(parts[r] - mw)
        accw = accw + jnp.dot(pe.astype(jnp.bfloat16), vw_ref[0, pl.ds(k0, TQ), :], preferred_element_type=jnp.float32)
    o_win = accw[:, :NSA_DK] / accw[:, NSA_DK:NSA_DK + 1]

    g = g_ref[0].reshape(H * TQ, 3)
    o = g[:, 0:1] * o_cmp + g[:, 1:2] * o_slc + g[:, 2:3] * o_win
    o_ref[0] = o.reshape(H, TQ, NSA_DK)


def _with_ones_column(v):
    ones = jnp.ones(v.shape[:-1] + (1,), v.dtype)
    zeros = jnp.zeros(v.shape[:-1] + (127 - v.shape[-1],), v.dtype)
    return jnp.concatenate([v, ones, zeros], axis=-1).astype(jnp.bfloat16)


def nsa_prompt_attention(q, slc_k, slc_v, win_k, win_v, kc, vc, gates):
    B, T, H, dk = q.shape
    nb = T // NSA_BLK
    nbp = 128
    TQ = Q_TILE
    assert nb <= nbp and T % SLC_CHUNK == 0 and NSA_WIN % TQ == 0
    slopes = alibi_slopes_np(H)
    bf = jnp.bfloat16
    qh = jnp.transpose(q, (0, 2, 1, 3)) * (dk ** -0.5)
    al = np.zeros((H, 128 - dk), np.float32)
    al[:, 0] = slopes * 128.0
    al[:, 1] = slopes
    q_lo = jnp.concatenate([qh, jnp.broadcast_to(jnp.asarray(al)[None, :, None, :], (B, H, T, 128 - dk))],
                           axis=-1).astype(bf)
    pos = np.arange(T)
    crow = np.zeros((128 - dk + nbp, T), np.float32)
    crow[0] = pos // 128
    crow[1] = pos % 128
    crow[128 - dk + pos // NSA_BLK, pos] = 1.0
    kt = jnp.concatenate([jnp.transpose(slc_k, (0, 2, 1)),
                          jnp.broadcast_to(jnp.asarray(crow)[None], (B,) + crow.shape)], axis=1).astype(bf)
    kwt = jnp.transpose(win_k, (0, 2, 1)).astype(bf)
    kct = jnp.pad(jnp.transpose(kc, (0, 2, 1)), ((0, 0), (0, 128 - dk), (0, nbp - nb))).astype(bf)
    vcp = jnp.pad(vc, ((0, 0), (0, nbp - nb), (0, 0))).astype(bf)
    gh = jnp.transpose(gates, (0, 2, 1, 3))
    nwc = NSA_WIN // TQ + 1
    ii = np.arange(TQ)[:, None]
    jj = np.arange(TQ)[None, :]
    wb = np.zeros((nwc, H, TQ, TQ), np.float32)
    for r in range(nwc):
        d = ii - jj + TQ * (nwc - 1 - r)
        ok = (d >= 0) & (d <= NSA_WIN)
        wb[r] = np.where(ok[None], -slopes[:, None, None] * d[None].astype(np.float32), NEG_INF)
    nv = SLC_CHUNK // TQ
    cb = np.zeros((nv, TQ, SLC_CHUNK), np.float32)
    for v in range(nv):
        cb[v] = np.where(np.arange(SLC_CHUNK)[None, :] <= v * TQ + ii, 0.0, NEG_INF)
    kern = functools.partial(_nsa_prompt_kernel, n_blocks=nbp)
    return pl.pallas_call(
        kern,
        grid=(B, T // TQ),
        in_specs=[pl.BlockSpec((1, H, TQ, 128), lambda b, i: (b, 0, i, 0)),
                  pl.BlockSpec((1, 256, T), lambda b, i: (b, 0, 0)),
                  pl.BlockSpec((1, T, 128), lambda b, i: (b, 0, 0)),
                  pl.BlockSpec((1, dk, T), lambda b, i: (b, 0, 0)),
                  pl.BlockSpec((1, T, 128), lambda b, i: (b, 0, 0)),
                  pl.BlockSpec((1, 128, nbp), lambda b, i: (b, 0, 0)),
                  pl.BlockSpec((1, nbp, dk), lambda b, i: (b, 0, 0)),
                  pl.BlockSpec((1, H, TQ, 3), lambda b, i: (b, 0, i, 0)),
                  pl.BlockSpec((H, 1, 1), lambda b, i: (0, 0, 0)),
                  pl.BlockSpec((nwc, H, TQ, TQ), lambda b, i: (0, 0, 0, 0)),
                  pl.BlockSpec((nv, TQ, SLC_CHUNK), lambda b, i: (0, 0, 0))],
        out_specs=pl.BlockSpec((1, H, TQ, dk), lambda b, i: (b, 0, i, 0)),
        out_shape=jax.ShapeDtypeStruct((B, H, T, dk), jnp.float32),
        compiler_params=pltpu.CompilerParams(dimension_semantics=("arbitrary", "arbitrary"),
                                             vmem_limit_bytes=VMEM_LIMIT_BYTES),
        name="nsa_prompt",
    )(q_lo, kt, _with_ones_column(slc_v), kwt, _with_ones_column(win_v), kct, vcp, gh,
      jnp.asarray(slopes).reshape(H, 1, 1), jnp.asarray(wb), jnp.asarray(cb))


MLA_TILE = 512


def _mla_prompt_kernel(q_ref, kt_ref, v_ref, tril_ref, o_ref, *, scale_log2e):
    TQ = MLA_TILE
    i = pl.program_id(2)
    q = q_ref[0, 0]

    def step(c, carry, bias):
        m_i, acc = carry
        k0 = pl.multiple_of(c * TQ, TQ)
        s = jnp.dot(q, kt_ref[0, 0, :, pl.ds(k0, TQ)], preferred_element_type=jnp.float32)
        if bias is not None:
            s = s + bias
        m_new = jnp.maximum(m_i, jnp.max(s, axis=-1, keepdims=True))
        corr = jnp.exp2((m_i - m_new) * scale_log2e)
        p = jnp.exp2((s - m_new) * scale_log2e)
        acc_new = acc * corr + jnp.dot(p.astype(jnp.bfloat16), v_ref[0, 0, pl.ds(k0, TQ), :],
                                       preferred_element_type=jnp.float32)
        return m_new, acc_new

    init = (jnp.full((TQ, 1), NEG_INF, jnp.float32), jnp.zeros((TQ, 128), jnp.float32))
    carry = lax.fori_loop(0, i, lambda c, cr: step(c, cr, None), init)
    m_i, acc = step(i, carry, tril_ref[...])
    o_ref[0, 0] = acc[:, :MLA_V] / acc[:, MLA_V:MLA_V + 1]


def mla_prompt_attention(qn, qp, kn, kp, v):
    B, T, H, _ = qn.shape
    TQ = MLA_TILE
    assert T % TQ == 0
    bf = jnp.bfloat16
    dq = MLA_NOPE + MLA_ROPE
    qcat = jnp.concatenate([qn, qp, jnp.zeros((B, T, H, 128 - dq), qn.dtype)], axis=-1)
    qcat = jnp.transpose(qcat, (0, 2, 1, 3)).astype(bf)
    kcat = jnp.concatenate([kn, jnp.broadcast_to(kp[:, :, None, :], (B, T, H, MLA_ROPE)),
                            jnp.zeros((B, T, H, 128 - dq), kn.dtype)], axis=-1)
    kt = jnp.transpose(kcat, (0, 2, 3, 1)).astype(bf)
    vh = _with_ones_column(jnp.transpose(v, (0, 2, 1, 3)))
    tril = np.where(np.arange(TQ)[None, :] <= np.arange(TQ)[:, None], 0.0, NEG_INF).astype(np.float32)
    kern = functools.partial(_mla_prompt_kernel, scale_log2e=float(dq ** -0.5 * math.log2(math.e)))
    return pl.pallas_call(
        kern,
        grid=(B, H, T // TQ),
        in_specs=[pl.BlockSpec((1, 1, TQ, 128), lambda b, h, i: (b, h, i, 0)),
                  pl.BlockSpec((1, 1, 128, T), lambda b, h, i: (b, h, 0, 0)),
                  pl.BlockSpec((1, 1, T, 128), lambda b, h, i: (b, h, 0, 0)),
                  pl.BlockSpec((TQ, TQ), lambda b, h, i: (0, 0))],
        out_specs=pl.BlockSpec((1, 1, TQ, MLA_V), lambda b, h, i: (b, h, i, 0)),
        out_shape=jax.ShapeDtypeStruct((B, H, T, MLA_V), jnp.float32),
        compiler_params=pltpu.CompilerParams(dimension_semantics=("arbitrary", "arbitrary", "arbitrary"),
                                             vmem_limit_bytes=VMEM_LIMIT_BYTES),
        name="mla_prompt",
    )(qcat, kt, vh, jnp.asarray(tril))


def _swa_prompt_kernel(q_ref, kt_ref, v_ref, bias_ref, sink_ref, o_ref):
    TQ = Q_TILE
    R = SWA_HEADS // SWA_KV_HEADS
    i = pl.program_id(2)
    q = q_ref[0, 0].reshape(R * TQ, SWA_HD)
    kprev = pl.multiple_of(jnp.maximum(i - 1, 0) * TQ, TQ)
    kcur = pl.multiple_of(i * TQ, TQ)
    s0 = jnp.dot(q, kt_ref[0, 0, :, pl.ds(kprev, TQ)], preferred_element_type=jnp.float32).reshape(R, TQ, TQ)
    s1 = jnp.dot(q, kt_ref[0, 0, :, pl.ds(kcur, TQ)], preferred_element_type=jnp.float32).reshape(R, TQ, TQ)
    s0 = s0 + jnp.where(i > 0, bias_ref[0, 0], NEG_INF)
    s1 = s1 + bias_ref[0, 1]
    sink = sink_ref[0]
    m = jnp.maximum(jnp.maximum(s0.max(axis=-1, keepdims=True), s1.max(axis=-1, keepdims=True)), sink)
    p0 = jnp.exp(s0 - m)
    p1 = jnp.exp(s1 - m)
    den = p0.sum(axis=-1, keepdims=True) + p1.sum(axis=-1, keepdims=True) + jnp.exp(sink - m)
    o = (jnp.dot(p0.reshape(R * TQ, TQ).astype(jnp.bfloat16), v_ref[0, 0, pl.ds(kprev, TQ), :],
                 preferred_element_type=jnp.float32)
         + jnp.dot(p1.reshape(R * TQ, TQ).astype(jnp.bfloat16), v_ref[0, 0, pl.ds(kcur, TQ), :],
                   preferred_element_type=jnp.float32))
    o_ref[0, 0] = o.reshape(R, TQ, SWA_HD) / jnp.maximum(den, 1e-30)


def swa_prompt_attention(q, k, v, sinks):
    B, T, H, hd = q.shape
    G = k.shape[2]
    R = H // G
    TQ = Q_TILE
    assert SWA_WIN == TQ and T % TQ == 0
    bf = jnp.bfloat16
    slopes = alibi_slopes_np(H)
    qh = (jnp.transpose(q, (0, 2, 1, 3)) * (hd ** -0.5)).astype(bf).reshape(B, G, R, T, hd)
    kt = jnp.transpose(k, (0, 2, 3, 1)).astype(bf)
    vh = jnp.transpose(v, (0, 2, 1, 3)).astype(bf)
    ii = np.arange(TQ)[:, None]
    jj = np.arange(TQ)[None, :]
    bias = np.zeros((G, 2, R, TQ, TQ), np.float32)
    for r in range(2):
        d = ii - jj + TQ * (1 - r)
        ok = (d >= 0) & (d <= SWA_WIN)
        bias[:, r] = np.where(ok[None, None],
                              -slopes.reshape(G, R)[:, :, None, None] * d[None, None].astype(np.float32), NEG_INF)
    return pl.pallas_call(
        _swa_prompt_kernel,
        grid=(B, G, T // TQ),
        in_specs=[pl.BlockSpec((1, 1, R, TQ, hd), lambda b, g, i: (b, g, 0, i, 0)),
                  pl.BlockSpec((1, 1, hd, T), lambda b, g, i: (b, g, 0, 0)),
                  pl.BlockSpec((1, 1, T, hd), lambda b, g, i: (b, g, 0, 0)),
                  pl.BlockSpec((1, 2, R, TQ, TQ), lambda b, g, i: (g, 0, 0, 0, 0)),
                  pl.BlockSpec((1, R, 1, 1), lambda b, g, i: (g, 0, 0, 0))],
        out_specs=pl.BlockSpec((1, 1, R, TQ, hd), lambda b, g, i: (b, g, 0, i, 0)),
        out_shape=jax.ShapeDtypeStruct((B, G, R, T, hd), jnp.float32),
        compiler_params=pltpu.CompilerParams(dimension_semantics=("arbitrary", "arbitrary", "arbitrary"),
                                             vmem_limit_bytes=VMEM_LIMIT_BYTES),
        name="swa_prompt",
    )(qh, kt, vh, jnp.asarray(bias), sinks.reshape(G, R, 1, 1)).reshape(B, H, T, hd)


def window_sample(q, k_buf, v_buf, k_new, v_new, past_len, window, slopes, sink=None):
    Wb, Tn = k_buf.shape[1], q.shape[1]
    k = jnp.concatenate([k_buf, k_new], axis=1)
    v = jnp.concatenate([v_buf, v_new], axis=1)
    qpos = past_len + jnp.arange(Tn)
    kpos = past_len - Wb + jnp.arange(Wb + Tn)
    dist = qpos[:, None] - kpos[None, :]
    valid = (dist >= 0) & (dist <= window)
    return attend(q, k, v, dist, valid, slopes, sink)[0]


def nsa_compress(blocks, pos_emb, w1, w2):
    x = blocks + pos_emb
    h = jax.nn.gelu(jnp.einsum("...lcd,lcdh->...ch", x, w1))
    return jnp.einsum("...ch,chd->...cd", h, w2)


def nsa_select(p_cmp, qpos, n_blocks):
    imp = jnp.sum(p_cmp, axis=1)
    blk = jnp.arange(n_blocks)[None, :]
    cur = (qpos // NSA_BLK)[:, None]
    imp = jnp.where(blk == cur, float(NSA_HEADS + 1), jnp.where(blk < cur, imp, -1.0))
    _, idx = lax.top_k(imp, min(NSA_TOPN, n_blocks))
    return idx


def nsa_compressed_branch(q, qpos, kc, vc, c_end, slopes):
    dist = qpos[:, None] - c_end[None, :]
    o, p = attend(q, kc[:, :, None, :], vc[:, :, None, :], dist, dist >= 0, slopes)
    return o, nsa_select(p, qpos, kc.shape[1])


def nsa_selected_branch(q, qpos, sel, idx, slopes):
    B, Q, n = idx.shape
    kpos = (idx[..., None] * NSA_BLK + jnp.arange(NSA_BLK)).reshape(B, Q, n * NSA_BLK)
    dist = qpos[None, :, None] - kpos
    ks = sel[..., 0, :].reshape(B, Q, n * NSA_BLK, 1, NSA_DK)
    vs = sel[..., 1, :].reshape(B, Q, n * NSA_BLK, 1, NSA_DK)
    return attend(q, ks, vs, dist, dist >= 0, slopes)[0]


def nsa_gate(gates, o_cmp, o_slc, o_win):
    g = gates.astype(o_cmp.dtype)
    return g[..., 0:1] * o_cmp + g[..., 1:2] * o_slc + g[..., 2:3] * o_win


CMP_PAGES_PER_STEP = 64
PAGED_VMEM_LIMIT_BYTES = 56 * 1024 * 1024


def pages_feature_major(cache):
    return jnp.transpose(cache, (0, 1, 3, 4, 2)).reshape(-1, 2, NSA_DK, PAGE_SIZE)


def _nsa_compress_kernel(pt_ref, pos_ref, w1b_ref, w2_ref, *rest, n_pages):
    page_refs = rest[:n_pages]
    o0_ref, o1_ref, slab = rest[n_pages:]
    feat = 2 * NSA_DK
    for j, r in enumerate(page_refs):
        slab[j * feat:(j + 1) * feat, :] = r[0].reshape(feat, PAGE_SIZE)
    hid2 = 2 * NSA_CMP_HID
    hs = []
    for c in range(2):
        h = jnp.zeros((n_pages, hid2), jnp.float32)
        for d in range(NSA_DK):
            f = c * NSA_DK + d
            xf = slab[pl.ds(f, n_pages, stride=feat), :] + pos_ref[f:f + 1, :]
            h = h + jnp.dot(xf.astype(jnp.bfloat16), w1b_ref[f], preferred_element_type=jnp.float32)
        hs.append(jax.nn.gelu(h).astype(jnp.bfloat16))
    for j, o_ref in enumerate((o0_ref, o1_ref)):
        hj = jnp.concatenate([hs[c][:, j * NSA_CMP_HID:(j + 1) * NSA_CMP_HID] for c in range(2)], axis=1)
        o_ref[0] = jnp.dot(hj, w2_ref[...], preferred_element_type=jnp.float32)


def nsa_compress_paged(pages_t, page_table, pos_emb, w1, w2):
    S, P = page_table.shape
    PG = math.gcd(CMP_PAGES_PER_STEP, P)
    assert PAGE_SIZE == 2 * NSA_BLK and PG % 8 == 0
    bf = jnp.bfloat16
    feat = 2 * NSA_DK
    pos_t = jnp.tile(jnp.transpose(pos_emb, (1, 2, 0)).reshape(feat, NSA_BLK), (1, 2))
    w1f = jnp.transpose(w1, (1, 2, 0, 3)).reshape(feat, NSA_BLK, NSA_CMP_HID)
    z = jnp.zeros_like(w1f)
    w1b = jnp.concatenate([jnp.concatenate([w1f, z], axis=2), jnp.concatenate([z, w1f], axis=2)], axis=1).astype(bf)
    eye = jnp.eye(2, dtype=w2.dtype)
    w2z = jnp.einsum("chd,ce->ched", w2, eye).reshape(2 * NSA_CMP_HID, feat).astype(bf)

    def page_spec(j):
        return pl.BlockSpec((1, 2, NSA_DK, PAGE_SIZE), lambda s, g, pt: (pt[s, g * PG + j], 0, 0, 0))

    grid_spec = pltpu.PrefetchScalarGridSpec(
        num_scalar_prefetch=1,
        grid=(S, P // PG),
        in_specs=[pl.BlockSpec(pos_t.shape, lambda s, g, pt: (0, 0)),
                  pl.BlockSpec(w1b.shape, lambda s, g, pt: (0, 0, 0), pipeline_mode=pl.Buffered(1)),
                  pl.BlockSpec(w2z.shape, lambda s, g, pt: (0, 0))] + [page_spec(j) for j in range(PG)],
        out_specs=[pl.BlockSpec((1, PG, feat), lambda s, g, pt: (s, g, 0)),
                   pl.BlockSpec((1, PG, feat), lambda s, g, pt: (s, g, 0))],
        scratch_shapes=[pltpu.VMEM((PG * feat, PAGE_SIZE), jnp.float32)])
    o0, o1 = pl.pallas_call(
        functools.partial(_nsa_compress_kernel, n_pages=PG),
        grid_spec=grid_spec,
        out_shape=[jax.ShapeDtypeStruct((S, P, feat), jnp.float32)] * 2,
        compiler_params=pltpu.CompilerParams(dimension_semantics=("arbitrary", "arbitrary"),
                                             vmem_limit_bytes=PAGED_VMEM_LIMIT_BYTES),
        name="nsa_compress",
    )(page_table, pos_t, w1b, w2z, *([pages_t] * PG))
    return jnp.stack([o0, o1], axis=2).reshape(S, 2 * P, 2, NSA_DK)


def nsa_prompt(q, cmp_kv, slc_kv, win_kv, gates, ep, slopes):
    B, T = q.shape[:2]
    nb = T // NSA_BLK
    ppb = T // PAGE_SIZE
    prompt_pages = jnp.arange(B * ppb, dtype=jnp.int32).reshape(B, ppb)
    comp = nsa_compress_paged(pages_feature_major(cmp_kv.reshape(1, B * ppb, PAGE_SIZE, 2, NSA_DK)), prompt_pages,
                              ep["cmp_pos"], ep["cmp_w1"], ep["cmp_w2"])
    kc = rmsnorm(comp[:, :, 0], ep["g_k"][0])
    vc = comp[:, :, 1]
    o = nsa_prompt_attention(q, slc_kv[:, :, 0], slc_kv[:, :, 1], win_kv[:, :, 0], win_kv[:, :, 1], kc, vc, gates)
    return jnp.transpose(o, (0, 2, 1, 3))


def _nsa_sample_selected_kernel(phys_ref, idx_ref, q_ref, new_ref, slope_ref, *rest, n_sel, nb_past, qpos):
    page_refs = rest[:n_sel]
    o_ref = rest[n_sel]
    b = pl.program_id(0)
    nt = (((1,), (1,)), ((), ()))
    q = q_ref[0]
    lane = lax.broadcasted_iota(jnp.int32, (1, PAGE_SIZE), 1)
    ss, vts, dists = [], [], []
    for n, r in enumerate(page_refs):
        blk_id = idx_ref[b, n]
        is_new = blk_id >= nb_past
        half = jnp.where(is_new, 0, blk_id % 2)
        kt = jnp.where(is_new, new_ref[0, 0], r[0, 0]).astype(jnp.bfloat16)
        vts.append(jnp.where(is_new, new_ref[0, 1], r[0, 1]).astype(jnp.bfloat16))
        ss.append(jnp.dot(q, kt, preferred_element_type=jnp.float32))
        in_blk = (lane // NSA_BLK) == half
        dists.append(jnp.where(in_blk, qpos - blk_id * NSA_BLK - (lane % NSA_BLK), -1))
    s = jnp.concatenate(ss, axis=1)
    dist = jnp.concatenate(dists, axis=1)
    s = jnp.where(dist >= 0, s - slope_ref[...] * dist.astype(jnp.float32), NEG_INF)
    m = jnp.max(s, axis=-1, keepdims=True)
    e = jnp.where(dist >= 0, jnp.exp(s - m), 0.0)
    p = (e / jnp.maximum(jnp.sum(e, axis=-1, keepdims=True), 1e-30)).astype(jnp.bfloat16)
    o = jnp.zeros((q.shape[0], NSA_DK), jnp.float32)
    for n in range(n_sel):
        o = o + lax.dot_general(p[:, n * PAGE_SIZE:(n + 1) * PAGE_SIZE], vts[n], nt, preferred_element_type=jnp.float32)
    o_ref[0] = o


def nsa_sample_selected(q, idx, slc_new, cache_slc, page_table, e, qpos):
    DB, Tn, H, dk = q.shape
    n_sel = idx.shape[-1]
    assert Tn == 1
    P = page_table.shape[1]
    bpp = PAGE_SIZE // NSA_BLK
    nb_past = P * bpp
    idx2 = idx[:, 0].astype(jnp.int32)
    jp = jnp.minimum(idx2, nb_past - 1)
    phys = jnp.take_along_axis(page_table, jp // bpp, axis=1) + e * cache_slc.shape[1]
    qs = (q[:, 0] * (dk ** -0.5)).astype(jnp.bfloat16)
    new_t = jnp.zeros((DB, 2, dk, PAGE_SIZE), jnp.float32).at[:, :, :, 0].set(slc_new[:, 0])
    kern = functools.partial(_nsa_sample_selected_kernel, n_sel=n_sel, nb_past=nb_past, qpos=int(qpos))

    def page_spec(n):
        return pl.BlockSpec((1, 2, dk, PAGE_SIZE), lambda b, ph, ix: (ph[b, n], 0, 0, 0))

    grid_spec = pltpu.PrefetchScalarGridSpec(
        num_scalar_prefetch=2,
        grid=(DB,),
        in_specs=[pl.BlockSpec((1, H, dk), lambda b, ph, ix: (b, 0, 0)),
                  pl.BlockSpec((1, 2, dk, PAGE_SIZE), lambda b, ph, ix: (b, 0, 0, 0)),
                  pl.BlockSpec((H, 1), lambda b, ph, ix: (0, 0))] + [page_spec(n) for n in range(n_sel)],
        out_specs=pl.BlockSpec((1, H, dk), lambda b, ph, ix: (b, 0, 0)))
    out = pl.pallas_call(
        kern,
        grid_spec=grid_spec,
        out_shape=jax.ShapeDtypeStruct((DB, H, dk), jnp.float32),
        compiler_params=pltpu.CompilerParams(dimension_semantics=("arbitrary",), vmem_limit_bytes=VMEM_LIMIT_BYTES),
        name="nsa_sample_selected",
    )(phys, idx2, qs, new_t, jnp.asarray(alibi_slopes_np(H)).reshape(H, 1), *([pages_feature_major(cache_slc)] * n_sel))
    return out[:, None]


def nsa_sample(q, qpos, cmp_new, slc_new, win_new, gates, cache_cmp, cache_slc, win_buf, page_table, e, ep, slopes):
    DB, Tn = q.shape[:2]
    n_pages = page_table.shape[1]
    bpp = PAGE_SIZE // NSA_BLK
    nb_past = n_pages * bpp
    past_len = n_pages * PAGE_SIZE
    nb_new = -(-Tn // NSA_BLK)
    pad = ((0, 0), (0, nb_new * NSA_BLK - Tn), (0, 0), (0, 0))

    def compress(rows):
        return nsa_compress(rows, ep["cmp_pos"], ep["cmp_w1"], ep["cmp_w2"])

    comp_past = nsa_compress_paged(pages_feature_major(cache_cmp), page_table + e * cache_cmp.shape[1],
                                   ep["cmp_pos"], ep["cmp_w1"], ep["cmp_w2"])
    comp_new = compress(jnp.pad(cmp_new, pad).reshape(DB, nb_new, NSA_BLK, 2, NSA_DK))
    comp = jnp.concatenate([comp_past, comp_new], axis=1)
    kc = rmsnorm(comp[:, :, 0], ep["g_k"][0])
    vc = comp[:, :, 1]
    c_end = (jnp.arange(nb_past + nb_new) + 1) * NSA_BLK - 1
    o_cmp, idx = nsa_compressed_branch(q, qpos, kc, vc, c_end, slopes)

    o_slc = nsa_sample_selected(q, idx, slc_new, cache_slc, page_table, e, past_len)

    o_win = window_sample(q, win_buf[:, :, 0:1], win_buf[:, :, 1:2], win_new[:, :, 0:1], win_new[:, :, 1:2],
                          past_len, NSA_WIN, slopes)
    return nsa_gate(gates, o_cmp, o_slc, o_win)


def mla_keys(lat, ep):
    c = lat[..., :MLA_LORA]
    kp = lat[..., MLA_LORA:]
    kn = rmsnorm(jnp.einsum("bkc,chd->bkhd", c, ep["w_uk"]), ep["g_kn"])
    v = jnp.einsum("bkc,chd->bkhd", c, ep["w_uv"])
    return kn, kp, v


def mla_scores(qn, qp, kn, kp):
    s = jnp.einsum("bqhd,bkhd->bhqk", qn, kn) + jnp.einsum("bqhd,bkd->bhqk", qp, kp)
    return s.astype(jnp.float32) * ((MLA_NOPE + MLA_ROPE) ** -0.5)


def mla_prompt(qn, qp, lat, ep):
    B, T = qn.shape[:2]
    kn, kp, v = mla_keys(lat, ep)
    return jnp.transpose(mla_prompt_attention(qn, qp, kn, kp, v), (0, 2, 1, 3))


MLA_PAGES_PER_STEP = 64


def _mla_sample_kernel(pt_ref, qg_ref, qp_ref, wukt_ref, wuv_ref, gt_ref, new_ref, *rest, scale, n_pages):
    page_refs = rest[:n_pages]
    o_ref, c_scr, kp_scr, a_scr, m_scr, l_scr, acc_scr = rest[n_pages:]
    g = pl.program_id(1)
    ng = pl.num_programs(1)
    nt = (((1,), (1,)), ((), ()))

    @pl.when(g == 0)
    def _():
        a_scr[...] = lax.dot_general(qg_ref[0], wukt_ref[...], (((1,), (0,)), ((), ())),
                                     preferred_element_type=jnp.float32).astype(jnp.bfloat16)
        m_scr[...] = jnp.full(m_scr.shape, NEG_INF, jnp.float32)
        l_scr[...] = jnp.zeros(l_scr.shape, jnp.float32)
        acc_scr[...] = jnp.zeros(acc_scr.shape, jnp.float32)

    def scores(ct, kpt):
        projt = jnp.dot(wukt_ref[...], ct, preferred_element_type=jnp.float32)
        ss = jnp.dot(gt_ref[...], (projt * projt).astype(jnp.bfloat16), preferred_element_type=jnp.float32)
        num = jnp.dot(a_scr[...], ct, preferred_element_type=jnp.float32)
        sp = jnp.dot(qp_ref[0], kpt, preferred_element_type=jnp.float32)
        return (num * lax.rsqrt(ss * (1.0 / MLA_NOPE) + EPS) + sp) * scale

    def softmax_update(ct, st):
        m_i = m_scr[...]
        m_new = jnp.maximum(m_i, jnp.max(st, axis=-1, keepdims=True))
        corr = jnp.exp(m_i - m_new)
        p = jnp.exp(st - m_new)
        m_scr[...] = m_new
        l_scr[...] = l_scr[...] * corr + jnp.sum(p, axis=-1, keepdims=True)
        acc_scr[...] = acc_scr[...] * corr + lax.dot_general(p.astype(jnp.bfloat16), ct, nt,
                                                             preferred_element_type=jnp.float32)

    for j, r in enumerate(page_refs):
        page = r[0]
        c_scr[:, j * PAGE_SIZE:(j + 1) * PAGE_SIZE] = page[:MLA_LORA, :].astype(jnp.bfloat16)
        kp_scr[:, j * PAGE_SIZE:(j + 1) * PAGE_SIZE] = page[MLA_LORA:, :].astype(jnp.bfloat16)
    c_all = c_scr[...]
    softmax_update(c_all, scores(c_all, kp_scr[...]))

    @pl.when(g == ng - 1)
    def _():
        new = new_ref[0]
        ct = new[:MLA_LORA, :].astype(jnp.bfloat16)
        st = scores(ct, new[MLA_LORA:, :].astype(jnp.bfloat16))
        key = lax.broadcasted_iota(jnp.int32, st.shape, 1)
        softmax_update(ct, jnp.where(key == 0, st, NEG_INF))
        o8 = jnp.dot((acc_scr[...] / l_scr[...]).astype(jnp.bfloat16), wuv_ref[...],
                     preferred_element_type=jnp.float32)
        row = lax.broadcasted_iota(jnp.int32, o8.shape, 0)
        col = lax.broadcasted_iota(jnp.int32, o8.shape, 1)
        o_ref[0] = jnp.sum(jnp.where(col // MLA_V == row, o8, 0.0), axis=0, keepdims=True)


def mla_sample(qn, qp, lat_new, cache_mla, page_table, e, ep):
    DB, Tn = qn.shape[:2]
    H = MLA_HEADS
    P = page_table.shape[1]
    PG = math.gcd(MLA_PAGES_PER_STEP, P)
    assert Tn == 1
    bf = jnp.bfloat16
    dq = MLA_NOPE + MLA_ROPE
    qg = qn[:, 0] * ep["g_kn"]
    eye = jnp.eye(H, dtype=qg.dtype)
    qg_exp = (qg[:, :, None, :] * eye[None, :, :, None]).reshape(DB, H, H * MLA_NOPE).astype(bf)
    qp_h = qp[:, 0].astype(bf)
    wukt = ep["w_uk"].reshape(MLA_LORA, H * MLA_NOPE).T.astype(bf)
    wuv = ep["w_uv"].reshape(MLA_LORA, H * MLA_V).astype(bf)
    gt = np.zeros((H, H * MLA_NOPE), np.float32)
    gt[np.arange(H * MLA_NOPE) // MLA_NOPE, np.arange(H * MLA_NOPE)] = 1.0
    pages_t = jnp.swapaxes(cache_mla, -1, -2).reshape(-1, MLA_ROW, PAGE_SIZE)
    new_t = jnp.zeros((DB, MLA_ROW, PAGE_SIZE), jnp.float32).at[:, :, 0].set(lat_new[:, 0])
    kern = functools.partial(_mla_sample_kernel, scale=float(dq ** -0.5), n_pages=PG)

    def page_spec(j):
        return pl.BlockSpec((1, MLA_ROW, PAGE_SIZE), lambda b, g, pt: (pt[b, g * PG + j], 0, 0))

    const2 = lambda b, g, pt: (0, 0)
    grid_spec = pltpu.PrefetchScalarGridSpec(
        num_scalar_prefetch=1,
        grid=(DB, P // PG),
        in_specs=[pl.BlockSpec((1, H, H * MLA_NOPE), lambda b, g, pt: (b, 0, 0)),
                  pl.BlockSpec((1, H, MLA_ROPE), lambda b, g, pt: (b, 0, 0)),
                  pl.BlockSpec((H * MLA_NOPE, MLA_LORA), const2),
                  pl.BlockSpec((MLA_LORA, H * MLA_V), const2),
                  pl.BlockSpec((H, H * MLA_NOPE), const2),
                  pl.BlockSpec((1, MLA_ROW, PAGE_SIZE), lambda b, g, pt: (b, 0, 0))]
                 + [page_spec(j) for j in range(PG)],
        out_specs=pl.BlockSpec((1, 1, H * MLA_V), lambda b, g, pt: (b, 0, 0)),
        scratch_shapes=[pltpu.VMEM((MLA_LORA, PG * PAGE_SIZE), bf),
                        pltpu.VMEM((MLA_ROPE, PG * PAGE_SIZE), bf),
                        pltpu.VMEM((H, MLA_LORA), bf),
                        pltpu.VMEM((8, 1), jnp.float32),
                        pltpu.VMEM((8, 1), jnp.float32),
                        pltpu.VMEM((8, MLA_LORA), jnp.float32)])
    out = pl.pallas_call(
        kern,
        grid_spec=grid_spec,
        out_shape=jax.ShapeDtypeStruct((DB, 1, H * MLA_V), jnp.float32),
        compiler_params=pltpu.CompilerParams(dimension_semantics=("arbitrary", "arbitrary"),
                                             vmem_limit_bytes=VMEM_LIMIT_BYTES),
        name="mla_sample",
    )(page_table + e * cache_mla.shape[1], qg_exp, qp_h, wukt, wuv, jnp.asarray(gt).astype(bf), new_t,
      *([pages_t] * PG))
    return out.reshape(DB, 1, H, MLA_V)


def even_project(x, g_norm, pos, ep):
    B, T, D = x.shape
    h = norm_matmul(x.reshape(B * T, D), g_norm, ep["w_in"]).reshape(B, T, EVEN_IN)
    cuts = np.cumsum([NSA_Q_COLS, NSA_KV_COLS, NSA_GATE_COLS, MLA_Q_COLS]).tolist()
    q, kv, g, mq, lat = jnp.split(h, cuts, axis=-1)
    q = rmsnorm(q.reshape(B, T, NSA_HEADS, NSA_DK), ep["g_q"])
    kv = kv.reshape(B, T, 3, 2, NSA_DK)
    cmp_kv = kv[:, :, 0]
    slc_kv = jnp.stack([rmsnorm(kv[:, :, 1, 0], ep["g_k"][1]), kv[:, :, 1, 1]], axis=2)
    win_kv = jnp.stack([rmsnorm(kv[:, :, 2, 0], ep["g_k"][2]), kv[:, :, 2, 1]], axis=2)
    gates = jax.nn.sigmoid(g.astype(jnp.float32)).reshape(B, T, NSA_HEADS, 3)
    mq = mq.reshape(B, T, MLA_HEADS, MLA_NOPE + MLA_ROPE)
    qn = rmsnorm(mq[..., :MLA_NOPE], ep["g_qn"])
    qp = rope(rmsnorm(mq[..., MLA_NOPE:], ep["g_qpe"]), pos)
    c = rmsnorm(lat[..., :MLA_LORA], ep["g_lat"])
    kp = rope(rmsnorm(lat[..., MLA_LORA:], ep["g_kpe"])[:, :, None, :], pos)[:, :, 0, :]
    return q, cmp_kv, slc_kv, win_kv, gates, qn, qp, jnp.concatenate([c, kp], axis=-1)


def even_output(x, o_nsa, o_mla, w_out):
    B, T, D = x.shape
    a = jnp.concatenate([o_nsa.reshape(B, T, -1), o_mla.reshape(B, T, -1)], axis=-1)
    return matmul_res(a.reshape(B * T, EVEN_OUT), w_out, x.reshape(B * T, D)).reshape(B, T, D)


def odd_project(x, g_norm, op):
    B, T, D = x.shape
    h = norm_matmul(x.reshape(B * T, D), g_norm, op["w_in"]).reshape(B, T, ODD_IN)
    q, k, v = jnp.split(h, [SWA_HEADS * SWA_HD, SWA_HEADS * SWA_HD + SWA_KV_HEADS * SWA_HD], axis=-1)
    q = rmsnorm(q.reshape(B, T, SWA_HEADS, SWA_HD), op["g_q"])
    k = rmsnorm(k.reshape(B, T, SWA_KV_HEADS, SWA_HD), op["g_k"])
    v = v.reshape(B, T, SWA_KV_HEADS, SWA_HD)
    return q, jnp.stack([k, v], axis=2)


def memory_kv(mem, cp):
    B, M, D = mem.shape
    w_kv = jnp.concatenate([cp["w_k"], cp["w_v"]], axis=1)
    kv = norm_matmul(mem.reshape(B * M, D), cp["g_mem"], w_kv).reshape(B, M, 2, MEM_HEADS, MEM_HD)
    k = rmsnorm(kv[:, :, 0], cp["g_k"])
    return jnp.stack([k, kv[:, :, 1]], axis=2)


MEM_TQ = 256


def _memory_cross_kernel(x_ref, g_ref, wq_ref, gq_ref, k_ref, v_ref, wo_ref, o_ref):
    x = x_ref[0]
    xn = x * lax.rsqrt(jnp.mean(x * x, axis=-1, keepdims=True) + EPS) * g_ref[...]
    q = jnp.dot(xn.astype(jnp.bfloat16), wq_ref[...], preferred_element_type=jnp.float32)
    nt = (((1,), (1,)), ((), ()))
    outs = []
    for h in range(MEM_HEADS):
        cols = slice(h * MEM_HD, (h + 1) * MEM_HD)
        qh = q[:, cols]
        qh = qh * lax.rsqrt(jnp.mean(qh * qh, axis=-1, keepdims=True) + EPS) * gq_ref[...]
        s = lax.dot_general(qh.astype(jnp.bfloat16), k_ref[0, :, cols], nt,
                            preferred_element_type=jnp.float32) * (MEM_HD ** -0.5)
        e = jnp.exp(s - jnp.max(s, axis=-1, keepdims=True))
        p = e / jnp.sum(e, axis=-1, keepdims=True)
        outs.append(jnp.dot(p.astype(jnp.bfloat16), v_ref[0, :, cols], preferred_element_type=jnp.float32))
    o = jnp.concatenate(outs, axis=1).astype(jnp.bfloat16)
    o_ref[0] = x + jnp.dot(o, wo_ref[...], preferred_element_type=jnp.float32)


def memory_cross_prompt(x, g_norm, mem_kv, cp):
    B, T, D = x.shape
    M = mem_kv.shape[1]
    bf = jnp.bfloat16
    tq = MEM_TQ
    k = mem_kv[:, :, 0].reshape(B, M, MEM_W).astype(bf)
    v = mem_kv[:, :, 1].reshape(B, M, MEM_W).astype(bf)
    return pl.pallas_call(
        _memory_cross_kernel,
        grid=(B, T // tq),
        in_specs=[pl.BlockSpec((1, tq, D), lambda b, i: (b, i, 0)),
                  pl.BlockSpec((1, D), lambda b, i: (0, 0)),
                  pl.BlockSpec((D, MEM_W), lambda b, i: (0, 0)),
                  pl.BlockSpec((1, MEM_HD), lambda b, i: (0, 0)),
                  pl.BlockSpec((1, M, MEM_W), lambda b, i: (b, 0, 0)),
                  pl.BlockSpec((1, M, MEM_W), lambda b, i: (b, 0, 0)),
                  pl.BlockSpec((MEM_W, D), lambda b, i: (0, 0))],
        out_specs=pl.BlockSpec((1, tq, D), lambda b, i: (b, i, 0)),
        out_shape=jax.ShapeDtypeStruct((B, T, D), jnp.float32),
        compiler_params=pltpu.CompilerParams(dimension_semantics=("arbitrary", "arbitrary"),
                                             vmem_limit_bytes=VMEM_LIMIT_BYTES),
        name="memory_cross",
    )(x, g_norm.reshape(1, D), cp["w_q"].astype(bf), cp["g_q"].reshape(1, MEM_HD), k, v, cp["w_o"].astype(bf))


def memory_cross(x, g_norm, mem_kv, cp):
    B, T, D = x.shape
    if T % MEM_TQ == 0:
        return memory_cross_prompt(x, g_norm, mem_kv, cp)
    q = norm_matmul(x.reshape(B * T, D), g_norm, cp["w_q"]).reshape(B, T, MEM_HEADS, MEM_HD)
    q = rmsnorm(q, cp["g_q"])
    s = jnp.einsum("bthd,bmhd->bhtm", q, mem_kv[:, :, 0]).astype(jnp.float32) * (MEM_HD ** -0.5)
    p = jax.nn.softmax(s, axis=-1).astype(x.dtype)
    o = jnp.einsum("bhtm,bmhd->bthd", p, mem_kv[:, :, 1]).reshape(B * T, MEM_W)
    return matmul_res(o, cp["w_o"], x.reshape(B * T, D)).reshape(B, T, D)


PEER_TB = 128
PEER_VMEM_LIMIT_BYTES = 56 * 1024 * 1024
F32_NEG_INF = float("-inf")


def _topk_rows(s, row, k):
    nrow = float(s.shape[0])
    vals, ids = [], []
    for _ in range(k):
        m = jnp.max(s, axis=0, keepdims=True)
        i = jnp.min(jnp.where(s == m, row, nrow), axis=0, keepdims=True)
        vals.append(m)
        ids.append(i)
        s = jnp.where(row == i, F32_NEG_INF, s)
    return vals, ids


def _peer_route_kernel(x_ref, g_ref, wq_ref, sk0_ref, sk1_ref, xn_ref, idx_ref, gate_ref):
    x = x_ref[...]
    xn = x * lax.rsqrt(jnp.mean(x * x, axis=-1, keepdims=True) + EPS) * g_ref[...]
    xn_ref[...] = xn
    q = jnp.dot(xn.astype(jnp.bfloat16), wq_ref[...], preferred_element_type=jnp.float32)
    tb = x.shape[0]
    row_k = lax.broadcasted_iota(jnp.int32, (PEER_NKEYS, tb), 0).astype(jnp.float32)
    sub8 = lax.broadcasted_iota(jnp.int32, (8, tb), 0).astype(jnp.float32)
    assert PEER_TOPK % 16 == 0
    nt = (((1,), (1,)), ((), ()))
    ids, gates = [], []
    for h in range(PEER_HEADS):
        qh = q[:, h * PEER_DKEY:(h + 1) * PEER_DKEY].astype(jnp.bfloat16)
        s1 = lax.dot_general(sk0_ref[...], qh, nt, preferred_element_type=jnp.float32)
        s2 = lax.dot_general(sk1_ref[...], qh, nt, preferred_element_type=jnp.float32)
        v1, i1 = _topk_rows(s1, row_k, PEER_TOPK)
        v2, i2 = _topk_rows(s2, row_k, PEER_TOPK)
        v1c = jnp.concatenate(v1, axis=0)
        i1c = jnp.concatenate(i1, axis=0)
        v2c = jnp.concatenate(v2, axis=0)
        i2c = jnp.concatenate(i2, axis=0)
        cands, poss, cids = [], [], []
        for a in range(PEER_TOPK // 2):
            bmax = PEER_TOPK // (a + 1) - 1
            for b0 in range(0, bmax + 1, 8):
                c = v1[a] + v2c[b0:b0 + 8]
                if bmax - b0 + 1 < 8:
                    c = jnp.where(sub8 <= float(bmax - b0), c, F32_NEG_INF)
                cands.append(c)
                poss.append(sub8 + float(a * PEER_TOPK + b0))
                cids.append(i1[a] * float(PEER_NKEYS) + i2c[b0:b0 + 8])
        for a0 in range(PEER_TOPK // 2, PEER_TOPK, 8):
            cands.append(v1c[a0:a0 + 8] + v2[0])
            poss.append((sub8 + float(a0)) * float(PEER_TOPK))
            cids.append(i1c[a0:a0 + 8] * float(PEER_NKEYS) + i2[0])
        cand = jnp.concatenate(cands, axis=0)
        row_c = jnp.concatenate(poss, axis=0)
        cid = jnp.concatenate(cids, axis=0)
        tops, tids = [], []
        for _ in range(PEER_TOPK):
            m = jnp.max(cand, axis=0, keepdims=True)
            pos = jnp.min(jnp.where(cand == m, row_c, float(PEER_TOPK * PEER_TOPK)), axis=0, keepdims=True)
            hit = row_c == pos
            tids.append(jnp.sum(jnp.where(hit, cid, 0.0), axis=0, keepdims=True))
            tops.append(m)
            cand = jnp.where(hit, F32_NEG_INF, cand)
        ts = jnp.concatenate(tops, axis=0)
        e = jnp.exp(ts - tops[0])
        gates.append(e / jnp.sum(e, axis=0, keepdims=True))
        ids.append(jnp.concatenate(tids, axis=0))
    table_rows = x.shape[1] // 256
    idx_ref[...] = (jnp.concatenate(ids, axis=0) * float(table_rows)).T.astype(jnp.int32)
    gate_ref[...] = jnp.concatenate(gates, axis=0).T


def peer_route(x, g, w_q, subkeys):
    n, d = x.shape
    tb = min(PEER_TB, n)
    hk = PEER_HEADS * PEER_TOPK
    half = PEER_DKEY // 2
    z = jnp.zeros((PEER_NKEYS, half), jnp.float32)
    sk0 = jnp.concatenate([subkeys[0], z], axis=1).astype(jnp.bfloat16)
    sk1 = jnp.concatenate([z, subkeys[1]], axis=1).astype(jnp.bfloat16)
    return pl.pallas_call(
        _peer_route_kernel,
        grid=(n // tb,),
        in_specs=[pl.BlockSpec((tb, d), lambda i: (i, 0)),
                  pl.BlockSpec((1, d), lambda i: (0, 0)),
                  pl.BlockSpec((d, PEER_HEADS * PEER_DKEY), lambda i: (0, 0)),
                  pl.BlockSpec((PEER_NKEYS, PEER_DKEY), lambda i: (0, 0)),
                  pl.BlockSpec((PEER_NKEYS, PEER_DKEY), lambda i: (0, 0))],
        out_specs=[pl.BlockSpec((tb, d), lambda i: (i, 0)),
                   pl.BlockSpec((tb, hk), lambda i: (i, 0)),
                   pl.BlockSpec((tb, hk), lambda i: (i, 0))],
        out_shape=[jax.ShapeDtypeStruct((n, d), jnp.float32),
                   jax.ShapeDtypeStruct((n, hk), jnp.int32),
                   jax.ShapeDtypeStruct((n, hk), jnp.float32)],
        compiler_params=pltpu.CompilerParams(dimension_semantics=("arbitrary",),
                                             vmem_limit_bytes=PEER_VMEM_LIMIT_BYTES),
        name="peer_route",
    )(x, g.reshape(1, d), w_q.astype(jnp.bfloat16), sk0, sk1)


def pack_table(t):
    e, d = t.shape
    b = lax.bitcast_convert_type(t.astype(jnp.bfloat16), jnp.uint16).astype(jnp.uint32)
    w = b[:, : d // 2] | (b[:, d // 2:] << 16)
    return w.reshape(e * d // 256, 128)


def _table_spec(tab):
    return pl.BlockSpec(tab.shape, lambda i: (0, 0), pipeline_mode=pl.Buffered(1))


def _gather_row(tab, row0, rows):
    wds = tab[pl.ds(pl.multiple_of(row0, rows), rows), :]
    lo = pltpu.bitcast(wds << 16, jnp.float32)
    hi = pltpu.bitcast(wds & jnp.uint32(0xFFFF0000), jnp.float32)
    return lo, hi


def _peer_u_kernel(idx_ref, xn_ref, gate_ref, tab, w_ref, slots, rsum, act):
    tb, hk = gate_ref.shape
    rows = xn_ref.shape[1] // 2

    def token(t, c):
        xt = xn_ref[t]
        xlo, xhi = xt[:rows], xt[rows:]
        for k in range(hk):
            lo, hi = _gather_row(tab, idx_ref[t, k], rows)
            slots[k * rows:(k + 1) * rows, :] = lo * xlo + hi * xhi
        r = slots[pl.ds(0, hk, stride=rows), :]
        for s in range(1, rows):
            r = r + slots[pl.ds(s, hk, stride=rows), :]
        rsum[pl.ds(pl.multiple_of(t * hk, hk), hk), :] = r
        return c

    lax.fori_loop(0, tb, token, 0)

    ones = jnp.ones((128, 128), jnp.bfloat16)
    grp = 8
    ri = lax.broadcasted_iota(jnp.int32, (grp * hk, 128), 0)
    ci = lax.broadcasted_iota(jnp.int32, (grp * hk, 128), 1)
    eye = (ri % hk) == ci

    def lane_sum(c, carry):
        rr = rsum[pl.ds(pl.multiple_of(c * grp * hk, grp * hk), grp * hk), :]
        hi = rr.astype(jnp.bfloat16)
        lo = (rr - hi.astype(jnp.float32)).astype(jnp.bfloat16)
        m = (jnp.dot(hi, ones, preferred_element_type=jnp.float32)
             + jnp.dot(lo, ones, preferred_element_type=jnp.float32))
        d = jnp.sum(jnp.where(eye, m, 0.0).reshape(grp, hk, 128), axis=1)
        act[pl.ds(pl.multiple_of(c * grp, grp), grp), :] = d
        return carry

    lax.fori_loop(0, tb // grp, lane_sum, 0)
    w_ref[...] = gate_ref[...] * jax.nn.gelu(act[...])


def peer_activate(idx, xn, gate, tab):
    n, d = xn.shape
    hk = idx.shape[1]
    tb = min(PEER_TB, n)
    sub = d // 128
    return pl.pallas_call(
        _peer_u_kernel,
        grid=(n // tb,),
        in_specs=[pl.BlockSpec((tb, hk), lambda i: (i, 0), memory_space=pltpu.SMEM),
                  pl.BlockSpec((tb, sub, 128), lambda i: (i, 0, 0)),
                  pl.BlockSpec((tb, hk), lambda i: (i, 0)),
                  _table_spec(tab)],
        out_specs=pl.BlockSpec((tb, hk), lambda i: (i, 0)),
        out_shape=jax.ShapeDtypeStruct((n, hk), jnp.float32),
        scratch_shapes=[pltpu.VMEM((hk * sub // 2, 128), jnp.float32),
                        pltpu.VMEM((tb * hk, 128), jnp.float32),
                        pltpu.VMEM((tb, hk), jnp.float32)],
        compiler_params=pltpu.CompilerParams(dimension_semantics=("arbitrary",),
                                             vmem_limit_bytes=PEER_VMEM_LIMIT_BYTES),
        name="peer_u",
    )(idx, xn.reshape(n, sub, 128), gate, tab)


def _peer_v_kernel(idx_ref, w_ref, x_ref, tab, o_ref):
    tb, hk = idx_ref.shape
    rows = x_ref.shape[1] // 2

    def token(t, c):
        nacc = 2
        lo_acc = [jnp.zeros((rows, 128), jnp.float32) for _ in range(nacc)]
        hi_acc = [jnp.zeros((rows, 128), jnp.float32) for _ in range(nacc)]
        for k in range(hk):
            lo, hi = _gather_row(tab, idx_ref[t, k], rows)
            wk = w_ref[t, k]
            lo_acc[k % nacc] = lo_acc[k % nacc] + wk * lo
            hi_acc[k % nacc] = hi_acc[k % nacc] + wk * hi
        xt = x_ref[t]
        o_ref[t, :rows, :] = xt[:rows] + (lo_acc[0] + lo_acc[1])
        o_ref[t, rows:, :] = xt[rows:] + (hi_acc[0] + hi_acc[1])
        return c

    lax.fori_loop(0, tb, token, 0)


def peer_combine(idx, w, x, tab):
    n, d = x.shape
    hk = idx.shape[1]
    tb = min(PEER_TB, n)
    sub = d // 128
    out = pl.pallas_call(
        _peer_v_kernel,
        grid=(n // tb,),
        in_specs=[pl.BlockSpec((tb, hk), lambda i: (i, 0), memory_space=pltpu.SMEM),
                  pl.BlockSpec((tb, hk), lambda i: (i, 0), memory_space=pltpu.SMEM),
                  pl.BlockSpec((tb, sub, 128), lambda i: (i, 0, 0)),
                  _table_spec(tab)],
        out_specs=pl.BlockSpec((tb, sub, 128), lambda i: (i, 0, 0)),
        out_shape=jax.ShapeDtypeStruct((n, sub, 128), jnp.float32),
        compiler_params=pltpu.CompilerParams(dimension_semantics=("arbitrary",),
                                             vmem_limit_bytes=PEER_VMEM_LIMIT_BYTES),
        name="peer_v",
    )(idx, w, x.reshape(n, sub, 128), tab)
    return out.reshape(n, d)


def peer_ffn(x, g_norm, pp):
    B, T, D = x.shape
    x2 = x.reshape(B * T, D)
    xn, idx, gate = peer_route(x2, g_norm, pp["w_q"], pp["subkeys"])
    w = peer_activate(idx, xn, gate, pp["u_packed"])
    return peer_combine(idx, w, x2, pp["v_packed"]).reshape(B, T, D)


def kernel(x_prompt, x_sample, mem_prompt, cache_nsa_cmp, cache_nsa_slc, cache_nsa_win, cache_mla, cache_swa,
           cache_mem, page_table, norm_mix, norm_cross, norm_ffn, even_w_in, even_w_out, nsa_g_q, nsa_g_k,
           nsa_cmp_pos, nsa_cmp_w1, nsa_cmp_w2, mla_g_qn, mla_g_qpe, mla_g_lat, mla_g_kpe, mla_g_kn, mla_w_uk,
           mla_w_uv, odd_w_in, odd_w_out, swa_g_q, swa_g_k, swa_sinks, mem_g, mem_w_q, mem_w_k, mem_w_v,
           mem_g_q, mem_g_k, mem_w_o, peer_w_q, peer_subkeys, peer_u, peer_v):
    depth = norm_mix.shape[0]
    slopes_nsa = alibi_slopes(NSA_HEADS)
    slopes_swa = alibi_slopes(SWA_HEADS)
    past_len = page_table.shape[1] * PAGE_SIZE
    T, Tn = x_prompt.shape[1], x_sample.shape[1]
    pos_p = jnp.arange(T)
    pos_s = past_len + jnp.arange(Tn)
    xp, xs = x_prompt, x_sample
    st_cmp_p, st_cmp_s, st_slc_p, st_slc_s, st_win_p, st_win_s = [], [], [], [], [], []
    st_mla_p, st_mla_s, st_swa_p, st_swa_s, st_mem_p = [], [], [], [], []

    for li in range(depth):
        if li % 2 == 0:
            e = li // 2
            ep = dict(w_in=even_w_in[e], g_q=nsa_g_q[e], g_k=nsa_g_k[e], cmp_pos=nsa_cmp_pos[e],
                      cmp_w1=nsa_cmp_w1[e], cmp_w2=nsa_cmp_w2[e], g_qn=mla_g_qn[e], g_qpe=mla_g_qpe[e],
                      g_lat=mla_g_lat[e], g_kpe=mla_g_kpe[e], g_kn=mla_g_kn[e], w_uk=mla_w_uk[e], w_uv=mla_w_uv[e])
            q, cmp_kv, slc_kv, win_kv, gates, qn, qp, mla_row = even_project(xp, norm_mix[li], pos_p, ep)
            o_nsa = nsa_prompt(q, cmp_kv, slc_kv, win_kv, gates, ep, slopes_nsa)
            o_mla = mla_prompt(qn, qp, mla_row, ep)
            xp = even_output(xp, o_nsa, o_mla, even_w_out[e])
            st_cmp_p.append(cmp_kv)
            st_slc_p.append(slc_kv)
            st_win_p.append(win_kv[:, -min(NSA_WIN, T):])
            st_mla_p.append(mla_row)
            win_buf = cache_nsa_win[e]
            q, cmp_kv, slc_kv, win_kv, gates, qn, qp, mla_row = even_project(xs, norm_mix[li], pos_s, ep)
            o_nsa = nsa_sample(q, pos_s, cmp_kv, slc_kv, win_kv, gates, cache_nsa_cmp, cache_nsa_slc, win_buf,
                               page_table, e, ep, slopes_nsa)
            o_mla = mla_sample(qn, qp, mla_row, cache_mla, page_table, e, ep)
            xs = even_output(xs, o_nsa, o_mla, even_w_out[e])
            st_cmp_s.append(cmp_kv)
            st_slc_s.append(slc_kv)
            st_win_s.append(jnp.concatenate([win_buf, win_kv], axis=1)[:, -win_buf.shape[1]:])
            st_mla_s.append(mla_row)
        else:
            o = li // 2
            op = dict(w_in=odd_w_in[o], g_q=swa_g_q[o], g_k=swa_g_k[o])
            B = xp.shape[0]
            q, kv = odd_project(xp, norm_mix[li], op)
            att = jnp.transpose(swa_prompt_attention(q, kv[:, :, 0], kv[:, :, 1], swa_sinks[o]), (0, 2, 1, 3))
            xp = matmul_res(att.reshape(B * T, ODD_OUT), odd_w_out[o], xp.reshape(B * T, -1)).reshape(xp.shape)
            st_swa_p.append(kv[:, -min(SWA_WIN, T):])
            buf = cache_swa[o]
            q, kv = odd_project(xs, norm_mix[li], op)
            att = window_sample(q, buf[:, :, 0], buf[:, :, 1], kv[:, :, 0], kv[:, :, 1], past_len, SWA_WIN,
                                slopes_swa, swa_sinks[o])
            xs = matmul_res(att.reshape(xs.shape[0] * Tn, ODD_OUT), odd_w_out[o],
                            xs.reshape(xs.shape[0] * Tn, -1)).reshape(xs.shape)
            st_swa_s.append(jnp.concatenate([buf, kv], axis=1)[:, -buf.shape[1]:])

        cp = dict(g_mem=mem_g[li], w_q=mem_w_q[li], w_k=mem_w_k[li], w_v=mem_w_v[li], g_q=mem_g_q[li],
                  g_k=mem_g_k[li], w_o=mem_w_o[li])
        mem_kv_p = memory_kv(mem_prompt, cp)
        xp = memory_cross(xp, norm_cross[li], mem_kv_p, cp)
        xs = memory_cross(xs, norm_cross[li], cache_mem[li], cp)
        st_mem_p.append(mem_kv_p)

        pp = dict(w_q=peer_w_q[li], subkeys=peer_subkeys[li],
                  u_packed=pack_table(peer_u[li]), v_packed=pack_table(peer_v[li]))
        xp = peer_ffn(xp, norm_ffn[li], pp)
        xs = peer_ffn(xs, norm_ffn[li], pp)

    return (xp, xs, jnp.stack(st_cmp_p), jnp.stack(st_cmp_s), jnp.stack(st_slc_p), jnp.stack(st_slc_s),
            jnp.stack(st_win_p), jnp.stack(st_win_s), jnp.stack(st_mla_p), jnp.stack(st_mla_s),
            jnp.stack(st_swa_p), jnp.stack(st_swa_s), jnp.stack(st_mem_p))
```

```python
import functools
import math

import jax
import jax.numpy as jnp
import numpy as np
from jax import lax
from jax.experimental import pallas as pl
from jax.experimental.pallas import tpu as pltpu

PAGE_SIZE = 128
Q_BLOCK = 128
EPS = 1e-6
NEG_INF = -1e30
ROPE_BASE = 10000.0

NSA_HEADS = 8
NSA_DK = 64
NSA_BLK = 64
NSA_TOPN = 16
NSA_WIN = 512
NSA_CMP_HID = 256
MLA_HEADS = 8
MLA_NOPE = 64
MLA_ROPE = 32
MLA_V = 64
MLA_LORA = 128
MLA_ROW = MLA_LORA + MLA_ROPE
SWA_HEADS = 16
SWA_KV_HEADS = 4
SWA_HD = 64
SWA_WIN = 128
N_MEM = 256
MEM_HEADS = 4
MEM_HD = 128
MEM_W = MEM_HEADS * MEM_HD
PEER_HEADS = 8
PEER_NKEYS = 128
PEER_N = PEER_NKEYS * PEER_NKEYS
PEER_DKEY = 128
PEER_TOPK = 16
PEER_CHUNK = 256

NSA_Q_COLS = NSA_HEADS * NSA_DK
NSA_KV_COLS = 3 * 2 * NSA_DK
NSA_GATE_COLS = NSA_HEADS * 3
MLA_Q_COLS = MLA_HEADS * (MLA_NOPE + MLA_ROPE)
EVEN_IN = NSA_Q_COLS + NSA_KV_COLS + NSA_GATE_COLS + MLA_Q_COLS + MLA_ROW
EVEN_OUT = NSA_HEADS * NSA_DK + MLA_HEADS * MLA_V
ODD_IN = SWA_HEADS * SWA_HD + 2 * SWA_KV_HEADS * SWA_HD
ODD_OUT = SWA_HEADS * SWA_HD

VMEM_LIMIT_BYTES = 48 * 1024 * 1024


def _row_tile(n, target=512):
    t = min(n, target)
    while n % t:
        t //= 2
    return t


def _norm_matmul_kernel(x_ref, g_ref, w_ref, o_ref):
    x = x_ref[...]
    y = x * lax.rsqrt(jnp.mean(x * x, axis=-1, keepdims=True) + EPS) * g_ref[...]
    o_ref[...] = jnp.dot(y.astype(jnp.bfloat16), w_ref[...], preferred_element_type=jnp.float32)


def norm_matmul(x, g, w):
    n, d = x.shape
    c = w.shape[1]
    tm = _row_tile(n)
    return pl.pallas_call(
        _norm_matmul_kernel,
        grid=(n // tm,),
        in_specs=[pl.BlockSpec((tm, d), lambda i: (i, 0)),
                  pl.BlockSpec((1, d), lambda i: (0, 0)),
                  pl.BlockSpec((d, c), lambda i: (0, 0))],
        out_specs=pl.BlockSpec((tm, c), lambda i: (i, 0)),
        out_shape=jax.ShapeDtypeStruct((n, c), jnp.float32),
        compiler_params=pltpu.CompilerParams(dimension_semantics=("arbitrary",),
                                             vmem_limit_bytes=VMEM_LIMIT_BYTES),
        name="norm_matmul",
    )(x, g.reshape(1, d), w.astype(jnp.bfloat16))


def _matmul_res_kernel(a_ref, w_ref, r_ref, o_ref):
    o_ref[...] = r_ref[...] + jnp.dot(a_ref[...].astype(jnp.bfloat16), w_ref[...],
                                      preferred_element_type=jnp.float32)


def matmul_res(a, w, res):
    n, k = a.shape
    c = w.shape[1]
    tm = _row_tile(n)
    return pl.pallas_call(
        _matmul_res_kernel,
        grid=(n // tm,),
        in_specs=[pl.BlockSpec((tm, k), lambda i: (i, 0)),
                  pl.BlockSpec((k, c), lambda i: (0, 0)),
                  pl.BlockSpec((tm, c), lambda i: (i, 0))],
        out_specs=pl.BlockSpec((tm, c), lambda i: (i, 0)),
        out_shape=jax.ShapeDtypeStruct((n, c), jnp.float32),
        compiler_params=pltpu.CompilerParams(dimension_semantics=("arbitrary",),
                                             vmem_limit_bytes=VMEM_LIMIT_BYTES),
        name="matmul_res",
    )(a, w.astype(jnp.bfloat16), res)


def rmsnorm(x, g):
    xf = x.astype(jnp.float32)
    y = xf * lax.rsqrt(jnp.mean(xf * xf, axis=-1, keepdims=True) + EPS)
    return (y * g.astype(jnp.float32)).astype(x.dtype)


def alibi_slopes(n_heads):
    return jnp.asarray((2.0 ** (-8.0 * np.arange(1, n_heads + 1) / n_heads)).astype(np.float32))


def rope(x, pos):
    d = x.shape[-1]
    inv = jnp.asarray(np.power(ROPE_BASE, -np.arange(0, d, 2, dtype=np.float32) / d).astype(np.float32))
    ang = pos.astype(jnp.float32)[:, None] * inv[None, :]
    cos = jnp.cos(ang)[None, :, None, :]
    sin = jnp.sin(ang)[None, :, None, :]
    xf = x.astype(jnp.float32)
    x1, x2 = xf[..., : d // 2], xf[..., d // 2:]
    return jnp.concatenate([x1 * cos - x2 * sin, x1 * sin + x2 * cos], axis=-1).astype(x.dtype)


def masked_softmax(s, valid, sink=None):
    s = jnp.where(valid, s, NEG_INF)
    m = jnp.max(s, axis=-1, keepdims=True)
    if sink is not None:
        sk = sink.astype(jnp.float32)[:, None, None]
        m = jnp.maximum(m, sk)
    e = jnp.where(valid, jnp.exp(s - m), 0.0)
    den = jnp.sum(e, axis=-1, keepdims=True)
    if sink is not None:
        den = den + jnp.exp(sk - m)
    return e / jnp.maximum(den, 1e-30)


def attend(q, k, v, dist, valid, slopes, sink=None):
    B, Q, H, d = q.shape
    G = k.shape[-2]
    qg = q.reshape(B, Q, G, H // G, d)
    if k.ndim == 5:
        s = jnp.einsum("bqgrd,bqkgd->bgrqk", qg, k)
    else:
        s = jnp.einsum("bqgrd,bkgd->bgrqk", qg, k)
    K = s.shape[-1]
    s = s.reshape(B, H, Q, K).astype(jnp.float32) * (d ** -0.5)
    if slopes is not None:
        s = s - slopes[:, None, None] * dist[..., None, :, :].astype(jnp.float32)
    p = masked_softmax(s, valid[..., None, :, :], sink)
    pg = p.reshape(B, G, H // G, Q, K).astype(v.dtype)
    if v.ndim == 5:
        o = jnp.einsum("bgrqk,bqkgd->bqgrd", pg, v)
    else:
        o = jnp.einsum("bgrqk,bkgd->bqgrd", pg, v)
    return o.reshape(B, Q, H, v.shape[-1]), p


def alibi_slopes_np(n_heads):
    return (2.0 ** (-8.0 * np.arange(1, n_heads + 1) / n_heads)).astype(np.float32)


Q_TILE = 128
SLC_CHUNK = 512
MASK_BIG = -(2.0 ** 100)


def _nsa_prompt_kernel(q_ref, kt_ref, vs_ref, kwt_ref, vw_ref, kct_ref, vc_ref, g_ref, slope_ref, wbias_ref, cbias_ref,
                       o_ref, *, n_blocks):
    H, TQ = NSA_HEADS, Q_TILE
    i = pl.program_id(1)
    s0 = i * TQ
    q_lo = q_ref[0].reshape(H * TQ, 128)
    slope = slope_ref[...]

    s = jnp.dot(q_lo, kct_ref[0], preferred_element_type=jnp.float32).reshape(H, TQ, n_blocks)
    tpos = s0 + lax.broadcasted_iota(jnp.int32, (TQ, n_blocks), 0)
    blk = lax.broadcasted_iota(jnp.int32, (TQ, n_blocks), 1)
    dist = tpos - ((blk + 1) * NSA_BLK - 1)
    valid = (dist >= 0)[None]
    s = jnp.where(valid, s - slope * dist.astype(jnp.float32)[None], NEG_INF)
    m = jnp.max(s, axis=-1, keepdims=True)
    e = jnp.where(valid, jnp.exp(s - m), 0.0)
    p = e / jnp.maximum(jnp.sum(e, axis=-1, keepdims=True), 1e-30)
    o_cmp = jnp.dot(p.reshape(H * TQ, n_blocks).astype(jnp.bfloat16), vc_ref[0],
                    preferred_element_type=jnp.float32)
    imp = jnp.sum(p, axis=0)

    cur = tpos // NSA_BLK
    imp = jnp.where(blk == cur, float(H + 1), jnp.where(blk < cur, imp, -1.0))
    blk_f = blk.astype(jnp.float32)
    sel = jnp.zeros((TQ, n_blocks), jnp.bool_)
    for _ in range(min(NSA_TOPN, n_blocks)):
        mx = jnp.max(imp, axis=-1, keepdims=True)
        pick = jnp.min(jnp.where(imp == mx, blk_f, float(n_blocks)), axis=-1, keepdims=True)
        hit = blk_f == pick
        sel = sel | hit
        imp = jnp.where(hit, -2.0, imp)
    nsb = jnp.where(sel, 0.0, MASK_BIG).astype(jnp.bfloat16)
    q_aug = jnp.concatenate([q_lo, jnp.broadcast_to(nsb[None], (H, TQ, n_blocks)).reshape(H * TQ, n_blocks)], axis=1)

    KC = SLC_CHUNK

    def slc_step(c, carry, bias):
        m_i, acc = carry
        k0 = pl.multiple_of(c * KC, KC)
        sc = jnp.dot(q_aug, kt_ref[0, :, pl.ds(k0, KC)], preferred_element_type=jnp.float32)
        if bias is not None:
            sc = (sc.reshape(H, TQ, KC) + bias[None]).reshape(H * TQ, KC)
        m_new = jnp.maximum(m_i, jnp.max(sc, axis=-1, keepdims=True))
        corr = jnp.exp(m_i - m_new)
        pe = jnp.exp(sc - m_new)
        acc_new = acc * corr + jnp.dot(pe.astype(jnp.bfloat16), vs_ref[0, pl.ds(k0, KC), :],
                                       preferred_element_type=jnp.float32)
        return m_new, acc_new

    init = (jnp.full((H * TQ, 1), NEG_INF, jnp.float32), jnp.zeros((H * TQ, 128), jnp.float32))
    c_diag = s0 // KC
    carry = lax.fori_loop(0, c_diag, lambda c, cr: slc_step(c, cr, None), init)
    m_i, acc = slc_step(c_diag, carry, cbias_ref[(s0 % KC) // TQ])
    o_slc = acc[:, :NSA_DK] / acc[:, NSA_DK:NSA_DK + 1]

    nwc = NSA_WIN // TQ + 1
    parts = []
    for r in range(nwc):
        cidx = i - (nwc - 1) + r
        k0 = pl.multiple_of(jnp.maximum(cidx, 0) * TQ, TQ)
        sw = jnp.dot(q_lo[:, :NSA_DK], kwt_ref[0, :, pl.ds(k0, TQ)], preferred_element_type=jnp.float32)
        b = jnp.where(cidx >= 0, wbias_ref[r], NEG_INF)
        parts.append((sw.reshape(H, TQ, TQ) + b).reshape(H * TQ, TQ))
    mw = parts[0].max(axis=-1, keepdims=True)
    for r in range(1, nwc):
        mw = jnp.maximum(mw, parts[r].max(axis=-1, keepdims=True))
    accw = jnp.zeros((H * TQ, 128), jnp.float32)
    for r in range(nwc):
        cidx = i - (nwc - 1) + r
        k0 = pl.multiple_of(jnp.maximum(cidx, 0) * TQ, TQ)
        pe = jnp.exp(parts[r] - mw)
        accw = accw + jnp.dot(pe.astype(jnp.bfloat16), vw_ref[0, pl.ds(k0, TQ), :], preferred_element_type=jnp.float32)
    o_win = accw[:, :NSA_DK] / accw[:, NSA_DK:NSA_DK + 1]

    g = g_ref[0].reshape(H * TQ, 3)
    o = g[:, 0:1] * o_cmp + g[:, 1:2] * o_slc + g[:, 2:3] * o_win
    o_ref[0] = o.reshape(H, TQ, NSA_DK)


def _with_ones_column(v):
    ones = jnp.ones(v.shape[:-1] + (1,), v.dtype)
    zeros = jnp.zeros(v.shape[:-1] + (127 - v.shape[-1],), v.dtype)
    return jnp.concatenate([v, ones, zeros], axis=-1).astype(jnp.bfloat16)


def nsa_prompt_attention(q, slc_k, slc_v, win_k, win_v, kc, vc, gates):
    B, T, H, dk = q.shape
    nb = T // NSA_BLK
    nbp = 128
    TQ = Q_TILE
    assert nb <= nbp and T % SLC_CHUNK == 0 and NSA_WIN % TQ == 0
    slopes = alibi_slopes_np(H)
    bf = jnp.bfloat16
    qh = jnp.transpose(q, (0, 2, 1, 3)) * (dk ** -0.5)
    al = np.zeros((H, 128 - dk), np.float32)
    al[:, 0] = slopes * 128.0
    al[:, 1] = slopes
    q_lo = jnp.concatenate([qh, jnp.broadcast_to(jnp.asarray(al)[None, :, None, :], (B, H, T, 128 - dk))],
                           axis=-1).astype(bf)
    pos = np.arange(T)
    crow = np.zeros((128 - dk + nbp, T), np.float32)
    crow[0] = pos // 128
    crow[1] = pos % 128
    crow[128 - dk + pos // NSA_BLK, pos] = 1.0
    kt = jnp.concatenate([jnp.transpose(slc_k, (0, 2, 1)),
                          jnp.broadcast_to(jnp.asarray(crow)[None], (B,) + crow.shape)], axis=1).astype(bf)
    kwt = jnp.transpose(win_k, (0, 2, 1)).astype(bf)
    kct = jnp.pad(jnp.transpose(kc, (0, 2, 1)), ((0, 0), (0, 128 - dk), (0, nbp - nb))).astype(bf)
    vcp = jnp.pad(vc, ((0, 0), (0, nbp - nb), (0, 0))).astype(bf)
    gh = jnp.transpose(gates, (0, 2, 1, 3))
    nwc = NSA_WIN // TQ + 1
    ii = np.arange(TQ)[:, None]
    jj = np.arange(TQ)[None, :]
    wb = np.zeros((nwc, H, TQ, TQ), np.float32)
    for r in range(nwc):
        d = ii - jj + TQ * (nwc - 1 - r)
        ok = (d >= 0) & (d <= NSA_WIN)
        wb[r] = np.where(ok[None], -slopes[:, None, None] * d[None].astype(np.float32), NEG_INF)
    nv = SLC_CHUNK // TQ
    cb = np.zeros((nv, TQ, SLC_CHUNK), np.float32)
    for v in range(nv):
        cb[v] = np.where(np.arange(SLC_CHUNK)[None, :] <= v * TQ + ii, 0.0, NEG_INF)
    kern = functools.partial(_nsa_prompt_kernel, n_blocks=nbp)
    return pl.pallas_call(
        kern,
        grid=(B, T // TQ),
        in_specs=[pl.BlockSpec((1, H, TQ, 128), lambda b, i: (b, 0, i, 0)),
                  pl.BlockSpec((1, 256, T), lambda b, i: (b, 0, 0)),
                  pl.BlockSpec((1, T, 128), lambda b, i: (b, 0, 0)),
                  pl.BlockSpec((1, dk, T), lambda b, i: (b, 0, 0)),
                  pl.BlockSpec((1, T, 128), lambda b, i: (b, 0, 0)),
                  pl.BlockSpec((1, 128, nbp), lambda b, i: (b, 0, 0)),
                  pl.BlockSpec((1, nbp, dk), lambda b, i: (b, 0, 0)),
                  pl.BlockSpec((1, H, TQ, 3), lambda b, i: (b, 0, i, 0)),
                  pl.BlockSpec((H, 1, 1), lambda b, i: (0, 0, 0)),
                  pl.BlockSpec((nwc, H, TQ, TQ), lambda b, i: (0, 0, 0, 0)),
                  pl.BlockSpec((nv, TQ, SLC_CHUNK), lambda b, i: (0, 0, 0))],
        out_specs=pl.BlockSpec((1, H, TQ, dk), lambda b, i: (b, 0, i, 0)),
        out_shape=jax.ShapeDtypeStruct((B, H, T, dk), jnp.float32),
        compiler_params=pltpu.CompilerParams(dimension_semantics=("arbitrary", "arbitrary"),
                                             vmem_limit_bytes=VMEM_LIMIT_BYTES),
        name="nsa_prompt",
    )(q_lo, kt, _with_ones_column(slc_v), kwt, _with_ones_column(win_v), kct, vcp, gh,
      jnp.asarray(slopes).reshape(H, 1, 1), jnp.asarray(wb), jnp.asarray(cb))


MLA_TILE = 512


def _mla_prompt_kernel(q_ref, kt_ref, v_ref, tril_ref, o_ref, *, scale_log2e):
    TQ = MLA_TILE
    i = pl.program_id(2)
    q = q_ref[0, 0]

    def step(c, carry, bias):
        m_i, acc = carry
        k0 = pl.multiple_of(c * TQ, TQ)
        s = jnp.dot(q, kt_ref[0, 0, :, pl.ds(k0, TQ)], preferred_element_type=jnp.float32)
        if bias is not None:
            s = s + bias
        m_new = jnp.maximum(m_i, jnp.max(s, axis=-1, keepdims=True))
        corr = jnp.exp2((m_i - m_new) * scale_log2e)
        p = jnp.exp2((s - m_new) * scale_log2e)
        acc_new = acc * corr + jnp.dot(p.astype(jnp.bfloat16), v_ref[0, 0, pl.ds(k0, TQ), :],
                                       preferred_element_type=jnp.float32)
        return m_new, acc_new

    init = (jnp.full((TQ, 1), NEG_INF, jnp.float32), jnp.zeros((TQ, 128), jnp.float32))
    carry = lax.fori_loop(0, i, lambda c, cr: step(c, cr, None), init)
    m_i, acc = step(i, carry, tril_ref[...])
    o_ref[0, 0] = acc[:, :MLA_V] / acc[:, MLA_V:MLA_V + 1]


def mla_prompt_attention(qn, qp, kn, kp, v):
    B, T, H, _ = qn.shape
    TQ = MLA_TILE
    assert T % TQ == 0
    bf = jnp.bfloat16
    dq = MLA_NOPE + MLA_ROPE
    qcat = jnp.concatenate([qn, qp, jnp.zeros((B, T, H, 128 - dq), qn.dtype)], axis=-1)
    qcat = jnp.transpose(qcat, (0, 2, 1, 3)).astype(bf)
    kcat = jnp.concatenate([kn, jnp.broadcast_to(kp[:, :, None, :], (B, T, H, MLA_ROPE)),
                            jnp.zeros((B, T, H, 128 - dq), kn.dtype)], axis=-1)
    kt = jnp.transpose(kcat, (0, 2, 3, 1)).astype(bf)
    vh = _with_ones_column(jnp.transpose(v, (0, 2, 1, 3)))
    tril = np.where(np.arange(TQ)[None, :] <= np.arange(TQ)[:, None], 0.0, NEG_INF).astype(np.float32)
    kern = functools.partial(_mla_prompt_kernel, scale_log2e=float(dq ** -0.5 * math.log2(math.e)))
    return pl.pallas_call(
        kern,
        grid=(B, H, T // TQ),
        in_specs=[pl.BlockSpec((1, 1, TQ, 128), lambda b, h, i: (b, h, i, 0)),
                  pl.BlockSpec((1, 1, 128, T), lambda b, h, i: (b, h, 0, 0)),
                  pl.BlockSpec((1, 1, T, 128), lambda b, h, i: (b, h, 0, 0)),
                  pl.BlockSpec((TQ, TQ), lambda b, h, i: (0, 0))],
        out_specs=pl.BlockSpec((1, 1, TQ, MLA_V), lambda b, h, i: (b, h, i, 0)),
        out_shape=jax.ShapeDtypeStruct((B, H, T, MLA_V), jnp.float32),
        compiler_params=pltpu.CompilerParams(dimension_semantics=("arbitrary", "arbitrary", "arbitrary"),
                                             vmem_limit_bytes=VMEM_LIMIT_BYTES),
        name="mla_prompt",
    )(qcat, kt, vh, jnp.asarray(tril))


def _swa_prompt_kernel(q_ref, kt_ref, v_ref, bias_ref, sink_ref, o_ref):
    TQ = Q_TILE
    R = SWA_HEADS // SWA_KV_HEADS
    i = pl.program_id(2)
    q = q_ref[0, 0].reshape(R * TQ, SWA_HD)
    kprev = pl.multiple_of(jnp.maximum(i - 1, 0) * TQ, TQ)
    kcur = pl.multiple_of(i * TQ, TQ)
    s0 = jnp.dot(q, kt_ref[0, 0, :, pl.ds(kprev, TQ)], preferred_element_type=jnp.float32).reshape(R, TQ, TQ)
    s1 = jnp.dot(q, kt_ref[0, 0, :, pl.ds(kcur, TQ)], preferred_element_type=jnp.float32).reshape(R, TQ, TQ)
    s0 = s0 + jnp.where(i > 0, bias_ref[0, 0], NEG_INF)
    s1 = s1 + bias_ref[0, 1]
    sink = sink_ref[0]
    m = jnp.maximum(jnp.maximum(s0.max(axis=-1, keepdims=True), s1.max(axis=-1, keepdims=True)), sink)
    p0 = jnp.exp(s0 - m)
    p1 = jnp.exp(s1 - m)
    den = p0.sum(axis=-1, keepdims=True) + p1.sum(axis=-1, keepdims=True) + jnp.exp(sink - m)
    o = (jnp.dot(p0.reshape(R * TQ, TQ).astype(jnp.bfloat16), v_ref[0, 0, pl.ds(kprev, TQ), :],
                 preferred_element_type=jnp.float32)
         + jnp.dot(p1.reshape(R * TQ, TQ).astype(jnp.bfloat16), v_ref[0, 0, pl.ds(kcur, TQ), :],
                   preferred_element_type=jnp.float32))
    o_ref[0, 0] = o.reshape(R, TQ, SWA_HD) / jnp.maximum(den, 1e-30)


def swa_prompt_attention(q, k, v, sinks):
    B, T, H, hd = q.shape
    G = k.shape[2]
    R = H // G
    TQ = Q_TILE
    assert SWA_WIN == TQ and T % TQ == 0
    bf = jnp.bfloat16
    slopes = alibi_slopes_np(H)
    qh = (jnp.transpose(q, (0, 2, 1, 3)) * (hd ** -0.5)).astype(bf).reshape(B, G, R, T, hd)
    kt = jnp.transpose(k, (0, 2, 3, 1)).astype(bf)
    vh = jnp.transpose(v, (0, 2, 1, 3)).astype(bf)
    ii = np.arange(TQ)[:, None]
    jj = np.arange(TQ)[None, :]
    bias = np.zeros((G, 2, R, TQ, TQ), np.float32)
    for r in range(2):
        d = ii - jj + TQ * (1 - r)
        ok = (d >= 0) & (d <= SWA_WIN)
        bias[:, r] = np.where(ok[None, None],
                              -slopes.reshape(G, R)[:, :, None, None] * d[None, None].astype(np.float32), NEG_INF)
    return pl.pallas_call(
        _swa_prompt_kernel,
        grid=(B, G, T // TQ),
        in_specs=[pl.BlockSpec((1, 1, R, TQ, hd), lambda b, g, i: (b, g, 0, i, 0)),
                  pl.BlockSpec((1, 1, hd, T), lambda b, g, i: (b, g, 0, 0)),
                  pl.BlockSpec((1, 1, T, hd), lambda b, g, i: (b, g, 0, 0)),
                  pl.BlockSpec((1, 2, R, TQ, TQ), lambda b, g, i: (g, 0, 0, 0, 0)),
                  pl.BlockSpec((1, R, 1, 1), lambda b, g, i: (g, 0, 0, 0))],
        out_specs=pl.BlockSpec((1, 1, R, TQ, hd), lambda b, g, i: (b, g, 0, i, 0)),
        out_shape=jax.ShapeDtypeStruct((B, G, R, T, hd), jnp.float32),
        compiler_params=pltpu.CompilerParams(dimension_semantics=("arbitrary", "arbitrary", "arbitrary"),
                                             vmem_limit_bytes=VMEM_LIMIT_BYTES),
        name="swa_prompt",
    )(qh, kt, vh, jnp.asarray(bias), sinks.reshape(G, R, 1, 1)).reshape(B, H, T, hd)


def window_sample(q, k_buf, v_buf, k_new, v_new, past_len, window, slopes, sink=None):
    Wb, Tn = k_buf.shape[1], q.shape[1]
    k = jnp.concatenate([k_buf, k_new], axis=1)
    v = jnp.concatenate([v_buf, v_new], axis=1)
    qpos = past_len + jnp.arange(Tn)
    kpos = past_len - Wb + jnp.arange(Wb + Tn)
    dist = qpos[:, None] - kpos[None, :]
    valid = (dist >= 0) & (dist <= window)
    return attend(q, k, v, dist, valid, slopes, sink)[0]


def nsa_compress(blocks, pos_emb, w1, w2):
    x = blocks + pos_emb
    h = jax.nn.gelu(jnp.einsum("...lcd,lcdh->...ch", x, w1))
    return jnp.einsum("...ch,chd->...cd", h, w2)


def nsa_select(p_cmp, qpos, n_blocks):
    imp = jnp.sum(p_cmp, axis=1)
    blk = jnp.arange(n_blocks)[None, :]
    cur = (qpos // NSA_BLK)[:, None]
    imp = jnp.where(blk == cur, float(NSA_HEADS + 1), jnp.where(blk < cur, imp, -1.0))
    picks = []
    for _ in range(min(NSA_TOPN, n_blocks)):
        i = jnp.argmax(imp, axis=-1)
        picks.append(i)
        imp = jnp.where(blk == i[..., None], -jnp.inf, imp)
    return jnp.stack(picks, axis=-1)


def nsa_compressed_branch(q, qpos, kc, vc, c_end, slopes):
    dist = qpos[:, None] - c_end[None, :]
    o, p = attend(q, kc[:, :, None, :], vc[:, :, None, :], dist, dist >= 0, slopes)
    return o, nsa_select(p, qpos, kc.shape[1])


def nsa_selected_branch(q, qpos, sel, idx, slopes):
    B, Q, n = idx.shape
    kpos = (idx[..., None] * NSA_BLK + jnp.arange(NSA_BLK)).reshape(B, Q, n * NSA_BLK)
    dist = qpos[None, :, None] - kpos
    ks = sel[..., 0, :].reshape(B, Q, n * NSA_BLK, 1, NSA_DK)
    vs = sel[..., 1, :].reshape(B, Q, n * NSA_BLK, 1, NSA_DK)
    return attend(q, ks, vs, dist, dist >= 0, slopes)[0]


def nsa_gate(gates, o_cmp, o_slc, o_win):
    g = gates.astype(o_cmp.dtype)
    return g[..., 0:1] * o_cmp + g[..., 1:2] * o_slc + g[..., 2:3] * o_win


CMP_PAGES_PER_STEP = 64
PAGED_VMEM_LIMIT_BYTES = 56 * 1024 * 1024


def pages_feature_major(cache):
    return jnp.transpose(cache, (0, 1, 3, 4, 2)).reshape(-1, 2, NSA_DK, PAGE_SIZE)


def _nsa_compress_kernel(pt_ref, pos_ref, w1b_ref, w2_ref, *rest, n_pages):
    page_refs = rest[:n_pages]
    o0_ref, o1_ref, slab = rest[n_pages:]
    feat = 2 * NSA_DK
    for j, r in enumerate(page_refs):
        slab[j * feat:(j + 1) * feat, :] = r[0].reshape(feat, PAGE_SIZE)
    hs = []
    for c in range(2):
        xs = [(slab[pl.ds(c * NSA_DK + d, n_pages, stride=feat), :] + pos_ref[c * NSA_DK + d:c * NSA_DK + d + 1, :]
               ).astype(jnp.bfloat16) for d in range(NSA_DK)]
        h = jnp.dot(jnp.concatenate(xs, axis=1), w1b_ref[c], preferred_element_type=jnp.float32)
        hs.append(jax.nn.gelu(h).astype(jnp.bfloat16))
    for j, o_ref in enumerate((o0_ref, o1_ref)):
        hj = jnp.concatenate([hs[c][:, j * NSA_CMP_HID:(j + 1) * NSA_CMP_HID] for c in range(2)], axis=1)
        o_ref[0] = jnp.dot(hj, w2_ref[...], preferred_element_type=jnp.float32)


def nsa_compress_paged(pages_t, page_table, pos_emb, w1, w2):
    S, P = page_table.shape
    PG = math.gcd(CMP_PAGES_PER_STEP, P)
    assert PAGE_SIZE == 2 * NSA_BLK and PG % 8 == 0
    bf = jnp.bfloat16
    feat = 2 * NSA_DK
    pos_t = jnp.tile(jnp.transpose(pos_emb, (1, 2, 0)).reshape(feat, NSA_BLK), (1, 2))
    w1f = jnp.transpose(w1, (1, 2, 0, 3)).reshape(feat, NSA_BLK, NSA_CMP_HID)
    z = jnp.zeros_like(w1f)
    w1b = jnp.concatenate([jnp.concatenate([w1f, z], axis=2), jnp.concatenate([z, w1f], axis=2)], axis=1).astype(bf)
    w1b = w1b.reshape(2, NSA_DK * PAGE_SIZE, 2 * NSA_CMP_HID)
    eye = jnp.eye(2, dtype=w2.dtype)
    w2z = jnp.einsum("chd,ce->ched", w2, eye).reshape(2 * NSA_CMP_HID, feat).astype(bf)

    def page_spec(j):
        return pl.BlockSpec((1, 2, NSA_DK, PAGE_SIZE), lambda s, g, pt: (pt[s, g * PG + j], 0, 0, 0))

    grid_spec = pltpu.PrefetchScalarGridSpec(
        num_scalar_prefetch=1,
        grid=(S, P // PG),
        in_specs=[pl.BlockSpec(pos_t.shape, lambda s, g, pt: (0, 0)),
                  pl.BlockSpec(w1b.shape, lambda s, g, pt: (0, 0, 0), pipeline_mode=pl.Buffered(1)),
                  pl.BlockSpec(w2z.shape, lambda s, g, pt: (0, 0))] + [page_spec(j) for j in range(PG)],
        out_specs=[pl.BlockSpec((1, PG, feat), lambda s, g, pt: (s, g, 0)),
                   pl.BlockSpec((1, PG, feat), lambda s, g, pt: (s, g, 0))],
        scratch_shapes=[pltpu.VMEM((PG * feat, PAGE_SIZE), jnp.float32)])
    o0, o1 = pl.pallas_call(
        functools.partial(_nsa_compress_kernel, n_pages=PG),
        grid_spec=grid_spec,
        out_shape=[jax.ShapeDtypeStruct((S, P, feat), jnp.float32)] * 2,
        compiler_params=pltpu.CompilerParams(dimension_semantics=("arbitrary", "arbitrary"),
                                             vmem_limit_bytes=PAGED_VMEM_LIMIT_BYTES),
        name="nsa_compress",
    )(page_table, pos_t, w1b, w2z, *([pages_t] * PG))
    return jnp.stack([o0, o1], axis=2).reshape(S, 2 * P, 2, NSA_DK)


def nsa_prompt(q, cmp_kv, slc_kv, win_kv, gates, ep, slopes):
    B, T = q.shape[:2]
    nb = T // NSA_BLK
    ppb = T // PAGE_SIZE
    prompt_pages = jnp.arange(B * ppb, dtype=jnp.int32).reshape(B, ppb)
    comp = nsa_compress_paged(pages_feature_major(cmp_kv.reshape(1, B * ppb, PAGE_SIZE, 2, NSA_DK)), prompt_pages,
                              ep["cmp_pos"], ep["cmp_w1"], ep["cmp_w2"])
    kc = rmsnorm(comp[:, :, 0], ep["g_k"][0])
    vc = comp[:, :, 1]
    o = nsa_prompt_attention(q, slc_kv[:, :, 0], slc_kv[:, :, 1], win_kv[:, :, 0], win_kv[:, :, 1], kc, vc, gates)
    return jnp.transpose(o, (0, 2, 1, 3))


def _nsa_sample_selected_kernel(phys_ref, idx_ref, q_ref, new_ref, slope_ref, *rest, n_sel, nb_past, qpos):
    page_refs = rest[:n_sel]
    o_ref = rest[n_sel]
    b = pl.program_id(0)
    nt = (((1,), (1,)), ((), ()))
    q = q_ref[0]
    lane = lax.broadcasted_iota(jnp.int32, (1, PAGE_SIZE), 1)
    ss, vts, dists = [], [], []
    for n, r in enumerate(page_refs):
        blk_id = idx_ref[b, n]
        is_new = blk_id >= nb_past
        half = jnp.where(is_new, 0, blk_id % 2)
        kt = jnp.where(is_new, new_ref[0, 0], r[0, 0]).astype(jnp.bfloat16)
        vts.append(jnp.where(is_new, new_ref[0, 1], r[0, 1]).astype(jnp.bfloat16))
        ss.append(jnp.dot(q, kt, preferred_element_type=jnp.float32))
        in_blk = (lane // NSA_BLK) == half
        dists.append(jnp.where(in_blk, qpos - blk_id * NSA_BLK - (lane % NSA_BLK), -1))
    s = jnp.concatenate(ss, axis=1)
    dist = jnp.concatenate(dists, axis=1)
    s = jnp.where(dist >= 0, s - slope_ref[...] * dist.astype(jnp.float32), NEG_INF)
    m = jnp.max(s, axis=-1, keepdims=True)
    e = jnp.where(dist >= 0, jnp.exp(s - m), 0.0)
    p = (e / jnp.maximum(jnp.sum(e, axis=-1, keepdims=True), 1e-30)).astype(jnp.bfloat16)
    o = jnp.zeros((q.shape[0], NSA_DK), jnp.float32)
    for n in range(n_sel):
        o = o + lax.dot_general(p[:, n * PAGE_SIZE:(n + 1) * PAGE_SIZE], vts[n], nt, preferred_element_type=jnp.float32)
    o_ref[0] = o


def nsa_sample_selected(q, idx, slc_new, cache_slc, page_table, e, qpos):
    DB, Tn, H, dk = q.shape
    n_sel = idx.shape[-1]
    assert Tn == 1
    P = page_table.shape[1]
    bpp = PAGE_SIZE // NSA_BLK
    nb_past = P * bpp
    idx2 = idx[:, 0].astype(jnp.int32)
    jp = jnp.minimum(idx2, nb_past - 1)
    phys = jnp.take_along_axis(page_table, jp // bpp, axis=1) + e * cache_slc.shape[1]
    qs = (q[:, 0] * (dk ** -0.5)).astype(jnp.bfloat16)
    new_t = jnp.zeros((DB, 2, dk, PAGE_SIZE), jnp.float32).at[:, :, :, 0].set(slc_new[:, 0])
    kern = functools.partial(_nsa_sample_selected_kernel, n_sel=n_sel, nb_past=nb_past, qpos=int(qpos))

    def page_spec(n):
        return pl.BlockSpec((1, 2, dk, PAGE_SIZE), lambda b, ph, ix: (ph[b, n], 0, 0, 0))

    grid_spec = pltpu.PrefetchScalarGridSpec(
        num_scalar_prefetch=2,
        grid=(DB,),
        in_specs=[pl.BlockSpec((1, H, dk), lambda b, ph, ix: (b, 0, 0)),
                  pl.BlockSpec((1, 2, dk, PAGE_SIZE), lambda b, ph, ix: (b, 0, 0, 0)),
                  pl.BlockSpec((H, 1), lambda b, ph, ix: (0, 0))] + [page_spec(n) for n in range(n_sel)],
        out_specs=pl.BlockSpec((1, H, dk), lambda b, ph, ix: (b, 0, 0)))
    out = pl.pallas_call(
        kern,
        grid_spec=grid_spec,
        out_shape=jax.ShapeDtypeStruct((DB, H, dk), jnp.float32),
        compiler_params=pltpu.CompilerParams(dimension_semantics=("arbitrary",), vmem_limit_bytes=VMEM_LIMIT_BYTES),
        name="nsa_sample_selected",
    )(phys, idx2, qs, new_t, jnp.asarray(alibi_slopes_np(H)).reshape(H, 1), *([pages_feature_major(cache_slc)] * n_sel))
    return out[:, None]


def nsa_sample(q, qpos, cmp_new, slc_new, win_new, gates, cache_cmp, cache_slc, win_buf, page_table, e, ep, slopes):
    DB, Tn = q.shape[:2]
    n_pages = page_table.shape[1]
    bpp = PAGE_SIZE // NSA_BLK
    nb_past = n_pages * bpp
    past_len = n_pages * PAGE_SIZE
    nb_new = -(-Tn // NSA_BLK)
    pad = ((0, 0), (0, nb_new * NSA_BLK - Tn), (0, 0), (0, 0))

    def compress(rows):
        return nsa_compress(rows, ep["cmp_pos"], ep["cmp_w1"], ep["cmp_w2"])

    comp_past = nsa_compress_paged(pages_feature_major(cache_cmp), page_table + e * cache_cmp.shape[1],
                                   ep["cmp_pos"], ep["cmp_w1"], ep["cmp_w2"])
    comp_new = compress(jnp.pad(cmp_new, pad).reshape(DB, nb_new, NSA_BLK, 2, NSA_DK))
    comp = jnp.concatenate([comp_past, comp_new], axis=1)
    kc = rmsnorm(comp[:, :, 0], ep["g_k"][0])
    vc = comp[:, :, 1]
    c_end = (jnp.arange(nb_past + nb_new) + 1) * NSA_BLK - 1
    o_cmp, idx = nsa_compressed_branch(q, qpos, kc, vc, c_end, slopes)

    o_slc = nsa_sample_selected(q, idx, slc_new, cache_slc, page_table, e, past_len)

    o_win = window_sample(q, win_buf[:, :, 0:1], win_buf[:, :, 1:2], win_new[:, :, 0:1], win_new[:, :, 1:2],
                          past_len, NSA_WIN, slopes)
    return nsa_gate(gates, o_cmp, o_slc, o_win)


def mla_keys(lat, ep):
    c = lat[..., :MLA_LORA]
    kp = lat[..., MLA_LORA:]
    kn = rmsnorm(jnp.einsum("bkc,chd->bkhd", c, ep["w_uk"]), ep["g_kn"])
    v = jnp.einsum("bkc,chd->bkhd", c, ep["w_uv"])
    return kn, kp, v


def mla_scores(qn, qp, kn, kp):
    s = jnp.einsum("bqhd,bkhd->bhqk", qn, kn) + jnp.einsum("bqhd,bkd->bhqk", qp, kp)
    return s.astype(jnp.float32) * ((MLA_NOPE + MLA_ROPE) ** -0.5)


def mla_prompt(qn, qp, lat, ep):
    B, T = qn.shape[:2]
    kn, kp, v = mla_keys(lat, ep)
    return jnp.transpose(mla_prompt_attention(qn, qp, kn, kp, v), (0, 2, 1, 3))


MLA_PAGES_PER_STEP = 64


def _mla_sample_kernel(pt_ref, qg_ref, qp_ref, wukt_ref, wuv_ref, gt_ref, new_ref, *rest, scale, n_pages):
    page_refs = rest[:n_pages]
    o_ref, c_scr, kp_scr, a_scr, m_scr, l_scr, acc_scr = rest[n_pages:]
    g = pl.program_id(1)
    ng = pl.num_programs(1)
    nt = (((1,), (1,)), ((), ()))

    @pl.when(g == 0)
    def _():
        a_scr[...] = lax.dot_general(qg_ref[0], wukt_ref[...], (((1,), (0,)), ((), ())),
                                     preferred_element_type=jnp.float32).astype(jnp.bfloat16)
        m_scr[...] = jnp.full(m_scr.shape, NEG_INF, jnp.float32)
        l_scr[...] = jnp.zeros(l_scr.shape, jnp.float32)
        acc_scr[...] = jnp.zeros(acc_scr.shape, jnp.float32)

    def scores(ct, kpt):
        projt = jnp.dot(wukt_ref[...], ct, preferred_element_type=jnp.float32)
        ss = jnp.dot(gt_ref[...], (projt * projt).astype(jnp.bfloat16), preferred_element_type=jnp.float32)
        num = jnp.dot(a_scr[...], ct, preferred_element_type=jnp.float32)
        sp = jnp.dot(qp_ref[0], kpt, preferred_element_type=jnp.float32)
        return (num * lax.rsqrt(ss * (1.0 / MLA_NOPE) + EPS) + sp) * scale

    def softmax_update(ct, st):
        m_i = m_scr[...]
        m_new = jnp.maximum(m_i, jnp.max(st, axis=-1, keepdims=True))
        corr = jnp.exp(m_i - m_new)
        p = jnp.exp(st - m_new)
        m_scr[...] = m_new
        l_scr[...] = l_scr[...] * corr + jnp.sum(p, axis=-1, keepdims=True)
        acc_scr[...] = acc_scr[...] * corr + lax.dot_general(p.astype(jnp.bfloat16), ct, nt,
                                                             preferred_element_type=jnp.float32)

    for j, r in enumerate(page_refs):
        page = r[0]
        c_scr[:, j * PAGE_SIZE:(j + 1) * PAGE_SIZE] = page[:MLA_LORA, :].astype(jnp.bfloat16)
        kp_scr[:, j * PAGE_SIZE:(j + 1) * PAGE_SIZE] = page[MLA_LORA:, :].astype(jnp.bfloat16)
    c_all = c_scr[...]
    softmax_update(c_all, scores(c_all, kp_scr[...]))

    @pl.when(g == ng - 1)
    def _():
        new = new_ref[0]
        ct = new[:MLA_LORA, :].astype(jnp.bfloat16)
        st = scores(ct, new[MLA_LORA:, :].astype(jnp.bfloat16))
        key = lax.broadcasted_iota(jnp.int32, st.shape, 1)
        softmax_update(ct, jnp.where(key == 0, st, NEG_INF))
        o8 = jnp.dot((acc_scr[...] / l_scr[...]).astype(jnp.bfloat16), wuv_ref[...],
                     preferred_element_type=jnp.float32)
        row = lax.broadcasted_iota(jnp.int32, o8.shape, 0)
        col = lax.broadcasted_iota(jnp.int32, o8.shape, 1)
        o_ref[0] = jnp.sum(jnp.where(col // MLA_V == row, o8, 0.0), axis=0, keepdims=True)


def mla_sample(qn, qp, lat_new, cache_mla, page_table, e, ep):
    DB, Tn = qn.shape[:2]
    H = MLA_HEADS
    P = page_table.shape[1]
    PG = math.gcd(MLA_PAGES_PER_STEP, P)
    assert Tn == 1
    bf = jnp.bfloat16
    dq = MLA_NOPE + MLA_ROPE
    qg = qn[:, 0] * ep["g_kn"]
    eye = jnp.eye(H, dtype=qg.dtype)
    qg_exp = (qg[:, :, None, :] * eye[None, :, :, None]).reshape(DB, H, H * MLA_NOPE).astype(bf)
    qp_h = qp[:, 0].astype(bf)
    wukt = ep["w_uk"].reshape(MLA_LORA, H * MLA_NOPE).T.astype(bf)
    wuv = ep["w_uv"].reshape(MLA_LORA, H * MLA_V).astype(bf)
    gt = np.zeros((H, H * MLA_NOPE), np.float32)
    gt[np.arange(H * MLA_NOPE) // MLA_NOPE, np.arange(H * MLA_NOPE)] = 1.0
    pages_t = jnp.swapaxes(cache_mla, -1, -2).reshape(-1, MLA_ROW, PAGE_SIZE)
    new_t = jnp.zeros((DB, MLA_ROW, PAGE_SIZE), jnp.float32).at[:, :, 0].set(lat_new[:, 0])
    kern = functools.partial(_mla_sample_kernel, scale=float(dq ** -0.5), n_pages=PG)

    def page_spec(j):
        return pl.BlockSpec((1, MLA_ROW, PAGE_SIZE), lambda b, g, pt: (pt[b, g * PG + j], 0, 0))

    const2 = lambda b, g, pt: (0, 0)
    grid_spec = pltpu.PrefetchScalarGridSpec(
        num_scalar_prefetch=1,
        grid=(DB, P // PG),
        in_specs=[pl.BlockSpec((1, H, H * MLA_NOPE), lambda b, g, pt: (b, 0, 0)),
                  pl.BlockSpec((1, H, MLA_ROPE), lambda b, g, pt: (b, 0, 0)),
                  pl.BlockSpec((H * MLA_NOPE, MLA_LORA), const2),
                  pl.BlockSpec((MLA_LORA, H * MLA_V), const2),
                  pl.BlockSpec((H, H * MLA_NOPE), const2),
                  pl.BlockSpec((1, MLA_ROW, PAGE_SIZE), lambda b, g, pt: (b, 0, 0))]
                 + [page_spec(j) for j in range(PG)],
        out_specs=pl.BlockSpec((1, 1, H * MLA_V), lambda b, g, pt: (b, 0, 0)),
        scratch_shapes=[pltpu.VMEM((MLA_LORA, PG * PAGE_SIZE), bf),
                        pltpu.VMEM((MLA_ROPE, PG * PAGE_SIZE), bf),
                        pltpu.VMEM((H, MLA_LORA), bf),
                        pltpu.VMEM((8, 1), jnp.float32),
                        pltpu.VMEM((8, 1), jnp.float32),
                        pltpu.VMEM((8, MLA_LORA), jnp.float32)])
    out = pl.pallas_call(
        kern,
        grid_spec=grid_spec,
        out_shape=jax.ShapeDtypeStruct((DB, 1, H * MLA_V), jnp.float32),
        compiler_params=pltpu.CompilerParams(dimension_semantics=("arbitrary", "arbitrary"),
                                             vmem_limit_bytes=VMEM_LIMIT_BYTES),
        name="mla_sample",
    )(page_table + e * cache_mla.shape[1], qg_exp, qp_h, wukt, wuv, jnp.asarray(gt).astype(bf), new_t,
      *([pages_t] * PG))
    return out.reshape(DB, 1, H, MLA_V)


def even_project(x, g_norm, pos, ep):
    B, T, D = x.shape
    h = norm_matmul(x.reshape(B * T, D), g_norm, ep["w_in"]).reshape(B, T, EVEN_IN)
    cuts = np.cumsum([NSA_Q_COLS, NSA_KV_COLS, NSA_GATE_COLS, MLA_Q_COLS]).tolist()
    q, kv, g, mq, lat = jnp.split(h, cuts, axis=-1)
    q = rmsnorm(q.reshape(B, T, NSA_HEADS, NSA_DK), ep["g_q"])
    kv = kv.reshape(B, T, 3, 2, NSA_DK)
    cmp_kv = kv[:, :, 0]
    slc_kv = jnp.stack([rmsnorm(kv[:, :, 1, 0], ep["g_k"][1]), kv[:, :, 1, 1]], axis=2)
    win_kv = jnp.stack([rmsnorm(kv[:, :, 2, 0], ep["g_k"][2]), kv[:, :, 2, 1]], axis=2)
    gates = jax.nn.sigmoid(g.astype(jnp.float32)).reshape(B, T, NSA_HEADS, 3)
    mq = mq.reshape(B, T, MLA_HEADS, MLA_NOPE + MLA_ROPE)
    qn = rmsnorm(mq[..., :MLA_NOPE], ep["g_qn"])
    qp = rope(rmsnorm(mq[..., MLA_NOPE:], ep["g_qpe"]), pos)
    c = rmsnorm(lat[..., :MLA_LORA], ep["g_lat"])
    kp = rope(rmsnorm(lat[..., MLA_LORA:], ep["g_kpe"])[:, :, None, :], pos)[:, :, 0, :]
    return q, cmp_kv, slc_kv, win_kv, gates, qn, qp, jnp.concatenate([c, kp], axis=-1)


def even_output(x, o_nsa, o_mla, w_out):
    B, T, D = x.shape
    a = jnp.concatenate([o_nsa.reshape(B, T, -1), o_mla.reshape(B, T, -1)], axis=-1)
    return matmul_res(a.reshape(B * T, EVEN_OUT), w_out, x.reshape(B * T, D)).reshape(B, T, D)


def odd_project(x, g_norm, op):
    B, T, D = x.shape
    h = norm_matmul(x.reshape(B * T, D), g_norm, op["w_in"]).reshape(B, T, ODD_IN)
    q, k, v = jnp.split(h, [SWA_HEADS * SWA_HD, SWA_HEADS * SWA_HD + SWA_KV_HEADS * SWA_HD], axis=-1)
    q = rmsnorm(q.reshape(B, T, SWA_HEADS, SWA_HD), op["g_q"])
    k = rmsnorm(k.reshape(B, T, SWA_KV_HEADS, SWA_HD), op["g_k"])
    v = v.reshape(B, T, SWA_KV_HEADS, SWA_HD)
    return q, jnp.stack([k, v], axis=2)


def memory_kv(mem, cp):
    B, M, D = mem.shape
    w_kv = jnp.concatenate([cp["w_k"], cp["w_v"]], axis=1)
    kv = norm_matmul(mem.reshape(B * M, D), cp["g_mem"], w_kv).reshape(B, M, 2, MEM_HEADS, MEM_HD)
    k = rmsnorm(kv[:, :, 0], cp["g_k"])
    return jnp.stack([k, kv[:, :, 1]], axis=2)


MEM_TQ = 256


def _memory_cross_kernel(x_ref, g_ref, wq_ref, gq_ref, k_ref, v_ref, wo_ref, o_ref):
    x = x_ref[0]
    xn = x * lax.rsqrt(jnp.mean(x * x, axis=-1, keepdims=True) + EPS) * g_ref[...]
    q = jnp.dot(xn.astype(jnp.bfloat16), wq_ref[...], preferred_element_type=jnp.float32)
    nt = (((1,), (1,)), ((), ()))
    outs = []
    for h in range(MEM_HEADS):
        cols = slice(h * MEM_HD, (h + 1) * MEM_HD)
        qh = q[:, cols]
        qh = qh * lax.rsqrt(jnp.mean(qh * qh, axis=-1, keepdims=True) + EPS) * gq_ref[...]
        s = lax.dot_general(qh.astype(jnp.bfloat16), k_ref[0, :, cols], nt,
                            preferred_element_type=jnp.float32) * (MEM_HD ** -0.5)
        e = jnp.exp(s - jnp.max(s, axis=-1, keepdims=True))
        p = e / jnp.sum(e, axis=-1, keepdims=True)
        outs.append(jnp.dot(p.astype(jnp.bfloat16), v_ref[0, :, cols], preferred_element_type=jnp.float32))
    o = jnp.concatenate(outs, axis=1).astype(jnp.bfloat16)
    o_ref[0] = x + jnp.dot(o, wo_ref[...], preferred_element_type=jnp.float32)


def memory_cross_prompt(x, g_norm, mem_kv, cp):
    B, T, D = x.shape
    M = mem_kv.shape[1]
    bf = jnp.bfloat16
    tq = MEM_TQ
    k = mem_kv[:, :, 0].reshape(B, M, MEM_W).astype(bf)
    v = mem_kv[:, :, 1].reshape(B, M, MEM_W).astype(bf)
    return pl.pallas_call(
        _memory_cross_kernel,
        grid=(B, T // tq),
        in_specs=[pl.BlockSpec((1, tq, D), lambda b, i: (b, i, 0)),
                  pl.BlockSpec((1, D), lambda b, i: (0, 0)),
                  pl.BlockSpec((D, MEM_W), lambda b, i: (0, 0)),
                  pl.BlockSpec((1, MEM_HD), lambda b, i: (0, 0)),
                  pl.BlockSpec((1, M, MEM_W), lambda b, i: (b, 0, 0)),
                  pl.BlockSpec((1, M, MEM_W), lambda b, i: (b, 0, 0)),
                  pl.BlockSpec((MEM_W, D), lambda b, i: (0, 0))],
        out_specs=pl.BlockSpec((1, tq, D), lambda b, i: (b, i, 0)),
        out_shape=jax.ShapeDtypeStruct((B, T, D), jnp.float32),
        compiler_params=pltpu.CompilerParams(dimension_semantics=("arbitrary", "arbitrary"),
                                             vmem_limit_bytes=VMEM_LIMIT_BYTES),
        name="memory_cross",
    )(x, g_norm.reshape(1, D), cp["w_q"].astype(bf), cp["g_q"].reshape(1, MEM_HD), k, v, cp["w_o"].astype(bf))


def memory_cross(x, g_norm, mem_kv, cp):
    B, T, D = x.shape
    if T % MEM_TQ == 0:
        return memory_cross_prompt(x, g_norm, mem_kv, cp)
    q = norm_matmul(x.reshape(B * T, D), g_norm, cp["w_q"]).reshape(B, T, MEM_HEADS, MEM_HD)
    q = rmsnorm(q, cp["g_q"])
    s = jnp.einsum("bthd,bmhd->bhtm", q, mem_kv[:, :, 0]).astype(jnp.float32) * (MEM_HD ** -0.5)
    p = jax.nn.softmax(s, axis=-1).astype(x.dtype)
    o = jnp.einsum("bhtm,bmhd->bthd", p, mem_kv[:, :, 1]).reshape(B * T, MEM_W)
    return matmul_res(o, cp["w_o"], x.reshape(B * T, D)).reshape(B, T, D)


PEER_TB = 128
PEER_VMEM_LIMIT_BYTES = 56 * 1024 * 1024
F32_NEG_INF = float("-inf")


def _topk_rows(s, row, k):
    nrow = float(s.shape[0])
    vals, ids = [], []
    for _ in range(k):
        m = jnp.max(s, axis=0, keepdims=True)
        i = jnp.min(jnp.where(s == m, row, nrow), axis=0, keepdims=True)
        vals.append(m)
        ids.append(i)
        s = jnp.where(row == i, F32_NEG_INF, s)
    return vals, ids


def _peer_route_kernel(x_ref, g_ref, wq_ref, sk0_ref, sk1_ref, xn_ref, idx_ref, gate_ref):
    x = x_ref[...]
    xn = x * lax.rsqrt(jnp.mean(x * x, axis=-1, keepdims=True) + EPS) * g_ref[...]
    xn_ref[...] = xn
    q = jnp.dot(xn.astype(jnp.bfloat16), wq_ref[...], preferred_element_type=jnp.float32)
    tb = x.shape[0]
    row_k = lax.broadcasted_iota(jnp.int32, (PEER_NKEYS, tb), 0).astype(jnp.float32)
    sub8 = lax.broadcasted_iota(jnp.int32, (8, tb), 0).astype(jnp.float32)
    assert PEER_TOPK % 16 == 0
    nt = (((1,), (1,)), ((), ()))
    ids, gates = [], []
    for h in range(PEER_HEADS):
        qh = q[:, h * PEER_DKEY:(h + 1) * PEER_DKEY].astype(jnp.bfloat16)
        s1 = lax.dot_general(sk0_ref[...], qh, nt, preferred_element_type=jnp.float32)
        s2 = lax.dot_general(sk1_ref[...], qh, nt, preferred_element_type=jnp.float32)
        v1, i1 = _topk_rows(s1, row_k, PEER_TOPK)
        v2, i2 = _topk_rows(s2, row_k, PEER_TOPK)
        v1c = jnp.concatenate(v1, axis=0)
        i1c = jnp.concatenate(i1, axis=0)
        v2c = jnp.concatenate(v2, axis=0)
        i2c = jnp.concatenate(i2, axis=0)
        cands, poss, cids = [], [], []
        for a in range(PEER_TOPK // 2):
            bmax = PEER_TOPK // (a + 1) - 1
            for b0 in range(0, bmax + 1, 8):
                c = v1[a] + v2c[b0:b0 + 8]
                if bmax - b0 + 1 < 8:
                    c = jnp.where(sub8 <= float(bmax - b0), c, F32_NEG_INF)
                cands.append(c)
                poss.append(sub8 + float(a * PEER_TOPK + b0))
                cids.append(i1[a] * float(PEER_NKEYS) + i2c[b0:b0 + 8])
        for a0 in range(PEER_TOPK // 2, PEER_TOPK, 8):
            cands.append(v1c[a0:a0 + 8] + v2[0])
            poss.append((sub8 + float(a0)) * float(PEER_TOPK))
            cids.append(i1c[a0:a0 + 8] * float(PEER_NKEYS) + i2[0])
        cand = jnp.concatenate(cands, axis=0)
        row_c = jnp.concatenate(poss, axis=0)
        cid = jnp.concatenate(cids, axis=0)
        tops, tids = [], []
        for _ in range(PEER_TOPK):
            m = jnp.max(cand, axis=0, keepdims=True)
            pos = jnp.min(jnp.where(cand == m, row_c, float(PEER_TOPK * PEER_TOPK)), axis=0, keepdims=True)
            hit = row_c == pos
            tids.append(jnp.sum(jnp.where(hit, cid, 0.0), axis=0, keepdims=True))
            tops.append(m)
            cand = jnp.where(hit, F32_NEG_INF, cand)
        ts = jnp.concatenate(tops, axis=0)
        e = jnp.exp(ts - tops[0])
        gates.append(e / jnp.sum(e, axis=0, keepdims=True))
        ids.append(jnp.concatenate(tids, axis=0))
    table_rows = x.shape[1] // 256
    idx_ref[...] = (jnp.concatenate(ids, axis=0) * float(table_rows)).T.astype(jnp.int32)
    gate_ref[...] = jnp.concatenate(gates, axis=0).T


def peer_route(x, g, w_q, subkeys):
    n, d = x.shape
    tb = min(PEER_TB, n)
    hk = PEER_HEADS * PEER_TOPK
    half = PEER_DKEY // 2
    z = jnp.zeros((PEER_NKEYS, half), jnp.float32)
    sk0 = jnp.concatenate([subkeys[0], z], axis=1).astype(jnp.bfloat16)
    sk1 = jnp.concatenate([z, subkeys[1]], axis=1).astype(jnp.bfloat16)
    return pl.pallas_call(
        _peer_route_kernel,
        grid=(n // tb,),
        in_specs=[pl.BlockSpec((tb, d), lambda i: (i, 0)),
                  pl.BlockSpec((1, d), lambda i: (0, 0)),
                  pl.BlockSpec((d, PEER_HEADS * PEER_DKEY), lambda i: (0, 0)),
                  pl.BlockSpec((PEER_NKEYS, PEER_DKEY), lambda i: (0, 0)),
                  pl.BlockSpec((PEER_NKEYS, PEER_DKEY), lambda i: (0, 0))],
        out_specs=[pl.BlockSpec((tb, d), lambda i: (i, 0)),
                   pl.BlockSpec((tb, hk), lambda i: (i, 0)),
                   pl.BlockSpec((tb, hk), lambda i: (i, 0))],
        out_shape=[jax.ShapeDtypeStruct((n, d), jnp.float32),
                   jax.ShapeDtypeStruct((n, hk), jnp.int32),
                   jax.ShapeDtypeStruct((n, hk), jnp.float32)],
        compiler_params=pltpu.CompilerParams(dimension_semantics=("arbitrary",),
                                             vmem_limit_bytes=PEER_VMEM_LIMIT_BYTES),
        name="peer_route",
    )(x, g.reshape(1, d), w_q.astype(jnp.bfloat16), sk0, sk1)


def pack_table(t):
    e, d = t.shape
    b = lax.bitcast_convert_type(t.astype(jnp.bfloat16), jnp.uint16).astype(jnp.uint32)
    w = b[:, : d // 2] | (b[:, d // 2:] << 16)
    return w.reshape(e * d // 256, 128)


def _table_spec(tab):
    return pl.BlockSpec(tab.shape, lambda i: (0, 0), pipeline_mode=pl.Buffered(1))


def _gather_row(tab, row0, rows):
    wds = tab[pl.ds(pl.multiple_of(row0, rows), rows), :]
    lo = pltpu.bitcast(wds << 16, jnp.float32)
    hi = pltpu.bitcast(wds & jnp.uint32(0xFFFF0000), jnp.float32)
    return lo, hi


def _peer_u_kernel(idx_ref, xn_ref, gate_ref, tab, w_ref, slots, rsum, act):
    tb, hk = gate_ref.shape
    rows = xn_ref.shape[1] // 2

    def token(t, c):
        xt = xn_ref[t]
        xlo, xhi = xt[:rows], xt[rows:]
        for k in range(hk):
            lo, hi = _gather_row(tab, idx_ref[t, k], rows)
            slots[k * rows:(k + 1) * rows, :] = lo * xlo + hi * xhi
        r = slots[pl.ds(0, hk, stride=rows), :]
        for s in range(1, rows):
            r = r + slots[pl.ds(s, hk, stride=rows), :]
        rsum[pl.ds(pl.multiple_of(t * hk, hk), hk), :] = r
        return c

    lax.fori_loop(0, tb, token, 0)

    grp = 8

    def lane_sum(c, carry):
        sums = []
        for i in range(grp):
            rr = rsum[pl.ds(pl.multiple_of((c * grp + i) * hk, hk), hk), :]
            sums.append(jnp.sum(rr.T, axis=0, keepdims=True))
        act[pl.ds(pl.multiple_of(c * grp, grp), grp), :] = jnp.concatenate(sums, axis=0)
        return carry

    lax.fori_loop(0, tb // grp, lane_sum, 0)
    w_ref[...] = gate_ref[...] * jax.nn.gelu(act[...])


def peer_activate(idx, xn, gate, tab):
    n, d = xn.shape
    hk = idx.shape[1]
    tb = min(PEER_TB, n)
    sub = d // 128
    return pl.pallas_call(
        _peer_u_kernel,
        grid=(n // tb,),
        in_specs=[pl.BlockSpec((tb, hk), lambda i: (i, 0), memory_space=pltpu.SMEM),
                  pl.BlockSpec((tb, sub, 128), lambda i: (i, 0, 0)),
                  pl.BlockSpec((tb, hk), lambda i: (i, 0)),
                  _table_spec(tab)],
        out_specs=pl.BlockSpec((tb, hk), lambda i: (i, 0)),
        out_shape=jax.ShapeDtypeStruct((n, hk), jnp.float32),
        scratch_shapes=[pltpu.VMEM((hk * sub // 2, 128), jnp.float32),
                        pltpu.VMEM((tb * hk, 128), jnp.float32),
                        pltpu.VMEM((tb, hk), jnp.float32)],
        compiler_params=pltpu.CompilerParams(dimension_semantics=("arbitrary",),
                                             vmem_limit_bytes=PEER_VMEM_LIMIT_BYTES),
        name="peer_u",
    )(idx, xn.reshape(n, sub, 128), gate, tab)


def _peer_v_kernel(idx_ref, w_ref, x_ref, tab, o_ref):
    tb, hk = idx_ref.shape
    rows = x_ref.shape[1] // 2

    def token(t, c):
        nacc = 2
        lo_acc = [jnp.zeros((rows, 128), jnp.float32) for _ in range(nacc)]
        hi_acc = [jnp.zeros((rows, 128), jnp.float32) for _ in range(nacc)]
        for k in range(hk):
            lo, hi = _gather_row(tab, idx_ref[t, k], rows)
            wk = w_ref[t, k]
            lo_acc[k % nacc] = lo_acc[k % nacc] + wk * lo
            hi_acc[k % nacc] = hi_acc[k % nacc] + wk * hi
        xt = x_ref[t]
        o_ref[t, :rows, :] = xt[:rows] + (lo_acc[0] + lo_acc[1])
        o_ref[t, rows:, :] = xt[rows:] + (hi_acc[0] + hi_acc[1])
        return c

    lax.fori_loop(0, tb, token, 0)


def peer_combine(idx, w, x, tab):
    n, d = x.shape
    hk = idx.shape[1]
    tb = min(PEER_TB, n)
    sub = d // 128
    out = pl.pallas_call(
        _peer_v_kernel,
        grid=(n // tb,),
        in_specs=[pl.BlockSpec((tb, hk), lambda i: (i, 0), memory_space=pltpu.SMEM),
                  pl.BlockSpec((tb, hk), lambda i: (i, 0), memory_space=pltpu.SMEM),
                  pl.BlockSpec((tb, sub, 128), lambda i: (i, 0, 0)),
                  _table_spec(tab)],
        out_specs=pl.BlockSpec((tb, sub, 128), lambda i: (i, 0, 0)),
        out_shape=jax.ShapeDtypeStruct((n, sub, 128), jnp.float32),
        compiler_params=pltpu.CompilerParams(dimension_semantics=("arbitrary",),
                                             vmem_limit_bytes=PEER_VMEM_LIMIT_BYTES),
        name="peer_v",
    )(idx, w, x.reshape(n, sub, 128), tab)
    return out.reshape(n, d)


def peer_ffn(x, g_norm, pp):
    B, T, D = x.shape
    x2 = x.reshape(B * T, D)
    xn, idx, gate = peer_route(x2, g_norm, pp["w_q"], pp["subkeys"])
    w = peer_activate(idx, xn, gate, pp["u_packed"])
    return peer_combine(idx, w, x2, pp["v_packed"]).reshape(B, T, D)


def kernel(x_prompt, x_sample, mem_prompt, cache_nsa_cmp, cache_nsa_slc, cache_nsa_win, cache_mla, cache_swa,
           cache_mem, page_table, norm_mix, norm_cross, norm_ffn, even_w_in, even_w_out, nsa_g_q, nsa_g_k,
           nsa_cmp_pos, nsa_cmp_w1, nsa_cmp_w2, mla_g_qn, mla_g_qpe, mla_g_lat, mla_g_kpe, mla_g_kn, mla_w_uk,
           mla_w_uv, odd_w_in, odd_w_out, swa_g_q, swa_g_k, swa_sinks, mem_g, mem_w_q, mem_w_k, mem_w_v,
           mem_g_q, mem_g_k, mem_w_o, peer_w_q, peer_subkeys, peer_u, peer_v):
    depth = norm_mix.shape[0]
    slopes_nsa = alibi_slopes(NSA_HEADS)
    slopes_swa = alibi_slopes(SWA_HEADS)
    past_len = page_table.shape[1] * PAGE_SIZE
    T, Tn = x_prompt.shape[1], x_sample.shape[1]
    pos_p = jnp.arange(T)
    pos_s = past_len + jnp.arange(Tn)
    xp, xs = x_prompt, x_sample
    st_cmp_p, st_cmp_s, st_slc_p, st_slc_s, st_win_p, st_win_s = [], [], [], [], [], []
    st_mla_p, st_mla_s, st_swa_p, st_swa_s, st_mem_p = [], [], [], [], []

    for li in range(depth):
        if li % 2 == 0:
            e = li // 2
            ep = dict(w_in=even_w_in[e], g_q=nsa_g_q[e], g_k=nsa_g_k[e], cmp_pos=nsa_cmp_pos[e],
                      cmp_w1=nsa_cmp_w1[e], cmp_w2=nsa_cmp_w2[e], g_qn=mla_g_qn[e], g_qpe=mla_g_qpe[e],
                      g_lat=mla_g_lat[e], g_kpe=mla_g_kpe[e], g_kn=mla_g_kn[e], w_uk=mla_w_uk[e], w_uv=mla_w_uv[e])
            q, cmp_kv, slc_kv, win_kv, gates, qn, qp, mla_row = even_project(xp, norm_mix[li], pos_p, ep)
            o_nsa = nsa_prompt(q, cmp_kv, slc_kv, win_kv, gates, ep, slopes_nsa)
            o_mla = mla_prompt(qn, qp, mla_row, ep)
            xp = even_output(xp, o_nsa, o_mla, even_w_out[e])
            st_cmp_p.append(cmp_kv)
            st_slc_p.append(slc_kv)
            st_win_p.append(win_kv[:, -min(NSA_WIN, T):])
            st_mla_p.append(mla_row)
            win_buf = cache_nsa_win[e]
            q, cmp_kv, slc_kv, win_kv, gates, qn, qp, mla_row = even_project(xs, norm_mix[li], pos_s, ep)
            o_nsa = nsa_sample(q, pos_s, cmp_kv, slc_kv, win_kv, gates, cache_nsa_cmp, cache_nsa_slc, win_buf,
                               page_table, e, ep, slopes_nsa)
            o_mla = mla_sample(qn, qp, mla_row, cache_mla, page_table, e, ep)
            xs = even_output(xs, o_nsa, o_mla, even_w_out[e])
            st_cmp_s.append(cmp_kv)
            st_slc_s.append(slc_kv)
            st_win_s.append(jnp.concatenate([win_buf, win_kv], axis=1)[:, -win_buf.shape[1]:])
            st_mla_s.append(mla_row)
        else:
            o = li // 2
            op = dict(w_in=odd_w_in[o], g_q=swa_g_q[o], g_k=swa_g_k[o])
            B = xp.shape[0]
            q, kv = odd_project(xp, norm_mix[li], op)
            att = jnp.transpose(swa_prompt_attention(q, kv[:, :, 0], kv[:, :, 1], swa_sinks[o]), (0, 2, 1, 3))
            xp = matmul_res(att.reshape(B * T, ODD_OUT), odd_w_out[o], xp.reshape(B * T, -1)).reshape(xp.shape)
            st_swa_p.append(kv[:, -min(SWA_WIN, T):])
            buf = cache_swa[o]
            q, kv = odd_project(xs, norm_mix[li], op)
            att = window_sample(q, buf[:, :, 0], buf[:, :, 1], kv[:, :, 0], kv[:, :, 1], past_len, SWA_WIN,
                                slopes_swa, swa_sinks[o])
            xs = matmul_res(att.reshape(xs.shape[0] * Tn, ODD_OUT), odd_w_out[o],
                            xs.reshape(xs.shape[0] * Tn, -1)).reshape(xs.shape)
            st_swa_s.append(jnp.concatenate([buf, kv], axis=1)[:, -buf.shape[1]:])

        cp = dict(g_mem=mem_g[li], w_q=mem_w_q[li], w_k=mem_w_k[li], w_v=mem_w_v[li], g_q=mem_g_q[li],
                  g_k=mem_g_k[li], w_o=mem_w_o[li])
        mem_kv_p = memory_kv(mem_prompt, cp)
        xp = memory_cross(xp, norm_cross[li], mem_kv_p, cp)
        xs = memory_cross(xs, norm_cross[li], cache_mem[li], cp)
        st_mem_p.append(mem_kv_p)

        pp = dict(w_q=peer_w_q[li], subkeys=peer_subkeys[li],
                  u_packed=pack_table(peer_u[li]), v_packed=pack_table(peer_v[li]))
        xp = peer_ffn(xp, norm_ffn[li], pp)
        xs = peer_ffn(xs, norm_ffn[li], pp)

    return (xp, xs, jnp.stack(st_cmp_p), jnp.stack(st_cmp_s), jnp.stack(st_slc_p), jnp.stack(st_slc_s),
            jnp.stack(st_win_p), jnp.stack(st_win_s), jnp.stack(st_mla_p), jnp.stack(st_mla_s),
            jnp.stack(st_swa_p), jnp.stack(st_swa_s), jnp.stack(st_mem_p))
```

```python
import functools
import math

import jax
import jax.numpy as jnp
import numpy as np
from jax import lax
from jax.experimental import pallas as pl
from jax.experimental.pallas import tpu as pltpu

PAGE_SIZE = 128
EPS = 1e-6
NEG_INF = -1e30
ROPE_BASE = 10000.0

NSA_HEADS = 8
NSA_DK = 64
NSA_BLK = 64
NSA_TOPN = 16
NSA_WIN = 512
NSA_CMP_HID = 256
MLA_HEADS = 8
MLA_NOPE = 64
MLA_ROPE = 32
MLA_V = 64
MLA_LORA = 128
MLA_ROW = MLA_LORA + MLA_ROPE
SWA_HEADS = 16
SWA_KV_HEADS = 4
SWA_HD = 64
SWA_WIN = 128
MEM_HEADS = 4
MEM_HD = 128
MEM_W = MEM_HEADS * MEM_HD
PEER_HEADS = 8
PEER_NKEYS = 128
PEER_DKEY = 128
PEER_TOPK = 16

NSA_Q_COLS = NSA_HEADS * NSA_DK
NSA_KV_COLS = 3 * 2 * NSA_DK
NSA_GATE_COLS = NSA_HEADS * 3
MLA_Q_COLS = MLA_HEADS * (MLA_NOPE + MLA_ROPE)
EVEN_IN = NSA_Q_COLS + NSA_KV_COLS + NSA_GATE_COLS + MLA_Q_COLS + MLA_ROW
EVEN_OUT = NSA_HEADS * NSA_DK + MLA_HEADS * MLA_V
ODD_IN = SWA_HEADS * SWA_HD + 2 * SWA_KV_HEADS * SWA_HD
ODD_OUT = SWA_HEADS * SWA_HD

VMEM_LIMIT_BYTES = 48 * 1024 * 1024


def _row_tile(n, target=512):
    t = min(n, target)
    while n % t:
        t //= 2
    return t


def _norm_matmul_kernel(x_ref, g_ref, w_ref, o_ref):
    x = x_ref[...]
    y = x * lax.rsqrt(jnp.mean(x * x, axis=-1, keepdims=True) + EPS) * g_ref[...]
    o_ref[...] = jnp.dot(y.astype(jnp.bfloat16), w_ref[...], preferred_element_type=jnp.float32)


def norm_matmul(x, g, w):
    n, d = x.shape
    c = w.shape[1]
    tm = _row_tile(n)
    return pl.pallas_call(
        _norm_matmul_kernel,
        grid=(n // tm,),
        in_specs=[pl.BlockSpec((tm, d), lambda i: (i, 0)),
                  pl.BlockSpec((1, d), lambda i: (0, 0)),
                  pl.BlockSpec((d, c), lambda i: (0, 0))],
        out_specs=pl.BlockSpec((tm, c), lambda i: (i, 0)),
        out_shape=jax.ShapeDtypeStruct((n, c), jnp.float32),
        compiler_params=pltpu.CompilerParams(dimension_semantics=("arbitrary",),
                                             vmem_limit_bytes=VMEM_LIMIT_BYTES),
        name="norm_matmul",
    )(x, g.reshape(1, d), w.astype(jnp.bfloat16))


def _matmul_res_kernel(a_ref, w_ref, r_ref, o_ref):
    o_ref[...] = r_ref[...] + jnp.dot(a_ref[...].astype(jnp.bfloat16), w_ref[...],
                                      preferred_element_type=jnp.float32)


def matmul_res(a, w, res):
    n, k = a.shape
    c = w.shape[1]
    tm = _row_tile(n)
    return pl.pallas_call(
        _matmul_res_kernel,
        grid=(n // tm,),
        in_specs=[pl.BlockSpec((tm, k), lambda i: (i, 0)),
                  pl.BlockSpec((k, c), lambda i: (0, 0)),
                  pl.BlockSpec((tm, c), lambda i: (i, 0))],
        out_specs=pl.BlockSpec((tm, c), lambda i: (i, 0)),
        out_shape=jax.ShapeDtypeStruct((n, c), jnp.float32),
        compiler_params=pltpu.CompilerParams(dimension_semantics=("arbitrary",),
                                             vmem_limit_bytes=VMEM_LIMIT_BYTES),
        name="matmul_res",
    )(a, w.astype(jnp.bfloat16), res)


def rmsnorm(x, g):
    xf = x.astype(jnp.float32)
    y = xf * lax.rsqrt(jnp.mean(xf * xf, axis=-1, keepdims=True) + EPS)
    return (y * g.astype(jnp.float32)).astype(x.dtype)


def alibi_slopes(n_heads):
    return jnp.asarray((2.0 ** (-8.0 * np.arange(1, n_heads + 1) / n_heads)).astype(np.float32))


def rope(x, pos):
    d = x.shape[-1]
    inv = jnp.asarray(np.power(ROPE_BASE, -np.arange(0, d, 2, dtype=np.float32) / d).astype(np.float32))
    ang = pos.astype(jnp.float32)[:, None] * inv[None, :]
    cos = jnp.cos(ang)[None, :, None, :]
    sin = jnp.sin(ang)[None, :, None, :]
    xf = x.astype(jnp.float32)
    x1, x2 = xf[..., : d // 2], xf[..., d // 2:]
    return jnp.concatenate([x1 * cos - x2 * sin, x1 * sin + x2 * cos], axis=-1).astype(x.dtype)


def masked_softmax(s, valid, sink=None):
    s = jnp.where(valid, s, NEG_INF)
    m = jnp.max(s, axis=-1, keepdims=True)
    if sink is not None:
        sk = sink.astype(jnp.float32)[:, None, None]
        m = jnp.maximum(m, sk)
    e = jnp.where(valid, jnp.exp(s - m), 0.0)
    den = jnp.sum(e, axis=-1, keepdims=True)
    if sink is not None:
        den = den + jnp.exp(sk - m)
    return e / jnp.maximum(den, 1e-30)


def attend(q, k, v, dist, valid, slopes, sink=None):
    B, Q, H, d = q.shape
    G = k.shape[-2]
    qg = q.reshape(B, Q, G, H // G, d)
    if k.ndim == 5:
        s = jnp.einsum("bqgrd,bqkgd->bgrqk", qg, k)
    else:
        s = jnp.einsum("bqgrd,bkgd->bgrqk", qg, k)
    K = s.shape[-1]
    s = s.reshape(B, H, Q, K).astype(jnp.float32) * (d ** -0.5)
    if slopes is not None:
        s = s - slopes[:, None, None] * dist[..., None, :, :].astype(jnp.float32)
    p = masked_softmax(s, valid[..., None, :, :], sink)
    pg = p.reshape(B, G, H // G, Q, K).astype(v.dtype)
    if v.ndim == 5:
        o = jnp.einsum("bgrqk,bqkgd->bqgrd", pg, v)
    else:
        o = jnp.einsum("bgrqk,bkgd->bqgrd", pg, v)
    return o.reshape(B, Q, H, v.shape[-1]), p


def alibi_slopes_np(n_heads):
    return (2.0 ** (-8.0 * np.arange(1, n_heads + 1) / n_heads)).astype(np.float32)


Q_TILE = 128
SLC_CHUNK = 512
MASK_BIG = -(2.0 ** 100)


def _nsa_prompt_kernel(q_ref, kt_ref, vs_ref, kwt_ref, vw_ref, kct_ref, vc_ref, g_ref, slope_ref, wbias_ref, cbias_ref,
                       o_ref, *, n_blocks):
    H, TQ = NSA_HEADS, Q_TILE
    i = pl.program_id(1)
    s0 = i * TQ
    q_lo = q_ref[0].reshape(H * TQ, 128)
    slope = slope_ref[...]

    s = jnp.dot(q_lo, kct_ref[0], preferred_element_type=jnp.float32).reshape(H, TQ, n_blocks)
    tpos = s0 + lax.broadcasted_iota(jnp.int32, (TQ, n_blocks), 0)
    blk = lax.broadcasted_iota(jnp.int32, (TQ, n_blocks), 1)
    dist = tpos - ((blk + 1) * NSA_BLK - 1)
    valid = (dist >= 0)[None]
    s = jnp.where(valid, s - slope * dist.astype(jnp.float32)[None], NEG_INF)
    m = jnp.max(s, axis=-1, keepdims=True)
    e = jnp.where(valid, jnp.exp(s - m), 0.0)
    p = e / jnp.maximum(jnp.sum(e, axis=-1, keepdims=True), 1e-30)
    o_cmp = jnp.dot(p.reshape(H * TQ, n_blocks).astype(jnp.bfloat16), vc_ref[0],
                    preferred_element_type=jnp.float32)
    imp = jnp.sum(p, axis=0)

    cur = tpos // NSA_BLK
    imp = jnp.where(blk == cur, float(H + 1), jnp.where(blk < cur, imp, -1.0))
    blk_f = blk.astype(jnp.float32)
    sel = jnp.zeros((TQ, n_blocks), jnp.bool_)
    for _ in range(min(NSA_TOPN, n_blocks)):
        mx = jnp.max(imp, axis=-1, keepdims=True)
        pick = jnp.min(jnp.where(imp == mx, blk_f, float(n_blocks)), axis=-1, keepdims=True)
        hit = blk_f == pick
        sel = sel | hit
        imp = jnp.where(hit, -2.0, imp)
    nsb = jnp.where(sel, 0.0, MASK_BIG).astype(jnp.bfloat16)
    q_aug = jnp.concatenate([q_lo, jnp.broadcast_to(nsb[None], (H, TQ, n_blocks)).reshape(H * TQ, n_blocks)], axis=1)

    KC = SLC_CHUNK

    def slc_step(c, carry, bias):
        m_i, acc = carry
        k0 = pl.multiple_of(c * KC, KC)
        sc = jnp.dot(q_aug, kt_ref[0, :, pl.ds(k0, KC)], preferred_element_type=jnp.float32)
        if bias is not None:
            sc = (sc.reshape(H, TQ, KC) + bias[None]).reshape(H * TQ, KC)
        m_new = jnp.maximum(m_i, jnp.max(sc, axis=-1, keepdims=True))
        corr = jnp.exp(m_i - m_new)
        pe = jnp.exp(sc - m_new)
        acc_new = acc * corr + jnp.dot(pe.astype(jnp.bfloat16), vs_ref[0, pl.ds(k0, KC), :],
                                       preferred_element_type=jnp.float32)
        return m_new, acc_new

    init = (jnp.full((H * TQ, 1), NEG_INF, jnp.float32), jnp.zeros((H * TQ, 128), jnp.float32))
    c_diag = s0 // KC
    carry = lax.fori_loop(0, c_diag, lambda c, cr: slc_step(c, cr, None), init)
    m_i, acc = slc_step(c_diag, carry, cbias_ref[(s0 % KC) // TQ])
    o_slc = acc[:, :NSA_DK] / acc[:, NSA_DK:NSA_DK + 1]

    nwc = NSA_WIN // TQ + 1
    parts = []
    for r in range(nwc):
        cidx = i - (nwc - 1) + r
        k0 = pl.multiple_of(jnp.maximum(cidx, 0) * TQ, TQ)
        sw = jnp.dot(q_lo[:, :NSA_DK], kwt_ref[0, :, pl.ds(k0, TQ)], preferred_element_type=jnp.float32)
        b = jnp.where(cidx >= 0, wbias_ref[r], NEG_INF)
        parts.append((sw.reshape(H, TQ, TQ) + b).reshape(H * TQ, TQ))
    mw = parts[0].max(axis=-1, keepdims=True)
    for r in range(1, nwc):
        mw = jnp.maximum(mw, parts[r].max(axis=-1, keepdims=True))
    accw = jnp.zeros((H * TQ, 128), jnp.float32)
    for r in range(nwc):
        cidx = i - (nwc - 1) + r
        k0 = pl.multiple_of(jnp.maximum(cidx, 0) * TQ, TQ)
        pe = jnp.exp(parts[r] - mw)
        accw = accw + jnp.dot(pe.astype(jnp.bfloat16), vw_ref[0, pl.ds(k0, TQ), :], preferred_element_type=jnp.float32)
    o_win = accw[:, :NSA_DK] / accw[:, NSA_DK:NSA_DK + 1]

    g = g_ref[0].reshape(H * TQ, 3)
    o = g[:, 0:1] * o_cmp + g[:, 1:2] * o_slc + g[:, 2:3] * o_win
    o_ref[0] = o.reshape(H, TQ, NSA_DK)


def _with_ones_column(v):
    ones = jnp.ones(v.shape[:-1] + (1,), v.dtype)
    zeros = jnp.zeros(v.shape[:-1] + (127 - v.shape[-1],), v.dtype)
    return jnp.concatenate([v, ones, zeros], axis=-1).astype(jnp.bfloat16)


def nsa_prompt_attention(q, slc_k, slc_v, win_k, win_v, kc, vc, gates):
    B, T, H, dk = q.shape
    nb = T // NSA_BLK
    nbp = 128
    TQ = Q_TILE
    assert nb <= nbp and T % SLC_CHUNK == 0 and NSA_WIN % TQ == 0
    slopes = alibi_slopes_np(H)
    bf = jnp.bfloat16
    qh = jnp.transpose(q, (0, 2, 1, 3)) * (dk ** -0.5)
    al = np.zeros((H, 128 - dk), np.float32)
    al[:, 0] = slopes * 128.0
    al[:, 1] = slopes
    q_lo = jnp.concatenate([qh, jnp.broadcast_to(jnp.asarray(al)[None, :, None, :], (B, H, T, 128 - dk))],
                           axis=-1).astype(bf)
    pos = np.arange(T)
    crow = np.zeros((128 - dk + nbp, T), np.float32)
    crow[0] = pos // 128
    crow[1] = pos % 128
    crow[128 - dk + pos // NSA_BLK, pos] = 1.0
    kt = jnp.concatenate([jnp.transpose(slc_k, (0, 2, 1)),
                          jnp.broadcast_to(jnp.asarray(crow)[None], (B,) + crow.shape)], axis=1).astype(bf)
    kwt = jnp.transpose(win_k, (0, 2, 1)).astype(bf)
    kct = jnp.pad(jnp.transpose(kc, (0, 2, 1)), ((0, 0), (0, 128 - dk), (0, nbp - nb))).astype(bf)
    vcp = jnp.pad(vc, ((0, 0), (0, nbp - nb), (0, 0))).astype(bf)
    gh = jnp.transpose(gates, (0, 2, 1, 3))
    nwc = NSA_WIN // TQ + 1
    ii = np.arange(TQ)[:, None]
    jj = np.arange(TQ)[None, :]
    wb = np.zeros((nwc, H, TQ, TQ), np.float32)
    for r in range(nwc):
        d = ii - jj + TQ * (nwc - 1 - r)
        ok = (d >= 0) & (d <= NSA_WIN)
        wb[r] = np.where(ok[None], -slopes[:, None, None] * d[None].astype(np.float32), NEG_INF)
    nv = SLC_CHUNK // TQ
    cb = np.zeros((nv, TQ, SLC_CHUNK), np.float32)
    for v in range(nv):
        cb[v] = np.where(np.arange(SLC_CHUNK)[None, :] <= v * TQ + ii, 0.0, NEG_INF)
    kern = functools.partial(_nsa_prompt_kernel, n_blocks=nbp)
    return pl.pallas_call(
        kern,
        grid=(B, T // TQ),
        in_specs=[pl.BlockSpec((1, H, TQ, 128), lambda b, i: (b, 0, i, 0)),
                  pl.BlockSpec((1, 256, T), lambda b, i: (b, 0, 0)),
                  pl.BlockSpec((1, T, 128), lambda b, i: (b, 0, 0)),
                  pl.BlockSpec((1, dk, T), lambda b, i: (b, 0, 0)),
                  pl.BlockSpec((1, T, 128), lambda b, i: (b, 0, 0)),
                  pl.BlockSpec((1, 128, nbp), lambda b, i: (b, 0, 0)),
                  pl.BlockSpec((1, nbp, dk), lambda b, i: (b, 0, 0)),
                  pl.BlockSpec((1, H, TQ, 3), lambda b, i: (b, 0, i, 0)),
                  pl.BlockSpec((H, 1, 1), lambda b, i: (0, 0, 0)),
                  pl.BlockSpec((nwc, H, TQ, TQ), lambda b, i: (0, 0, 0, 0)),
                  pl.BlockSpec((nv, TQ, SLC_CHUNK), lambda b, i: (0, 0, 0))],
        out_specs=pl.BlockSpec((1, H, TQ, dk), lambda b, i: (b, 0, i, 0)),
        out_shape=jax.ShapeDtypeStruct((B, H, T, dk), jnp.float32),
        compiler_params=pltpu.CompilerParams(dimension_semantics=("arbitrary", "arbitrary"),
                                             vmem_limit_bytes=VMEM_LIMIT_BYTES),
        name="nsa_prompt",
    )(q_lo, kt, _with_ones_column(slc_v), kwt, _with_ones_column(win_v), kct, vcp, gh,
      jnp.asarray(slopes).reshape(H, 1, 1), jnp.asarray(wb), jnp.asarray(cb))


MLA_TILE = 512


def _mla_prompt_kernel(q_ref, kt_ref, v_ref, tril_ref, o_ref, *, scale_log2e):
    TQ = MLA_TILE
    i = pl.program_id(2)
    q = q_ref[0, 0]

    def step(c, carry, bias):
        m_i, acc = carry
        k0 = pl.multiple_of(c * TQ, TQ)
        s = jnp.dot(q, kt_ref[0, 0, :, pl.ds(k0, TQ)], preferred_element_type=jnp.float32)
        if bias is not None:
            s = s + bias
        m_new = jnp.maximum(m_i, jnp.max(s, axis=-1, keepdims=True))
        corr = jnp.exp2((m_i - m_new) * scale_log2e)
        p = jnp.exp2((s - m_new) * scale_log2e)
        acc_new = acc * corr + jnp.dot(p.astype(jnp.bfloat16), v_ref[0, 0, pl.ds(k0, TQ), :],
                                       preferred_element_type=jnp.float32)
        return m_new, acc_new

    init = (jnp.full((TQ, 1), NEG_INF, jnp.float32), jnp.zeros((TQ, 128), jnp.float32))
    carry = lax.fori_loop(0, i, lambda c, cr: step(c, cr, None), init)
    m_i, acc = step(i, carry, tril_ref[...])
    o_ref[0, 0] = acc[:, :MLA_V] / acc[:, MLA_V:MLA_V + 1]


def mla_prompt_attention(qn, qp, kn, kp, v):
    B, T, H, _ = qn.shape
    TQ = MLA_TILE
    assert T % TQ == 0
    bf = jnp.bfloat16
    dq = MLA_NOPE + MLA_ROPE
    qcat = jnp.concatenate([qn, qp, jnp.zeros((B, T, H, 128 - dq), qn.dtype)], axis=-1)
    qcat = jnp.transpose(qcat, (0, 2, 1, 3)).astype(bf)
    kcat = jnp.concatenate([kn, jnp.broadcast_to(kp[:, :, None, :], (B, T, H, MLA_ROPE)),
                            jnp.zeros((B, T, H, 128 - dq), kn.dtype)], axis=-1)
    kt = jnp.transpose(kcat, (0, 2, 3, 1)).astype(bf)
    vh = _with_ones_column(jnp.transpose(v, (0, 2, 1, 3)))
    tril = np.where(np.arange(TQ)[None, :] <= np.arange(TQ)[:, None], 0.0, NEG_INF).astype(np.float32)
    kern = functools.partial(_mla_prompt_kernel, scale_log2e=float(dq ** -0.5 * math.log2(math.e)))
    return pl.pallas_call(
        kern,
        grid=(B, H, T // TQ),
        in_specs=[pl.BlockSpec((1, 1, TQ, 128), lambda b, h, i: (b, h, i, 0)),
                  pl.BlockSpec((1, 1, 128, T), lambda b, h, i: (b, h, 0, 0)),
                  pl.BlockSpec((1, 1, T, 128), lambda b, h, i: (b, h, 0, 0)),
                  pl.BlockSpec((TQ, TQ), lambda b, h, i: (0, 0))],
        out_specs=pl.BlockSpec((1, 1, TQ, MLA_V), lambda b, h, i: (b, h, i, 0)),
        out_shape=jax.ShapeDtypeStruct((B, H, T, MLA_V), jnp.float32),
        compiler_params=pltpu.CompilerParams(dimension_semantics=("arbitrary", "arbitrary", "arbitrary"),
                                             vmem_limit_bytes=VMEM_LIMIT_BYTES),
        name="mla_prompt",
    )(qcat, kt, vh, jnp.asarray(tril))


def _swa_prompt_kernel(q_ref, kt_ref, v_ref, bias_ref, sink_ref, o_ref):
    TQ = Q_TILE
    R = SWA_HEADS // SWA_KV_HEADS
    i = pl.program_id(2)
    q = q_ref[0, 0].reshape(R * TQ, SWA_HD)
    kprev = pl.multiple_of(jnp.maximum(i - 1, 0) * TQ, TQ)
    kcur = pl.multiple_of(i * TQ, TQ)
    s0 = jnp.dot(q, kt_ref[0, 0, :, pl.ds(kprev, TQ)], preferred_element_type=jnp.float32).reshape(R, TQ, TQ)
    s1 = jnp.dot(q, kt_ref[0, 0, :, pl.ds(kcur, TQ)], preferred_element_type=jnp.float32).reshape(R, TQ, TQ)
    s0 = s0 + jnp.where(i > 0, bias_ref[0, 0], NEG_INF)
    s1 = s1 + bias_ref[0, 1]
    sink = sink_ref[0]
    m = jnp.maximum(jnp.maximum(s0.max(axis=-1, keepdims=True), s1.max(axis=-1, keepdims=True)), sink)
    p0 = jnp.exp(s0 - m)
    p1 = jnp.exp(s1 - m)
    den = p0.sum(axis=-1, keepdims=True) + p1.sum(axis=-1, keepdims=True) + jnp.exp(sink - m)
    o = (jnp.dot(p0.reshape(R * TQ, TQ).astype(jnp.bfloat16), v_ref[0, 0, pl.ds(kprev, TQ), :],
                 preferred_element_type=jnp.float32)
         + jnp.dot(p1.reshape(R * TQ, TQ).astype(jnp.bfloat16), v_ref[0, 0, pl.ds(kcur, TQ), :],
                   preferred_element_type=jnp.float32))
    o_ref[0, 0] = o.reshape(R, TQ, SWA_HD) / jnp.maximum(den, 1e-30)


def swa_prompt_attention(q, k, v, sinks):
    B, T, H, hd = q.shape
    G = k.shape[2]
    R = H // G
    TQ = Q_TILE
    assert SWA_WIN == TQ and T % TQ == 0
    bf = jnp.bfloat16
    slopes = alibi_slopes_np(H)
    qh = (jnp.transpose(q, (0, 2, 1, 3)) * (hd ** -0.5)).astype(bf).reshape(B, G, R, T, hd)
    kt = jnp.transpose(k, (0, 2, 3, 1)).astype(bf)
    vh = jnp.transpose(v, (0, 2, 1, 3)).astype(bf)
    ii = np.arange(TQ)[:, None]
    jj = np.arange(TQ)[None, :]
    bias = np.zeros((G, 2, R, TQ, TQ), np.float32)
    for r in range(2):
        d = ii - jj + TQ * (1 - r)
        ok = (d >= 0) & (d <= SWA_WIN)
        bias[:, r] = np.where(ok[None, None],
                              -slopes.reshape(G, R)[:, :, None, None] * d[None, None].astype(np.float32), NEG_INF)
    return pl.pallas_call(
        _swa_prompt_kernel,
        grid=(B, G, T // TQ),
        in_specs=[pl.BlockSpec((1, 1, R, TQ, hd), lambda b, g, i: (b, g, 0, i, 0)),
                  pl.BlockSpec((1, 1, hd, T), lambda b, g, i: (b, g, 0, 0)),
                  pl.BlockSpec((1, 1, T, hd), lambda b, g, i: (b, g, 0, 0)),
                  pl.BlockSpec((1, 2, R, TQ, TQ), lambda b, g, i: (g, 0, 0, 0, 0)),
                  pl.BlockSpec((1, R, 1, 1), lambda b, g, i: (g, 0, 0, 0))],
        out_specs=pl.BlockSpec((1, 1, R, TQ, hd), lambda b, g, i: (b, g, 0, i, 0)),
        out_shape=jax.ShapeDtypeStruct((B, G, R, T, hd), jnp.float32),
        compiler_params=pltpu.CompilerParams(dimension_semantics=("arbitrary", "arbitrary", "arbitrary"),
                                             vmem_limit_bytes=VMEM_LIMIT_BYTES),
        name="swa_prompt",
    )(qh, kt, vh, jnp.asarray(bias), sinks.reshape(G, R, 1, 1)).reshape(B, H, T, hd)


def window_sample(q, k_buf, v_buf, k_new, v_new, past_len, window, slopes, sink=None):
    Wb, Tn = k_buf.shape[1], q.shape[1]
    k = jnp.concatenate([k_buf, k_new], axis=1)
    v = jnp.concatenate([v_buf, v_new], axis=1)
    qpos = past_len + jnp.arange(Tn)
    kpos = past_len - Wb + jnp.arange(Wb + Tn)
    dist = qpos[:, None] - kpos[None, :]
    valid = (dist >= 0) & (dist <= window)
    return attend(q, k, v, dist, valid, slopes, sink)[0]


def nsa_compress(blocks, pos_emb, w1, w2):
    x = blocks + pos_emb
    h = jax.nn.gelu(jnp.einsum("...lcd,lcdh->...ch", x, w1))
    return jnp.einsum("...ch,chd->...cd", h, w2)


def nsa_select(p_cmp, qpos, n_blocks):
    imp = jnp.sum(p_cmp, axis=1)
    blk = jnp.arange(n_blocks)[None, :]
    cur = (qpos // NSA_BLK)[:, None]
    imp = jnp.where(blk == cur, float(NSA_HEADS + 1), jnp.where(blk < cur, imp, -1.0))
    picks = []
    for _ in range(min(NSA_TOPN, n_blocks)):
        i = jnp.argmax(imp, axis=-1)
        picks.append(i)
        imp = jnp.where(blk == i[..., None], -jnp.inf, imp)
    return jnp.stack(picks, axis=-1)


def nsa_compressed_branch(q, qpos, kc, vc, c_end, slopes):
    dist = qpos[:, None] - c_end[None, :]
    o, p = attend(q, kc[:, :, None, :], vc[:, :, None, :], dist, dist >= 0, slopes)
    return o, nsa_select(p, qpos, kc.shape[1])


def nsa_gate(gates, o_cmp, o_slc, o_win):
    g = gates.astype(o_cmp.dtype)
    return g[..., 0:1] * o_cmp + g[..., 1:2] * o_slc + g[..., 2:3] * o_win


CMP_PAGES_PER_STEP = 64
PAGED_VMEM_LIMIT_BYTES = 56 * 1024 * 1024


def pages_feature_major(cache):
    return jnp.transpose(cache, (0, 1, 3, 4, 2)).reshape(-1, 2, NSA_DK, PAGE_SIZE)


def _nsa_compress_kernel(pt_ref, pos_ref, w1b_ref, w2_ref, *rest, n_pages):
    page_refs = rest[:n_pages]
    o0_ref, o1_ref, slab = rest[n_pages:]
    feat = 2 * NSA_DK
    for j, r in enumerate(page_refs):
        slab[j * feat:(j + 1) * feat, :] = r[0].reshape(feat, PAGE_SIZE)
    hs = []
    for c in range(2):
        xs = [(slab[pl.ds(c * NSA_DK + d, n_pages, stride=feat), :] + pos_ref[c * NSA_DK + d:c * NSA_DK + d + 1, :]
               ).astype(jnp.bfloat16) for d in range(NSA_DK)]
        h = jnp.dot(jnp.concatenate(xs, axis=1), w1b_ref[c], preferred_element_type=jnp.float32)
        hs.append(jax.nn.gelu(h).astype(jnp.bfloat16))
    for j, o_ref in enumerate((o0_ref, o1_ref)):
        hj = jnp.concatenate([hs[c][:, j * NSA_CMP_HID:(j + 1) * NSA_CMP_HID] for c in range(2)], axis=1)
        o_ref[0] = jnp.dot(hj, w2_ref[...], preferred_element_type=jnp.float32)


def nsa_compress_paged(pages_t, page_table, pos_emb, w1, w2):
    S, P = page_table.shape
    PG = math.gcd(CMP_PAGES_PER_STEP, P)
    assert PAGE_SIZE == 2 * NSA_BLK and PG % 8 == 0
    bf = jnp.bfloat16
    feat = 2 * NSA_DK
    pos_t = jnp.tile(jnp.transpose(pos_emb, (1, 2, 0)).reshape(feat, NSA_BLK), (1, 2))
    w1f = jnp.transpose(w1, (1, 2, 0, 3)).reshape(feat, NSA_BLK, NSA_CMP_HID)
    z = jnp.zeros_like(w1f)
    w1b = jnp.concatenate([jnp.concatenate([w1f, z], axis=2), jnp.concatenate([z, w1f], axis=2)], axis=1).astype(bf)
    w1b = w1b.reshape(2, NSA_DK * PAGE_SIZE, 2 * NSA_CMP_HID)
    eye = jnp.eye(2, dtype=w2.dtype)
    w2z = jnp.einsum("chd,ce->ched", w2, eye).reshape(2 * NSA_CMP_HID, feat).astype(bf)

    def page_spec(j):
        return pl.BlockSpec((1, 2, NSA_DK, PAGE_SIZE), lambda s, g, pt: (pt[s, g * PG + j], 0, 0, 0))

    grid_spec = pltpu.PrefetchScalarGridSpec(
        num_scalar_prefetch=1,
        grid=(S, P // PG),
        in_specs=[pl.BlockSpec(pos_t.shape, lambda s, g, pt: (0, 0)),
                  pl.BlockSpec(w1b.shape, lambda s, g, pt: (0, 0, 0), pipeline_mode=pl.Buffered(1)),
                  pl.BlockSpec(w2z.shape, lambda s, g, pt: (0, 0))] + [page_spec(j) for j in range(PG)],
        out_specs=[pl.BlockSpec((1, PG, feat), lambda s, g, pt: (s, g, 0)),
                   pl.BlockSpec((1, PG, feat), lambda s, g, pt: (s, g, 0))],
        scratch_shapes=[pltpu.VMEM((PG * feat, PAGE_SIZE), jnp.float32)])
    o0, o1 = pl.pallas_call(
        functools.partial(_nsa_compress_kernel, n_pages=PG),
        grid_spec=grid_spec,
        out_shape=[jax.ShapeDtypeStruct((S, P, feat), jnp.float32)] * 2,
        compiler_params=pltpu.CompilerParams(dimension_semantics=("arbitrary", "arbitrary"),
                                             vmem_limit_bytes=PAGED_VMEM_LIMIT_BYTES),
        name="nsa_compress",
    )(page_table, pos_t, w1b, w2z, *([pages_t] * PG))
    return jnp.stack([o0, o1], axis=2).reshape(S, 2 * P, 2, NSA_DK)


def nsa_prompt(q, cmp_kv, slc_kv, win_kv, gates, ep, slopes):
    B, T = q.shape[:2]
    nb = T // NSA_BLK
    ppb = T // PAGE_SIZE
    prompt_pages = jnp.arange(B * ppb, dtype=jnp.int32).reshape(B, ppb)
    comp = nsa_compress_paged(pages_feature_major(cmp_kv.reshape(1, B * ppb, PAGE_SIZE, 2, NSA_DK)), prompt_pages,
                              ep["cmp_pos"], ep["cmp_w1"], ep["cmp_w2"])
    kc = rmsnorm(comp[:, :, 0], ep["g_k"][0])
    vc = comp[:, :, 1]
    o = nsa_prompt_attention(q, slc_kv[:, :, 0], slc_kv[:, :, 1], win_kv[:, :, 0], win_kv[:, :, 1], kc, vc, gates)
    return jnp.transpose(o, (0, 2, 1, 3))


def _nsa_sample_selected_kernel(phys_ref, idx_ref, q_ref, new_ref, slope_ref, *rest, n_sel, nb_past, qpos):
    page_refs = rest[:n_sel]
    o_ref = rest[n_sel]
    b = pl.program_id(0)
    nt = (((1,), (1,)), ((), ()))
    q = q_ref[0]
    lane = lax.broadcasted_iota(jnp.int32, (1, PAGE_SIZE), 1)
    ss, vts, dists = [], [], []
    for n, r in enumerate(page_refs):
        blk_id = idx_ref[b, n]
        is_new = blk_id >= nb_past
        half = jnp.where(is_new, 0, blk_id % 2)
        kt = jnp.where(is_new, new_ref[0, 0], r[0, 0]).astype(jnp.bfloat16)
        vts.append(jnp.where(is_new, new_ref[0, 1], r[0, 1]).astype(jnp.bfloat16))
        ss.append(jnp.dot(q, kt, preferred_element_type=jnp.float32))
        in_blk = (lane // NSA_BLK) == half
        dists.append(jnp.where(in_blk, qpos - blk_id * NSA_BLK - (lane % NSA_BLK), -1))
    s = jnp.concatenate(ss, axis=1)
    dist = jnp.concatenate(dists, axis=1)
    s = jnp.where(dist >= 0, s - slope_ref[...] * dist.astype(jnp.float32), NEG_INF)
    m = jnp.max(s, axis=-1, keepdims=True)
    e = jnp.where(dist >= 0, jnp.exp(s - m), 0.0)
    p = (e / jnp.maximum(jnp.sum(e, axis=-1, keepdims=True), 1e-30)).astype(jnp.bfloat16)
    o = jnp.zeros((q.shape[0], NSA_DK), jnp.float32)
    for n in range(n_sel):
        o = o + lax.dot_general(p[:, n * PAGE_SIZE:(n + 1) * PAGE_SIZE], vts[n], nt, preferred_element_type=jnp.float32)
    o_ref[0] = o


def nsa_sample_selected(q, idx, slc_new, cache_slc, page_table, e, qpos):
    DB, Tn, H, dk = q.shape
    n_sel = idx.shape[-1]
    assert Tn == 1
    P = page_table.shape[1]
    bpp = PAGE_SIZE // NSA_BLK
    nb_past = P * bpp
    idx2 = idx[:, 0].astype(jnp.int32)
    jp = jnp.minimum(idx2, nb_past - 1)
    phys = jnp.take_along_axis(page_table, jp // bpp, axis=1) + e * cache_slc.shape[1]
    qs = (q[:, 0] * (dk ** -0.5)).astype(jnp.bfloat16)
    new_t = jnp.zeros((DB, 2, dk, PAGE_SIZE), jnp.float32).at[:, :, :, 0].set(slc_new[:, 0])
    kern = functools.partial(_nsa_sample_selected_kernel, n_sel=n_sel, nb_past=nb_past, qpos=int(qpos))

    def page_spec(n):
        return pl.BlockSpec((1, 2, dk, PAGE_SIZE), lambda b, ph, ix: (ph[b, n], 0, 0, 0))

    grid_spec = pltpu.PrefetchScalarGridSpec(
        num_scalar_prefetch=2,
        grid=(DB,),
        in_specs=[pl.BlockSpec((1, H, dk), lambda b, ph, ix: (b, 0, 0)),
                  pl.BlockSpec((1, 2, dk, PAGE_SIZE), lambda b, ph, ix: (b, 0, 0, 0)),
                  pl.BlockSpec((H, 1), lambda b, ph, ix: (0, 0))] + [page_spec(n) for n in range(n_sel)],
        out_specs=pl.BlockSpec((1, H, dk), lambda b, ph, ix: (b, 0, 0)))
    out = pl.pallas_call(
        kern,
        grid_spec=grid_spec,
        out_shape=jax.ShapeDtypeStruct((DB, H, dk), jnp.float32),
        compiler_params=pltpu.CompilerParams(dimension_semantics=("arbitrary",), vmem_limit_bytes=VMEM_LIMIT_BYTES),
        name="nsa_sample_selected",
    )(phys, idx2, qs, new_t, jnp.asarray(alibi_slopes_np(H)).reshape(H, 1), *([pages_feature_major(cache_slc)] * n_sel))
    return out[:, None]


def nsa_sample(q, qpos, cmp_new, slc_new, win_new, gates, cache_cmp, cache_slc, win_buf, page_table, e, ep, slopes):
    DB, Tn = q.shape[:2]
    n_pages = page_table.shape[1]
    bpp = PAGE_SIZE // NSA_BLK
    nb_past = n_pages * bpp
    past_len = n_pages * PAGE_SIZE
    nb_new = -(-Tn // NSA_BLK)
    pad = ((0, 0), (0, nb_new * NSA_BLK - Tn), (0, 0), (0, 0))

    def compress(rows):
        return nsa_compress(rows, ep["cmp_pos"], ep["cmp_w1"], ep["cmp_w2"])

    comp_past = nsa_compress_paged(pages_feature_major(cache_cmp), page_table + e * cache_cmp.shape[1],
                                   ep["cmp_pos"], ep["cmp_w1"], ep["cmp_w2"])
    comp_new = compress(jnp.pad(cmp_new, pad).reshape(DB, nb_new, NSA_BLK, 2, NSA_DK))
    comp = jnp.concatenate([comp_past, comp_new], axis=1)
    kc = rmsnorm(comp[:, :, 0], ep["g_k"][0])
    vc = comp[:, :, 1]
    c_end = (jnp.arange(nb_past + nb_new) + 1) * NSA_BLK - 1
    o_cmp, idx = nsa_compressed_branch(q, qpos, kc, vc, c_end, slopes)

    o_slc = nsa_sample_selected(q, idx, slc_new, cache_slc, page_table, e, past_len)

    o_win = window_sample(q, win_buf[:, :, 0:1], win_buf[:, :, 1:2], win_new[:, :, 0:1], win_new[:, :, 1:2],
                          past_len, NSA_WIN, slopes)
    return nsa_gate(gates, o_cmp, o_slc, o_win)


def mla_keys(lat, ep):
    c = lat[..., :MLA_LORA]
    kp = lat[..., MLA_LORA:]
    kn = rmsnorm(jnp.einsum("bkc,chd->bkhd", c, ep["w_uk"]), ep["g_kn"])
    v = jnp.einsum("bkc,chd->bkhd", c, ep["w_uv"])
    return kn, kp, v


def mla_prompt(qn, qp, lat, ep):
    B, T = qn.shape[:2]
    kn, kp, v = mla_keys(lat, ep)
    return jnp.transpose(mla_prompt_attention(qn, qp, kn, kp, v), (0, 2, 1, 3))


MLA_PAGES_PER_STEP = 64


def _mla_sample_kernel(pt_ref, qg_ref, qp_ref, wukt_ref, wuv_ref, gt_ref, new_ref, *rest, scale, n_pages):
    page_refs = rest[:n_pages]
    o_ref, c_scr, kp_scr, a_scr, m_scr, l_scr, acc_scr = rest[n_pages:]
    g = pl.program_id(1)
    ng = pl.num_programs(1)
    nt = (((1,), (1,)), ((), ()))

    @pl.when(g == 0)
    def _():
        a_scr[...] = lax.dot_general(qg_ref[0], wukt_ref[...], (((1,), (0,)), ((), ())),
                                     preferred_element_type=jnp.float32).astype(jnp.bfloat16)
        m_scr[...] = jnp.full(m_scr.shape, NEG_INF, jnp.float32)
        l_scr[...] = jnp.zeros(l_scr.shape, jnp.float32)
        acc_scr[...] = jnp.zeros(acc_scr.shape, jnp.float32)

    def scores(ct, kpt):
        projt = jnp.dot(wukt_ref[...], ct, preferred_element_type=jnp.float32)
        ss = jnp.dot(gt_ref[...], (projt * projt).astype(jnp.bfloat16), preferred_element_type=jnp.float32)
        num = jnp.dot(a_scr[...], ct, preferred_element_type=jnp.float32)
        sp = jnp.dot(qp_ref[0], kpt, preferred_element_type=jnp.float32)
        return (num * lax.rsqrt(ss * (1.0 / MLA_NOPE) + EPS) + sp) * scale

    def softmax_update(ct, st):
        m_i = m_scr[...]
        m_new = jnp.maximum(m_i, jnp.max(st, axis=-1, keepdims=True))
        corr = jnp.exp(m_i - m_new)
        p = jnp.exp(st - m_new)
        m_scr[...] = m_new
        l_scr[...] = l_scr[...] * corr + jnp.sum(p, axis=-1, keepdims=True)
        acc_scr[...] = acc_scr[...] * corr + lax.dot_general(p.astype(jnp.bfloat16), ct, nt,
                                                             preferred_element_type=jnp.float32)

    for j, r in enumerate(page_refs):
        page = r[0]
        c_scr[:, j * PAGE_SIZE:(j + 1) * PAGE_SIZE] = page[:MLA_LORA, :].astype(jnp.bfloat16)
        kp_scr[:, j * PAGE_SIZE:(j + 1) * PAGE_SIZE] = page[MLA_LORA:, :].astype(jnp.bfloat16)
    c_all = c_scr[...]
    softmax_update(c_all, scores(c_all, kp_scr[...]))

    @pl.when(g == ng - 1)
    def _():
        new = new_ref[0]
        ct = new[:MLA_LORA, :].astype(jnp.bfloat16)
        st = scores(ct, new[MLA_LORA:, :].astype(jnp.bfloat16))
        key = lax.broadcasted_iota(jnp.int32, st.shape, 1)
        softmax_update(ct, jnp.where(key == 0, st, NEG_INF))
        o8 = jnp.dot((acc_scr[...] / l_scr[...]).astype(jnp.bfloat16), wuv_ref[...],
                     preferred_element_type=jnp.float32)
        row = lax.broadcasted_iota(jnp.int32, o8.shape, 0)
        col = lax.broadcasted_iota(jnp.int32, o8.shape, 1)
        o_ref[0] = jnp.sum(jnp.where(col // MLA_V == row, o8, 0.0), axis=0, keepdims=True)


def mla_sample(qn, qp, lat_new, cache_mla, page_table, e, ep):
    DB, Tn = qn.shape[:2]
    H = MLA_HEADS
    P = page_table.shape[1]
    PG = math.gcd(MLA_PAGES_PER_STEP, P)
    assert Tn == 1
    bf = jnp.bfloat16
    dq = MLA_NOPE + MLA_ROPE
    qg = qn[:, 0] * ep["g_kn"]
    eye = jnp.eye(H, dtype=qg.dtype)
    qg_exp = (qg[:, :, None, :] * eye[None, :, :, None]).reshape(DB, H, H * MLA_NOPE).astype(bf)
    qp_h = qp[:, 0].astype(bf)
    wukt = ep["w_uk"].reshape(MLA_LORA, H * MLA_NOPE).T.astype(bf)
    wuv = ep["w_uv"].reshape(MLA_LORA, H * MLA_V).astype(bf)
    gt = np.zeros((H, H * MLA_NOPE), np.float32)
    gt[np.arange(H * MLA_NOPE) // MLA_NOPE, np.arange(H * MLA_NOPE)] = 1.0
    pages_t = jnp.swapaxes(cache_mla, -1, -2).reshape(-1, MLA_ROW, PAGE_SIZE)
    new_t = jnp.zeros((DB, MLA_ROW, PAGE_SIZE), jnp.float32).at[:, :, 0].set(lat_new[:, 0])
    kern = functools.partial(_mla_sample_kernel, scale=float(dq ** -0.5), n_pages=PG)

    def page_spec(j):
        return pl.BlockSpec((1, MLA_ROW, PAGE_SIZE), lambda b, g, pt: (pt[b, g * PG + j], 0, 0))

    const2 = lambda b, g, pt: (0, 0)
    grid_spec = pltpu.PrefetchScalarGridSpec(
        num_scalar_prefetch=1,
        grid=(DB, P // PG),
        in_specs=[pl.BlockSpec((1, H, H * MLA_NOPE), lambda b, g, pt: (b, 0, 0)),
                  pl.BlockSpec((1, H, MLA_ROPE), lambda b, g, pt: (b, 0, 0)),
                  pl.BlockSpec((H * MLA_NOPE, MLA_LORA), const2),
                  pl.BlockSpec((MLA_LORA, H * MLA_V), const2),
                  pl.BlockSpec((H, H * MLA_NOPE), const2),
                  pl.BlockSpec((1, MLA_ROW, PAGE_SIZE), lambda b, g, pt: (b, 0, 0))]
                 + [page_spec(j) for j in range(PG)],
        out_specs=pl.BlockSpec((1, 1, H * MLA_V), lambda b, g, pt: (b, 0, 0)),
        scratch_shapes=[pltpu.VMEM((MLA_LORA, PG * PAGE_SIZE), bf),
                        pltpu.VMEM((MLA_ROPE, PG * PAGE_SIZE), bf),
                        pltpu.VMEM((H, MLA_LORA), bf),
                        pltpu.VMEM((8, 1), jnp.float32),
                        pltpu.VMEM((8, 1), jnp.float32),
                        pltpu.VMEM((8, MLA_LORA), jnp.float32)])
    out = pl.pallas_call(
        kern,
        grid_spec=grid_spec,
        out_shape=jax.ShapeDtypeStruct((DB, 1, H * MLA_V), jnp.float32),
        compiler_params=pltpu.CompilerParams(dimension_semantics=("arbitrary", "arbitrary"),
                                             vmem_limit_bytes=VMEM_LIMIT_BYTES),
        name="mla_sample",
    )(page_table + e * cache_mla.shape[1], qg_exp, qp_h, wukt, wuv, jnp.asarray(gt).astype(bf), new_t,
      *([pages_t] * PG))
    return out.reshape(DB, 1, H, MLA_V)


def even_project(x, g_norm, pos, ep):
    B, T, D = x.shape
    h = norm_matmul(x.reshape(B * T, D), g_norm, ep["w_in"]).reshape(B, T, EVEN_IN)
    cuts = np.cumsum([NSA_Q_COLS, NSA_KV_COLS, NSA_GATE_COLS, MLA_Q_COLS]).tolist()
    q, kv, g, mq, lat = jnp.split(h, cuts, axis=-1)
    q = rmsnorm(q.reshape(B, T, NSA_HEADS, NSA_DK), ep["g_q"])
    kv = kv.reshape(B, T, 3, 2, NSA_DK)
    cmp_kv = kv[:, :, 0]
    slc_kv = jnp.stack([rmsnorm(kv[:, :, 1, 0], ep["g_k"][1]), kv[:, :, 1, 1]], axis=2)
    win_kv = jnp.stack([rmsnorm(kv[:, :, 2, 0], ep["g_k"][2]), kv[:, :, 2, 1]], axis=2)
    gates = jax.nn.sigmoid(g.astype(jnp.float32)).reshape(B, T, NSA_HEADS, 3)
    mq = mq.reshape(B, T, MLA_HEADS, MLA_NOPE + MLA_ROPE)
    qn = rmsnorm(mq[..., :MLA_NOPE], ep["g_qn"])
    qp = rope(rmsnorm(mq[..., MLA_NOPE:], ep["g_qpe"]), pos)
    c = rmsnorm(lat[..., :MLA_LORA], ep["g_lat"])
    kp = rope(rmsnorm(lat[..., MLA_LORA:], ep["g_kpe"])[:, :, None, :], pos)[:, :, 0, :]
    return q, cmp_kv, slc_kv, win_kv, gates, qn, qp, jnp.concatenate([c, kp], axis=-1)


def even_output(x, o_nsa, o_mla, w_out):
    B, T, D = x.shape
    a = jnp.concatenate([o_nsa.reshape(B, T, -1), o_mla.reshape(B, T, -1)], axis=-1)
    return matmul_res(a.reshape(B * T, EVEN_OUT), w_out, x.reshape(B * T, D)).reshape(B, T, D)


def odd_project(x, g_norm, op):
    B, T, D = x.shape
    h = norm_matmul(x.reshape(B * T, D), g_norm, op["w_in"]).reshape(B, T, ODD_IN)
    q, k, v = jnp.split(h, [SWA_HEADS * SWA_HD, SWA_HEADS * SWA_HD + SWA_KV_HEADS * SWA_HD], axis=-1)
    q = rmsnorm(q.reshape(B, T, SWA_HEADS, SWA_HD), op["g_q"])
    k = rmsnorm(k.reshape(B, T, SWA_KV_HEADS, SWA_HD), op["g_k"])
    v = v.reshape(B, T, SWA_KV_HEADS, SWA_HD)
    return q, jnp.stack([k, v], axis=2)


def memory_kv(mem, cp):
    B, M, D = mem.shape
    w_kv = jnp.concatenate([cp["w_k"], cp["w_v"]], axis=1)
    kv = norm_matmul(mem.reshape(B * M, D), cp["g_mem"], w_kv).reshape(B, M, 2, MEM_HEADS, MEM_HD)
    k = rmsnorm(kv[:, :, 0], cp["g_k"])
    return jnp.stack([k, kv[:, :, 1]], axis=2)


MEM_TQ = 256


def _memory_cross_kernel(x_ref, g_ref, wq_ref, gq_ref, k_ref, v_ref, wo_ref, o_ref):
    x = x_ref[0]
    xn = x * lax.rsqrt(jnp.mean(x * x, axis=-1, keepdims=True) + EPS) * g_ref[...]
    q = jnp.dot(xn.astype(jnp.bfloat16), wq_ref[...], preferred_element_type=jnp.float32)
    nt = (((1,), (1,)), ((), ()))
    outs = []
    for h in range(MEM_HEADS):
        cols = slice(h * MEM_HD, (h + 1) * MEM_HD)
        qh = q[:, cols]
        qh = qh * lax.rsqrt(jnp.mean(qh * qh, axis=-1, keepdims=True) + EPS) * gq_ref[...]
        s = lax.dot_general(qh.astype(jnp.bfloat16), k_ref[0, :, cols], nt,
                            preferred_element_type=jnp.float32) * (MEM_HD ** -0.5)
        e = jnp.exp(s - jnp.max(s, axis=-1, keepdims=True))
        p = e / jnp.sum(e, axis=-1, keepdims=True)
        outs.append(jnp.dot(p.astype(jnp.bfloat16), v_ref[0, :, cols], preferred_element_type=jnp.float32))
    o = jnp.concatenate(outs, axis=1).astype(jnp.bfloat16)
    o_ref[0] = x + jnp.dot(o, wo_ref[...], preferred_element_type=jnp.float32)


def memory_cross_prompt(x, g_norm, mem_kv, cp):
    B, T, D = x.shape
    M = mem_kv.shape[1]
    bf = jnp.bfloat16
    tq = MEM_TQ
    k = mem_kv[:, :, 0].reshape(B, M, MEM_W).astype(bf)
    v = mem_kv[:, :, 1].reshape(B, M, MEM_W).astype(bf)
    return pl.pallas_call(
        _memory_cross_kernel,
        grid=(B, T // tq),
        in_specs=[pl.BlockSpec((1, tq, D), lambda b, i: (b, i, 0)),
                  pl.BlockSpec((1, D), lambda b, i: (0, 0)),
                  pl.BlockSpec((D, MEM_W), lambda b, i: (0, 0)),
                  pl.BlockSpec((1, MEM_HD), lambda b, i: (0, 0)),
                  pl.BlockSpec((1, M, MEM_W), lambda b, i: (b, 0, 0)),
                  pl.BlockSpec((1, M, MEM_W), lambda b, i: (b, 0, 0)),
                  pl.BlockSpec((MEM_W, D), lambda b, i: (0, 0))],
        out_specs=pl.BlockSpec((1, tq, D), lambda b, i: (b, i, 0)),
        out_shape=jax.ShapeDtypeStruct((B, T, D), jnp.float32),
        compiler_params=pltpu.CompilerParams(dimension_semantics=("arbitrary", "arbitrary"),
                                             vmem_limit_bytes=VMEM_LIMIT_BYTES),
        name="memory_cross",
    )(x, g_norm.reshape(1, D), cp["w_q"].astype(bf), cp["g_q"].reshape(1, MEM_HD), k, v, cp["w_o"].astype(bf))


def memory_cross(x, g_norm, mem_kv, cp):
    B, T, D = x.shape
    if T % MEM_TQ == 0:
        return memory_cross_prompt(x, g_norm, mem_kv, cp)
    q = norm_matmul(x.reshape(B * T, D), g_norm, cp["w_q"]).reshape(B, T, MEM_HEADS, MEM_HD)
    q = rmsnorm(q, cp["g_q"])
    s = jnp.einsum("bthd,bmhd->bhtm", q, mem_kv[:, :, 0]).astype(jnp.float32) * (MEM_HD ** -0.5)
    p = jax.nn.softmax(s, axis=-1).astype(x.dtype)
    o = jnp.einsum("bhtm,bmhd->bthd", p, mem_kv[:, :, 1]).reshape(B * T, MEM_W)
    return matmul_res(o, cp["w_o"], x.reshape(B * T, D)).reshape(B, T, D)


PEER_TB = 128
PEER_VMEM_LIMIT_BYTES = 56 * 1024 * 1024
F32_NEG_INF = float("-inf")


def _topk_rows(s, row, k):
    nrow = float(s.shape[0])
    vals, ids = [], []
    for _ in range(k):
        m = jnp.max(s, axis=0, keepdims=True)
        i = jnp.min(jnp.where(s == m, row, nrow), axis=0, keepdims=True)
        vals.append(m)
        ids.append(i)
        s = jnp.where(row == i, F32_NEG_INF, s)
    return vals, ids


def _peer_route_kernel(x_ref, g_ref, wq_ref, sk0_ref, sk1_ref, xn_ref, idx_ref, gate_ref):
    x = x_ref[...]
    xn = x * lax.rsqrt(jnp.mean(x * x, axis=-1, keepdims=True) + EPS) * g_ref[...]
    xn_ref[...] = xn
    q = jnp.dot(xn.astype(jnp.bfloat16), wq_ref[...], preferred_element_type=jnp.float32)
    tb = x.shape[0]
    row_k = lax.broadcasted_iota(jnp.int32, (PEER_NKEYS, tb), 0).astype(jnp.float32)
    sub8 = lax.broadcasted_iota(jnp.int32, (8, tb), 0).astype(jnp.float32)
    assert PEER_TOPK % 16 == 0
    nt = (((1,), (1,)), ((), ()))
    ids, gates = [], []
    for h in range(PEER_HEADS):
        qh = q[:, h * PEER_DKEY:(h + 1) * PEER_DKEY].astype(jnp.bfloat16)
        s1 = lax.dot_general(sk0_ref[...], qh, nt, preferred_element_type=jnp.float32)
        s2 = lax.dot_general(sk1_ref[...], qh, nt, preferred_element_type=jnp.float32)
        v1, i1 = _topk_rows(s1, row_k, PEER_TOPK)
        v2, i2 = _topk_rows(s2, row_k, PEER_TOPK)
        v1c = jnp.concatenate(v1, axis=0)
        i1c = jnp.concatenate(i1, axis=0)
        v2c = jnp.concatenate(v2, axis=0)
        i2c = jnp.concatenate(i2, axis=0)
        cands, poss, cids = [], [], []
        for a in range(PEER_TOPK // 2):
            bmax = PEER_TOPK // (a + 1) - 1
            for b0 in range(0, bmax + 1, 8):
                c = v1[a] + v2c[b0:b0 + 8]
                if bmax - b0 + 1 < 8:
                    c = jnp.where(sub8 <= float(bmax - b0), c, F32_NEG_INF)
                cands.append(c)
                poss.append(sub8 + float(a * PEER_TOPK + b0))
                cids.append(i1[a] * float(PEER_NKEYS) + i2c[b0:b0 + 8])
        for a0 in range(PEER_TOPK // 2, PEER_TOPK, 8):
            cands.append(v1c[a0:a0 + 8] + v2[0])
            poss.append((sub8 + float(a0)) * float(PEER_TOPK))
            cids.append(i1c[a0:a0 + 8] * float(PEER_NKEYS) + i2[0])
        cand = jnp.concatenate(cands, axis=0)
        row_c = jnp.concatenate(poss, axis=0)
        cid = jnp.concatenate(cids, axis=0)
        tops, tids = [], []
        for _ in range(PEER_TOPK):
            m = jnp.max(cand, axis=0, keepdims=True)
            pos = jnp.min(jnp.where(cand == m, row_c, float(PEER_TOPK * PEER_TOPK)), axis=0, keepdims=True)
            hit = row_c == pos
            tids.append(jnp.sum(jnp.where(hit, cid, 0.0), axis=0, keepdims=True))
            tops.append(m)
            cand = jnp.where(hit, F32_NEG_INF, cand)
        ts = jnp.concatenate(tops, axis=0)
        e = jnp.exp(ts - tops[0])
        gates.append(e / jnp.sum(e, axis=0, keepdims=True))
        ids.append(jnp.concatenate(tids, axis=0))
    table_rows = x.shape[1] // 256
    idx_ref[...] = (jnp.concatenate(ids, axis=0) * float(table_rows)).T.astype(jnp.int32)
    gate_ref[...] = jnp.concatenate(gates, axis=0).T


def peer_route(x, g, w_q, subkeys):
    n, d = x.shape
    tb = min(PEER_TB, n)
    hk = PEER_HEADS * PEER_TOPK
    half = PEER_DKEY // 2
    z = jnp.zeros((PEER_NKEYS, half), jnp.float32)
    sk0 = jnp.concatenate([subkeys[0], z], axis=1).astype(jnp.bfloat16)
    sk1 = jnp.concatenate([z, subkeys[1]], axis=1).astype(jnp.bfloat16)
    return pl.pallas_call(
        _peer_route_kernel,
        grid=(n // tb,),
        in_specs=[pl.BlockSpec((tb, d), lambda i: (i, 0)),
                  pl.BlockSpec((1, d), lambda i: (0, 0)),
                  pl.BlockSpec((d, PEER_HEADS * PEER_DKEY), lambda i: (0, 0)),
                  pl.BlockSpec((PEER_NKEYS, PEER_DKEY), lambda i: (0, 0)),
                  pl.BlockSpec((PEER_NKEYS, PEER_DKEY), lambda i: (0, 0))],
        out_specs=[pl.BlockSpec((tb, d), lambda i: (i, 0)),
                   pl.BlockSpec((tb, hk), lambda i: (i, 0)),
                   pl.BlockSpec((tb, hk), lambda i: (i, 0))],
        out_shape=[jax.ShapeDtypeStruct((n, d), jnp.float32),
                   jax.ShapeDtypeStruct((n, hk), jnp.int32),
                   jax.ShapeDtypeStruct((n, hk), jnp.float32)],
        compiler_params=pltpu.CompilerParams(dimension_semantics=("arbitrary",),
                                             vmem_limit_bytes=PEER_VMEM_LIMIT_BYTES),
        name="peer_route",
    )(x, g.reshape(1, d), w_q.astype(jnp.bfloat16), sk0, sk1)


def pack_table(t):
    e, d = t.shape
    b = lax.bitcast_convert_type(t.astype(jnp.bfloat16), jnp.uint16).astype(jnp.uint32)
    w = b[:, : d // 2] | (b[:, d // 2:] << 16)
    return w.reshape(e * d // 256, 128)


def _table_spec(tab):
    return pl.BlockSpec(tab.shape, lambda i: (0, 0), pipeline_mode=pl.Buffered(1))


def _gather_row(tab, row0, rows):
    wds = tab[pl.ds(pl.multiple_of(row0, rows), rows), :]
    lo = pltpu.bitcast(wds << 16, jnp.float32)
    hi = pltpu.bitcast(wds & jnp.uint32(0xFFFF0000), jnp.float32)
    return lo, hi


def _peer_u_kernel(idx_ref, xn_ref, gate_ref, tab, w_ref, slots, rsum, act):
    tb, hk = gate_ref.shape
    rows = xn_ref.shape[1] // 2

    def token(t, c):
        xt = xn_ref[t]
        xlo, xhi = xt[:rows], xt[rows:]
        for k in range(hk):
            lo, hi = _gather_row(tab, idx_ref[t, k], rows)
            slots[k * rows:(k + 1) * rows, :] = lo * xlo + hi * xhi
        r = slots[pl.ds(0, hk, stride=rows), :]
        for s in range(1, rows):
            r = r + slots[pl.ds(s, hk, stride=rows), :]
        rsum[pl.ds(pl.multiple_of(t * hk, hk), hk), :] = r
        return c

    lax.fori_loop(0, tb, token, 0)

    grp = 8

    def lane_sum(c, carry):
        sums = []
        for i in range(grp):
            rr = rsum[pl.ds(pl.multiple_of((c * grp + i) * hk, hk), hk), :]
            sums.append(jnp.sum(rr.T, axis=0, keepdims=True))
        act[pl.ds(pl.multiple_of(c * grp, grp), grp), :] = jnp.concatenate(sums, axis=0)
        return carry

    lax.fori_loop(0, tb // grp, lane_sum, 0)
    w_ref[...] = gate_ref[...] * jax.nn.gelu(act[...])


def peer_activate(idx, xn, gate, tab):
    n, d = xn.shape
    hk = idx.shape[1]
    tb = min(PEER_TB, n)
    sub = d // 128
    return pl.pallas_call(
        _peer_u_kernel,
        grid=(n // tb,),
        in_specs=[pl.BlockSpec((tb, hk), lambda i: (i, 0), memory_space=pltpu.SMEM),
                  pl.BlockSpec((tb, sub, 128), lambda i: (i, 0, 0)),
                  pl.BlockSpec((tb, hk), lambda i: (i, 0)),
                  _table_spec(tab)],
        out_specs=pl.BlockSpec((tb, hk), lambda i: (i, 0)),
        out_shape=jax.ShapeDtypeStruct((n, hk), jnp.float32),
        scratch_shapes=[pltpu.VMEM((hk * sub // 2, 128), jnp.float32),
                        pltpu.VMEM((tb * hk, 128), jnp.float32),
                        pltpu.VMEM((tb, hk), jnp.float32)],
        compiler_params=pltpu.CompilerParams(dimension_semantics=("arbitrary",),
                                             vmem_limit_bytes=PEER_VMEM_LIMIT_BYTES),
        name="peer_u",
    )(idx, xn.reshape(n, sub, 128), gate, tab)


def _peer_v_kernel(idx_ref, w_ref, x_ref, tab, o_ref):
    tb, hk = idx_ref.shape
    rows = x_ref.shape[1] // 2

    def token(t, c):
        nacc = 2
        lo_acc = [jnp.zeros((rows, 128), jnp.float32) for _ in range(nacc)]
        hi_acc = [jnp.zeros((rows, 128), jnp.float32) for _ in range(nacc)]
        for k in range(hk):
            lo, hi = _gather_row(tab, idx_ref[t, k], rows)
            wk = w_ref[t, k]
            lo_acc[k % nacc] = lo_acc[k % nacc] + wk * lo
            hi_acc[k % nacc] = hi_acc[k % nacc] + wk * hi
        xt = x_ref[t]
        o_ref[t, :rows, :] = xt[:rows] + (lo_acc[0] + lo_acc[1])
        o_ref[t, rows:, :] = xt[rows:] + (hi_acc[0] + hi_acc[1])
        return c

    lax.fori_loop(0, tb, token, 0)


def peer_combine(idx, w, x, tab):
    n, d = x.shape
    hk = idx.shape[1]
    tb = min(PEER_TB, n)
    sub = d // 128
    out = pl.pallas_call(
        _peer_v_kernel,
        grid=(n // tb,),
        in_specs=[pl.BlockSpec((tb, hk), lambda i: (i, 0), memory_space=pltpu.SMEM),
                  pl.BlockSpec((tb, hk), lambda i: (i, 0), memory_space=pltpu.SMEM),
                  pl.BlockSpec((tb, sub, 128), lambda i: (i, 0, 0)),
                  _table_spec(tab)],
        out_specs=pl.BlockSpec((tb, sub, 128), lambda i: (i, 0, 0)),
        out_shape=jax.ShapeDtypeStruct((n, sub, 128), jnp.float32),
        compiler_params=pltpu.CompilerParams(dimension_semantics=("arbitrary",),
                                             vmem_limit_bytes=PEER_VMEM_LIMIT_BYTES),
        name="peer_v",
    )(idx, w, x.reshape(n, sub, 128), tab)
    return out.reshape(n, d)


def peer_ffn(x, g_norm, pp):
    B, T, D = x.shape
    x2 = x.reshape(B * T, D)
    xn, idx, gate = peer_route(x2, g_norm, pp["w_q"], pp["subkeys"])
    w = peer_activate(idx, xn, gate, pp["u_packed"])
    return peer_combine(idx, w, x2, pp["v_packed"]).reshape(B, T, D)


def kernel(x_prompt, x_sample, mem_prompt, cache_nsa_cmp, cache_nsa_slc, cache_nsa_win, cache_mla, cache_swa,
           cache_mem, page_table, norm_mix, norm_cross, norm_ffn, even_w_in, even_w_out, nsa_g_q, nsa_g_k,
           nsa_cmp_pos, nsa_cmp_w1, nsa_cmp_w2, mla_g_qn, mla_g_qpe, mla_g_lat, mla_g_kpe, mla_g_kn, mla_w_uk,
           mla_w_uv, odd_w_in, odd_w_out, swa_g_q, swa_g_k, swa_sinks, mem_g, mem_w_q, mem_w_k, mem_w_v,
           mem_g_q, mem_g_k, mem_w_o, peer_w_q, peer_subkeys, peer_u, peer_v):
    depth = norm_mix.shape[0]
    slopes_nsa = alibi_slopes(NSA_HEADS)
    slopes_swa = alibi_slopes(SWA_HEADS)
    past_len = page_table.shape[1] * PAGE_SIZE
    T, Tn = x_prompt.shape[1], x_sample.shape[1]
    pos_p = jnp.arange(T)
    pos_s = past_len + jnp.arange(Tn)
    xp, xs = x_prompt, x_sample
    st_cmp_p, st_cmp_s, st_slc_p, st_slc_s, st_win_p, st_win_s = [], [], [], [], [], []
    st_mla_p, st_mla_s, st_swa_p, st_swa_s, st_mem_p = [], [], [], [], []

    for li in range(depth):
        if li % 2 == 0:
            e = li // 2
            ep = dict(w_in=even_w_in[e], g_q=nsa_g_q[e], g_k=nsa_g_k[e], cmp_pos=nsa_cmp_pos[e],
                      cmp_w1=nsa_cmp_w1[e], cmp_w2=nsa_cmp_w2[e], g_qn=mla_g_qn[e], g_qpe=mla_g_qpe[e],
                      g_lat=mla_g_lat[e], g_kpe=mla_g_kpe[e], g_kn=mla_g_kn[e], w_uk=mla_w_uk[e], w_uv=mla_w_uv[e])
            q, cmp_kv, slc_kv, win_kv, gates, qn, qp, mla_row = even_project(xp, norm_mix[li], pos_p, ep)
            o_nsa = nsa_prompt(q, cmp_kv, slc_kv, win_kv, gates, ep, slopes_nsa)
            o_mla = mla_prompt(qn, qp, mla_row, ep)
            xp = even_output(xp, o_nsa, o_mla, even_w_out[e])
            st_cmp_p.append(cmp_kv)
            st_slc_p.append(slc_kv)
            st_win_p.append(win_kv[:, -min(NSA_WIN, T):])
            st_mla_p.append(mla_row)
            win_buf = cache_nsa_win[e]
            q, cmp_kv, slc_kv, win_kv, gates, qn, qp, mla_row = even_project(xs, norm_mix[li], pos_s, ep)
            o_nsa = nsa_sample(q, pos_s, cmp_kv, slc_kv, win_kv, gates, cache_nsa_cmp, cache_nsa_slc, win_buf,
                               page_table, e, ep, slopes_nsa)
            o_mla = mla_sample(qn, qp, mla_row, cache_mla, page_table, e, ep)
            xs = even_output(xs, o_nsa, o_mla, even_w_out[e])
            st_cmp_s.append(cmp_kv)
            st_slc_s.append(slc_kv)
            st_win_s.append(jnp.concatenate([win_buf, win_kv], axis=1)[:, -win_buf.shape[1]:])
            st_mla_s.append(mla_row)
        else:
            o = li // 2
            op = dict(w_in=odd_w_in[o], g_q=swa_g_q[o], g_k=swa_g_k[o])
            B = xp.shape[0]
            q, kv = odd_project(xp, norm_mix[li], op)
            att = jnp.transpose(swa_prompt_attention(q, kv[:, :, 0], kv[:, :, 1], swa_sinks[o]), (0, 2, 1, 3))
            xp = matmul_res(att.reshape(B * T, ODD_OUT), odd_w_out[o], xp.reshape(B * T, -1)).reshape(xp.shape)
            st_swa_p.append(kv[:, -min(SWA_WIN, T):])
            buf = cache_swa[o]
            q, kv = odd_project(xs, norm_mix[li], op)
            att = window_sample(q, buf[:, :, 0], buf[:, :, 1], kv[:, :, 0], kv[:, :, 1], past_len, SWA_WIN,
                                slopes_swa, swa_sinks[o])
            xs = matmul_res(att.reshape(xs.shape[0] * Tn, ODD_OUT), odd_w_out[o],
                            xs.reshape(xs.shape[0] * Tn, -1)).reshape(xs.shape)
            st_swa_s.append(jnp.concatenate([buf, kv], axis=1)[:, -buf.shape[1]:])

        cp = dict(g_mem=mem_g[li], w_q=mem_w_q[li], w_k=mem_w_k[li], w_v=mem_w_v[li], g_q=mem_g_q[li],
                  g_k=mem_g_k[li], w_o=mem_w_o[li])
        mem_kv_p = memory_kv(mem_prompt, cp)
        xp = memory_cross(xp, norm_cross[li], mem_kv_p, cp)
        xs = memory_cross(xs, norm_cross[li], cache_mem[li], cp)
        st_mem_p.append(mem_kv_p)

        pp = dict(w_q=peer_w_q[li], subkeys=peer_subkeys[li],
                  u_packed=pack_table(peer_u[li]), v_packed=pack_table(peer_v[li]))
        xp = peer_ffn(xp, norm_ffn[li], pp)
        xs = peer_ffn(xs, norm_ffn[li], pp)

    return (xp, xs, jnp.stack(st_cmp_p), jnp.stack(st_cmp_s), jnp.stack(st_slc_p), jnp.stack(st_slc_s),
            jnp.stack(st_win_p), jnp.stack(st_win_s), jnp.stack(st_mla_p), jnp.stack(st_mla_s),
            jnp.stack(st_swa_p), jnp.stack(st_swa_s), jnp.stack(st_mem_p))
```

```python
import functools
import math

import jax
import jax.numpy as jnp
import numpy as np
from jax import lax
from jax.experimental import pallas as pl
from jax.experimental.pallas import tpu as pltpu

PAGE_SIZE = 128
EPS = 1e-6
NEG_INF = -1e30
ROPE_BASE = 10000.0

NSA_HEADS = 8
NSA_DK = 64
NSA_BLK = 64
NSA_TOPN = 16
NSA_WIN = 512
NSA_CMP_HID = 256
MLA_HEADS = 8
MLA_NOPE = 64
MLA_ROPE = 32
MLA_V = 64
MLA_LORA = 128
MLA_ROW = MLA_LORA + MLA_ROPE
SWA_HEADS = 16
SWA_KV_HEADS = 4
SWA_HD = 64
SWA_WIN = 128
MEM_HEADS = 4
MEM_HD = 128
MEM_W = MEM_HEADS * MEM_HD
PEER_HEADS = 8
PEER_NKEYS = 128
PEER_DKEY = 128
PEER_TOPK = 16

NSA_Q_COLS = NSA_HEADS * NSA_DK
NSA_KV_COLS = 3 * 2 * NSA_DK
NSA_GATE_COLS = NSA_HEADS * 3
MLA_Q_COLS = MLA_HEADS * (MLA_NOPE + MLA_ROPE)
EVEN_IN = NSA_Q_COLS + NSA_KV_COLS + NSA_GATE_COLS + MLA_Q_COLS + MLA_ROW
EVEN_OUT = NSA_HEADS * NSA_DK + MLA_HEADS * MLA_V
ODD_IN = SWA_HEADS * SWA_HD + 2 * SWA_KV_HEADS * SWA_HD
ODD_OUT = SWA_HEADS * SWA_HD

VMEM_LIMIT_BYTES = 48 * 1024 * 1024


def _row_tile(n, target=512):
    t = min(n, target)
    while n % t:
        t //= 2
    return t


def _norm_matmul_kernel(x_ref, g_ref, w_ref, o_ref):
    x = x_ref[...]
    y = x * lax.rsqrt(jnp.mean(x * x, axis=-1, keepdims=True) + EPS) * g_ref[...]
    o_ref[...] = jnp.dot(y.astype(jnp.bfloat16), w_ref[...], preferred_element_type=jnp.float32)


def norm_matmul(x, g, w):
    n, d = x.shape
    c = w.shape[1]
    tm = _row_tile(n)
    return pl.pallas_call(
        _norm_matmul_kernel,
        grid=(n // tm,),
        in_specs=[pl.BlockSpec((tm, d), lambda i: (i, 0)),
                  pl.BlockSpec((1, d), lambda i: (0, 0)),
                  pl.BlockSpec((d, c), lambda i: (0, 0))],
        out_specs=pl.BlockSpec((tm, c), lambda i: (i, 0)),
        out_shape=jax.ShapeDtypeStruct((n, c), jnp.float32),
        compiler_params=pltpu.CompilerParams(dimension_semantics=("arbitrary",),
                                             vmem_limit_bytes=VMEM_LIMIT_BYTES),
        name="norm_matmul",
    )(x, g.reshape(1, d), w.astype(jnp.bfloat16))


def _matmul_res_kernel(a_ref, w_ref, r_ref, o_ref):
    o_ref[...] = r_ref[...] + jnp.dot(a_ref[...].astype(jnp.bfloat16), w_ref[...],
                                      preferred_element_type=jnp.float32)


def matmul_res(a, w, res):
    n, k = a.shape
    c = w.shape[1]
    tm = _row_tile(n)
    return pl.pallas_call(
        _matmul_res_kernel,
        grid=(n // tm,),
        in_specs=[pl.BlockSpec((tm, k), lambda i: (i, 0)),
                  pl.BlockSpec((k, c), lambda i: (0, 0)),
                  pl.BlockSpec((tm, c), lambda i: (i, 0))],
        out_specs=pl.BlockSpec((tm, c), lambda i: (i, 0)),
        out_shape=jax.ShapeDtypeStruct((n, c), jnp.float32),
        compiler_params=pltpu.CompilerParams(dimension_semantics=("arbitrary",),
                                             vmem_limit_bytes=VMEM_LIMIT_BYTES),
        name="matmul_res",
    )(a, w.astype(jnp.bfloat16), res)


def rmsnorm(x, g):
    xf = x.astype(jnp.float32)
    y = xf * lax.rsqrt(jnp.mean(xf * xf, axis=-1, keepdims=True) + EPS)
    return (y * g.astype(jnp.float32)).astype(x.dtype)


def alibi_slopes(n_heads):
    return jnp.asarray((2.0 ** (-8.0 * np.arange(1, n_heads + 1) / n_heads)).astype(np.float32))


def rope(x, pos):
    d = x.shape[-1]
    inv = jnp.asarray(np.power(ROPE_BASE, -np.arange(0, d, 2, dtype=np.float32) / d).astype(np.float32))
    ang = pos.astype(jnp.float32)[:, None] * inv[None, :]
    cos = jnp.cos(ang)[None, :, None, :]
    sin = jnp.sin(ang)[None, :, None, :]
    xf = x.astype(jnp.float32)
    x1, x2 = xf[..., : d // 2], xf[..., d // 2:]
    return jnp.concatenate([x1 * cos - x2 * sin, x1 * sin + x2 * cos], axis=-1).astype(x.dtype)


def masked_softmax(s, valid, sink=None):
    s = jnp.where(valid, s, NEG_INF)
    m = jnp.max(s, axis=-1, keepdims=True)
    if sink is not None:
        sk = sink.astype(jnp.float32)[:, None, None]
        m = jnp.maximum(m, sk)
    e = jnp.where(valid, jnp.exp(s - m), 0.0)
    den = jnp.sum(e, axis=-1, keepdims=True)
    if sink is not None:
        den = den + jnp.exp(sk - m)
    return e / jnp.maximum(den, 1e-30)


def attend(q, k, v, dist, valid, slopes, sink=None):
    B, Q, H, d = q.shape
    G = k.shape[-2]
    qg = q.reshape(B, Q, G, H // G, d)
    if k.ndim == 5:
        s = jnp.einsum("bqgrd,bqkgd->bgrqk", qg, k)
    else:
        s = jnp.einsum("bqgrd,bkgd->bgrqk", qg, k)
    K = s.shape[-1]
    s = s.reshape(B, H, Q, K).astype(jnp.float32) * (d ** -0.5)
    if slopes is not None:
        s = s - slopes[:, None, None] * dist[..., None, :, :].astype(jnp.float32)
    p = masked_softmax(s, valid[..., None, :, :], sink)
    pg = p.reshape(B, G, H // G, Q, K).astype(v.dtype)
    if v.ndim == 5:
        o = jnp.einsum("bgrqk,bqkgd->bqgrd", pg, v)
    else:
        o = jnp.einsum("bgrqk,bkgd->bqgrd", pg, v)
    return o.reshape(B, Q, H, v.shape[-1]), p


def alibi_slopes_np(n_heads):
    return (2.0 ** (-8.0 * np.arange(1, n_heads + 1) / n_heads)).astype(np.float32)


Q_TILE = 128
SLC_CHUNK = 512
MASK_BIG = -(2.0 ** 100)


def _nsa_prompt_kernel(q_ref, kt_ref, vs_ref, kwt_ref, vw_ref, kct_ref, vc_ref, g_ref, slope_ref, wbias_ref, cbias_ref,
                       o_ref, *, n_blocks):
    H, TQ = NSA_HEADS, Q_TILE
    i = pl.program_id(1)
    s0 = i * TQ
    q_lo = q_ref[0].reshape(H * TQ, 128)
    slope = slope_ref[...]

    s = jnp.dot(q_lo, kct_ref[0], preferred_element_type=jnp.float32).reshape(H, TQ, n_blocks)
    tpos = s0 + lax.broadcasted_iota(jnp.int32, (TQ, n_blocks), 0)
    blk = lax.broadcasted_iota(jnp.int32, (TQ, n_blocks), 1)
    dist = tpos - ((blk + 1) * NSA_BLK - 1)
    valid = (dist >= 0)[None]
    s = jnp.where(valid, s - slope * dist.astype(jnp.float32)[None], NEG_INF)
    m = jnp.max(s, axis=-1, keepdims=True)
    e = jnp.where(valid, jnp.exp(s - m), 0.0)
    p = e / jnp.maximum(jnp.sum(e, axis=-1, keepdims=True), 1e-30)
    o_cmp = jnp.dot(p.reshape(H * TQ, n_blocks).astype(jnp.bfloat16), vc_ref[0],
                    preferred_element_type=jnp.float32)
    imp = jnp.sum(p, axis=0)

    cur = tpos // NSA_BLK
    imp = jnp.where(blk == cur, float(H + 1), jnp.where(blk < cur, imp, -1.0))
    imp_t = imp.T
    blk_t = lax.broadcasted_iota(jnp.int32, (n_blocks, TQ), 0).astype(jnp.float32)
    sel_t = jnp.zeros((n_blocks, TQ), jnp.float32)
    for _ in range(min(NSA_TOPN, n_blocks)):
        mx = jnp.max(imp_t, axis=0, keepdims=True)
        pick = jnp.min(jnp.where(imp_t == mx, blk_t, float(n_blocks)), axis=0, keepdims=True)
        hit = blk_t == pick
        sel_t = jnp.where(hit, 1.0, sel_t)
        imp_t = jnp.where(hit, -2.0, imp_t)
    nsb = jnp.where(sel_t.T > 0.5, 0.0, MASK_BIG).astype(jnp.bfloat16)
    q_aug = jnp.concatenate([q_lo, jnp.broadcast_to(nsb[None], (H, TQ, n_blocks)).reshape(H * TQ, n_blocks)], axis=1)

    KC = SLC_CHUNK

    def slc_step(c, carry, bias):
        m_i, acc = carry
        k0 = pl.multiple_of(c * KC, KC)
        sc = jnp.dot(q_aug, kt_ref[0, :, pl.ds(k0, KC)], preferred_element_type=jnp.float32)
        if bias is not None:
            sc = (sc.reshape(H, TQ, KC) + bias[None]).reshape(H * TQ, KC)
        m_new = jnp.maximum(m_i, jnp.max(sc, axis=-1, keepdims=True))
        corr = jnp.exp(m_i - m_new)
        pe = jnp.exp(sc - m_new)
        acc_new = acc * corr + jnp.dot(pe.astype(jnp.bfloat16), vs_ref[0, pl.ds(k0, KC), :],
                                       preferred_element_type=jnp.float32)
        return m_new, acc_new

    init = (jnp.full((H * TQ, 1), NEG_INF, jnp.float32), jnp.zeros((H * TQ, 128), jnp.float32))
    c_diag = s0 // KC
    carry = lax.fori_loop(0, c_diag, lambda c, cr: slc_step(c, cr, None), init)
    m_i, acc = slc_step(c_diag, carry, cbias_ref[(s0 % KC) // TQ])
    o_slc = acc[:, :NSA_DK] / acc[:, NSA_DK:NSA_DK + 1]

    nwc = NSA_WIN // TQ + 1
    parts = []
    for r in range(nwc):
        cidx = i - (nwc - 1) + r
        k0 = pl.multiple_of(jnp.maximum(cidx, 0) * TQ, TQ)
        sw = jnp.dot(q_lo[:, :NSA_DK], kwt_ref[0, :, pl.ds(k0, TQ)], preferred_element_type=jnp.float32)
        b = jnp.where(cidx >= 0, wbias_ref[r], NEG_INF)
        parts.append((sw.reshape(H, TQ, TQ) + b).reshape(H * TQ, TQ))
    mw = parts[0].max(axis=-1, keepdims=True)
    for r in range(1, nwc):
        mw = jnp.maximum(mw, parts[r].max(axis=-1, keepdims=True))
    accw = jnp.zeros((H * TQ, 128), jnp.float32)
    for r in range(nwc):
        cidx = i - (nwc - 1) + r
        k0 = pl.multiple_of(jnp.maximum(cidx, 0) * TQ, TQ)
        pe = jnp.exp(parts[r] - mw)
        accw = accw + jnp.dot(pe.astype(jnp.bfloat16), vw_ref[0, pl.ds(k0, TQ), :], preferred_element_type=jnp.float32)
    o_win = accw[:, :NSA_DK] / accw[:, NSA_DK:NSA_DK + 1]

    g = g_ref[0].reshape(H * TQ, 3)
    o = g[:, 0:1] * o_cmp + g[:, 1:2] * o_slc + g[:, 2:3] * o_win
    o_ref[0] = o.reshape(H, TQ, NSA_DK)


def _with_ones_column(v):
    ones = jnp.ones(v.shape[:-1] + (1,), v.dtype)
    zeros = jnp.zeros(v.shape[:-1] + (127 - v.shape[-1],), v.dtype)
    return jnp.concatenate([v, ones, zeros], axis=-1).astype(jnp.bfloat16)


def nsa_prompt_attention(q, slc_k, slc_v, win_k, win_v, kc, vc, gates):
    B, T, H, dk = q.shape
    nb = T // NSA_BLK
    nbp = 128
    TQ = Q_TILE
    assert nb <= nbp and T % SLC_CHUNK == 0 and NSA_WIN % TQ == 0
    slopes = alibi_slopes_np(H)
    bf = jnp.bfloat16
    qh = jnp.transpose(q, (0, 2, 1, 3)) * (dk ** -0.5)
    al = np.zeros((H, 128 - dk), np.float32)
    al[:, 0] = slopes * 128.0
    al[:, 1] = slopes
    q_lo = jnp.concatenate([qh, jnp.broadcast_to(jnp.asarray(al)[None, :, None, :], (B, H, T, 128 - dk))],
                           axis=-1).astype(bf)
    pos = np.arange(T)
    crow = np.zeros((128 - dk + nbp, T), np.float32)
    crow[0] = pos // 128
    crow[1] = pos % 128
    crow[128 - dk + pos // NSA_BLK, pos] = 1.0
    kt = jnp.concatenate([jnp.transpose(slc_k, (0, 2, 1)),
                          jnp.broadcast_to(jnp.asarray(crow)[None], (B,) + crow.shape)], axis=1).astype(bf)
    kwt = jnp.transpose(win_k, (0, 2, 1)).astype(bf)
    kct = jnp.pad(jnp.transpose(kc, (0, 2, 1)), ((0, 0), (0, 128 - dk), (0, nbp - nb))).astype(bf)
    vcp = jnp.pad(vc, ((0, 0), (0, nbp - nb), (0, 0))).astype(bf)
    gh = jnp.transpose(gates, (0, 2, 1, 3))
    nwc = NSA_WIN // TQ + 1
    ii = np.arange(TQ)[:, None]
    jj = np.arange(TQ)[None, :]
    wb = np.zeros((nwc, H, TQ, TQ), np.float32)
    for r in range(nwc):
        d = ii - jj + TQ * (nwc - 1 - r)
        ok = (d >= 0) & (d <= NSA_WIN)
        wb[r] = np.where(ok[None], -slopes[:, None, None] * d[None].astype(np.float32), NEG_INF)
    nv = SLC_CHUNK // TQ
    cb = np.zeros((nv, TQ, SLC_CHUNK), np.float32)
    for v in range(nv):
        cb[v] = np.where(np.arange(SLC_CHUNK)[None, :] <= v * TQ + ii, 0.0, NEG_INF)
    kern = functools.partial(_nsa_prompt_kernel, n_blocks=nbp)
    return pl.pallas_call(
        kern,
        grid=(B, T // TQ),
        in_specs=[pl.BlockSpec((1, H, TQ, 128), lambda b, i: (b, 0, i, 0)),
                  pl.BlockSpec((1, 256, T), lambda b, i: (b, 0, 0)),
                  pl.BlockSpec((1, T, 128), lambda b, i: (b, 0, 0)),
                  pl.BlockSpec((1, dk, T), lambda b, i: (b, 0, 0)),
                  pl.BlockSpec((1, T, 128), lambda b, i: (b, 0, 0)),
                  pl.BlockSpec((1, 128, nbp), lambda b, i: (b, 0, 0)),
                  pl.BlockSpec((1, nbp, dk), lambda b, i: (b, 0, 0)),
                  pl.BlockSpec((1, H, TQ, 3), lambda b, i: (b, 0, i, 0)),
                  pl.BlockSpec((H, 1, 1), lambda b, i: (0, 0, 0)),
                  pl.BlockSpec((nwc, H, TQ, TQ), lambda b, i: (0, 0, 0, 0)),
                  pl.BlockSpec((nv, TQ, SLC_CHUNK), lambda b, i: (0, 0, 0))],
        out_specs=pl.BlockSpec((1, H, TQ, dk), lambda b, i: (b, 0, i, 0)),
        out_shape=jax.ShapeDtypeStruct((B, H, T, dk), jnp.float32),
        compiler_params=pltpu.CompilerParams(dimension_semantics=("arbitrary", "arbitrary"),
                                             vmem_limit_bytes=VMEM_LIMIT_BYTES),
        name="nsa_prompt",
    )(q_lo, kt, _with_ones_column(slc_v), kwt, _with_ones_column(win_v), kct, vcp, gh,
      jnp.asarray(slopes).reshape(H, 1, 1), jnp.asarray(wb), jnp.asarray(cb))


MLA_TILE = 512


def _mla_prompt_kernel(q_ref, kt_ref, v_ref, tril_ref, o_ref, *, scale_log2e):
    TQ = MLA_TILE
    i = pl.program_id(2)
    q = q_ref[0, 0]

    def step(c, carry, bias):
        m_i, acc = carry
        k0 = pl.multiple_of(c * TQ, TQ)
        s = jnp.dot(q, kt_ref[0, 0, :, pl.ds(k0, TQ)], preferred_element_type=jnp.float32)
        if bias is not None:
            s = s + bias
        m_new = jnp.maximum(m_i, jnp.max(s, axis=-1, keepdims=True))
        corr = jnp.exp2((m_i - m_new) * scale_log2e)
        p = jnp.exp2((s - m_new) * scale_log2e)
        acc_new = acc * corr + jnp.dot(p.astype(jnp.bfloat16), v_ref[0, 0, pl.ds(k0, TQ), :],
                                       preferred_element_type=jnp.float32)
        return m_new, acc_new

    init = (jnp.full((TQ, 1), NEG_INF, jnp.float32), jnp.zeros((TQ, 128), jnp.float32))
    carry = lax.fori_loop(0, i, lambda c, cr: step(c, cr, None), init)
    m_i, acc = step(i, carry, tril_ref[...])
    o_ref[0, 0] = acc[:, :MLA_V] / acc[:, MLA_V:MLA_V + 1]


def mla_prompt_attention(qn, qp, kn, kp, v):
    B, T, H, _ = qn.shape
    TQ = MLA_TILE
    assert T % TQ == 0
    bf = jnp.bfloat16
    dq = MLA_NOPE + MLA_ROPE
    qcat = jnp.concatenate([qn, qp, jnp.zeros((B, T, H, 128 - dq), qn.dtype)], axis=-1)
    qcat = jnp.transpose(qcat, (0, 2, 1, 3)).astype(bf)
    kcat = jnp.concatenate([kn, jnp.broadcast_to(kp[:, :, None, :], (B, T, H, MLA_ROPE)),
                            jnp.zeros((B, T, H, 128 - dq), kn.dtype)], axis=-1)
    kt = jnp.transpose(kcat, (0, 2, 3, 1)).astype(bf)
    vh = _with_ones_column(jnp.transpose(v, (0, 2, 1, 3)))
    tril = np.where(np.arange(TQ)[None, :] <= np.arange(TQ)[:, None], 0.0, NEG_INF).astype(np.float32)
    kern = functools.partial(_mla_prompt_kernel, scale_log2e=float(dq ** -0.5 * math.log2(math.e)))
    return pl.pallas_call(
        kern,
        grid=(B, H, T // TQ),
        in_specs=[pl.BlockSpec((1, 1, TQ, 128), lambda b, h, i: (b, h, i, 0)),
                  pl.BlockSpec((1, 1, 128, T), lambda b, h, i: (b, h, 0, 0)),
                  pl.BlockSpec((1, 1, T, 128), lambda b, h, i: (b, h, 0, 0)),
                  pl.BlockSpec((TQ, TQ), lambda b, h, i: (0, 0))],
        out_specs=pl.BlockSpec((1, 1, TQ, MLA_V), lambda b, h, i: (b, h, i, 0)),
        out_shape=jax.ShapeDtypeStruct((B, H, T, MLA_V), jnp.float32),
        compiler_params=pltpu.CompilerParams(dimension_semantics=("arbitrary", "arbitrary", "arbitrary"),
                                             vmem_limit_bytes=VMEM_LIMIT_BYTES),
        name="mla_prompt",
    )(qcat, kt, vh, jnp.asarray(tril))


def _swa_prompt_kernel(q_ref, kt_ref, v_ref, bias_ref, sink_ref, o_ref):
    TQ = Q_TILE
    R = SWA_HEADS // SWA_KV_HEADS
    i = pl.program_id(2)
    q = q_ref[0, 0].reshape(R * TQ, SWA_HD)
    kprev = pl.multiple_of(jnp.maximum(i - 1, 0) * TQ, TQ)
    kcur = pl.multiple_of(i * TQ, TQ)
    s0 = jnp.dot(q, kt_ref[0, 0, :, pl.ds(kprev, TQ)], preferred_element_type=jnp.float32).reshape(R, TQ, TQ)
    s1 = jnp.dot(q, kt_ref[0, 0, :, pl.ds(kcur, TQ)], preferred_element_type=jnp.float32).reshape(R, TQ, TQ)
    s0 = s0 + jnp.where(i > 0, bias_ref[0, 0], NEG_INF)
    s1 = s1 + bias_ref[0, 1]
    sink = sink_ref[0]
    m = jnp.maximum(jnp.maximum(s0.max(axis=-1, keepdims=True), s1.max(axis=-1, keepdims=True)), sink)
    p0 = jnp.exp(s0 - m)
    p1 = jnp.exp(s1 - m)
    den = p0.sum(axis=-1, keepdims=True) + p1.sum(axis=-1, keepdims=True) + jnp.exp(sink - m)
    o = (jnp.dot(p0.reshape(R * TQ, TQ).astype(jnp.bfloat16), v_ref[0, 0, pl.ds(kprev, TQ), :],
                 preferred_element_type=jnp.float32)
         + jnp.dot(p1.reshape(R * TQ, TQ).astype(jnp.bfloat16), v_ref[0, 0, pl.ds(kcur, TQ), :],
                   preferred_element_type=jnp.float32))
    o_ref[0, 0] = o.reshape(R, TQ, SWA_HD) / jnp.maximum(den, 1e-30)


def swa_prompt_attention(q, k, v, sinks):
    B, T, H, hd = q.shape
    G = k.shape[2]
    R = H // G
    TQ = Q_TILE
    assert SWA_WIN == TQ and T % TQ == 0
    bf = jnp.bfloat16
    slopes = alibi_slopes_np(H)
    qh = (jnp.transpose(q, (0, 2, 1, 3)) * (hd ** -0.5)).astype(bf).reshape(B, G, R, T, hd)
    kt = jnp.transpose(k, (0, 2, 3, 1)).astype(bf)
    vh = jnp.transpose(v, (0, 2, 1, 3)).astype(bf)
    ii = np.arange(TQ)[:, None]
    jj = np.arange(TQ)[None, :]
    bias = np.zeros((G, 2, R, TQ, TQ), np.float32)
    for r in range(2):
        d = ii - jj + TQ * (1 - r)
        ok = (d >= 0) & (d <= SWA_WIN)
        bias[:, r] = np.where(ok[None, None],
                              -slopes.reshape(G, R)[:, :, None, None] * d[None, None].astype(np.float32), NEG_INF)
    return pl.pallas_call(
        _swa_prompt_kernel,
        grid=(B, G, T // TQ),
        in_specs=[pl.BlockSpec((1, 1, R, TQ, hd), lambda b, g, i: (b, g, 0, i, 0)),
                  pl.BlockSpec((1, 1, hd, T), lambda b, g, i: (b, g, 0, 0)),
                  pl.BlockSpec((1, 1, T, hd), lambda b, g, i: (b, g, 0, 0)),
                  pl.BlockSpec((1, 2, R, TQ, TQ), lambda b, g, i: (g, 0, 0, 0, 0)),
                  pl.BlockSpec((1, R, 1, 1), lambda b, g, i: (g, 0, 0, 0))],
        out_specs=pl.BlockSpec((1, 1, R, TQ, hd), lambda b, g, i: (b, g, 0, i, 0)),
        out_shape=jax.ShapeDtypeStruct((B, G, R, T, hd), jnp.float32),
        compiler_params=pltpu.CompilerParams(dimension_semantics=("arbitrary", "arbitrary", "arbitrary"),
                                             vmem_limit_bytes=VMEM_LIMIT_BYTES),
        name="swa_prompt",
    )(qh, kt, vh, jnp.asarray(bias), sinks.reshape(G, R, 1, 1)).reshape(B, H, T, hd)


def window_sample(q, k_buf, v_buf, k_new, v_new, past_len, window, slopes, sink=None):
    Wb, Tn = k_buf.shape[1], q.shape[1]
    k = jnp.concatenate([k_buf, k_new], axis=1)
    v = jnp.concatenate([v_buf, v_new], axis=1)
    qpos = past_len + jnp.arange(Tn)
    kpos = past_len - Wb + jnp.arange(Wb + Tn)
    dist = qpos[:, None] - kpos[None, :]
    valid = (dist >= 0) & (dist <= window)
    return attend(q, k, v, dist, valid, slopes, sink)[0]


def nsa_compress(blocks, pos_emb, w1, w2):
    x = blocks + pos_emb
    h = jax.nn.gelu(jnp.einsum("...lcd,lcdh->...ch", x, w1))
    return jnp.einsum("...ch,chd->...cd", h, w2)


def nsa_select(p_cmp, qpos, n_blocks):
    imp = jnp.sum(p_cmp, axis=1)
    blk = jnp.arange(n_blocks)[None, :]
    cur = (qpos // NSA_BLK)[:, None]
    imp = jnp.where(blk == cur, float(NSA_HEADS + 1), jnp.where(blk < cur, imp, -1.0))
    picks = []
    for _ in range(min(NSA_TOPN, n_blocks)):
        i = jnp.argmax(imp, axis=-1)
        picks.append(i)
        imp = jnp.where(blk == i[..., None], -jnp.inf, imp)
    return jnp.stack(picks, axis=-1)


def nsa_compressed_branch(q, qpos, kc, vc, c_end, slopes):
    dist = qpos[:, None] - c_end[None, :]
    o, p = attend(q, kc[:, :, None, :], vc[:, :, None, :], dist, dist >= 0, slopes)
    return o, nsa_select(p, qpos, kc.shape[1])


def nsa_gate(gates, o_cmp, o_slc, o_win):
    g = gates.astype(o_cmp.dtype)
    return g[..., 0:1] * o_cmp + g[..., 1:2] * o_slc + g[..., 2:3] * o_win


CMP_PAGES_PER_STEP = 64
PAGED_VMEM_LIMIT_BYTES = 56 * 1024 * 1024


def pages_feature_major(cache):
    return jnp.transpose(cache, (0, 1, 3, 4, 2)).reshape(-1, 2, NSA_DK, PAGE_SIZE)


def _nsa_compress_kernel(pt_ref, pos_ref, w1b_ref, w2_ref, *rest, n_pages):
    page_refs = rest[:n_pages]
    o0_ref, o1_ref, slab = rest[n_pages:]
    feat = 2 * NSA_DK
    for j, r in enumerate(page_refs):
        slab[j * feat:(j + 1) * feat, :] = r[0].reshape(feat, PAGE_SIZE)
    hs = []
    for c in range(2):
        xs = [(slab[pl.ds(c * NSA_DK + d, n_pages, stride=feat), :] + pos_ref[c * NSA_DK + d:c * NSA_DK + d + 1, :]
               ).astype(jnp.bfloat16) for d in range(NSA_DK)]
        h = jnp.dot(jnp.concatenate(xs, axis=1), w1b_ref[c], preferred_element_type=jnp.float32)
        hs.append(jax.nn.gelu(h).astype(jnp.bfloat16))
    for j, o_ref in enumerate((o0_ref, o1_ref)):
        hj = jnp.concatenate([hs[c][:, j * NSA_CMP_HID:(j + 1) * NSA_CMP_HID] for c in range(2)], axis=1)
        o_ref[0] = jnp.dot(hj, w2_ref[...], preferred_element_type=jnp.float32)


def nsa_compress_paged(pages_t, page_table, pos_emb, w1, w2):
    S, P = page_table.shape
    PG = math.gcd(CMP_PAGES_PER_STEP, P)
    assert PAGE_SIZE == 2 * NSA_BLK and PG % 8 == 0
    bf = jnp.bfloat16
    feat = 2 * NSA_DK
    pos_t = jnp.tile(jnp.transpose(pos_emb, (1, 2, 0)).reshape(feat, NSA_BLK), (1, 2))
    w1f = jnp.transpose(w1, (1, 2, 0, 3)).reshape(feat, NSA_BLK, NSA_CMP_HID)
    z = jnp.zeros_like(w1f)
    w1b = jnp.concatenate([jnp.concatenate([w1f, z], axis=2), jnp.concatenate([z, w1f], axis=2)], axis=1).astype(bf)
    w1b = w1b.reshape(2, NSA_DK * PAGE_SIZE, 2 * NSA_CMP_HID)
    eye = jnp.eye(2, dtype=w2.dtype)
    w2z = jnp.einsum("chd,ce->ched", w2, eye).reshape(2 * NSA_CMP_HID, feat).astype(bf)

    def page_spec(j):
        return pl.BlockSpec((1, 2, NSA_DK, PAGE_SIZE), lambda s, g, pt: (pt[s, g * PG + j], 0, 0, 0))

    grid_spec = pltpu.PrefetchScalarGridSpec(
        num_scalar_prefetch=1,
        grid=(S, P // PG),
        in_specs=[pl.BlockSpec(pos_t.shape, lambda s, g, pt: (0, 0)),
                  pl.BlockSpec(w1b.shape, lambda s, g, pt: (0, 0, 0), pipeline_mode=pl.Buffered(1)),
                  pl.BlockSpec(w2z.shape, lambda s, g, pt: (0, 0))] + [page_spec(j) for j in range(PG)],
        out_specs=[pl.BlockSpec((1, PG, feat), lambda s, g, pt: (s, g, 0)),
                   pl.BlockSpec((1, PG, feat), lambda s, g, pt: (s, g, 0))],
        scratch_shapes=[pltpu.VMEM((PG * feat, PAGE_SIZE), jnp.float32)])
    o0, o1 = pl.pallas_call(
        functools.partial(_nsa_compress_kernel, n_pages=PG),
        grid_spec=grid_spec,
        out_shape=[jax.ShapeDtypeStruct((S, P, feat), jnp.float32)] * 2,
        compiler_params=pltpu.CompilerParams(dimension_semantics=("arbitrary", "arbitrary"),
                                             vmem_limit_bytes=PAGED_VMEM_LIMIT_BYTES),
        name="nsa_compress",
    )(page_table, pos_t, w1b, w2z, *([pages_t] * PG))
    return jnp.stack([o0, o1], axis=2).reshape(S, 2 * P, 2, NSA_DK)


def nsa_prompt(q, cmp_kv, slc_kv, win_kv, gates, ep, slopes):
    B, T = q.shape[:2]
    nb = T // NSA_BLK
    ppb = T // PAGE_SIZE
    prompt_pages = jnp.arange(B * ppb, dtype=jnp.int32).reshape(B, ppb)
    comp = nsa_compress_paged(pages_feature_major(cmp_kv.reshape(1, B * ppb, PAGE_SIZE, 2, NSA_DK)), prompt_pages,
                              ep["cmp_pos"], ep["cmp_w1"], ep["cmp_w2"])
    kc = rmsnorm(comp[:, :, 0], ep["g_k"][0])
    vc = comp[:, :, 1]
    o = nsa_prompt_attention(q, slc_kv[:, :, 0], slc_kv[:, :, 1], win_kv[:, :, 0], win_kv[:, :, 1], kc, vc, gates)
    return jnp.transpose(o, (0, 2, 1, 3))


def _nsa_sample_selected_kernel(phys_ref, idx_ref, q_ref, new_ref, slope_ref, *rest, n_sel, nb_past, qpos):
    page_refs = rest[:n_sel]
    o_ref = rest[n_sel]
    b = pl.program_id(0)
    nt = (((1,), (1,)), ((), ()))
    q = q_ref[0]
    lane = lax.broadcasted_iota(jnp.int32, (1, PAGE_SIZE), 1)
    ss, vts, dists = [], [], []
    for n, r in enumerate(page_refs):
        blk_id = idx_ref[b, n]
        is_new = blk_id >= nb_past
        half = jnp.where(is_new, 0, blk_id % 2)
        kt = jnp.where(is_new, new_ref[0, 0], r[0, 0]).astype(jnp.bfloat16)
        vts.append(jnp.where(is_new, new_ref[0, 1], r[0, 1]).astype(jnp.bfloat16))
        ss.append(jnp.dot(q, kt, preferred_element_type=jnp.float32))
        in_blk = (lane // NSA_BLK) == half
        dists.append(jnp.where(in_blk, qpos - blk_id * NSA_BLK - (lane % NSA_BLK), -1))
    s = jnp.concatenate(ss, axis=1)
    dist = jnp.concatenate(dists, axis=1)
    s = jnp.where(dist >= 0, s - slope_ref[...] * dist.astype(jnp.float32), NEG_INF)
    m = jnp.max(s, axis=-1, keepdims=True)
    e = jnp.where(dist >= 0, jnp.exp(s - m), 0.0)
    p = (e / jnp.maximum(jnp.sum(e, axis=-1, keepdims=True), 1e-30)).astype(jnp.bfloat16)
    o = jnp.zeros((q.shape[0], NSA_DK), jnp.float32)
    for n in range(n_sel):
        o = o + lax.dot_general(p[:, n * PAGE_SIZE:(n + 1) * PAGE_SIZE], vts[n], nt, preferred_element_type=jnp.float32)
    o_ref[0] = o


def nsa_sample_selected(q, idx, slc_new, cache_slc, page_table, e, qpos):
    DB, Tn, H, dk = q.shape
    n_sel = idx.shape[-1]
    assert Tn == 1
    P = page_table.shape[1]
    bpp = PAGE_SIZE // NSA_BLK
    nb_past = P * bpp
    idx2 = idx[:, 0].astype(jnp.int32)
    jp = jnp.minimum(idx2, nb_past - 1)
    phys = jnp.take_along_axis(page_table, jp // bpp, axis=1) + e * cache_slc.shape[1]
    qs = (q[:, 0] * (dk ** -0.5)).astype(jnp.bfloat16)
    new_t = jnp.zeros((DB, 2, dk, PAGE_SIZE), jnp.float32).at[:, :, :, 0].set(slc_new[:, 0])
    kern = functools.partial(_nsa_sample_selected_kernel, n_sel=n_sel, nb_past=nb_past, qpos=int(qpos))

    def page_spec(n):
        return pl.BlockSpec((1, 2, dk, PAGE_SIZE), lambda b, ph, ix: (ph[b, n], 0, 0, 0))

    grid_spec = pltpu.PrefetchScalarGridSpec(
        num_scalar_prefetch=2,
        grid=(DB,),
        in_specs=[pl.BlockSpec((1, H, dk), lambda b, ph, ix: (b, 0, 0)),
                  pl.BlockSpec((1, 2, dk, PAGE_SIZE), lambda b, ph, ix: (b, 0, 0, 0)),
                  pl.BlockSpec((H, 1), lambda b, ph, ix: (0, 0))] + [page_spec(n) for n in range(n_sel)],
        out_specs=pl.BlockSpec((1, H, dk), lambda b, ph, ix: (b, 0, 0)))
    out = pl.pallas_call(
        kern,
        grid_spec=grid_spec,
        out_shape=jax.ShapeDtypeStruct((DB, H, dk), jnp.float32),
        compiler_params=pltpu.CompilerParams(dimension_semantics=("arbitrary",), vmem_limit_bytes=VMEM_LIMIT_BYTES),
        name="nsa_sample_selected",
    )(phys, idx2, qs, new_t, jnp.asarray(alibi_slopes_np(H)).reshape(H, 1), *([pages_feature_major(cache_slc)] * n_sel))
    return out[:, None]


def nsa_sample(q, qpos, cmp_new, slc_new, win_new, gates, cache_cmp, cache_slc, win_buf, page_table, e, ep, slopes):
    DB, Tn = q.shape[:2]
    n_pages = page_table.shape[1]
    bpp = PAGE_SIZE // NSA_BLK
    nb_past = n_pages * bpp
    past_len = n_pages * PAGE_SIZE
    nb_new = -(-Tn // NSA_BLK)
    pad = ((0, 0), (0, nb_new * NSA_BLK - Tn), (0, 0), (0, 0))

    def compress(rows):
        return nsa_compress(rows, ep["cmp_pos"], ep["cmp_w1"], ep["cmp_w2"])

    comp_past = nsa_compress_paged(pages_feature_major(cache_cmp), page_table + e * cache_cmp.shape[1],
                                   ep["cmp_pos"], ep["cmp_w1"], ep["cmp_w2"])
    comp_new = compress(jnp.pad(cmp_new, pad).reshape(DB, nb_new, NSA_BLK, 2, NSA_DK))
    comp = jnp.concatenate([comp_past, comp_new], axis=1)
    kc = rmsnorm(comp[:, :, 0], ep["g_k"][0])
    vc = comp[:, :, 1]
    c_end = (jnp.arange(nb_past + nb_new) + 1) * NSA_BLK - 1
    o_cmp, idx = nsa_compressed_branch(q, qpos, kc, vc, c_end, slopes)

    o_slc = nsa_sample_selected(q, idx, slc_new, cache_slc, page_table, e, past_len)

    o_win = window_sample(q, win_buf[:, :, 0:1], win_buf[:, :, 1:2], win_new[:, :, 0:1], win_new[:, :, 1:2],
                          past_len, NSA_WIN, slopes)
    return nsa_gate(gates, o_cmp, o_slc, o_win)


def mla_keys(lat, ep):
    c = lat[..., :MLA_LORA]
    kp = lat[..., MLA_LORA:]
    kn = rmsnorm(jnp.einsum("bkc,chd->bkhd", c, ep["w_uk"]), ep["g_kn"])
    v = jnp.einsum("bkc,chd->bkhd", c, ep["w_uv"])
    return kn, kp, v


def mla_prompt(qn, qp, lat, ep):
    B, T = qn.shape[:2]
    kn, kp, v = mla_keys(lat, ep)
    return jnp.transpose(mla_prompt_attention(qn, qp, kn, kp, v), (0, 2, 1, 3))


MLA_PAGES_PER_STEP = 64


def _mla_sample_kernel(pt_ref, qg_ref, qp_ref, wukt_ref, wuv_ref, gt_ref, new_ref, *rest, scale, n_pages):
    page_refs = rest[:n_pages]
    o_ref, c_scr, kp_scr, a_scr, m_scr, l_scr, acc_scr = rest[n_pages:]
    g = pl.program_id(1)
    ng = pl.num_programs(1)
    nt = (((1,), (1,)), ((), ()))

    @pl.when(g == 0)
    def _():
        a_scr[...] = lax.dot_general(qg_ref[0], wukt_ref[...], (((1,), (0,)), ((), ())),
                                     preferred_element_type=jnp.float32).astype(jnp.bfloat16)
        m_scr[...] = jnp.full(m_scr.shape, NEG_INF, jnp.float32)
        l_scr[...] = jnp.zeros(l_scr.shape, jnp.float32)
        acc_scr[...] = jnp.zeros(acc_scr.shape, jnp.float32)

    def scores(ct, kpt):
        projt = jnp.dot(wukt_ref[...], ct, preferred_element_type=jnp.float32)
        ss = jnp.dot(gt_ref[...], (projt * projt).astype(jnp.bfloat16), preferred_element_type=jnp.float32)
        num = jnp.dot(a_scr[...], ct, preferred_element_type=jnp.float32)
        sp = jnp.dot(qp_ref[0], kpt, preferred_element_type=jnp.float32)
        return (num * lax.rsqrt(ss * (1.0 / MLA_NOPE) + EPS) + sp) * scale

    def softmax_update(ct, st):
        m_i = m_scr[...]
        m_new = jnp.maximum(m_i, jnp.max(st, axis=-1, keepdims=True))
        corr = jnp.exp(m_i - m_new)
        p = jnp.exp(st - m_new)
        m_scr[...] = m_new
        l_scr[...] = l_scr[...] * corr + jnp.sum(p, axis=-1, keepdims=True)
        acc_scr[...] = acc_scr[...] * corr + lax.dot_general(p.astype(jnp.bfloat16), ct, nt,
                                                             preferred_element_type=jnp.float32)

    for j, r in enumerate(page_refs):
        page = r[0]
        c_scr[:, j * PAGE_SIZE:(j + 1) * PAGE_SIZE] = page[:MLA_LORA, :].astype(jnp.bfloat16)
        kp_scr[:, j * PAGE_SIZE:(j + 1) * PAGE_SIZE] = page[MLA_LORA:, :].astype(jnp.bfloat16)
    c_all = c_scr[...]
    softmax_update(c_all, scores(c_all, kp_scr[...]))

    @pl.when(g == ng - 1)
    def _():
        new = new_ref[0]
        ct = new[:MLA_LORA, :].astype(jnp.bfloat16)
        st = scores(ct, new[MLA_LORA:, :].astype(jnp.bfloat16))
        key = lax.broadcasted_iota(jnp.int32, st.shape, 1)
        softmax_update(ct, jnp.where(key == 0, st, NEG_INF))
        o8 = jnp.dot((acc_scr[...] / l_scr[...]).astype(jnp.bfloat16), wuv_ref[...],
                     preferred_element_type=jnp.float32)
        row = lax.broadcasted_iota(jnp.int32, o8.shape, 0)
        col = lax.broadcasted_iota(jnp.int32, o8.shape, 1)
        o_ref[0] = jnp.sum(jnp.where(col // MLA_V == row, o8, 0.0), axis=0, keepdims=True)


def mla_sample(qn, qp, lat_new, cache_mla, page_table, e, ep):
    DB, Tn = qn.shape[:2]
    H = MLA_HEADS
    P = page_table.shape[1]
    PG = math.gcd(MLA_PAGES_PER_STEP, P)
    assert Tn == 1
    bf = jnp.bfloat16
    dq = MLA_NOPE + MLA_ROPE
    qg = qn[:, 0] * ep["g_kn"]
    eye = jnp.eye(H, dtype=qg.dtype)
    qg_exp = (qg[:, :, None, :] * eye[None, :, :, None]).reshape(DB, H, H * MLA_NOPE).astype(bf)
    qp_h = qp[:, 0].astype(bf)
    wukt = ep["w_uk"].reshape(MLA_LORA, H * MLA_NOPE).T.astype(bf)
    wuv = ep["w_uv"].reshape(MLA_LORA, H * MLA_V).astype(bf)
    gt = np.zeros((H, H * MLA_NOPE), np.float32)
    gt[np.arange(H * MLA_NOPE) // MLA_NOPE, np.arange(H * MLA_NOPE)] = 1.0
    pages_t = jnp.swapaxes(cache_mla, -1, -2).reshape(-1, MLA_ROW, PAGE_SIZE)
    new_t = jnp.zeros((DB, MLA_ROW, PAGE_SIZE), jnp.float32).at[:, :, 0].set(lat_new[:, 0])
    kern = functools.partial(_mla_sample_kernel, scale=float(dq ** -0.5), n_pages=PG)

    def page_spec(j):
        return pl.BlockSpec((1, MLA_ROW, PAGE_SIZE), lambda b, g, pt: (pt[b, g * PG + j], 0, 0))

    const2 = lambda b, g, pt: (0, 0)
    grid_spec = pltpu.PrefetchScalarGridSpec(
        num_scalar_prefetch=1,
        grid=(DB, P // PG),
        in_specs=[pl.BlockSpec((1, H, H * MLA_NOPE), lambda b, g, pt: (b, 0, 0)),
                  pl.BlockSpec((1, H, MLA_ROPE), lambda b, g, pt: (b, 0, 0)),
                  pl.BlockSpec((H * MLA_NOPE, MLA_LORA), const2),
                  pl.BlockSpec((MLA_LORA, H * MLA_V), const2),
                  pl.BlockSpec((H, H * MLA_NOPE), const2),
                  pl.BlockSpec((1, MLA_ROW, PAGE_SIZE), lambda b, g, pt: (b, 0, 0))]
                 + [page_spec(j) for j in range(PG)],
        out_specs=pl.BlockSpec((1, 1, H * MLA_V), lambda b, g, pt: (b, 0, 0)),
        scratch_shapes=[pltpu.VMEM((MLA_LORA, PG * PAGE_SIZE), bf),
                        pltpu.VMEM((MLA_ROPE, PG * PAGE_SIZE), bf),
                        pltpu.VMEM((H, MLA_LORA), bf),
                        pltpu.VMEM((8, 1), jnp.float32),
                        pltpu.VMEM((8, 1), jnp.float32),
                        pltpu.VMEM((8, MLA_LORA), jnp.float32)])
    out = pl.pallas_call(
        kern,
        grid_spec=grid_spec,
        out_shape=jax.ShapeDtypeStruct((DB, 1, H * MLA_V), jnp.float32),
        compiler_params=pltpu.CompilerParams(dimension_semantics=("arbitrary", "arbitrary"),
                                             vmem_limit_bytes=VMEM_LIMIT_BYTES),
        name="mla_sample",
    )(page_table + e * cache_mla.shape[1], qg_exp, qp_h, wukt, wuv, jnp.asarray(gt).astype(bf), new_t,
      *([pages_t] * PG))
    return out.reshape(DB, 1, H, MLA_V)


def even_project(x, g_norm, pos, ep):
    B, T, D = x.shape
    h = norm_matmul(x.reshape(B * T, D), g_norm, ep["w_in"]).reshape(B, T, EVEN_IN)
    cuts = np.cumsum([NSA_Q_COLS, NSA_KV_COLS, NSA_GATE_COLS, MLA_Q_COLS]).tolist()
    q, kv, g, mq, lat = jnp.split(h, cuts, axis=-1)
    q = rmsnorm(q.reshape(B, T, NSA_HEADS, NSA_DK), ep["g_q"])
    kv = kv.reshape(B, T, 3, 2, NSA_DK)
    cmp_kv = kv[:, :, 0]
    slc_kv = jnp.stack([rmsnorm(kv[:, :, 1, 0], ep["g_k"][1]), kv[:, :, 1, 1]], axis=2)
    win_kv = jnp.stack([rmsnorm(kv[:, :, 2, 0], ep["g_k"][2]), kv[:, :, 2, 1]], axis=2)
    gates = jax.nn.sigmoid(g.astype(jnp.float32)).reshape(B, T, NSA_HEADS, 3)
    mq = mq.reshape(B, T, MLA_HEADS, MLA_NOPE + MLA_ROPE)
    qn = rmsnorm(mq[..., :MLA_NOPE], ep["g_qn"])
    qp = rope(rmsnorm(mq[..., MLA_NOPE:], ep["g_qpe"]), pos)
    c = rmsnorm(lat[..., :MLA_LORA], ep["g_lat"])
    kp = rope(rmsnorm(lat[..., MLA_LORA:], ep["g_kpe"])[:, :, None, :], pos)[:, :, 0, :]
    return q, cmp_kv, slc_kv, win_kv, gates, qn, qp, jnp.concatenate([c, kp], axis=-1)


def even_output(x, o_nsa, o_mla, w_out):
    B, T, D = x.shape
    a = jnp.concatenate([o_nsa.reshape(B, T, -1), o_mla.reshape(B, T, -1)], axis=-1)
    return matmul_res(a.reshape(B * T, EVEN_OUT), w_out, x.reshape(B * T, D)).reshape(B, T, D)


def odd_project(x, g_norm, op):
    B, T, D = x.shape
    h = norm_matmul(x.reshape(B * T, D), g_norm, op["w_in"]).reshape(B, T, ODD_IN)
    q, k, v = jnp.split(h, [SWA_HEADS * SWA_HD, SWA_HEADS * SWA_HD + SWA_KV_HEADS * SWA_HD], axis=-1)
    q = rmsnorm(q.reshape(B, T, SWA_HEADS, SWA_HD), op["g_q"])
    k = rmsnorm(k.reshape(B, T, SWA_KV_HEADS, SWA_HD), op["g_k"])
    v = v.reshape(B, T, SWA_KV_HEADS, SWA_HD)
    return q, jnp.stack([k, v], axis=2)


def memory_kv(mem, cp):
    B, M, D = mem.shape
    w_kv = jnp.concatenate([cp["w_k"], cp["w_v"]], axis=1)
    kv = norm_matmul(mem.reshape(B * M, D), cp["g_mem"], w_kv).reshape(B, M, 2, MEM_HEADS, MEM_HD)
    k = rmsnorm(kv[:, :, 0], cp["g_k"])
    return jnp.stack([k, kv[:, :, 1]], axis=2)


MEM_TQ = 256


def _memory_cross_kernel(x_ref, g_ref, wq_ref, gq_ref, k_ref, v_ref, wo_ref, o_ref):
    x = x_ref[0]
    xn = x * lax.rsqrt(jnp.mean(x * x, axis=-1, keepdims=True) + EPS) * g_ref[...]
    q = jnp.dot(xn.astype(jnp.bfloat16), wq_ref[...], preferred_element_type=jnp.float32)
    nt = (((1,), (1,)), ((), ()))
    outs = []
    for h in range(MEM_HEADS):
        cols = slice(h * MEM_HD, (h + 1) * MEM_HD)
        qh = q[:, cols]
        qh = qh * lax.rsqrt(jnp.mean(qh * qh, axis=-1, keepdims=True) + EPS) * gq_ref[...]
        s = lax.dot_general(qh.astype(jnp.bfloat16), k_ref[0, :, cols], nt,
                            preferred_element_type=jnp.float32) * (MEM_HD ** -0.5)
        e = jnp.exp(s - jnp.max(s, axis=-1, keepdims=True))
        p = e / jnp.sum(e, axis=-1, keepdims=True)
        outs.append(jnp.dot(p.astype(jnp.bfloat16), v_ref[0, :, cols], preferred_element_type=jnp.float32))
    o = jnp.concatenate(outs, axis=1).astype(jnp.bfloat16)
    o_ref[0] = x + jnp.dot(o, wo_ref[...], preferred_element_type=jnp.float32)


def memory_cross_prompt(x, g_norm, mem_kv, cp):
    B, T, D = x.shape
    M = mem_kv.shape[1]
    bf = jnp.bfloat16
    tq = MEM_TQ
    k = mem_kv[:, :, 0].reshape(B, M, MEM_W).astype(bf)
    v = mem_kv[:, :, 1].reshape(B, M, MEM_W).astype(bf)
    return pl.pallas_call(
        _memory_cross_kernel,
        grid=(B, T // tq),
        in_specs=[pl.BlockSpec((1, tq, D), lambda b, i: (b, i, 0)),
                  pl.BlockSpec((1, D), lambda b, i: (0, 0)),
                  pl.BlockSpec((D, MEM_W), lambda b, i: (0, 0)),
                  pl.BlockSpec((1, MEM_HD), lambda b, i: (0, 0)),
                  pl.BlockSpec((1, M, MEM_W), lambda b, i: (b, 0, 0)),
                  pl.BlockSpec((1, M, MEM_W), lambda b, i: (b, 0, 0)),
                  pl.BlockSpec((MEM_W, D), lambda b, i: (0, 0))],
        out_specs=pl.BlockSpec((1, tq, D), lambda b, i: (b, i, 0)),
        out_shape=jax.ShapeDtypeStruct((B, T, D), jnp.float32),
        compiler_params=pltpu.CompilerParams(dimension_semantics=("arbitrary", "arbitrary"),
                                             vmem_limit_bytes=VMEM_LIMIT_BYTES),
        name="memory_cross",
    )(x, g_norm.reshape(1, D), cp["w_q"].astype(bf), cp["g_q"].reshape(1, MEM_HD), k, v, cp["w_o"].astype(bf))


def memory_cross(x, g_norm, mem_kv, cp):
    B, T, D = x.shape
    if T % MEM_TQ == 0:
        return memory_cross_prompt(x, g_norm, mem_kv, cp)
    q = norm_matmul(x.reshape(B * T, D), g_norm, cp["w_q"]).reshape(B, T, MEM_HEADS, MEM_HD)
    q = rmsnorm(q, cp["g_q"])
    s = jnp.einsum("bthd,bmhd->bhtm", q, mem_kv[:, :, 0]).astype(jnp.float32) * (MEM_HD ** -0.5)
    p = jax.nn.softmax(s, axis=-1).astype(x.dtype)
    o = jnp.einsum("bhtm,bmhd->bthd", p, mem_kv[:, :, 1]).reshape(B * T, MEM_W)
    return matmul_res(o, cp["w_o"], x.reshape(B * T, D)).reshape(B, T, D)


PEER_TB = 128
PEER_VMEM_LIMIT_BYTES = 56 * 1024 * 1024
F32_NEG_INF = float("-inf")


def _topk_rows(s, row, k):
    nrow = float(s.shape[0])
    vals, ids = [], []
    for _ in range(k):
        m = jnp.max(s, axis=0, keepdims=True)
        i = jnp.min(jnp.where(s == m, row, nrow), axis=0, keepdims=True)
        vals.append(m)
        ids.append(i)
        s = jnp.where(row == i, F32_NEG_INF, s)
    return vals, ids


def _peer_route_kernel(x_ref, g_ref, wq_ref, sk0_ref, sk1_ref, xn_ref, idx_ref, gate_ref):
    x = x_ref[...]
    xn = x * lax.rsqrt(jnp.mean(x * x, axis=-1, keepdims=True) + EPS) * g_ref[...]
    xn_ref[...] = xn
    q = jnp.dot(xn.astype(jnp.bfloat16), wq_ref[...], preferred_element_type=jnp.float32)
    tb = x.shape[0]
    row_k = lax.broadcasted_iota(jnp.int32, (PEER_NKEYS, tb), 0).astype(jnp.float32)
    sub8 = lax.broadcasted_iota(jnp.int32, (8, tb), 0).astype(jnp.float32)
    assert PEER_TOPK % 16 == 0
    nt = (((1,), (1,)), ((), ()))
    ids, gates = [], []
    for h in range(PEER_HEADS):
        qh = q[:, h * PEER_DKEY:(h + 1) * PEER_DKEY].astype(jnp.bfloat16)
        s1 = lax.dot_general(sk0_ref[...], qh, nt, preferred_element_type=jnp.float32)
        s2 = lax.dot_general(sk1_ref[...], qh, nt, preferred_element_type=jnp.float32)
        v1, i1 = _topk_rows(s1, row_k, PEER_TOPK)
        v2, i2 = _topk_rows(s2, row_k, PEER_TOPK)
        v1c = jnp.concatenate(v1, axis=0)
        i1c = jnp.concatenate(i1, axis=0)
        v2c = jnp.concatenate(v2, axis=0)
        i2c = jnp.concatenate(i2, axis=0)
        cands, poss, cids = [], [], []
        for a in range(PEER_TOPK // 2):
            bmax = PEER_TOPK // (a + 1) - 1
            for b0 in range(0, bmax + 1, 8):
                c = v1[a] + v2c[b0:b0 + 8]
                if bmax - b0 + 1 < 8:
                    c = jnp.where(sub8 <= float(bmax - b0), c, F32_NEG_INF)
                cands.append(c)
                poss.append(sub8 + float(a * PEER_TOPK + b0))
                cids.append(i1[a] * float(PEER_NKEYS) + i2c[b0:b0 + 8])
        for a0 in range(PEER_TOPK // 2, PEER_TOPK, 8):
            cands.append(v1c[a0:a0 + 8] + v2[0])
            poss.append((sub8 + float(a0)) * float(PEER_TOPK))
            cids.append(i1c[a0:a0 + 8] * float(PEER_NKEYS) + i2[0])
        cand = jnp.concatenate(cands, axis=0)
        row_c = jnp.concatenate(poss, axis=0)
        cid = jnp.concatenate(cids, axis=0)
        tops, tids = [], []
        for _ in range(PEER_TOPK):
            m = jnp.max(cand, axis=0, keepdims=True)
            pos = jnp.min(jnp.where(cand == m, row_c, float(PEER_TOPK * PEER_TOPK)), axis=0, keepdims=True)
            hit = row_c == pos
            tids.append(jnp.sum(jnp.where(hit, cid, 0.0), axis=0, keepdims=True))
            tops.append(m)
            cand = jnp.where(hit, F32_NEG_INF, cand)
        ts = jnp.concatenate(tops, axis=0)
        e = jnp.exp(ts - tops[0])
        gates.append(e / jnp.sum(e, axis=0, keepdims=True))
        ids.append(jnp.concatenate(tids, axis=0))
    table_rows = x.shape[1] // 256
    idx_ref[...] = (jnp.concatenate(ids, axis=0) * float(table_rows)).T.astype(jnp.int32)
    gate_ref[...] = jnp.concatenate(gates, axis=0).T


def peer_route(x, g, w_q, subkeys):
    n, d = x.shape
    tb = min(PEER_TB, n)
    hk = PEER_HEADS * PEER_TOPK
    half = PEER_DKEY // 2
    z = jnp.zeros((PEER_NKEYS, half), jnp.float32)
    sk0 = jnp.concatenate([subkeys[0], z], axis=1).astype(jnp.bfloat16)
    sk1 = jnp.concatenate([z, subkeys[1]], axis=1).astype(jnp.bfloat16)
    return pl.pallas_call(
        _peer_route_kernel,
        grid=(n // tb,),
        in_specs=[pl.BlockSpec((tb, d), lambda i: (i, 0)),
                  pl.BlockSpec((1, d), lambda i: (0, 0)),
                  pl.BlockSpec((d, PEER_HEADS * PEER_DKEY), lambda i: (0, 0)),
                  pl.BlockSpec((PEER_NKEYS, PEER_DKEY), lambda i: (0, 0)),
                  pl.BlockSpec((PEER_NKEYS, PEER_DKEY), lambda i: (0, 0))],
        out_specs=[pl.BlockSpec((tb, d), lambda i: (i, 0)),
                   pl.BlockSpec((tb, hk), lambda i: (i, 0)),
                   pl.BlockSpec((tb, hk), lambda i: (i, 0))],
        out_shape=[jax.ShapeDtypeStruct((n, d), jnp.float32),
                   jax.ShapeDtypeStruct((n, hk), jnp.int32),
                   jax.ShapeDtypeStruct((n, hk), jnp.float32)],
        compiler_params=pltpu.CompilerParams(dimension_semantics=("arbitrary",),
                                             vmem_limit_bytes=PEER_VMEM_LIMIT_BYTES),
        name="peer_route",
    )(x, g.reshape(1, d), w_q.astype(jnp.bfloat16), sk0, sk1)


def pack_table(t):
    e, d = t.shape
    b = lax.bitcast_convert_type(t.astype(jnp.bfloat16), jnp.uint16).astype(jnp.uint32)
    w = b[:, : d // 2] | (b[:, d // 2:] << 16)
    return w.reshape(e * d // 256, 128)


def _table_spec(tab):
    return pl.BlockSpec(tab.shape, lambda i: (0, 0), pipeline_mode=pl.Buffered(1))


def _gather_row(tab, row0, rows):
    wds = tab[pl.ds(pl.multiple_of(row0, rows), rows), :]
    lo = pltpu.bitcast(wds << 16, jnp.float32)
    hi = pltpu.bitcast(wds & jnp.uint32(0xFFFF0000), jnp.float32)
    return lo, hi


def _peer_u_kernel(idx_ref, xn_ref, gate_ref, tab, w_ref, slots, rsum, act):
    tb, hk = gate_ref.shape
    rows = xn_ref.shape[1] // 2

    def token(t, c):
        xt = xn_ref[t]
        xlo, xhi = xt[:rows], xt[rows:]
        for k in range(hk):
            lo, hi = _gather_row(tab, idx_ref[t, k], rows)
            slots[k * rows:(k + 1) * rows, :] = lo * xlo + hi * xhi
        r = slots[pl.ds(0, hk, stride=rows), :]
        for s in range(1, rows):
            r = r + slots[pl.ds(s, hk, stride=rows), :]
        rsum[pl.ds(pl.multiple_of(t * hk, hk), hk), :] = r
        return c

    lax.fori_loop(0, tb, token, 0)

    grp = 8

    def lane_sum(c, carry):
        sums = []
        for i in range(grp):
            rr = rsum[pl.ds(pl.multiple_of((c * grp + i) * hk, hk), hk), :]
            sums.append(jnp.sum(rr.T, axis=0, keepdims=True))
        act[pl.ds(pl.multiple_of(c * grp, grp), grp), :] = jnp.concatenate(sums, axis=0)
        return carry

    lax.fori_loop(0, tb // grp, lane_sum, 0)
    w_ref[...] = gate_ref[...] * jax.nn.gelu(act[...])


def peer_activate(idx, xn, gate, tab):
    n, d = xn.shape
    hk = idx.shape[1]
    tb = min(PEER_TB, n)
    sub = d // 128
    return pl.pallas_call(
        _peer_u_kernel,
        grid=(n // tb,),
        in_specs=[pl.BlockSpec((tb, hk), lambda i: (i, 0), memory_space=pltpu.SMEM),
                  pl.BlockSpec((tb, sub, 128), lambda i: (i, 0, 0)),
                  pl.BlockSpec((tb, hk), lambda i: (i, 0)),
                  _table_spec(tab)],
        out_specs=pl.BlockSpec((tb, hk), lambda i: (i, 0)),
        out_shape=jax.ShapeDtypeStruct((n, hk), jnp.float32),
        scratch_shapes=[pltpu.VMEM((hk * sub // 2, 128), jnp.float32),
                        pltpu.VMEM((tb * hk, 128), jnp.float32),
                        pltpu.VMEM((tb, hk), jnp.float32)],
        compiler_params=pltpu.CompilerParams(dimension_semantics=("arbitrary",),
                                             vmem_limit_bytes=PEER_VMEM_LIMIT_BYTES),
        name="peer_u",
    )(idx, xn.reshape(n, sub, 128), gate, tab)


def _peer_v_kernel(idx_ref, w_ref, x_ref, tab, o_ref):
    tb, hk = idx_ref.shape
    rows = x_ref.shape[1] // 2

    def token(t, c):
        nacc = 2
        lo_acc = [jnp.zeros((rows, 128), jnp.float32) for _ in range(nacc)]
        hi_acc = [jnp.zeros((rows, 128), jnp.float32) for _ in range(nacc)]
        for k in range(hk):
            lo, hi = _gather_row(tab, idx_ref[t, k], rows)
            wk = w_ref[t, k]
            lo_acc[k % nacc] = lo_acc[k % nacc] + wk * lo
            hi_acc[k % nacc] = hi_acc[k % nacc] + wk * hi
        xt = x_ref[t]
        o_ref[t, :rows, :] = xt[:rows] + (lo_acc[0] + lo_acc[1])
        o_ref[t, rows:, :] = xt[rows:] + (hi_acc[0] + hi_acc[1])
        return c

    lax.fori_loop(0, tb, token, 0)


def peer_combine(idx, w, x, tab):
    n, d = x.shape
    hk = idx.shape[1]
    tb = min(PEER_TB, n)
    sub = d // 128
    out = pl.pallas_call(
        _peer_v_kernel,
        grid=(n // tb,),
        in_specs=[pl.BlockSpec((tb, hk), lambda i: (i, 0), memory_space=pltpu.SMEM),
                  pl.BlockSpec((tb, hk), lambda i: (i, 0), memory_space=pltpu.SMEM),
                  pl.BlockSpec((tb, sub, 128), lambda i: (i, 0, 0)),
                  _table_spec(tab)],
        out_specs=pl.BlockSpec((tb, sub, 128), lambda i: (i, 0, 0)),
        out_shape=jax.ShapeDtypeStruct((n, sub, 128), jnp.float32),
        compiler_params=pltpu.CompilerParams(dimension_semantics=("arbitrary",),
                                             vmem_limit_bytes=PEER_VMEM_LIMIT_BYTES),
        name="peer_v",
    )(idx, w, x.reshape(n, sub, 128), tab)
    return out.reshape(n, d)


def peer_ffn(x, g_norm, pp):
    B, T, D = x.shape
    x2 = x.reshape(B * T, D)
    xn, idx, gate = peer_route(x2, g_norm, pp["w_q"], pp["subkeys"])
    w = peer_activate(idx, xn, gate, pp["u_packed"])
    return peer_combine(idx, w, x2, pp["v_packed"]).reshape(B, T, D)


def kernel(x_prompt, x_sample, mem_prompt, cache_nsa_cmp, cache_nsa_slc, cache_nsa_win, cache_mla, cache_swa,
           cache_mem, page_table, norm_mix, norm_cross, norm_ffn, even_w_in, even_w_out, nsa_g_q, nsa_g_k,
           nsa_cmp_pos, nsa_cmp_w1, nsa_cmp_w2, mla_g_qn, mla_g_qpe, mla_g_lat, mla_g_kpe, mla_g_kn, mla_w_uk,
           mla_w_uv, odd_w_in, odd_w_out, swa_g_q, swa_g_k, swa_sinks, mem_g, mem_w_q, mem_w_k, mem_w_v,
           mem_g_q, mem_g_k, mem_w_o, peer_w_q, peer_subkeys, peer_u, peer_v):
    depth = norm_mix.shape[0]
    slopes_nsa = alibi_slopes(NSA_HEADS)
    slopes_swa = alibi_slopes(SWA_HEADS)
    past_len = page_table.shape[1] * PAGE_SIZE
    T, Tn = x_prompt.shape[1], x_sample.shape[1]
    pos_p = jnp.arange(T)
    pos_s = past_len + jnp.arange(Tn)
    xp, xs = x_prompt, x_sample
    st_cmp_p, st_cmp_s, st_slc_p, st_slc_s, st_win_p, st_win_s = [], [], [], [], [], []
    st_mla_p, st_mla_s, st_swa_p, st_swa_s, st_mem_p = [], [], [], [], []

    for li in range(depth):
        if li % 2 == 0:
            e = li // 2
            ep = dict(w_in=even_w_in[e], g_q=nsa_g_q[e], g_k=nsa_g_k[e], cmp_pos=nsa_cmp_pos[e],
                      cmp_w1=nsa_cmp_w1[e], cmp_w2=nsa_cmp_w2[e], g_qn=mla_g_qn[e], g_qpe=mla_g_qpe[e],
                      g_lat=mla_g_lat[e], g_kpe=mla_g_kpe[e], g_kn=mla_g_kn[e], w_uk=mla_w_uk[e], w_uv=mla_w_uv[e])
            q, cmp_kv, slc_kv, win_kv, gates, qn, qp, mla_row = even_project(xp, norm_mix[li], pos_p, ep)
            o_nsa = nsa_prompt(q, cmp_kv, slc_kv, win_kv, gates, ep, slopes_nsa)
            o_mla = mla_prompt(qn, qp, mla_row, ep)
            xp = even_output(xp, o_nsa, o_mla, even_w_out[e])
            st_cmp_p.append(cmp_kv)
            st_slc_p.append(slc_kv)
            st_win_p.append(win_kv[:, -min(NSA_WIN, T):])
            st_mla_p.append(mla_row)
            win_buf = cache_nsa_win[e]
            q, cmp_kv, slc_kv, win_kv, gates, qn, qp, mla_row = even_project(xs, norm_mix[li], pos_s, ep)
            o_nsa = nsa_sample(q, pos_s, cmp_kv, slc_kv, win_kv, gates, cache_nsa_cmp, cache_nsa_slc, win_buf,
                               page_table, e, ep, slopes_nsa)
            o_mla = mla_sample(qn, qp, mla_row, cache_mla, page_table, e, ep)
            xs = even_output(xs, o_nsa, o_mla, even_w_out[e])
            st_cmp_s.append(cmp_kv)
            st_slc_s.append(slc_kv)
            st_win_s.append(jnp.concatenate([win_buf, win_kv], axis=1)[:, -win_buf.shape[1]:])
            st_mla_s.append(mla_row)
        else:
            o = li // 2
            op = dict(w_in=odd_w_in[o], g_q=swa_g_q[o], g_k=swa_g_k[o])
            B = xp.shape[0]
            q, kv = odd_project(xp, norm_mix[li], op)
            att = jnp.transpose(swa_prompt_attention(q, kv[:, :, 0], kv[:, :, 1], swa_sinks[o]), (0, 2, 1, 3))
            xp = matmul_res(att.reshape(B * T, ODD_OUT), odd_w_out[o], xp.reshape(B * T, -1)).reshape(xp.shape)
            st_swa_p.append(kv[:, -min(SWA_WIN, T):])
            buf = cache_swa[o]
            q, kv = odd_project(xs, norm_mix[li], op)
            att = window_sample(q, buf[:, :, 0], buf[:, :, 1], kv[:, :, 0], kv[:, :, 1], past_len, SWA_WIN,
                                slopes_swa, swa_sinks[o])
            xs = matmul_res(att.reshape(xs.shape[0] * Tn, ODD_OUT), odd_w_out[o],
                            xs.reshape(xs.shape[0] * Tn, -1)).reshape(xs.shape)
            st_swa_s.append(jnp.concatenate([buf, kv], axis=1)[:, -buf.shape[1]:])

        cp = dict(g_mem=mem_g[li], w_q=mem_w_q[li], w_k=mem_w_k[li], w_v=mem_w_v[li], g_q=mem_g_q[li],
                  g_k=mem_g_k[li], w_o=mem_w_o[li])
        mem_kv_p = memory_kv(mem_prompt, cp)
        xp = memory_cross(xp, norm_cross[li], mem_kv_p, cp)
        xs = memory_cross(xs, norm_cross[li], cache_mem[li], cp)
        st_mem_p.append(mem_kv_p)

        pp = dict(w_q=peer_w_q[li], subkeys=peer_subkeys[li],
                  u_packed=pack_table(peer_u[li]), v_packed=pack_table(peer_v[li]))
        xp = peer_ffn(xp, norm_ffn[li], pp)
        xs = peer_ffn(xs, norm_ffn[li], pp)

    return (xp, xs, jnp.stack(st_cmp_p), jnp.stack(st_cmp_s), jnp.stack(st_slc_p), jnp.stack(st_slc_s),
            jnp.stack(st_win_p), jnp.stack(st_win_s), jnp.stack(st_mla_p), jnp.stack(st_mla_s),
            jnp.stack(st_swa_p), jnp.stack(st_swa_s), jnp.stack(st_mem_p))
```

```python
import functools
import math

import jax
import jax.numpy as jnp
import numpy as np
from jax import lax
from jax.experimental import pallas as pl
from jax.experimental.pallas import tpu as pltpu

PAGE_SIZE = 128
EPS = 1e-6
NEG_INF = -1e30
ROPE_BASE = 10000.0

NSA_HEADS = 8
NSA_DK = 64
NSA_BLK = 64
NSA_TOPN = 16
NSA_WIN = 512
NSA_CMP_HID = 256
MLA_HEADS = 8
MLA_NOPE = 64
MLA_ROPE = 32
MLA_V = 64
MLA_LORA = 128
MLA_ROW = MLA_LORA + MLA_ROPE
SWA_HEADS = 16
SWA_KV_HEADS = 4
SWA_HD = 64
SWA_WIN = 128
MEM_HEADS = 4
MEM_HD = 128
MEM_W = MEM_HEADS * MEM_HD
PEER_HEADS = 8
PEER_NKEYS = 128
PEER_DKEY = 128
PEER_TOPK = 16

NSA_Q_COLS = NSA_HEADS * NSA_DK
NSA_KV_COLS = 3 * 2 * NSA_DK
NSA_GATE_COLS = NSA_HEADS * 3
MLA_Q_COLS = MLA_HEADS * (MLA_NOPE + MLA_ROPE)
EVEN_IN = NSA_Q_COLS + NSA_KV_COLS + NSA_GATE_COLS + MLA_Q_COLS + MLA_ROW
EVEN_OUT = NSA_HEADS * NSA_DK + MLA_HEADS * MLA_V
ODD_IN = SWA_HEADS * SWA_HD + 2 * SWA_KV_HEADS * SWA_HD
ODD_OUT = SWA_HEADS * SWA_HD

VMEM_LIMIT_BYTES = 48 * 1024 * 1024


def _row_tile(n, target=512):
    t = min(n, target)
    while n % t:
        t //= 2
    return t


def _norm_matmul_kernel(x_ref, g_ref, w_ref, o_ref):
    x = x_ref[...]
    y = x * lax.rsqrt(jnp.mean(x * x, axis=-1, keepdims=True) + EPS) * g_ref[...]
    o_ref[...] = jnp.dot(y.astype(jnp.bfloat16), w_ref[...], preferred_element_type=jnp.float32)


def norm_matmul(x, g, w):
    n, d = x.shape
    c = w.shape[1]
    tm = _row_tile(n)
    return pl.pallas_call(
        _norm_matmul_kernel,
        grid=(n // tm,),
        in_specs=[pl.BlockSpec((tm, d), lambda i: (i, 0)),
                  pl.BlockSpec((1, d), lambda i: (0, 0)),
                  pl.BlockSpec((d, c), lambda i: (0, 0))],
        out_specs=pl.BlockSpec((tm, c), lambda i: (i, 0)),
        out_shape=jax.ShapeDtypeStruct((n, c), jnp.float32),
        compiler_params=pltpu.CompilerParams(dimension_semantics=("arbitrary",),
                                             vmem_limit_bytes=VMEM_LIMIT_BYTES),
        name="norm_matmul",
    )(x, g.reshape(1, d), w.astype(jnp.bfloat16))


def _matmul_res_kernel(a_ref, w_ref, r_ref, o_ref):
    o_ref[...] = r_ref[...] + jnp.dot(a_ref[...].astype(jnp.bfloat16), w_ref[...],
                                      preferred_element_type=jnp.float32)


def matmul_res(a, w, res):
    n, k = a.shape
    c = w.shape[1]
    tm = _row_tile(n)
    return pl.pallas_call(
        _matmul_res_kernel,
        grid=(n // tm,),
        in_specs=[pl.BlockSpec((tm, k), lambda i: (i, 0)),
                  pl.BlockSpec((k, c), lambda i: (0, 0)),
                  pl.BlockSpec((tm, c), lambda i: (i, 0))],
        out_specs=pl.BlockSpec((tm, c), lambda i: (i, 0)),
        out_shape=jax.ShapeDtypeStruct((n, c), jnp.float32),
        compiler_params=pltpu.CompilerParams(dimension_semantics=("arbitrary",),
                                             vmem_limit_bytes=VMEM_LIMIT_BYTES),
        name="matmul_res",
    )(a, w.astype(jnp.bfloat16), res)


def rmsnorm(x, g):
    xf = x.astype(jnp.float32)
    y = xf * lax.rsqrt(jnp.mean(xf * xf, axis=-1, keepdims=True) + EPS)
    return (y * g.astype(jnp.float32)).astype(x.dtype)


def alibi_slopes(n_heads):
    return jnp.asarray((2.0 ** (-8.0 * np.arange(1, n_heads + 1) / n_heads)).astype(np.float32))


def rope(x, pos):
    d = x.shape[-1]
    inv = jnp.asarray(np.power(ROPE_BASE, -np.arange(0, d, 2, dtype=np.float32) / d).astype(np.float32))
    ang = pos.astype(jnp.float32)[:, None] * inv[None, :]
    cos = jnp.cos(ang)[None, :, None, :]
    sin = jnp.sin(ang)[None, :, None, :]
    xf = x.astype(jnp.float32)
    x1, x2 = xf[..., : d // 2], xf[..., d // 2:]
    return jnp.concatenate([x1 * cos - x2 * sin, x1 * sin + x2 * cos], axis=-1).astype(x.dtype)


def masked_softmax(s, valid, sink=None):
    s = jnp.where(valid, s, NEG_INF)
    m = jnp.max(s, axis=-1, keepdims=True)
    if sink is not None:
        sk = sink.astype(jnp.float32)[:, None, None]
        m = jnp.maximum(m, sk)
    e = jnp.where(valid, jnp.exp(s - m), 0.0)
    den = jnp.sum(e, axis=-1, keepdims=True)
    if sink is not None:
        den = den + jnp.exp(sk - m)
    return e / jnp.maximum(den, 1e-30)


def attend(q, k, v, dist, valid, slopes, sink=None):
    B, Q, H, d = q.shape
    G = k.shape[-2]
    qg = q.reshape(B, Q, G, H // G, d)
    if k.ndim == 5:
        s = jnp.einsum("bqgrd,bqkgd->bgrqk", qg, k)
    else:
        s = jnp.einsum("bqgrd,bkgd->bgrqk", qg, k)
    K = s.shape[-1]
    s = s.reshape(B, H, Q, K).astype(jnp.float32) * (d ** -0.5)
    if slopes is not None:
        s = s - slopes[:, None, None] * dist[..., None, :, :].astype(jnp.float32)
    p = masked_softmax(s, valid[..., None, :, :], sink)
    pg = p.reshape(B, G, H // G, Q, K).astype(v.dtype)
    if v.ndim == 5:
        o = jnp.einsum("bgrqk,bqkgd->bqgrd", pg, v)
    else:
        o = jnp.einsum("bgrqk,bkgd->bqgrd", pg, v)
    return o.reshape(B, Q, H, v.shape[-1]), p


def alibi_slopes_np(n_heads):
    return (2.0 ** (-8.0 * np.arange(1, n_heads + 1) / n_heads)).astype(np.float32)


Q_TILE = 128
SLC_CHUNK = 512
MASK_BIG = -(2.0 ** 100)


def _nsa_prompt_kernel(q_ref, kt_ref, vs_ref, kwt_ref, vw_ref, kct_ref, vc_ref, g_ref, slope_ref, wbias_ref, cbias_ref,
                       o_ref, *, n_blocks):
    H, TQ = NSA_HEADS, Q_TILE
    i = pl.program_id(1)
    s0 = i * TQ
    q_lo = q_ref[0].reshape(H * TQ, 128)
    slope = slope_ref[...]

    s = jnp.dot(q_lo, kct_ref[0], preferred_element_type=jnp.float32).reshape(H, TQ, n_blocks)
    tpos = s0 + lax.broadcasted_iota(jnp.int32, (TQ, n_blocks), 0)
    blk = lax.broadcasted_iota(jnp.int32, (TQ, n_blocks), 1)
    dist = tpos - ((blk + 1) * NSA_BLK - 1)
    valid = (dist >= 0)[None]
    s = jnp.where(valid, s - slope * dist.astype(jnp.float32)[None], NEG_INF)
    m = jnp.max(s, axis=-1, keepdims=True)
    e = jnp.where(valid, jnp.exp(s - m), 0.0)
    p = e / jnp.maximum(jnp.sum(e, axis=-1, keepdims=True), 1e-30)
    o_cmp = jnp.dot(p.reshape(H * TQ, n_blocks).astype(jnp.bfloat16), vc_ref[0],
                    preferred_element_type=jnp.float32)
    imp = jnp.sum(p, axis=0)

    cur = tpos // NSA_BLK
    imp = jnp.where(blk == cur, float(H + 1), jnp.where(blk < cur, imp, -1.0))
    imp_t = imp.T
    blk_t = lax.broadcasted_iota(jnp.int32, (n_blocks, TQ), 0).astype(jnp.float32)
    sel_t = jnp.zeros((n_blocks, TQ), jnp.float32)
    for _ in range(min(NSA_TOPN, n_blocks)):
        mx = jnp.max(imp_t, axis=0, keepdims=True)
        pick = jnp.min(jnp.where(imp_t == mx, blk_t, float(n_blocks)), axis=0, keepdims=True)
        hit = blk_t == pick
        sel_t = jnp.where(hit, 1.0, sel_t)
        imp_t = jnp.where(hit, -2.0, imp_t)
    nsb = jnp.where(sel_t.T > 0.5, 0.0, MASK_BIG).astype(jnp.bfloat16)
    q_aug = jnp.concatenate([q_lo, jnp.broadcast_to(nsb[None], (H, TQ, n_blocks)).reshape(H * TQ, n_blocks)], axis=1)

    KC = SLC_CHUNK

    def slc_step(c, carry, bias):
        m_i, acc = carry
        k0 = pl.multiple_of(c * KC, KC)
        sc = jnp.dot(q_aug, kt_ref[0, :, pl.ds(k0, KC)], preferred_element_type=jnp.float32)
        if bias is not None:
            sc = (sc.reshape(H, TQ, KC) + bias[None]).reshape(H * TQ, KC)
        m_new = jnp.maximum(m_i, jnp.max(sc, axis=-1, keepdims=True))
        corr = jnp.exp(m_i - m_new)
        pe = jnp.exp(sc - m_new)
        acc_new = acc * corr + jnp.dot(pe.astype(jnp.bfloat16), vs_ref[0, pl.ds(k0, KC), :],
                                       preferred_element_type=jnp.float32)
        return m_new, acc_new

    init = (jnp.full((H * TQ, 1), NEG_INF, jnp.float32), jnp.zeros((H * TQ, 128), jnp.float32))
    c_diag = s0 // KC
    carry = lax.fori_loop(0, c_diag, lambda c, cr: slc_step(c, cr, None), init)
    m_i, acc = slc_step(c_diag, carry, cbias_ref[(s0 % KC) // TQ])
    o_slc = acc[:, :NSA_DK] / acc[:, NSA_DK:NSA_DK + 1]

    nwc = NSA_WIN // TQ + 1
    parts = []
    for r in range(nwc):
        cidx = i - (nwc - 1) + r
        k0 = pl.multiple_of(jnp.maximum(cidx, 0) * TQ, TQ)
        sw = jnp.dot(q_lo[:, :NSA_DK], kwt_ref[0, :, pl.ds(k0, TQ)], preferred_element_type=jnp.float32)
        b = jnp.where(cidx >= 0, wbias_ref[r], NEG_INF)
        parts.append((sw.reshape(H, TQ, TQ) + b).reshape(H * TQ, TQ))
    mw = parts[0].max(axis=-1, keepdims=True)
    for r in range(1, nwc):
        mw = jnp.maximum(mw, parts[r].max(axis=-1, keepdims=True))
    accw = jnp.zeros((H * TQ, 128), jnp.float32)
    for r in range(nwc):
        cidx = i - (nwc - 1) + r
        k0 = pl.multiple_of(jnp.maximum(cidx, 0) * TQ, TQ)
        pe = jnp.exp(parts[r] - mw)
        accw = accw + jnp.dot(pe.astype(jnp.bfloat16), vw_ref[0, pl.ds(k0, TQ), :], preferred_element_type=jnp.float32)
    o_win = accw[:, :NSA_DK] / accw[:, NSA_DK:NSA_DK + 1]

    g = g_ref[0].reshape(H * TQ, 3)
    o = g[:, 0:1] * o_cmp + g[:, 1:2] * o_slc + g[:, 2:3] * o_win
    o_ref[0] = o.reshape(H, TQ, NSA_DK)


def _with_ones_column(v):
    ones = jnp.ones(v.shape[:-1] + (1,), v.dtype)
    zeros = jnp.zeros(v.shape[:-1] + (127 - v.shape[-1],), v.dtype)
    return jnp.concatenate([v, ones, zeros], axis=-1).astype(jnp.bfloat16)


def nsa_prompt_attention(q, slc_k, slc_v, win_k, win_v, kc, vc, gates):
    B, T, H, dk = q.shape
    nb = T // NSA_BLK
    nbp = 128
    TQ = Q_TILE
    assert nb <= nbp and T % SLC_CHUNK == 0 and NSA_WIN % TQ == 0
    slopes = alibi_slopes_np(H)
    bf = jnp.bfloat16
    qh = jnp.transpose(q, (0, 2, 1, 3)) * (dk ** -0.5)
    al = np.zeros((H, 128 - dk), np.float32)
    al[:, 0] = slopes * 128.0
    al[:, 1] = slopes
    q_lo = jnp.concatenate([qh, jnp.broadcast_to(jnp.asarray(al)[None, :, None, :], (B, H, T, 128 - dk))],
                           axis=-1).astype(bf)
    pos = np.arange(T)
    crow = np.zeros((128 - dk + nbp, T), np.float32)
    crow[0] = pos // 128
    crow[1] = pos % 128
    crow[128 - dk + pos // NSA_BLK, pos] = 1.0
    kt = jnp.concatenate([jnp.transpose(slc_k, (0, 2, 1)),
                          jnp.broadcast_to(jnp.asarray(crow)[None], (B,) + crow.shape)], axis=1).astype(bf)
    kwt = jnp.transpose(win_k, (0, 2, 1)).astype(bf)
    kct = jnp.pad(jnp.transpose(kc, (0, 2, 1)), ((0, 0), (0, 128 - dk), (0, nbp - nb))).astype(bf)
    vcp = jnp.pad(vc, ((0, 0), (0, nbp - nb), (0, 0))).astype(bf)
    gh = jnp.transpose(gates, (0, 2, 1, 3))
    nwc = NSA_WIN // TQ + 1
    ii = np.arange(TQ)[:, None]
    jj = np.arange(TQ)[None, :]
    wb = np.zeros((nwc, H, TQ, TQ), np.float32)
    for r in range(nwc):
        d = ii - jj + TQ * (nwc - 1 - r)
        ok = (d >= 0) & (d <= NSA_WIN)
        wb[r] = np.where(ok[None], -slopes[:, None, None] * d[None].astype(np.float32), NEG_INF)
    nv = SLC_CHUNK // TQ
    cb = np.zeros((nv, TQ, SLC_CHUNK), np.float32)
    for v in range(nv):
        cb[v] = np.where(np.arange(SLC_CHUNK)[None, :] <= v * TQ + ii, 0.0, NEG_INF)
    kern = functools.partial(_nsa_prompt_kernel, n_blocks=nbp)
    return pl.pallas_call(
        kern,
        grid=(B, T // TQ),
        in_specs=[pl.BlockSpec((1, H, TQ, 128), lambda b, i: (b, 0, i, 0)),
                  pl.BlockSpec((1, 256, T), lambda b, i: (b, 0, 0)),
                  pl.BlockSpec((1, T, 128), lambda b, i: (b, 0, 0)),
                  pl.BlockSpec((1, dk, T), lambda b, i: (b, 0, 0)),
                  pl.BlockSpec((1, T, 128), lambda b, i: (b, 0, 0)),
                  pl.BlockSpec((1, 128, nbp), lambda b, i: (b, 0, 0)),
                  pl.BlockSpec((1, nbp, dk), lambda b, i: (b, 0, 0)),
                  pl.BlockSpec((1, H, TQ, 3), lambda b, i: (b, 0, i, 0)),
                  pl.BlockSpec((H, 1, 1), lambda b, i: (0, 0, 0)),
                  pl.BlockSpec((nwc, H, TQ, TQ), lambda b, i: (0, 0, 0, 0)),
                  pl.BlockSpec((nv, TQ, SLC_CHUNK), lambda b, i: (0, 0, 0))],
        out_specs=pl.BlockSpec((1, H, TQ, dk), lambda b, i: (b, 0, i, 0)),
        out_shape=jax.ShapeDtypeStruct((B, H, T, dk), jnp.float32),
        compiler_params=pltpu.CompilerParams(dimension_semantics=("arbitrary", "arbitrary"),
                                             vmem_limit_bytes=VMEM_LIMIT_BYTES),
        name="nsa_prompt",
    )(q_lo, kt, _with_ones_column(slc_v), kwt, _with_ones_column(win_v), kct, vcp, gh,
      jnp.asarray(slopes).reshape(H, 1, 1), jnp.asarray(wb), jnp.asarray(cb))


MLA_TILE = 512


def _mla_prompt_kernel(q_ref, kt_ref, v_ref, tril_ref, o_ref, *, scale_log2e):
    TQ = MLA_TILE
    i = pl.program_id(2)
    q = q_ref[0, 0]

    def step(c, carry, bias):
        m_i, acc = carry
        k0 = pl.multiple_of(c * TQ, TQ)
        s = jnp.dot(q, kt_ref[0, 0, :, pl.ds(k0, TQ)], preferred_element_type=jnp.float32)
        if bias is not None:
            s = s + bias
        m_new = jnp.maximum(m_i, jnp.max(s, axis=-1, keepdims=True))
        corr = jnp.exp2((m_i - m_new) * scale_log2e)
        p = jnp.exp2((s - m_new) * scale_log2e)
        acc_new = acc * corr + jnp.dot(p.astype(jnp.bfloat16), v_ref[0, 0, pl.ds(k0, TQ), :],
                                       preferred_element_type=jnp.float32)
        return m_new, acc_new

    init = (jnp.full((TQ, 1), NEG_INF, jnp.float32), jnp.zeros((TQ, 128), jnp.float32))
    carry = lax.fori_loop(0, i, lambda c, cr: step(c, cr, None), init)
    m_i, acc = step(i, carry, tril_ref[...])
    o_ref[0, 0] = acc[:, :MLA_V] / acc[:, MLA_V:MLA_V + 1]


def mla_prompt_attention(qn, qp, kn, kp, v):
    B, T, H, _ = qn.shape
    TQ = MLA_TILE
    assert T % TQ == 0
    bf = jnp.bfloat16
    dq = MLA_NOPE + MLA_ROPE
    qcat = jnp.concatenate([qn, qp, jnp.zeros((B, T, H, 128 - dq), qn.dtype)], axis=-1)
    qcat = jnp.transpose(qcat, (0, 2, 1, 3)).astype(bf)
    kcat = jnp.concatenate([kn, jnp.broadcast_to(kp[:, :, None, :], (B, T, H, MLA_ROPE)),
                            jnp.zeros((B, T, H, 128 - dq), kn.dtype)], axis=-1)
    kt = jnp.transpose(kcat, (0, 2, 3, 1)).astype(bf)
    vh = _with_ones_column(jnp.transpose(v, (0, 2, 1, 3)))
    tril = np.where(np.arange(TQ)[None, :] <= np.arange(TQ)[:, None], 0.0, NEG_INF).astype(np.float32)
    kern = functools.partial(_mla_prompt_kernel, scale_log2e=float(dq ** -0.5 * math.log2(math.e)))
    return pl.pallas_call(
        kern,
        grid=(B, H, T // TQ),
        in_specs=[pl.BlockSpec((1, 1, TQ, 128), lambda b, h, i: (b, h, i, 0)),
                  pl.BlockSpec((1, 1, 128, T), lambda b, h, i: (b, h, 0, 0)),
                  pl.BlockSpec((1, 1, T, 128), lambda b, h, i: (b, h, 0, 0)),
                  pl.BlockSpec((TQ, TQ), lambda b, h, i: (0, 0))],
        out_specs=pl.BlockSpec((1, 1, TQ, MLA_V), lambda b, h, i: (b, h, i, 0)),
        out_shape=jax.ShapeDtypeStruct((B, H, T, MLA_V), jnp.float32),
        compiler_params=pltpu.CompilerParams(dimension_semantics=("arbitrary", "arbitrary", "arbitrary"),
                                             vmem_limit_bytes=VMEM_LIMIT_BYTES),
        name="mla_prompt",
    )(qcat, kt, vh, jnp.asarray(tril))


def _swa_prompt_kernel(q_ref, kt_ref, v_ref, bias_ref, sink_ref, o_ref):
    TQ = Q_TILE
    R = SWA_HEADS // SWA_KV_HEADS
    i = pl.program_id(2)
    q = q_ref[0, 0].reshape(R * TQ, SWA_HD)
    kprev = pl.multiple_of(jnp.maximum(i - 1, 0) * TQ, TQ)
    kcur = pl.multiple_of(i * TQ, TQ)
    s0 = jnp.dot(q, kt_ref[0, 0, :, pl.ds(kprev, TQ)], preferred_element_type=jnp.float32).reshape(R, TQ, TQ)
    s1 = jnp.dot(q, kt_ref[0, 0, :, pl.ds(kcur, TQ)], preferred_element_type=jnp.float32).reshape(R, TQ, TQ)
    s0 = s0 + jnp.where(i > 0, bias_ref[0, 0], NEG_INF)
    s1 = s1 + bias_ref[0, 1]
    sink = sink_ref[0]
    m = jnp.maximum(jnp.maximum(s0.max(axis=-1, keepdims=True), s1.max(axis=-1, keepdims=True)), sink)
    p0 = jnp.exp(s0 - m)
    p1 = jnp.exp(s1 - m)
    o = (jnp.dot(p0.reshape(R * TQ, TQ).astype(jnp.bfloat16), v_ref[0, 0, pl.ds(kprev, TQ), :],
                 preferred_element_type=jnp.float32)
         + jnp.dot(p1.reshape(R * TQ, TQ).astype(jnp.bfloat16), v_ref[0, 0, pl.ds(kcur, TQ), :],
                   preferred_element_type=jnp.float32)).reshape(R, TQ, 128)
    den = o[:, :, SWA_HD:SWA_HD + 1] + jnp.exp(sink - m)
    o_ref[0, 0] = o[:, :, :SWA_HD] / jnp.maximum(den, 1e-30)


def swa_prompt_attention(q, k, v, sinks):
    B, T, H, hd = q.shape
    G = k.shape[2]
    R = H // G
    TQ = Q_TILE
    assert SWA_WIN == TQ and T % TQ == 0
    bf = jnp.bfloat16
    slopes = alibi_slopes_np(H)
    qh = (jnp.transpose(q, (0, 2, 1, 3)) * (hd ** -0.5)).astype(bf).reshape(B, G, R, T, hd)
    kt = jnp.transpose(k, (0, 2, 3, 1)).astype(bf)
    vh = _with_ones_column(jnp.transpose(v, (0, 2, 1, 3)))
    ii = np.arange(TQ)[:, None]
    jj = np.arange(TQ)[None, :]
    bias = np.zeros((G, 2, R, TQ, TQ), np.float32)
    for r in range(2):
        d = ii - jj + TQ * (1 - r)
        ok = (d >= 0) & (d <= SWA_WIN)
        bias[:, r] = np.where(ok[None, None],
                              -slopes.reshape(G, R)[:, :, None, None] * d[None, None].astype(np.float32), NEG_INF)
    return pl.pallas_call(
        _swa_prompt_kernel,
        grid=(B, G, T // TQ),
        in_specs=[pl.BlockSpec((1, 1, R, TQ, hd), lambda b, g, i: (b, g, 0, i, 0)),
                  pl.BlockSpec((1, 1, hd, T), lambda b, g, i: (b, g, 0, 0)),
                  pl.BlockSpec((1, 1, T, 128), lambda b, g, i: (b, g, 0, 0)),
                  pl.BlockSpec((1, 2, R, TQ, TQ), lambda b, g, i: (g, 0, 0, 0, 0)),
                  pl.BlockSpec((1, R, 1, 1), lambda b, g, i: (g, 0, 0, 0))],
        out_specs=pl.BlockSpec((1, 1, R, TQ, hd), lambda b, g, i: (b, g, 0, i, 0)),
        out_shape=jax.ShapeDtypeStruct((B, G, R, T, hd), jnp.float32),
        compiler_params=pltpu.CompilerParams(dimension_semantics=("arbitrary", "arbitrary", "arbitrary"),
                                             vmem_limit_bytes=VMEM_LIMIT_BYTES),
        name="swa_prompt",
    )(qh, kt, vh, jnp.asarray(bias), sinks.reshape(G, R, 1, 1)).reshape(B, H, T, hd)


def window_sample(q, k_buf, v_buf, k_new, v_new, past_len, window, slopes, sink=None):
    Wb, Tn = k_buf.shape[1], q.shape[1]
    k = jnp.concatenate([k_buf, k_new], axis=1)
    v = jnp.concatenate([v_buf, v_new], axis=1)
    qpos = past_len + jnp.arange(Tn)
    kpos = past_len - Wb + jnp.arange(Wb + Tn)
    dist = qpos[:, None] - kpos[None, :]
    valid = (dist >= 0) & (dist <= window)
    return attend(q, k, v, dist, valid, slopes, sink)[0]


def nsa_compress(blocks, pos_emb, w1, w2):
    x = blocks + pos_emb
    h = jax.nn.gelu(jnp.einsum("...lcd,lcdh->...ch", x, w1))
    return jnp.einsum("...ch,chd->...cd", h, w2)


def nsa_select(p_cmp, qpos, n_blocks):
    imp = jnp.sum(p_cmp, axis=1)
    blk = jnp.arange(n_blocks)[None, :]
    cur = (qpos // NSA_BLK)[:, None]
    imp = jnp.where(blk == cur, float(NSA_HEADS + 1), jnp.where(blk < cur, imp, -1.0))
    picks = []
    for _ in range(min(NSA_TOPN, n_blocks)):
        i = jnp.argmax(imp, axis=-1)
        picks.append(i)
        imp = jnp.where(blk == i[..., None], -jnp.inf, imp)
    return jnp.stack(picks, axis=-1)


def nsa_compressed_branch(q, qpos, kc, vc, c_end, slopes):
    dist = qpos[:, None] - c_end[None, :]
    o, p = attend(q, kc[:, :, None, :], vc[:, :, None, :], dist, dist >= 0, slopes)
    return o, nsa_select(p, qpos, kc.shape[1])


def nsa_gate(gates, o_cmp, o_slc, o_win):
    g = gates.astype(o_cmp.dtype)
    return g[..., 0:1] * o_cmp + g[..., 1:2] * o_slc + g[..., 2:3] * o_win


CMP_PAGES_PER_STEP = 64
PAGED_VMEM_LIMIT_BYTES = 56 * 1024 * 1024


def pages_feature_major(cache):
    return jnp.transpose(cache, (0, 1, 3, 4, 2)).reshape(-1, 2, NSA_DK, PAGE_SIZE)


def _nsa_compress_kernel(pt_ref, pos_ref, w1b_ref, w2_ref, *rest, n_pages):
    page_refs = rest[:n_pages]
    o0_ref, o1_ref, slab = rest[n_pages:]
    feat = 2 * NSA_DK
    for j, r in enumerate(page_refs):
        slab[j * feat:(j + 1) * feat, :] = r[0].reshape(feat, PAGE_SIZE)
    hs = []
    for c in range(2):
        xs = [(slab[pl.ds(c * NSA_DK + d, n_pages, stride=feat), :] + pos_ref[c * NSA_DK + d:c * NSA_DK + d + 1, :]
               ).astype(jnp.bfloat16) for d in range(NSA_DK)]
        h = jnp.dot(jnp.concatenate(xs, axis=1), w1b_ref[c], preferred_element_type=jnp.float32)
        hs.append(jax.nn.gelu(h).astype(jnp.bfloat16))
    for j, o_ref in enumerate((o0_ref, o1_ref)):
        hj = jnp.concatenate([hs[c][:, j * NSA_CMP_HID:(j + 1) * NSA_CMP_HID] for c in range(2)], axis=1)
        o_ref[0] = jnp.dot(hj, w2_ref[...], preferred_element_type=jnp.float32)


def nsa_compress_paged(pages_t, page_table, pos_emb, w1, w2):
    S, P = page_table.shape
    PG = math.gcd(CMP_PAGES_PER_STEP, P)
    assert PAGE_SIZE == 2 * NSA_BLK and PG % 8 == 0
    bf = jnp.bfloat16
    feat = 2 * NSA_DK
    pos_t = jnp.tile(jnp.transpose(pos_emb, (1, 2, 0)).reshape(feat, NSA_BLK), (1, 2))
    w1f = jnp.transpose(w1, (1, 2, 0, 3)).reshape(feat, NSA_BLK, NSA_CMP_HID)
    z = jnp.zeros_like(w1f)
    w1b = jnp.concatenate([jnp.concatenate([w1f, z], axis=2), jnp.concatenate([z, w1f], axis=2)], axis=1).astype(bf)
    w1b = w1b.reshape(2, NSA_DK * PAGE_SIZE, 2 * NSA_CMP_HID)
    eye = jnp.eye(2, dtype=w2.dtype)
    w2z = jnp.einsum("chd,ce->ched", w2, eye).reshape(2 * NSA_CMP_HID, feat).astype(bf)

    def page_spec(j):
        return pl.BlockSpec((1, 2, NSA_DK, PAGE_SIZE), lambda s, g, pt: (pt[s, g * PG + j], 0, 0, 0))

    grid_spec = pltpu.PrefetchScalarGridSpec(
        num_scalar_prefetch=1,
        grid=(S, P // PG),
        in_specs=[pl.BlockSpec(pos_t.shape, lambda s, g, pt: (0, 0)),
                  pl.BlockSpec(w1b.shape, lambda s, g, pt: (0, 0, 0), pipeline_mode=pl.Buffered(1)),
                  pl.BlockSpec(w2z.shape, lambda s, g, pt: (0, 0))] + [page_spec(j) for j in range(PG)],
        out_specs=[pl.BlockSpec((1, PG, feat), lambda s, g, pt: (s, g, 0)),
                   pl.BlockSpec((1, PG, feat), lambda s, g, pt: (s, g, 0))],
        scratch_shapes=[pltpu.VMEM((PG * feat, PAGE_SIZE), jnp.float32)])
    o0, o1 = pl.pallas_call(
        functools.partial(_nsa_compress_kernel, n_pages=PG),
        grid_spec=grid_spec,
        out_shape=[jax.ShapeDtypeStruct((S, P, feat), jnp.float32)] * 2,
        compiler_params=pltpu.CompilerParams(dimension_semantics=("arbitrary", "arbitrary"),
                                             vmem_limit_bytes=PAGED_VMEM_LIMIT_BYTES),
        name="nsa_compress",
    )(page_table, pos_t, w1b, w2z, *([pages_t] * PG))
    return jnp.stack([o0, o1], axis=2).reshape(S, 2 * P, 2, NSA_DK)


def nsa_prompt(q, cmp_kv, slc_kv, win_kv, gates, ep, slopes):
    B, T = q.shape[:2]
    nb = T // NSA_BLK
    ppb = T // PAGE_SIZE
    prompt_pages = jnp.arange(B * ppb, dtype=jnp.int32).reshape(B, ppb)
    comp = nsa_compress_paged(pages_feature_major(cmp_kv.reshape(1, B * ppb, PAGE_SIZE, 2, NSA_DK)), prompt_pages,
                              ep["cmp_pos"], ep["cmp_w1"], ep["cmp_w2"])
    kc = rmsnorm(comp[:, :, 0], ep["g_k"][0])
    vc = comp[:, :, 1]
    o = nsa_prompt_attention(q, slc_kv[:, :, 0], slc_kv[:, :, 1], win_kv[:, :, 0], win_kv[:, :, 1], kc, vc, gates)
    return jnp.transpose(o, (0, 2, 1, 3))


def _nsa_sample_selected_kernel(phys_ref, idx_ref, q_ref, new_ref, slope_ref, *rest, n_sel, nb_past, qpos):
    page_refs = rest[:n_sel]
    o_ref = rest[n_sel]
    b = pl.program_id(0)
    nt = (((1,), (1,)), ((), ()))
    q = q_ref[0]
    lane = lax.broadcasted_iota(jnp.int32, (1, PAGE_SIZE), 1)
    ss, vts, dists = [], [], []
    for n, r in enumerate(page_refs):
        blk_id = idx_ref[b, n]
        is_new = blk_id >= nb_past
        half = jnp.where(is_new, 0, blk_id % 2)
        kt = jnp.where(is_new, new_ref[0, 0], r[0, 0]).astype(jnp.bfloat16)
        vts.append(jnp.where(is_new, new_ref[0, 1], r[0, 1]).astype(jnp.bfloat16))
        ss.append(jnp.dot(q, kt, preferred_element_type=jnp.float32))
        in_blk = (lane // NSA_BLK) == half
        dists.append(jnp.where(in_blk, qpos - blk_id * NSA_BLK - (lane % NSA_BLK), -1))
    s = jnp.concatenate(ss, axis=1)
    dist = jnp.concatenate(dists, axis=1)
    s = jnp.where(dist >= 0, s - slope_ref[...] * dist.astype(jnp.float32), NEG_INF)
    m = jnp.max(s, axis=-1, keepdims=True)
    e = jnp.where(dist >= 0, jnp.exp(s - m), 0.0)
    p = (e / jnp.maximum(jnp.sum(e, axis=-1, keepdims=True), 1e-30)).astype(jnp.bfloat16)
    o = jnp.zeros((q.shape[0], NSA_DK), jnp.float32)
    for n in range(n_sel):
        o = o + lax.dot_general(p[:, n * PAGE_SIZE:(n + 1) * PAGE_SIZE], vts[n], nt, preferred_element_type=jnp.float32)
    o_ref[0] = o


def nsa_sample_selected(q, idx, slc_new, cache_slc, page_table, e, qpos):
    DB, Tn, H, dk = q.shape
    n_sel = idx.shape[-1]
    assert Tn == 1
    P = page_table.shape[1]
    bpp = PAGE_SIZE // NSA_BLK
    nb_past = P * bpp
    idx2 = idx[:, 0].astype(jnp.int32)
    jp = jnp.minimum(idx2, nb_past - 1)
    phys = jnp.take_along_axis(page_table, jp // bpp, axis=1) + e * cache_slc.shape[1]
    qs = (q[:, 0] * (dk ** -0.5)).astype(jnp.bfloat16)
    new_t = jnp.zeros((DB, 2, dk, PAGE_SIZE), jnp.float32).at[:, :, :, 0].set(slc_new[:, 0])
    kern = functools.partial(_nsa_sample_selected_kernel, n_sel=n_sel, nb_past=nb_past, qpos=int(qpos))

    def page_spec(n):
        return pl.BlockSpec((1, 2, dk, PAGE_SIZE), lambda b, ph, ix: (ph[b, n], 0, 0, 0))

    grid_spec = pltpu.PrefetchScalarGridSpec(
        num_scalar_prefetch=2,
        grid=(DB,),
        in_specs=[pl.BlockSpec((1, H, dk), lambda b, ph, ix: (b, 0, 0)),
                  pl.BlockSpec((1, 2, dk, PAGE_SIZE), lambda b, ph, ix: (b, 0, 0, 0)),
                  pl.BlockSpec((H, 1), lambda b, ph, ix: (0, 0))] + [page_spec(n) for n in range(n_sel)],
        out_specs=pl.BlockSpec((1, H, dk), lambda b, ph, ix: (b, 0, 0)))
    out = pl.pallas_call(
        kern,
        grid_spec=grid_spec,
        out_shape=jax.ShapeDtypeStruct((DB, H, dk), jnp.float32),
        compiler_params=pltpu.CompilerParams(dimension_semantics=("arbitrary",), vmem_limit_bytes=VMEM_LIMIT_BYTES),
        name="nsa_sample_selected",
    )(phys, idx2, qs, new_t, jnp.asarray(alibi_slopes_np(H)).reshape(H, 1), *([pages_feature_major(cache_slc)] * n_sel))
    return out[:, None]


def nsa_sample(q, qpos, cmp_new, slc_new, win_new, gates, cache_cmp, cache_slc, win_buf, page_table, e, ep, slopes):
    DB, Tn = q.shape[:2]
    n_pages = page_table.shape[1]
    bpp = PAGE_SIZE // NSA_BLK
    nb_past = n_pages * bpp
    past_len = n_pages * PAGE_SIZE
    nb_new = -(-Tn // NSA_BLK)
    pad = ((0, 0), (0, nb_new * NSA_BLK - Tn), (0, 0), (0, 0))

    def compress(rows):
        return nsa_compress(rows, ep["cmp_pos"], ep["cmp_w1"], ep["cmp_w2"])

    comp_past = nsa_compress_paged(pages_feature_major(cache_cmp), page_table + e * cache_cmp.shape[1],
                                   ep["cmp_pos"], ep["cmp_w1"], ep["cmp_w2"])
    comp_new = compress(jnp.pad(cmp_new, pad).reshape(DB, nb_new, NSA_BLK, 2, NSA_DK))
    comp = jnp.concatenate([comp_past, comp_new], axis=1)
    kc = rmsnorm(comp[:, :, 0], ep["g_k"][0])
    vc = comp[:, :, 1]
    c_end = (jnp.arange(nb_past + nb_new) + 1) * NSA_BLK - 1
    o_cmp, idx = nsa_compressed_branch(q, qpos, kc, vc, c_end, slopes)

    o_slc = nsa_sample_selected(q, idx, slc_new, cache_slc, page_table, e, past_len)

    o_win = window_sample(q, win_buf[:, :, 0:1], win_buf[:, :, 1:2], win_new[:, :, 0:1], win_new[:, :, 1:2],
                          past_len, NSA_WIN, slopes)
    return nsa_gate(gates, o_cmp, o_slc, o_win)


def mla_keys(lat, ep):
    c = lat[..., :MLA_LORA]
    kp = lat[..., MLA_LORA:]
    kn = rmsnorm(jnp.einsum("bkc,chd->bkhd", c, ep["w_uk"]), ep["g_kn"])
    v = jnp.einsum("bkc,chd->bkhd", c, ep["w_uv"])
    return kn, kp, v


def mla_prompt(qn, qp, lat, ep):
    B, T = qn.shape[:2]
    kn, kp, v = mla_keys(lat, ep)
    return jnp.transpose(mla_prompt_attention(qn, qp, kn, kp, v), (0, 2, 1, 3))


MLA_PAGES_PER_STEP = 64


def _mla_sample_kernel(pt_ref, qg_ref, qp_ref, wukt_ref, wuv_ref, gt_ref, new_ref, *rest, scale, n_pages):
    page_refs = rest[:n_pages]
    o_ref, c_scr, kp_scr, a_scr, m_scr, l_scr, acc_scr = rest[n_pages:]
    g = pl.program_id(1)
    ng = pl.num_programs(1)
    nt = (((1,), (1,)), ((), ()))

    @pl.when(g == 0)
    def _():
        a_scr[...] = lax.dot_general(qg_ref[0], wukt_ref[...], (((1,), (0,)), ((), ())),
                                     preferred_element_type=jnp.float32).astype(jnp.bfloat16)
        m_scr[...] = jnp.full(m_scr.shape, NEG_INF, jnp.float32)
        l_scr[...] = jnp.zeros(l_scr.shape, jnp.float32)
        acc_scr[...] = jnp.zeros(acc_scr.shape, jnp.float32)

    def scores(ct, kpt):
        projt = jnp.dot(wukt_ref[...], ct, preferred_element_type=jnp.float32)
        ss = jnp.dot(gt_ref[...], (projt * projt).astype(jnp.bfloat16), preferred_element_type=jnp.float32)
        num = jnp.dot(a_scr[...], ct, preferred_element_type=jnp.float32)
        sp = jnp.dot(qp_ref[0], kpt, preferred_element_type=jnp.float32)
        return (num * lax.rsqrt(ss * (1.0 / MLA_NOPE) + EPS) + sp) * scale

    def softmax_update(ct, st):
        m_i = m_scr[...]
        m_new = jnp.maximum(m_i, jnp.max(st, axis=-1, keepdims=True))
        corr = jnp.exp(m_i - m_new)
        p = jnp.exp(st - m_new)
        m_scr[...] = m_new
        l_scr[...] = l_scr[...] * corr + jnp.sum(p, axis=-1, keepdims=True)
        acc_scr[...] = acc_scr[...] * corr + lax.dot_general(p.astype(jnp.bfloat16), ct, nt,
                                                             preferred_element_type=jnp.float32)

    for j, r in enumerate(page_refs):
        page = r[0]
        c_scr[:, j * PAGE_SIZE:(j + 1) * PAGE_SIZE] = page[:MLA_LORA, :].astype(jnp.bfloat16)
        kp_scr[:, j * PAGE_SIZE:(j + 1) * PAGE_SIZE] = page[MLA_LORA:, :].astype(jnp.bfloat16)
    c_all = c_scr[...]
    softmax_update(c_all, scores(c_all, kp_scr[...]))

    @pl.when(g == ng - 1)
    def _():
        new = new_ref[0]
        ct = new[:MLA_LORA, :].astype(jnp.bfloat16)
        st = scores(ct, new[MLA_LORA:, :].astype(jnp.bfloat16))
        key = lax.broadcasted_iota(jnp.int32, st.shape, 1)
        softmax_update(ct, jnp.where(key == 0, st, NEG_INF))
        o8 = jnp.dot((acc_scr[...] / l_scr[...]).astype(jnp.bfloat16), wuv_ref[...],
                     preferred_element_type=jnp.float32)
        row = lax.broadcasted_iota(jnp.int32, o8.shape, 0)
        col = lax.broadcasted_iota(jnp.int32, o8.shape, 1)
        o_ref[0] = jnp.sum(jnp.where(col // MLA_V == row, o8, 0.0), axis=0, keepdims=True)


def mla_sample(qn, qp, lat_new, cache_mla, page_table, e, ep):
    DB, Tn = qn.shape[:2]
    H = MLA_HEADS
    P = page_table.shape[1]
    PG = math.gcd(MLA_PAGES_PER_STEP, P)
    assert Tn == 1
    bf = jnp.bfloat16
    dq = MLA_NOPE + MLA_ROPE
    qg = qn[:, 0] * ep["g_kn"]
    eye = jnp.eye(H, dtype=qg.dtype)
    qg_exp = (qg[:, :, None, :] * eye[None, :, :, None]).reshape(DB, H, H * MLA_NOPE).astype(bf)
    qp_h = qp[:, 0].astype(bf)
    wukt = ep["w_uk"].reshape(MLA_LORA, H * MLA_NOPE).T.astype(bf)
    wuv = ep["w_uv"].reshape(MLA_LORA, H * MLA_V).astype(bf)
    gt = np.zeros((H, H * MLA_NOPE), np.float32)
    gt[np.arange(H * MLA_NOPE) // MLA_NOPE, np.arange(H * MLA_NOPE)] = 1.0
    pages_t = jnp.swapaxes(cache_mla, -1, -2).reshape(-1, MLA_ROW, PAGE_SIZE)
    new_t = jnp.zeros((DB, MLA_ROW, PAGE_SIZE), jnp.float32).at[:, :, 0].set(lat_new[:, 0])
    kern = functools.partial(_mla_sample_kernel, scale=float(dq ** -0.5), n_pages=PG)

    def page_spec(j):
        return pl.BlockSpec((1, MLA_ROW, PAGE_SIZE), lambda b, g, pt: (pt[b, g * PG + j], 0, 0))

    const2 = lambda b, g, pt: (0, 0)
    grid_spec = pltpu.PrefetchScalarGridSpec(
        num_scalar_prefetch=1,
        grid=(DB, P // PG),
        in_specs=[pl.BlockSpec((1, H, H * MLA_NOPE), lambda b, g, pt: (b, 0, 0)),
                  pl.BlockSpec((1, H, MLA_ROPE), lambda b, g, pt: (b, 0, 0)),
                  pl.BlockSpec((H * MLA_NOPE, MLA_LORA), const2),
                  pl.BlockSpec((MLA_LORA, H * MLA_V), const2),
                  pl.BlockSpec((H, H * MLA_NOPE), const2),
                  pl.BlockSpec((1, MLA_ROW, PAGE_SIZE), lambda b, g, pt: (b, 0, 0))]
                 + [page_spec(j) for j in range(PG)],
        out_specs=pl.BlockSpec((1, 1, H * MLA_V), lambda b, g, pt: (b, 0, 0)),
        scratch_shapes=[pltpu.VMEM((MLA_LORA, PG * PAGE_SIZE), bf),
                        pltpu.VMEM((MLA_ROPE, PG * PAGE_SIZE), bf),
                        pltpu.VMEM((H, MLA_LORA), bf),
                        pltpu.VMEM((8, 1), jnp.float32),
                        pltpu.VMEM((8, 1), jnp.float32),
                        pltpu.VMEM((8, MLA_LORA), jnp.float32)])
    out = pl.pallas_call(
        kern,
        grid_spec=grid_spec,
        out_shape=jax.ShapeDtypeStruct((DB, 1, H * MLA_V), jnp.float32),
        compiler_params=pltpu.CompilerParams(dimension_semantics=("arbitrary", "arbitrary"),
                                             vmem_limit_bytes=VMEM_LIMIT_BYTES),
        name="mla_sample",
    )(page_table + e * cache_mla.shape[1], qg_exp, qp_h, wukt, wuv, jnp.asarray(gt).astype(bf), new_t,
      *([pages_t] * PG))
    return out.reshape(DB, 1, H, MLA_V)


def even_project(x, g_norm, pos, ep):
    B, T, D = x.shape
    h = norm_matmul(x.reshape(B * T, D), g_norm, ep["w_in"]).reshape(B, T, EVEN_IN)
    cuts = np.cumsum([NSA_Q_COLS, NSA_KV_COLS, NSA_GATE_COLS, MLA_Q_COLS]).tolist()
    q, kv, g, mq, lat = jnp.split(h, cuts, axis=-1)
    q = rmsnorm(q.reshape(B, T, NSA_HEADS, NSA_DK), ep["g_q"])
    kv = kv.reshape(B, T, 3, 2, NSA_DK)
    cmp_kv = kv[:, :, 0]
    slc_kv = jnp.stack([rmsnorm(kv[:, :, 1, 0], ep["g_k"][1]), kv[:, :, 1, 1]], axis=2)
    win_kv = jnp.stack([rmsnorm(kv[:, :, 2, 0], ep["g_k"][2]), kv[:, :, 2, 1]], axis=2)
    gates = jax.nn.sigmoid(g.astype(jnp.float32)).reshape(B, T, NSA_HEADS, 3)
    mq = mq.reshape(B, T, MLA_HEADS, MLA_NOPE + MLA_ROPE)
    qn = rmsnorm(mq[..., :MLA_NOPE], ep["g_qn"])
    qp = rope(rmsnorm(mq[..., MLA_NOPE:], ep["g_qpe"]), pos)
    c = rmsnorm(lat[..., :MLA_LORA], ep["g_lat"])
    kp = rope(rmsnorm(lat[..., MLA_LORA:], ep["g_kpe"])[:, :, None, :], pos)[:, :, 0, :]
    return q, cmp_kv, slc_kv, win_kv, gates, qn, qp, jnp.concatenate([c, kp], axis=-1)


def even_output(x, o_nsa, o_mla, w_out):
    B, T, D = x.shape
    a = jnp.concatenate([o_nsa.reshape(B, T, -1), o_mla.reshape(B, T, -1)], axis=-1)
    return matmul_res(a.reshape(B * T, EVEN_OUT), w_out, x.reshape(B * T, D)).reshape(B, T, D)


def odd_project(x, g_norm, op):
    B, T, D = x.shape
    h = norm_matmul(x.reshape(B * T, D), g_norm, op["w_in"]).reshape(B, T, ODD_IN)
    q, k, v = jnp.split(h, [SWA_HEADS * SWA_HD, SWA_HEADS * SWA_HD + SWA_KV_HEADS * SWA_HD], axis=-1)
    q = rmsnorm(q.reshape(B, T, SWA_HEADS, SWA_HD), op["g_q"])
    k = rmsnorm(k.reshape(B, T, SWA_KV_HEADS, SWA_HD), op["g_k"])
    v = v.reshape(B, T, SWA_KV_HEADS, SWA_HD)
    return q, jnp.stack([k, v], axis=2)


def memory_kv(mem, cp):
    B, M, D = mem.shape
    w_kv = jnp.concatenate([cp["w_k"], cp["w_v"]], axis=1)
    kv = norm_matmul(mem.reshape(B * M, D), cp["g_mem"], w_kv).reshape(B, M, 2, MEM_HEADS, MEM_HD)
    k = rmsnorm(kv[:, :, 0], cp["g_k"])
    return jnp.stack([k, kv[:, :, 1]], axis=2)


MEM_TQ = 256


def _memory_cross_kernel(x_ref, g_ref, wq_ref, gq_ref, k_ref, v_ref, wo_ref, o_ref):
    x = x_ref[0]
    xn = x * lax.rsqrt(jnp.mean(x * x, axis=-1, keepdims=True) + EPS) * g_ref[...]
    q = jnp.dot(xn.astype(jnp.bfloat16), wq_ref[...], preferred_element_type=jnp.float32)
    nt = (((1,), (1,)), ((), ()))
    outs = []
    for h in range(MEM_HEADS):
        cols = slice(h * MEM_HD, (h + 1) * MEM_HD)
        qh = q[:, cols]
        qh = qh * lax.rsqrt(jnp.mean(qh * qh, axis=-1, keepdims=True) + EPS) * gq_ref[...]
        s = lax.dot_general(qh.astype(jnp.bfloat16), k_ref[0, :, cols], nt,
                            preferred_element_type=jnp.float32) * (MEM_HD ** -0.5)
        e = jnp.exp(s - jnp.max(s, axis=-1, keepdims=True))
        p = e / jnp.sum(e, axis=-1, keepdims=True)
        outs.append(jnp.dot(p.astype(jnp.bfloat16), v_ref[0, :, cols], preferred_element_type=jnp.float32))
    o = jnp.concatenate(outs, axis=1).astype(jnp.bfloat16)
    o_ref[0] = x + jnp.dot(o, wo_ref[...], preferred_element_type=jnp.float32)


def memory_cross_prompt(x, g_norm, mem_kv, cp):
    B, T, D = x.shape
    M = mem_kv.shape[1]
    bf = jnp.bfloat16
    tq = MEM_TQ
    k = mem_kv[:, :, 0].reshape(B, M, MEM_W).astype(bf)
    v = mem_kv[:, :, 1].reshape(B, M, MEM_W).astype(bf)
    return pl.pallas_call(
        _memory_cross_kernel,
        grid=(B, T // tq),
        in_specs=[pl.BlockSpec((1, tq, D), lambda b, i: (b, i, 0)),
                  pl.BlockSpec((1, D), lambda b, i: (0, 0)),
                  pl.BlockSpec((D, MEM_W), lambda b, i: (0, 0)),
                  pl.BlockSpec((1, MEM_HD), lambda b, i: (0, 0)),
                  pl.BlockSpec((1, M, MEM_W), lambda b, i: (b, 0, 0)),
                  pl.BlockSpec((1, M, MEM_W), lambda b, i: (b, 0, 0)),
                  pl.BlockSpec((MEM_W, D), lambda b, i: (0, 0))],
        out_specs=pl.BlockSpec((1, tq, D), lambda b, i: (b, i, 0)),
        out_shape=jax.ShapeDtypeStruct((B, T, D), jnp.float32),
        compiler_params=pltpu.CompilerParams(dimension_semantics=("arbitrary", "arbitrary"),
                                             vmem_limit_bytes=VMEM_LIMIT_BYTES),
        name="memory_cross",
    )(x, g_norm.reshape(1, D), cp["w_q"].astype(bf), cp["g_q"].reshape(1, MEM_HD), k, v, cp["w_o"].astype(bf))


def memory_cross(x, g_norm, mem_kv, cp):
    B, T, D = x.shape
    if T % MEM_TQ == 0:
        return memory_cross_prompt(x, g_norm, mem_kv, cp)
    q = norm_matmul(x.reshape(B * T, D), g_norm, cp["w_q"]).reshape(B, T, MEM_HEADS, MEM_HD)
    q = rmsnorm(q, cp["g_q"])
    s = jnp.einsum("bthd,bmhd->bhtm", q, mem_kv[:, :, 0]).astype(jnp.float32) * (MEM_HD ** -0.5)
    p = jax.nn.softmax(s, axis=-1).astype(x.dtype)
    o = jnp.einsum("bhtm,bmhd->bthd", p, mem_kv[:, :, 1]).reshape(B * T, MEM_W)
    return matmul_res(o, cp["w_o"], x.reshape(B * T, D)).reshape(B, T, D)


PEER_TB = 128
PEER_VMEM_LIMIT_BYTES = 56 * 1024 * 1024
F32_NEG_INF = float("-inf")


def _topk_rows(s, row, k):
    nrow = float(s.shape[0])
    vals, ids = [], []
    for _ in range(k):
        m = jnp.max(s, axis=0, keepdims=True)
        i = jnp.min(jnp.where(s == m, row, nrow), axis=0, keepdims=True)
        vals.append(m)
        ids.append(i)
        s = jnp.where(row == i, F32_NEG_INF, s)
    return vals, ids


def _peer_route_kernel(x_ref, g_ref, wq_ref, sk0_ref, sk1_ref, xn_ref, idx_ref, gate_ref):
    x = x_ref[...]
    xn = x * lax.rsqrt(jnp.mean(x * x, axis=-1, keepdims=True) + EPS) * g_ref[...]
    xn_ref[...] = xn
    q = jnp.dot(xn.astype(jnp.bfloat16), wq_ref[...], preferred_element_type=jnp.float32)
    tb = x.shape[0]
    row_k = lax.broadcasted_iota(jnp.int32, (PEER_NKEYS, tb), 0).astype(jnp.float32)
    sub8 = lax.broadcasted_iota(jnp.int32, (8, tb), 0).astype(jnp.float32)
    assert PEER_TOPK % 16 == 0
    nt = (((1,), (1,)), ((), ()))
    ids, gates = [], []
    for h in range(PEER_HEADS):
        qh = q[:, h * PEER_DKEY:(h + 1) * PEER_DKEY].astype(jnp.bfloat16)
        s1 = lax.dot_general(sk0_ref[...], qh, nt, preferred_element_type=jnp.float32)
        s2 = lax.dot_general(sk1_ref[...], qh, nt, preferred_element_type=jnp.float32)
        v1, i1 = _topk_rows(s1, row_k, PEER_TOPK)
        v2, i2 = _topk_rows(s2, row_k, PEER_TOPK)
        v1c = jnp.concatenate(v1, axis=0)
        i1c = jnp.concatenate(i1, axis=0)
        v2c = jnp.concatenate(v2, axis=0)
        i2c = jnp.concatenate(i2, axis=0)
        cands, poss, cids = [], [], []
        for a in range(PEER_TOPK // 2):
            bmax = PEER_TOPK // (a + 1) - 1
            for b0 in range(0, bmax + 1, 8):
                c = v1[a] + v2c[b0:b0 + 8]
                if bmax - b0 + 1 < 8:
                    c = jnp.where(sub8 <= float(bmax - b0), c, F32_NEG_INF)
                cands.append(c)
                poss.append(sub8 + float(a * PEER_TOPK + b0))
                cids.append(i1[a] * float(PEER_NKEYS) + i2c[b0:b0 + 8])
        for a0 in range(PEER_TOPK // 2, PEER_TOPK, 8):
            cands.append(v1c[a0:a0 + 8] + v2[0])
            poss.append((sub8 + float(a0)) * float(PEER_TOPK))
            cids.append(i1c[a0:a0 + 8] * float(PEER_NKEYS) + i2[0])
        cand = jnp.concatenate(cands, axis=0)
        row_c = jnp.concatenate(poss, axis=0)
        cid = jnp.concatenate(cids, axis=0)
        tops, tids = [], []
        for _ in range(PEER_TOPK):
            m = jnp.max(cand, axis=0, keepdims=True)
            pos = jnp.min(jnp.where(cand == m, row_c, float(PEER_TOPK * PEER_TOPK)), axis=0, keepdims=True)
            hit = row_c == pos
            tids.append(jnp.sum(jnp.where(hit, cid, 0.0), axis=0, keepdims=True))
            tops.append(m)
            cand = jnp.where(hit, F32_NEG_INF, cand)
        ts = jnp.concatenate(tops, axis=0)
        e = jnp.exp(ts - tops[0])
        gates.append(e / jnp.sum(e, axis=0, keepdims=True))
        ids.append(jnp.concatenate(tids, axis=0))
    table_rows = x.shape[1] // 256
    idx_ref[...] = (jnp.concatenate(ids, axis=0) * float(table_rows)).T.astype(jnp.int32)
    gate_ref[...] = jnp.concatenate(gates, axis=0).T


def peer_route(x, g, w_q, subkeys):
    n, d = x.shape
    tb = min(PEER_TB, n)
    hk = PEER_HEADS * PEER_TOPK
    half = PEER_DKEY // 2
    z = jnp.zeros((PEER_NKEYS, half), jnp.float32)
    sk0 = jnp.concatenate([subkeys[0], z], axis=1).astype(jnp.bfloat16)
    sk1 = jnp.concatenate([z, subkeys[1]], axis=1).astype(jnp.bfloat16)
    return pl.pallas_call(
        _peer_route_kernel,
        grid=(n // tb,),
        in_specs=[pl.BlockSpec((tb, d), lambda i: (i, 0)),
                  pl.BlockSpec((1, d), lambda i: (0, 0)),
                  pl.BlockSpec((d, PEER_HEADS * PEER_DKEY), lambda i: (0, 0)),
                  pl.BlockSpec((PEER_NKEYS, PEER_DKEY), lambda i: (0, 0)),
                  pl.BlockSpec((PEER_NKEYS, PEER_DKEY), lambda i: (0, 0))],
        out_specs=[pl.BlockSpec((tb, d), lambda i: (i, 0)),
                   pl.BlockSpec((tb, hk), lambda i: (i, 0)),
                   pl.BlockSpec((tb, hk), lambda i: (i, 0))],
        out_shape=[jax.ShapeDtypeStruct((n, d), jnp.float32),
                   jax.ShapeDtypeStruct((n, hk), jnp.int32),
                   jax.ShapeDtypeStruct((n, hk), jnp.float32)],
        compiler_params=pltpu.CompilerParams(dimension_semantics=("arbitrary",),
                                             vmem_limit_bytes=PEER_VMEM_LIMIT_BYTES),
        name="peer_route",
    )(x, g.reshape(1, d), w_q.astype(jnp.bfloat16), sk0, sk1)


def pack_table(t):
    e, d = t.shape
    b = lax.bitcast_convert_type(t.astype(jnp.bfloat16), jnp.uint16).astype(jnp.uint32)
    w = b[:, : d // 2] | (b[:, d // 2:] << 16)
    return w.reshape(e * d // 256, 128)


def _table_spec(tab):
    return pl.BlockSpec(tab.shape, lambda i: (0, 0), pipeline_mode=pl.Buffered(1))


def _gather_row(tab, row0, rows):
    wds = tab[pl.ds(pl.multiple_of(row0, rows), rows), :]
    lo = pltpu.bitcast(wds << 16, jnp.float32)
    hi = pltpu.bitcast(wds & jnp.uint32(0xFFFF0000), jnp.float32)
    return lo, hi


def _peer_u_kernel(idx_ref, xn_ref, gate_ref, tab, w_ref, slots, rsum, act):
    tb, hk = gate_ref.shape
    rows = xn_ref.shape[1] // 2

    def token(t, c):
        xt = xn_ref[t]
        xlo, xhi = xt[:rows], xt[rows:]
        for k in range(hk):
            lo, hi = _gather_row(tab, idx_ref[t, k], rows)
            slots[k * rows:(k + 1) * rows, :] = lo * xlo + hi * xhi
        r = slots[pl.ds(0, hk, stride=rows), :]
        for s in range(1, rows):
            r = r + slots[pl.ds(s, hk, stride=rows), :]
        rsum[pl.ds(pl.multiple_of(t * hk, hk), hk), :] = r
        return c

    lax.fori_loop(0, tb, token, 0)

    grp = 8

    def lane_sum(c, carry):
        sums = []
        for i in range(grp):
            rr = rsum[pl.ds(pl.multiple_of((c * grp + i) * hk, hk), hk), :]
            sums.append(jnp.sum(rr.T, axis=0, keepdims=True))
        act[pl.ds(pl.multiple_of(c * grp, grp), grp), :] = jnp.concatenate(sums, axis=0)
        return carry

    lax.fori_loop(0, tb // grp, lane_sum, 0)
    w_ref[...] = gate_ref[...] * jax.nn.gelu(act[...])


def peer_activate(idx, xn, gate, tab):
    n, d = xn.shape
    hk = idx.shape[1]
    tb = min(PEER_TB, n)
    sub = d // 128
    return pl.pallas_call(
        _peer_u_kernel,
        grid=(n // tb,),
        in_specs=[pl.BlockSpec((tb, hk), lambda i: (i, 0), memory_space=pltpu.SMEM),
                  pl.BlockSpec((tb, sub, 128), lambda i: (i, 0, 0)),
                  pl.BlockSpec((tb, hk), lambda i: (i, 0)),
                  _table_spec(tab)],
        out_specs=pl.BlockSpec((tb, hk), lambda i: (i, 0)),
        out_shape=jax.ShapeDtypeStruct((n, hk), jnp.float32),
        scratch_shapes=[pltpu.VMEM((hk * sub // 2, 128), jnp.float32),
                        pltpu.VMEM((tb * hk, 128), jnp.float32),
                        pltpu.VMEM((tb, hk), jnp.float32)],
        compiler_params=pltpu.CompilerParams(dimension_semantics=("arbitrary",),
                                             vmem_limit_bytes=PEER_VMEM_LIMIT_BYTES),
        name="peer_u",
    )(idx, xn.reshape(n, sub, 128), gate, tab)


def _peer_v_kernel(idx_ref, w_ref, x_ref, tab, o_ref):
    tb, hk = idx_ref.shape
    rows = x_ref.shape[1] // 2

    def token(t, c):
        nacc = 2
        lo_acc = [jnp.zeros((rows, 128), jnp.float32) for _ in range(nacc)]
        hi_acc = [jnp.zeros((rows, 128), jnp.float32) for _ in range(nacc)]
        for k in range(hk):
            lo, hi = _gather_row(tab, idx_ref[t, k], rows)
            wk = w_ref[t, k]
            lo_acc[k % nacc] = lo_acc[k % nacc] + wk * lo
            hi_acc[k % nacc] = hi_acc[k % nacc] + wk * hi
        xt = x_ref[t]
        o_ref[t, :rows, :] = xt[:rows] + (lo_acc[0] + lo_acc[1])
        o_ref[t, rows:, :] = xt[rows:] + (hi_acc[0] + hi_acc[1])
        return c

    lax.fori_loop(0, tb, token, 0)


def peer_combine(idx, w, x, tab):
    n, d = x.shape
    hk = idx.shape[1]
    tb = min(PEER_TB, n)
    sub = d // 128
    out = pl.pallas_call(
        _peer_v_kernel,
        grid=(n // tb,),
        in_specs=[pl.BlockSpec((tb, hk), lambda i: (i, 0), memory_space=pltpu.SMEM),
                  pl.BlockSpec((tb, hk), lambda i: (i, 0), memory_space=pltpu.SMEM),
                  pl.BlockSpec((tb, sub, 128), lambda i: (i, 0, 0)),
                  _table_spec(tab)],
        out_specs=pl.BlockSpec((tb, sub, 128), lambda i: (i, 0, 0)),
        out_shape=jax.ShapeDtypeStruct((n, sub, 128), jnp.float32),
        compiler_params=pltpu.CompilerParams(dimension_semantics=("arbitrary",),
                                             vmem_limit_bytes=PEER_VMEM_LIMIT_BYTES),
        name="peer_v",
    )(idx, w, x.reshape(n, sub, 128), tab)
    return out.reshape(n, d)


def peer_ffn(x, g_norm, pp):
    B, T, D = x.shape
    x2 = x.reshape(B * T, D)
    xn, idx, gate = peer_route(x2, g_norm, pp["w_q"], pp["subkeys"])
    w = peer_activate(idx, xn, gate, pp["u_packed"])
    return peer_combine(idx, w, x2, pp["v_packed"]).reshape(B, T, D)


def kernel(x_prompt, x_sample, mem_prompt, cache_nsa_cmp, cache_nsa_slc, cache_nsa_win, cache_mla, cache_swa,
           cache_mem, page_table, norm_mix, norm_cross, norm_ffn, even_w_in, even_w_out, nsa_g_q, nsa_g_k,
           nsa_cmp_pos, nsa_cmp_w1, nsa_cmp_w2, mla_g_qn, mla_g_qpe, mla_g_lat, mla_g_kpe, mla_g_kn, mla_w_uk,
           mla_w_uv, odd_w_in, odd_w_out, swa_g_q, swa_g_k, swa_sinks, mem_g, mem_w_q, mem_w_k, mem_w_v,
           mem_g_q, mem_g_k, mem_w_o, peer_w_q, peer_subkeys, peer_u, peer_v):
    depth = norm_mix.shape[0]
    slopes_nsa = alibi_slopes(NSA_HEADS)
    slopes_swa = alibi_slopes(SWA_HEADS)
    past_len = page_table.shape[1] * PAGE_SIZE
    T, Tn = x_prompt.shape[1], x_sample.shape[1]
    pos_p = jnp.arange(T)
    pos_s = past_len + jnp.arange(Tn)
    xp, xs = x_prompt, x_sample
    st_cmp_p, st_cmp_s, st_slc_p, st_slc_s, st_win_p, st_win_s = [], [], [], [], [], []
    st_mla_p, st_mla_s, st_swa_p, st_swa_s, st_mem_p = [], [], [], [], []

    for li in range(depth):
        if li % 2 == 0:
            e = li // 2
            ep = dict(w_in=even_w_in[e], g_q=nsa_g_q[e], g_k=nsa_g_k[e], cmp_pos=nsa_cmp_pos[e],
                      cmp_w1=nsa_cmp_w1[e], cmp_w2=nsa_cmp_w2[e], g_qn=mla_g_qn[e], g_qpe=mla_g_qpe[e],
                      g_lat=mla_g_lat[e], g_kpe=mla_g_kpe[e], g_kn=mla_g_kn[e], w_uk=mla_w_uk[e], w_uv=mla_w_uv[e])
            q, cmp_kv, slc_kv, win_kv, gates, qn, qp, mla_row = even_project(xp, norm_mix[li], pos_p, ep)
            o_nsa = nsa_prompt(q, cmp_kv, slc_kv, win_kv, gates, ep, slopes_nsa)
            o_mla = mla_prompt(qn, qp, mla_row, ep)
            xp = even_output(xp, o_nsa, o_mla, even_w_out[e])
            st_cmp_p.append(cmp_kv)
            st_slc_p.append(slc_kv)
            st_win_p.append(win_kv[:, -min(NSA_WIN, T):])
            st_mla_p.append(mla_row)
            win_buf = cache_nsa_win[e]
            q, cmp_kv, slc_kv, win_kv, gates, qn, qp, mla_row = even_project(xs, norm_mix[li], pos_s, ep)
            o_nsa = nsa_sample(q, pos_s, cmp_kv, slc_kv, win_kv, gates, cache_nsa_cmp, cache_nsa_slc, win_buf,
                               page_table, e, ep, slopes_nsa)
            o_mla = mla_sample(qn, qp, mla_row, cache_mla, page_table, e, ep)
            xs = even_output(xs, o_nsa, o_mla, even_w_out[e])
            st_cmp_s.append(cmp_kv)
            st_slc_s.append(slc_kv)
            st_win_s.append(jnp.concatenate([win_buf, win_kv], axis=1)[:, -win_buf.shape[1]:])
            st_mla_s.append(mla_row)
        else:
            o = li // 2
            op = dict(w_in=odd_w_in[o], g_q=swa_g_q[o], g_k=swa_g_k[o])
            B = xp.shape[0]
            q, kv = odd_project(xp, norm_mix[li], op)
            att = jnp.transpose(swa_prompt_attention(q, kv[:, :, 0], kv[:, :, 1], swa_sinks[o]), (0, 2, 1, 3))
            xp = matmul_res(att.reshape(B * T, ODD_OUT), odd_w_out[o], xp.reshape(B * T, -1)).reshape(xp.shape)
            st_swa_p.append(kv[:, -min(SWA_WIN, T):])
            buf = cache_swa[o]
            q, kv = odd_project(xs, norm_mix[li], op)
            att = window_sample(q, buf[:, :, 0], buf[:, :, 1], kv[:, :, 0], kv[:, :, 1], past_len, SWA_WIN,
                                slopes_swa, swa_sinks[o])
            xs = matmul_res(att.reshape(xs.shape[0] * Tn, ODD_OUT), odd_w_out[o],
                            xs.reshape(xs.shape[0] * Tn, -1)).reshape(xs.shape)
            st_swa_s.append(jnp.concatenate([buf, kv], axis=1)[:, -buf.shape[1]:])

        cp = dict(g_mem=mem_g[li], w_q=mem_w_q[li], w_k=mem_w_k[li], w_v=mem_w_v[li], g_q=mem_g_q[li],
                  g_k=mem_g_k[li], w_o=mem_w_o[li])
        mem_kv_p = memory_kv(mem_prompt, cp)
        xp = memory_cross(xp, norm_cross[li], mem_kv_p, cp)
        xs = memory_cross(xs, norm_cross[li], cache_mem[li], cp)
        st_mem_p.append(mem_kv_p)

        pp = dict(w_q=peer_w_q[li], subkeys=peer_subkeys[li],
                  u_packed=pack_table(peer_u[li]), v_packed=pack_table(peer_v[li]))
        xp = peer_ffn(xp, norm_ffn[li], pp)
        xs = peer_ffn(xs, norm_ffn[li], pp)

    return (xp, xs, jnp.stack(st_cmp_p), jnp.stack(st_cmp_s), jnp.stack(st_slc_p), jnp.stack(st_slc_s),
            jnp.stack(st_win_p), jnp.stack(st_win_s), jnp.stack(st_mla_p), jnp.stack(st_mla_s),
            jnp.stack(st_swa_p), jnp.stack(st_swa_s), jnp.stack(st_mem_p))
```
